```python
import math
import jax
import jax.numpy as jnp
from jax import lax
import numpy as np

D_MODEL = 1024
BATCH = 16
SEQ = 2048
DEPTH = 1

GRID_W = 64
CTX_LEN = 256
D_GLA = 512
D_S5 = 512
GLA_HEADS = 4
GLA_DK_HEAD = 64
GLA_DV_HEAD = D_GLA // GLA_HEADS
GLA_DK = GLA_HEADS * GLA_DK_HEAD
GLA_DV = D_GLA
GLA_GATE_RANK = 16
GLA_GATE_TAU = 16.0
GLA_CHUNK = GRID_W
S5_GROUP_CH = 16
S5_GROUPS = D_S5 // S5_GROUP_CH
S5_STATE = 64
D_IN = 2 * GLA_DK + 2 * GLA_DV + 2 * GLA_GATE_RANK + D_S5
N_EXPERTS = 256
TOP_K = 8
N_EXPERT_GROUPS = 8
TOPK_GROUPS = 4
D_EXPERT = 256
D_SHARED = 256
ROUTE_SCALE = 2.5
MOE_BLOCK = 128
EPS = 1e-6

kernel_name = 'hybrid_gla_s5_moe_prefix_block'


def rmsnorm(x, g):
    xf = x.astype(jnp.float32)
    y = xf * lax.rsqrt(jnp.mean(xf * xf, axis=-1, keepdims=True) + EPS)
    return (y * g.astype(jnp.float32)).astype(x.dtype)


def adaln(cvec, w, b):
    mod = (jax.nn.silu(cvec) @ w + b)[..., None, :]
    return jnp.split(mod, 6, axis=-1)


def modulate(h, shift, scale):
    return h * (1.0 + scale) + shift


def split_heads(t, head_dim):
    b, l, _ = t.shape
    return t.reshape(b, l, -1, head_dim).transpose(0, 2, 1, 3)


def gla_log_gate(lr, w, b):
    return jax.nn.log_sigmoid((lr @ w + b).astype(jnp.float32)) / GLA_GATE_TAU


def gla_chunked(q, k, v, log_a, s0):
    bsz, nh, seq, dk = q.shape
    dv = v.shape[-1]
    n_chunks = seq // GLA_CHUNK
    cshape = (bsz, nh, n_chunks, GLA_CHUNK)
    cum = jnp.cumsum(log_a.reshape(cshape + (dk,)), axis=3)
    cum_last = cum[:, :, :, -1:, :]
    qc = q.reshape(cshape + (dk,)) * jnp.exp(cum)
    kc = k.reshape(cshape + (dk,))
    k_intra = kc * jnp.exp(-cum)
    k_state = kc * jnp.exp(cum_last - cum)
    vc = v.reshape(cshape + (dv,))
    tri = jnp.tril(jnp.ones((GLA_CHUNK, GLA_CHUNK), dtype=bool))
    scores = jnp.where(tri, jnp.einsum('bhncd,bhnsd->bhncs', qc, k_intra), 0.0)
    o_intra = jnp.einsum('bhncs,bhnsv->bhncv', scores, vc)
    decay = jnp.exp(cum_last[:, :, :, 0, :])

    def step(state, inp):
        q_n, k_n, v_n, d_n = inp
        o_n = jnp.einsum('bhcd,bhdv->bhcv', q_n, state)
        state = d_n[..., None] * state + jnp.einsum('bhcd,bhcv->bhdv', k_n, v_n)
        return state, o_n

    xs = tuple(jnp.moveaxis(t, 2, 0) for t in (qc, k_state, vc, decay))
    s_fin, o_inter = lax.scan(step, s0, xs)
    o = o_intra + jnp.moveaxis(o_inter, 0, 2)
    return o.reshape(bsz, nh, seq, dv), s_fin


def s5_discretize(lam_re, lam_im, log_step, b_re, b_im):
    lam_re = lam_re.astype(jnp.float32)
    lam_im = lam_im.astype(jnp.float32)
    step = jnp.exp(log_step.astype(jnp.float32))[:, None]
    mag = jnp.exp(lam_re * step)
    a_re = mag * jnp.cos(lam_im * step)
    a_im = mag * jnp.sin(lam_im * step)
    den = lam_re * lam_re + lam_im * lam_im
    f_re = ((a_re - 1.0) * lam_re + a_im * lam_im) / den
    f_im = (a_im * lam_re - (a_re - 1.0) * lam_im) / den
    b_re = b_re.astype(jnp.float32)
    b_im = b_im.astype(jnp.float32)
    bb_re = f_re[..., None] * b_re - f_im[..., None] * b_im
    bb_im = f_re[..., None] * b_im + f_im[..., None] * b_re
    return (a_re, a_im), (bb_re, bb_im)


def s5_scan(abar, bbar, u_g, x0, reverse):
    a_re, a_im = abar
    u32 = u_g.astype(jnp.float32)
    bu_re = jnp.einsum('blgh,gph->blgp', u32, bbar[0])
    bu_im = jnp.einsum('blgh,gph->blgp', u32, bbar[1])
    if x0 is not None:
        x0_re, x0_im = x0
        pos = -1 if reverse else 0
        bu_re = bu_re.at[:, pos].add(a_re * x0_re - a_im * x0_im)
        bu_im = bu_im.at[:, pos].add(a_re * x0_im + a_im * x0_re)
    seq = bu_re.shape[1]
    ar = jnp.broadcast_to(a_re, (1, seq) + a_re.shape)
    ai = jnp.broadcast_to(a_im, (1, seq) + a_im.shape)

    def combine(earlier, later):
        ar1, ai1, br1, bi1 = earlier
        ar2, ai2, br2, bi2 = later
        return (ar2 * ar1 - ai2 * ai1,
                ar2 * ai1 + ai2 * ar1,
                ar2 * br1 - ai2 * bi1 + br2,
                ar2 * bi1 + ai2 * br1 + bi2)

    _, _, x_re, x_im = lax.associative_scan(combine, (ar, ai, bu_re, bu_im), reverse=reverse, axis=1)
    return x_re, x_im


def s5_readout(xs, c_re, c_im):
    x_re, x_im = xs
    return (jnp.einsum('blgp,ghp->blgh', x_re, c_re.astype(jnp.float32))
            - jnp.einsum('blgp,ghp->blgh', x_im, c_im.astype(jnp.float32)))


def stream_mix(proj, lp, init_states, with_output):
    bsz = proj.shape[0]
    o1 = GLA_DK
    o2 = o1 + GLA_DK
    o3 = o2 + GLA_DV
    o4 = o3 + GLA_DV
    o5 = o4 + GLA_GATE_RANK
    o6 = o5 + GLA_GATE_RANK
    q, k, v, g_out, lr_f, lr_b, u = jnp.split(proj, [o1, o2, o3, o4, o5, o6], axis=-1)
    q = split_heads(q, GLA_DK_HEAD) * (GLA_DK_HEAD ** -0.5)
    k = split_heads(k, GLA_DK_HEAD)
    v = split_heads(v, GLA_DV_HEAD)
    la_f = split_heads(gla_log_gate(lr_f, lp['gla_wa_f'], lp['gla_ba_f']), GLA_DK_HEAD)
    la_b = split_heads(gla_log_gate(lr_b, lp['gla_wa_b'], lp['gla_ba_b']), GLA_DK_HEAD)
    if init_states is None:
        zero = jnp.zeros((bsz, GLA_HEADS, GLA_DK_HEAD, GLA_DV_HEAD), jnp.float32)
        gla_s0_f, gla_s0_b, s5_x0_f, s5_x0_b = zero, zero, None, None
    else:
        gla_s0_f, gla_s0_b, s5_x0_f, s5_x0_b = init_states
    o_f, gla_sf = gla_chunked(q, k, v, la_f, gla_s0_f)
    o_b, gla_sb = gla_chunked(jnp.flip(q, 2), jnp.flip(k, 2), jnp.flip(v, 2), jnp.flip(la_b, 2), gla_s0_b)
    u_g = u.reshape(bsz, -1, S5_GROUPS, S5_GROUP_CH)
    abar_f, bbar_f = s5_discretize(lp['s5_lam_re_f'], lp['s5_lam_im_f'], lp['s5_log_step_f'], lp['s5_b_re'], lp['s5_b_im'])
    abar_b, bbar_b = s5_discretize(lp['s5_lam_re_b'], lp['s5_lam_im_b'], lp['s5_log_step_b'], lp['s5_b_re'], lp['s5_b_im'])
    x_f = s5_scan(abar_f, bbar_f, u_g, s5_x0_f, False)
    x_b = s5_scan(abar_b, bbar_b, u_g, s5_x0_b, True)
    states = (gla_sf, gla_sb, (x_f[0][:, -1], x_f[1][:, -1]), (x_b[0][:, 0], x_b[1][:, 0]))
    if not with_output:
        return None, states
    o = rmsnorm(o_f + jnp.flip(o_b, 2), lp['gla_norm_g'])
    o = o.transpose(0, 2, 1, 3).reshape(bsz, -1, GLA_DV)
    gla_out = o * jax.nn.silu(g_out.astype(jnp.float32))
    y = s5_readout(x_f, lp['s5_c_re_f'], lp['s5_c_im_f']) + s5_readout(x_b, lp['s5_c_re_b'], lp['s5_c_im_b'])
    y = y.reshape(bsz, -1, D_S5) + lp['s5_d'] * u
    z = jax.nn.gelu(y)
    s5_out = z * jax.nn.sigmoid(z @ lp['s5_glu_w'] + lp['s5_glu_b'])
    return jnp.concatenate([gla_out, s5_out], axis=-1).astype(proj.dtype), states


def moe_ffn(h, router_w, router_b, w_gate, w_up, w_down, sh_gate, sh_up, sh_down):
    shp = h.shape
    t = h.reshape(-1, shp[-1])
    n_tok = t.shape[0]
    scores = jax.nn.sigmoid((t @ router_w).astype(jnp.float32))
    biased = scores + router_b.astype(jnp.float32)
    grp = biased.reshape(n_tok, N_EXPERT_GROUPS, N_EXPERTS // N_EXPERT_GROUPS)
    grp_score = lax.top_k(grp, 2)[0].sum(-1)
    _, top_grp = lax.top_k(grp_score, TOPK_GROUPS)
    grp_mask = jax.nn.one_hot(top_grp, N_EXPERT_GROUPS, dtype=jnp.float32).sum(1) > 0
    exp_mask = jnp.repeat(grp_mask, N_EXPERTS // N_EXPERT_GROUPS, axis=1)
    _, top_e = lax.top_k(jnp.where(exp_mask, biased, -jnp.inf), TOP_K)
    wts = jnp.take_along_axis(scores, top_e, axis=1)
    wts = wts / jnp.sum(wts, axis=-1, keepdims=True) * ROUTE_SCALE
    n_assign = n_tok * TOP_K
    flat_e = top_e.reshape(-1)
    order = jnp.argsort(flat_e)
    sorted_e = flat_e[order]
    counts = jnp.bincount(flat_e, length=N_EXPERTS)
    padded = (counts + MOE_BLOCK - 1) // MOE_BLOCK * MOE_BLOCK
    padded_end = jnp.cumsum(padded)
    padded_start = padded_end - padded
    start = jnp.cumsum(counts) - counts
    dest = padded_start[sorted_e] + jnp.arange(n_assign, dtype=jnp.int32) - start[sorted_e]
    n_rows = -(-(n_assign + N_EXPERTS * (MOE_BLOCK - 1)) // MOE_BLOCK) * MOE_BLOCK
    n_blocks = n_rows // MOE_BLOCK
    row_tok = jnp.zeros((n_rows,), jnp.int32).at[dest].set((order // TOP_K).astype(jnp.int32))
    row_w = jnp.zeros((n_rows,), jnp.float32).at[dest].set(wts.reshape(-1)[order])
    blk_e = jnp.minimum(jnp.searchsorted(padded_end, jnp.arange(n_blocks, dtype=jnp.int32) * MOE_BLOCK, side='right'), N_EXPERTS - 1)

    def body(acc, blk):
        tok, w, e = blk
        xb = t[tok]
        hid = jax.nn.silu(xb @ w_gate[e]) * (xb @ w_up[e])
        yb = (hid @ w_down[e]) * w[:, None]
        return acc.at[tok].add(yb.astype(acc.dtype)), None

    routed, _ = lax.scan(body, jnp.zeros_like(t),
                         (row_tok.reshape(n_blocks, MOE_BLOCK), row_w.reshape(n_blocks, MOE_BLOCK), blk_e))
    shared = (jax.nn.silu(t @ sh_gate) * (t @ sh_up)) @ sh_down
    return (routed + shared).reshape(shp)


def setup_inputs(seed: int = 0) -> dict:
    key = jax.random.key(seed)
    ks = iter(jax.random.split(key, 64))

    def nrm(shape, scale):
        return jax.random.normal(next(ks), shape, jnp.float32) * scale

    def gain(shape):
        return 1.0 + 0.02 * jax.random.normal(next(ks), shape, jnp.float32)

    lam_im_base = jnp.pi * jnp.arange(S5_STATE, dtype=jnp.float32)
    log_lo, log_hi = math.log(1e-3), math.log(1e-1)
    return {
        'x': nrm((BATCH, SEQ, D_MODEL), 1.0),
        'c': nrm((BATCH, D_MODEL), 1.0),
        'ctx': nrm((BATCH, CTX_LEN, D_MODEL), 1.0),
        'c_ctx': nrm((D_MODEL,), 1.0),
        'ada_w': nrm((DEPTH, D_MODEL, 6 * D_MODEL), 0.5 * D_MODEL ** -0.5),
        'ada_b': nrm((DEPTH, 6 * D_MODEL), 0.02),
        'norm1_g': gain((DEPTH, D_MODEL)),
        'norm2_g': gain((DEPTH, D_MODEL)),
        'w_in': nrm((DEPTH, D_MODEL, D_IN), D_MODEL ** -0.5),
        'gla_wa_f': nrm((DEPTH, GLA_GATE_RANK, GLA_DK), GLA_GATE_RANK ** -0.5),
        'gla_ba_f': nrm((DEPTH, GLA_DK), 0.1),
        'gla_wa_b': nrm((DEPTH, GLA_GATE_RANK, GLA_DK), GLA_GATE_RANK ** -0.5),
        'gla_ba_b': nrm((DEPTH, GLA_DK), 0.1),
        'gla_norm_g': gain((DEPTH, GLA_DV_HEAD)),
        's5_lam_re_f': -0.5 + nrm((DEPTH, S5_GROUPS, S5_STATE), 0.01),
        's5_lam_im_f': lam_im_base + nrm((DEPTH, S5_GROUPS, S5_STATE), 0.01),
        's5_log_step_f': jax.random.uniform(next(ks), (DEPTH, S5_GROUPS), jnp.float32, log_lo, log_hi),
        's5_lam_re_b': -0.5 + nrm((DEPTH, S5_GROUPS, S5_STATE), 0.01),
        's5_lam_im_b': lam_im_base + nrm((DEPTH, S5_GROUPS, S5_STATE), 0.01),
        's5_log_step_b': jax.random.uniform(next(ks), (DEPTH, S5_GROUPS), jnp.float32, log_lo, log_hi),
        's5_b_re': nrm((DEPTH, S5_GROUPS, S5_STATE, S5_GROUP_CH), (2.0 * S5_GROUP_CH) ** -0.5),
        's5_b_im': nrm((DEPTH, S5_GROUPS, S5_STATE, S5_GROUP_CH), (2.0 * S5_GROUP_CH) ** -0.5),
        's5_c_re_f': nrm((DEPTH, S5_GROUPS, S5_GROUP_CH, S5_STATE), (2.0 * S5_STATE) ** -0.5),
        's5_c_im_f': nrm((DEPTH, S5_GROUPS, S5_GROUP_CH, S5_STATE), (2.0 * S5_STATE) ** -0.5),
        's5_c_re_b': nrm((DEPTH, S5_GROUPS, S5_GROUP_CH, S5_STATE), (2.0 * S5_STATE) ** -0.5),
        's5_c_im_b': nrm((DEPTH, S5_GROUPS, S5_GROUP_CH, S5_STATE), (2.0 * S5_STATE) ** -0.5),
        's5_d': nrm((DEPTH, D_S5), 1.0),
        's5_glu_w': nrm((DEPTH, D_S5, D_S5), D_S5 ** -0.5),
        's5_glu_b': nrm((DEPTH, D_S5), 0.02),
        'w_out': nrm((DEPTH, D_MODEL, D_MODEL), D_MODEL ** -0.5),
        'router_w': nrm((DEPTH, D_MODEL, N_EXPERTS), D_MODEL ** -0.5),
        'router_b': nrm((DEPTH, N_EXPERTS), 0.01),
        'exp_w_gate': nrm((DEPTH, N_EXPERTS, D_MODEL, D_EXPERT), D_MODEL ** -0.5),
        'exp_w_up': nrm((DEPTH, N_EXPERTS, D_MODEL, D_EXPERT), D_MODEL ** -0.5),
        'exp_w_down': nrm((DEPTH, N_EXPERTS, D_EXPERT, D_MODEL), D_EXPERT ** -0.5),
        'sh_w_gate': nrm((DEPTH, D_MODEL, D_SHARED), D_MODEL ** -0.5),
        'sh_w_up': nrm((DEPTH, D_MODEL, D_SHARED), D_MODEL ** -0.5),
        'sh_w_down': nrm((DEPTH, D_SHARED, D_MODEL), D_SHARED ** -0.5),
        'final_norm_g': gain((D_MODEL,)),
    }


def reference(x, c, ctx, c_ctx, ada_w, ada_b, norm1_g, norm2_g, w_in,
              gla_wa_f, gla_ba_f, gla_wa_b, gla_ba_b, gla_norm_g,
              s5_lam_re_f, s5_lam_im_f, s5_log_step_f, s5_lam_re_b, s5_lam_im_b, s5_log_step_b,
              s5_b_re, s5_b_im, s5_c_re_f, s5_c_im_f, s5_c_re_b, s5_c_im_b, s5_d, s5_glu_w, s5_glu_b,
              w_out, router_w, router_b, exp_w_gate, exp_w_up, exp_w_down,
              sh_w_gate, sh_w_up, sh_w_down, final_norm_g):
    h_ctx = ctx
    for i in range(DEPTH):
        lp = {
            'gla_wa_f': gla_wa_f[i], 'gla_ba_f': gla_ba_f[i],
            'gla_wa_b': gla_wa_b[i], 'gla_ba_b': gla_ba_b[i], 'gla_norm_g': gla_norm_g[i],
            's5_lam_re_f': s5_lam_re_f[i], 's5_lam_im_f': s5_lam_im_f[i], 's5_log_step_f': s5_log_step_f[i],
            's5_lam_re_b': s5_lam_re_b[i], 's5_lam_im_b': s5_lam_im_b[i], 's5_log_step_b': s5_log_step_b[i],
            's5_b_re': s5_b_re[i], 's5_b_im': s5_b_im[i],
            's5_c_re_f': s5_c_re_f[i], 's5_c_im_f': s5_c_im_f[i],
            's5_c_re_b': s5_c_re_b[i], 's5_c_im_b': s5_c_im_b[i],
            's5_d': s5_d[i], 's5_glu_w': s5_glu_w[i], 's5_glu_b': s5_glu_b[i],
        }
        last = i + 1 == DEPTH
        sh1, sc1, g1, sh2, sc2, g2 = adaln(c, ada_w[i], ada_b[i])
        csh1, csc1, cg1, csh2, csc2, cg2 = adaln(c_ctx, ada_w[i], ada_b[i])
        hc = modulate(rmsnorm(h_ctx, norm1_g[i]), csh1, csc1)
        mix_c, ctx_states = stream_mix(hc @ w_in[i], lp, None, not last)
        hx = modulate(rmsnorm(x, norm1_g[i]), sh1, sc1)
        mix_x, _ = stream_mix(hx @ w_in[i], lp, ctx_states, True)
        x = x + g1 * (mix_x @ w_out[i])
        hx2 = modulate(rmsnorm(x, norm2_g[i]), sh2, sc2)
        x = x + g2 * moe_ffn(hx2, router_w[i], router_b[i], exp_w_gate[i], exp_w_up[i], exp_w_down[i],
                             sh_w_gate[i], sh_w_up[i], sh_w_down[i])
        if not last:
            h_ctx = h_ctx + cg1 * (mix_c @ w_out[i])
            hc2 = modulate(rmsnorm(h_ctx, norm2_g[i]), csh2, csc2)
            h_ctx = h_ctx + cg2 * moe_ffn(hc2, router_w[i], router_b[i], exp_w_gate[i], exp_w_up[i], exp_w_down[i],
                                          sh_w_gate[i], sh_w_up[i], sh_w_down[i])
    return rmsnorm(x, final_norm_g)
```

```python
import functools

import jax
import jax.numpy as jnp
from jax import lax
from jax.experimental import pallas as pl
from jax.experimental.pallas import tpu as pltpu

F32 = jnp.float32
BF16 = jnp.bfloat16
I32 = jnp.int32

D_MODEL = 1024
GLA_HEADS = 4
GLA_DK_HEAD = 64
GLA_DV_HEAD = 128
GLA_DK = 256
GLA_DV = 512
GLA_GATE_RANK = 16
GLA_GATE_TAU = 16.0
GLA_CHUNK = 64
D_S5 = 512
S5_GROUP_CH = 16
S5_GROUPS = 32
S5_STATE = 64
S5_CHUNK = 16
S5_VEC = S5_CHUNK * S5_GROUP_CH
N_EXPERTS = 256
TOP_K = 8
N_EXPERT_GROUPS = 8
TOPK_GROUPS = 4
D_EXPERT = 256
D_SHARED = 256
ROUTE_SCALE = 2.5
EPS = 1e-6

LANES = 128
SUBLANES = 8
ROW_TILES = D_MODEL // LANES
EXPERT_BLOCK = 256
VMEM_LIMIT = 56 * 1024 * 1024


def _cparams(sem):
    return pltpu.CompilerParams(dimension_semantics=sem, vmem_limit_bytes=VMEM_LIMIT)


def _dot(a, b):
    return jnp.dot(a, b, preferred_element_type=F32)


def _dot_nt(a, b):
    return lax.dot_general(a, b, (((1,), (1,)), ((), ())), preferred_element_type=F32)


def _dot_tn(a, b):
    return lax.dot_general(a, b, (((0,), (0,)), ((), ())), preferred_element_type=F32)


def _split2(x):
    hi = x.astype(BF16)
    lo = (x - hi.astype(F32)).astype(BF16)
    return hi, lo


def _dot3(a, b_hi, b_lo):
    a_hi, a_lo = _split2(a)
    return _dot(a_hi, b_hi) + (_dot(a_hi, b_lo) + _dot(a_lo, b_hi))


def _silu(x):
    return x * jax.nn.sigmoid(x)


def _rms(x, g):
    return x * lax.rsqrt(jnp.mean(x * x, axis=-1, keepdims=True) + EPS) * g


def _adaln_kernel(c_ref, w_ref, b_ref, o_ref):
    s = _silu(c_ref[...])
    w_hi, w_lo = _split2(w_ref[...])
    o_ref[...] = _dot3(s, w_hi, w_lo) + b_ref[...]


def _adaln(cs, w, b):
    rows, n = cs.shape[0], w.shape[1]
    tn = 1024
    return pl.pallas_call(
        _adaln_kernel,
        grid=(n // tn,),
        in_specs=[pl.BlockSpec((rows, D_MODEL), lambda j: (0, 0)),
                  pl.BlockSpec((D_MODEL, tn), lambda j: (0, j)),
                  pl.BlockSpec((1, tn), lambda j: (0, j))],
        out_specs=pl.BlockSpec((rows, tn), lambda j: (0, j)),
        out_shape=jax.ShapeDtypeStruct((rows, n), F32),
        compiler_params=_cparams(("arbitrary",)),
        name="adaln",
    )(cs, w, b)


def _proj_kernel(x_ref, sh_ref, sc_ref, g_ref, wm_ref, wl_ref, wa_ref, ba_ref,
                 q_ref, k_ref, v_ref, go_ref, u_ref, laf_ref, lab_ref):
    h = _rms(x_ref[0], g_ref[...]) * (1.0 + sc_ref[0]) + sh_ref[0]
    hb = h.astype(BF16)
    q_ref[0] = _dot(hb, wm_ref[:, 0:256]) * (GLA_DK_HEAD ** -0.5)
    k_ref[0] = _dot(hb, wm_ref[:, 256:512])
    v_ref[0] = _dot(hb, wm_ref[:, 512:1024])
    go_ref[0] = _dot(hb, wm_ref[:, 1024:1536])
    u_ref[0] = _dot(hb, wm_ref[:, 1536:2048])
    lr = _dot(hb, wl_ref[...])
    pre = _dot(lr.astype(BF16), wa_ref[...]) + ba_ref[...]
    la = (jnp.minimum(pre, 0.0) - jnp.log1p(jnp.exp(-jnp.abs(pre)))) * (1.0 / GLA_GATE_TAU)
    laf_ref[0] = la[:, 0:GLA_DK]
    lab_ref[0] = la[:, GLA_DK:2 * GLA_DK]


def _proj(x, shift, scale, gain, wm, wl, wa, ba):
    b, l, _ = x.shape
    tm = min(512, l)
    row = lambda bi, i: (bi, i, 0)
    mod = lambda bi, i: (bi, 0, 0)
    full = lambda bi, i: (0, 0)
    widths = (GLA_DK, GLA_DK, GLA_DV, GLA_DV, D_S5, GLA_DK, GLA_DK)
    return pl.pallas_call(
        _proj_kernel,
        grid=(b, l // tm),
        in_specs=[pl.BlockSpec((1, tm, D_MODEL), row),
                  pl.BlockSpec((1, 1, D_MODEL), mod),
                  pl.BlockSpec((1, 1, D_MODEL), mod),
                  pl.BlockSpec((1, D_MODEL), full),
                  pl.BlockSpec(wm.shape, full),
                  pl.BlockSpec(wl.shape, full),
                  pl.BlockSpec(wa.shape, full),
                  pl.BlockSpec(ba.shape, full)],
        out_specs=[pl.BlockSpec((1, tm, w), row) for w in widths],
        out_shape=[jax.ShapeDtypeStruct((b, l, w), F32) for w in widths],
        compiler_params=_cparams(("arbitrary", "arbitrary")),
        name="proj",
    )(x, shift, scale, gain, wm, wl, wa, ba)


def _gla_kernel(*refs, n_chunks, with_output):
    if with_output:
        q_ref, k_ref, v_ref, laf_ref, lab_ref, s0f_ref, s0b_ref, o_ref, sf_ref, sb_ref, st_ref = refs
    else:
        q_ref, k_ref, v_ref, laf_ref, lab_ref, s0f_ref, s0b_ref, sf_ref, sb_ref, st_ref = refs
        o_ref = None
    c = GLA_CHUNK
    row = lax.broadcasted_iota(I32, (c, c), 0)
    col = lax.broadcasted_iota(I32, (c, c), 1)
    lane = lax.broadcasted_iota(I32, (c, LANES), 1)
    for direction in (0, 1):
        la_ref = laf_ref if direction == 0 else lab_ref
        s0_ref = s0f_ref if direction == 0 else s0b_ref
        sfin_ref = sf_ref if direction == 0 else sb_ref
        tri = (row >= col) if direction == 0 else (row <= col)
        trib = jnp.where(tri, 1.0, 0.0).astype(BF16)
        st_ref[...] = s0_ref[0]

        def body(ci, carry, direction=direction, la_ref=la_ref, tri=tri, trib=trib):
            idx = ci if direction == 0 else n_chunks - 1 - ci
            r0 = pl.multiple_of(idx * c, c)
            q = q_ref[0, pl.ds(r0, c), :]
            k = k_ref[0, pl.ds(r0, c), :]
            v = v_ref[0, pl.ds(r0, c), :]
            la_hi, la_lo = _split2(la_ref[0, pl.ds(r0, c), :])
            cum = _dot(trib, la_hi) + _dot(trib, la_lo)
            tot = cum[c - 1:c, :] if direction == 0 else cum[0:1, :]
            qd = q * jnp.exp(cum)
            ki = k * jnp.exp(-cum)
            ks = k * jnp.exp(tot - cum)
            dec = jnp.exp(tot)
            for h in range(GLA_HEADS):
                pair = slice(LANES * (h // 2), LANES * (h // 2) + LANES)
                own = (lane >= GLA_DK_HEAD * (h % 2)) & (lane < GLA_DK_HEAD * (h % 2) + GLA_DK_HEAD)
                vb = v[:, GLA_DV_HEAD * h:GLA_DV_HEAD * (h + 1)].astype(BF16)
                st = st_ref[h]
                if with_output:
                    qb = qd[:, pair].astype(BF16)
                    kib = jnp.where(own, ki[:, pair], 0.0).astype(BF16)
                    sc = jnp.where(tri, _dot_nt(qb, kib), 0.0)
                    o = _dot(sc.astype(BF16), vb) + _dot_nt(qb, st.astype(BF16))
                    osl = (0, pl.ds(r0, c), slice(GLA_DV_HEAD * h, GLA_DV_HEAD * (h + 1)))
                    if direction == 0:
                        o_ref[osl] = o
                    else:
                        o_ref[osl] = o_ref[osl] + o
                ksb = jnp.where(own, ks[:, pair], 0.0).astype(BF16)
                st_ref[h] = st * dec[:, pair] + _dot_tn(vb, ksb)
            return carry

        lax.fori_loop(0, n_chunks, body, 0)
        sfin_ref[0] = st_ref[...]


def _gla(q, k, v, laf, lab, s0f, s0b, with_output):
    b, l, _ = q.shape
    n_chunks = l // GLA_CHUNK
    seq = lambda bi: (bi, 0, 0)
    st = lambda bi: (bi, 0, 0, 0)
    st_shape = (b, GLA_HEADS, GLA_DV_HEAD, LANES)
    st_spec = pl.BlockSpec((1, GLA_HEADS, GLA_DV_HEAD, LANES), st)
    out_specs = [st_spec, st_spec]
    out_shape = [jax.ShapeDtypeStruct(st_shape, F32)] * 2
    if with_output:
        out_specs = [pl.BlockSpec((1, l, GLA_DV), seq)] + out_specs
        out_shape = [jax.ShapeDtypeStruct((b, l, GLA_DV), F32)] + out_shape
    return pl.pallas_call(
        functools.partial(_gla_kernel, n_chunks=n_chunks, with_output=with_output),
        grid=(b,),
        in_specs=[pl.BlockSpec((1, l, GLA_DK), seq),
                  pl.BlockSpec((1, l, GLA_DK), seq),
                  pl.BlockSpec((1, l, GLA_DV), seq),
                  pl.BlockSpec((1, l, GLA_DK), seq),
                  pl.BlockSpec((1, l, GLA_DK), seq),
                  st_spec, st_spec],
        out_specs=out_specs,
        out_shape=out_shape,
        scratch_shapes=[pltpu.VMEM((GLA_HEADS, GLA_DV_HEAD, LANES), F32)],
        compiler_params=_cparams(("arbitrary",)),
        name="gla_out" if with_output else "gla_ctx",
    )(q, k, v, laf, lab, s0f, s0b)


def _s5gen_kernel(pc_ref, pr_ref, btr_ref, bti_ref, ctrf_ref, ctif_ref, ctrb_ref, ctib_ref,
                  m_ref, wt_ref, v_ref, ab_ref):
    pc = pc_ref[0]
    blk = lax.shift_right_logical(lax.broadcasted_iota(I32, (1, S5_VEC), 1), 4).astype(F32)
    lane = lax.broadcasted_iota(I32, (S5_GROUP_CH, S5_VEC), 1)
    n = float(S5_CHUNK)

    def cmul(ar, ai, br, bi):
        return ar * br - ai * bi, ar * bi + ai * br

    kcat = []
    for d in (0, 1):
        lre, lim, ls = pc[:, 3 * d:3 * d + 1], pc[:, 3 * d + 1:3 * d + 2], pc[:, 3 * d + 2:3 * d + 3]
        ctr = (ctrf_ref if d == 0 else ctrb_ref)[0]
        cti = (ctif_ref if d == 0 else ctib_ref)[0]
        step = jnp.exp(ls)
        mag = jnp.exp(lre * step)
        a_re = mag * jnp.cos(lim * step)
        a_im = mag * jnp.sin(lim * step)
        den = lre * lre + lim * lim
        f_re = ((a_re - 1.0) * lre + a_im * lim) / den
        f_im = (a_im * lre - (a_re - 1.0) * lim) / den
        bb_re, bb_im = cmul(f_re, f_im, btr_ref[0], bti_ref[0])

        def powers(e, lre=lre, lim=lim, step=step):
            m = jnp.exp(lre * step * e)
            ang = lim * step * e
            return m * jnp.cos(ang), m * jnp.sin(ang)

        w_re, w_im = cmul(*powers((n - 1.0 - blk) if d == 0 else blk), bb_re, bb_im)
        wt_ref[0, S5_STATE * d:S5_STATE * (d + 1), :] = w_re
        wt_ref[0, 2 * S5_STATE + S5_STATE * d:2 * S5_STATE + S5_STATE * (d + 1), :] = w_im
        c_re, c_im = cmul(*powers((blk + 1.0) if d == 0 else (n - blk)), ctr, cti)
        v_ref[0, S5_STATE * d:S5_STATE * (d + 1), :] = c_re
        v_ref[0, 2 * S5_STATE + S5_STATE * d:2 * S5_STATE + S5_STATE * (d + 1), :] = -c_im
        e_re, e_im = cmul(*powers(blk if d == 0 else (n - 1.0 - blk)), ctr, cti)
        b16r_hi, b16r_lo = _split2(bb_re[:, 0:S5_GROUP_CH])
        b16i_hi, b16i_lo = _split2(bb_im[:, 0:S5_GROUP_CH])
        er_hi, er_lo = _split2(e_re)
        ei_hi, ei_lo = _split2(e_im)
        kr = _dot_tn(b16r_hi, er_hi) + (_dot_tn(b16r_hi, er_lo) + _dot_tn(b16r_lo, er_hi))
        ki = _dot_tn(b16i_hi, ei_hi) + (_dot_tn(b16i_hi, ei_lo) + _dot_tn(b16i_lo, ei_hi))
        kcat.append(kr - ki)

    for s in range(S5_CHUNK):
        sh_f = S5_GROUP_CH * s
        fwd = kcat[0] if s == 0 else pltpu.roll(kcat[0], sh_f, 1)
        fwd = jnp.where(lane >= sh_f, fwd, 0.0)
        sh_b = S5_VEC - S5_GROUP_CH * (S5_CHUNK - 1 - s)
        bwd = kcat[1] if sh_b == S5_VEC else pltpu.roll(kcat[1], sh_b, 1)
        bwd = jnp.where(lane < S5_GROUP_CH * (s + 1), bwd, 0.0)
        m_ref[0, S5_GROUP_CH * s:S5_GROUP_CH * (s + 1), :] = fwd + bwd

    pr = pr_ref[0]
    for d in (0, 1):
        lre, lim, ls = pr[3 * d:3 * d + 1, :], pr[3 * d + 1:3 * d + 2, :], pr[3 * d + 2:3 * d + 3, :]
        stp = jnp.exp(ls) * n
        mg = jnp.exp(lre * stp)
        ab_ref[0, d:d + 1, :] = mg * jnp.cos(lim * stp)
        ab_ref[0, 2 + d:3 + d, :] = mg * jnp.sin(lim * stp)


def _s5gen(pc, pr, btr, bti, ctrf, ctif, ctrb, ctib):
    g = pc.shape[0]
    blk3 = lambda shape: pl.BlockSpec((1,) + shape, lambda i: (i, 0, 0))
    big = (S5_STATE, S5_VEC)
    sq = (S5_VEC, S5_VEC)
    return pl.pallas_call(
        _s5gen_kernel,
        grid=(g,),
        in_specs=[blk3((S5_STATE, 8)), blk3((8, S5_STATE))] + [blk3(big)] * 6,
        out_specs=[blk3(sq), blk3(sq), blk3(sq), blk3((4, S5_STATE))],
        out_shape=[jax.ShapeDtypeStruct((g,) + sq, F32)] * 3 + [jax.ShapeDtypeStruct((g, 4, S5_STATE), F32)],
        compiler_params=_cparams(("arbitrary",)),
        name="s5gen",
    )(pc, pr, btr, bti, ctrf, ctif, ctrb, ctib)


def _s5_kernel(*refs, n_chunks, nb, with_output):
    if with_output:
        u_ref, m_ref, wt_ref, v_ref, ab_ref, x0_ref, y_ref, xf_ref, z_ref, cin_ref = refs
    else:
        u_ref, wt_ref, ab_ref, x0_ref, xf_ref, z_ref = refs
    ub = u_ref[0].astype(BF16)
    z_ref[...] = _dot_nt(ub, wt_ref[0].astype(BF16))
    ab = ab_ref[0]
    ar, ai = ab[:, 0:LANES], ab[:, LANES:2 * LANES]
    is_f = lax.broadcasted_iota(I32, (nb, LANES), 1) < S5_STATE
    x0 = x0_ref[0]

    def body(i, carry):
        xr, xi = carry
        rf = pl.multiple_of(i * nb, nb)
        rb = pl.multiple_of((n_chunks - 1 - i) * nb, nb)
        if with_output:
            cin_ref[pl.ds(rf, nb), 0:64] = xr[:, 0:64]
            cin_ref[pl.ds(rf, nb), 128:192] = xi[:, 0:64]
            cin_ref[pl.ds(rb, nb), 64:128] = xr[:, 64:128]
            cin_ref[pl.ds(rb, nb), 192:256] = xi[:, 64:128]
        zf = z_ref[pl.ds(rf, nb), :]
        zb = z_ref[pl.ds(rb, nb), :]
        zr = jnp.where(is_f, zf[:, 0:LANES], zb[:, 0:LANES])
        zi = jnp.where(is_f, zf[:, LANES:2 * LANES], zb[:, LANES:2 * LANES])
        return ar * xr - ai * xi + zr, ar * xi + ai * xr + zi

    xr, xi = lax.fori_loop(0, n_chunks, body, (x0[:, 0:LANES], x0[:, LANES:2 * LANES]))
    xf_ref[0, :, 0:LANES] = xr
    xf_ref[0, :, LANES:2 * LANES] = xi
    if with_output:
        y_ref[0] = _dot(ub, m_ref[0].astype(BF16)) + _dot(cin_ref[...].astype(BF16), v_ref[0].astype(BF16))


def _s5(uvec, m, wt, v, ab, x0, nb, with_output):
    g, rows, _ = uvec.shape
    n_chunks = rows // nb
    blk3 = lambda shape: pl.BlockSpec((1,) + shape, lambda i: (i, 0, 0))
    sq = (S5_VEC, S5_VEC)
    st = (nb, S5_VEC)
    if with_output:
        args = (uvec, m, wt, v, ab, x0)
        in_specs = [blk3((rows, S5_VEC)), blk3(sq), blk3(sq), blk3(sq), blk3((1, S5_VEC)), blk3(st)]
        out_specs = [blk3((rows, S5_VEC)), blk3(st)]
        out_shape = [jax.ShapeDtypeStruct((g, rows, S5_VEC), F32), jax.ShapeDtypeStruct((g,) + st, F32)]
        scratch = [pltpu.VMEM((rows, S5_VEC), F32), pltpu.VMEM((rows, S5_VEC), F32)]
    else:
        args = (uvec, wt, ab, x0)
        in_specs = [blk3((rows, S5_VEC)), blk3(sq), blk3((1, S5_VEC)), blk3(st)]
        out_specs = [blk3(st)]
        out_shape = [jax.ShapeDtypeStruct((g,) + st, F32)]
        scratch = [pltpu.VMEM((rows, S5_VEC), F32)]
    return pl.pallas_call(
        functools.partial(_s5_kernel, n_chunks=n_chunks, nb=nb, with_output=with_output),
        grid=(g,),
        in_specs=in_specs,
        out_specs=out_specs,
        out_shape=out_shape,
        scratch_shapes=scratch,
        compiler_params=_cparams(("arbitrary",)),
        name="s5_out" if with_output else "s5_ctx",
    )(*args)


def _to_chunk_vectors(u):
    b, l, _ = u.shape
    n = l // S5_CHUNK
    t = u.reshape(b, n, S5_CHUNK, S5_GROUPS, S5_GROUP_CH).transpose(3, 1, 0, 2, 4)
    return t.reshape(S5_GROUPS, n * b, S5_VEC)


def _from_chunk_vectors(y, b):
    g, rows, _ = y.shape
    n = rows // b
    t = y.reshape(g, n, b, S5_CHUNK, S5_GROUP_CH).transpose(2, 1, 3, 0, 4)
    return t.reshape(b, n * S5_CHUNK, D_S5)


def _post_kernel(x_ref, o_ref, go_ref, y_ref, u_ref, g1_ref, sh2_ref, sc2_ref, g2_ref,
                 gn_ref, d_ref, gw_ref, gb_ref, wo_ref, n2_ref, rwh_ref, rwl_ref, sgu_ref, sd_ref,
                 base_ref, hrow_ref, lg_ref, *, tm):
    o = o_ref[0]
    gn = gn_ref[...]
    heads = [_rms(o[:, GLA_DV_HEAD * h:GLA_DV_HEAD * (h + 1)], gn) for h in range(GLA_HEADS)]
    gla_out = jnp.concatenate(heads, axis=1) * _silu(go_ref[0])
    yy = y_ref[0] + d_ref[...] * u_ref[0]
    z = 0.5 * yy * (1.0 + jnp.tanh(0.7978845608028654 * (yy + 0.044715 * (yy * yy * yy))))
    s5_out = z * jax.nn.sigmoid(_dot(z.astype(BF16), gw_ref[...]) + gb_ref[...])
    mix = jnp.concatenate([gla_out, s5_out], axis=1).astype(BF16)
    x1 = x_ref[0] + g1_ref[0] * _dot(mix, wo_ref[...])
    h2 = _rms(x1, n2_ref[...]) * (1.0 + sc2_ref[0]) + sh2_ref[0]
    lg_ref[0] = _dot3(h2, rwh_ref[...], rwl_ref[...])
    hb = h2.astype(BF16)
    gu = _dot(hb, sgu_ref[...])
    hid = _silu(gu[:, 0:D_SHARED]) * gu[:, D_SHARED:2 * D_SHARED]
    base_ref[0] = x1 + g2_ref[0] * _dot(hid.astype(BF16), sd_ref[...])
    for s in range(ROW_TILES):
        hrow_ref[pl.ds(s, tm, stride=ROW_TILES), :] = h2[:, LANES * s:LANES * (s + 1)]


def _post(x, o, go, y, u, g1, sh2, sc2, g2, gn, d, gw, gb, wo, n2, rwh, rwl, sgu, sd):
    b, l, _ = x.shape
    tm = 256
    nt = l // tm
    row = lambda bi, i: (bi, i, 0)
    mod = lambda bi, i: (bi, 0, 0)
    full = lambda bi, i: (0, 0)
    ws = (gn, d, gw, gb, wo, n2, rwh, rwl, sgu, sd)
    return pl.pallas_call(
        functools.partial(_post_kernel, tm=tm),
        grid=(b, nt),
        in_specs=[pl.BlockSpec((1, tm, D_MODEL), row)]
                 + [pl.BlockSpec((1, tm, 512), row)] * 4
                 + [pl.BlockSpec((1, 1, D_MODEL), mod)] * 4
                 + [pl.BlockSpec(w.shape, full) for w in ws],
        out_specs=[pl.BlockSpec((1, tm, D_MODEL), row),
                   pl.BlockSpec((tm * ROW_TILES, LANES), lambda bi, i: (bi * nt + i, 0)),
                   pl.BlockSpec((1, tm, N_EXPERTS), row)],
        out_shape=[jax.ShapeDtypeStruct((b, l, D_MODEL), F32),
                   jax.ShapeDtypeStruct((b * l * ROW_TILES, LANES), F32),
                   jax.ShapeDtypeStruct((b, l, N_EXPERTS), F32)],
        compiler_params=_cparams(("arbitrary", "arbitrary")),
        name="post",
    )(x, o, go, y, u, g1, sh2, sc2, g2, *ws)


def _route_kernel(lg_ref, rb_ref, e_ref, w_ref, r_ref, cnt_ref, run_ref, *, tm):
    @pl.when(pl.program_id(0) == 0)
    def _():
        run_ref[...] = jnp.zeros_like(run_ref)

    neg = -jnp.inf
    s = jax.nn.sigmoid(lg_ref[...])
    biased = s + rb_ref[...]
    lane = lax.broadcasted_iota(I32, (tm, N_EXPERTS), 1)
    grp = lax.shift_right_logical(lane, 5)

    def first_max(m):
        mx = jnp.max(m, axis=-1, keepdims=True)
        ix = jnp.min(jnp.where(m == mx, lane, N_EXPERTS), axis=-1, keepdims=True)
        return mx, ix

    gs = []
    for g in range(N_EXPERT_GROUPS):
        m = jnp.where(grp == g, biased, neg)
        m1, i1 = first_max(m)
        m2 = jnp.max(jnp.where(lane == i1, neg, m), axis=-1, keepdims=True)
        gs.append(m1 + m2)
    emask = jnp.zeros((tm, N_EXPERTS), F32)
    for g in range(N_EXPERT_GROUPS):
        ahead = jnp.zeros((tm, 1), F32)
        for j in range(N_EXPERT_GROUPS):
            if j < g:
                ahead = ahead + jnp.where(gs[j] >= gs[g], 1.0, 0.0)
            elif j > g:
                ahead = ahead + jnp.where(gs[j] > gs[g], 1.0, 0.0)
        emask = emask + jnp.where(grp == g, jnp.where(ahead < float(TOPK_GROUPS), 1.0, 0.0), 0.0)
    masked = jnp.where(emask > 0.5, biased, neg)

    onehot = jnp.zeros((tm, N_EXPERTS), F32)
    ids, ws = [], []
    for _ in range(TOP_K):
        _, ik = first_max(masked)
        hit = lane == ik
        ids.append(ik)
        ws.append(jnp.sum(jnp.where(hit, s, 0.0), axis=-1, keepdims=True))
        onehot = onehot + jnp.where(hit, 1.0, 0.0)
        masked = jnp.where(hit, neg, masked)
    wsum = ws[0]
    for k in range(1, TOP_K):
        wsum = wsum + ws[k]

    rr = lax.broadcasted_iota(I32, (tm, tm), 0)
    cc = lax.broadcasted_iota(I32, (tm, tm), 1)
    before = jnp.where(rr > cc, 1.0, 0.0).astype(BF16)
    pos = _dot(before, onehot.astype(BF16)) + run_ref[...]
    lane8 = lax.broadcasted_iota(I32, (tm, TOP_K), 1)
    e_out = jnp.zeros((tm, TOP_K), I32)
    w_out = jnp.zeros((tm, TOP_K), F32)
    r_out = jnp.zeros((tm, TOP_K), F32)
    for k in range(TOP_K):
        rk = jnp.sum(jnp.where(lane == ids[k], pos, 0.0), axis=-1, keepdims=True)
        e_out = jnp.where(lane8 == k, ids[k], e_out)
        w_out = jnp.where(lane8 == k, ws[k] / wsum * ROUTE_SCALE, w_out)
        r_out = jnp.where(lane8 == k, rk, r_out)
    e_ref[...] = e_out
    w_ref[...] = w_out
    r_ref[...] = r_out.astype(I32)
    run = run_ref[...] + jnp.sum(onehot, axis=0, keepdims=True)
    run_ref[...] = run
    cnt_ref[...] = run.astype(I32)


def _route(logits, rb):
    t = logits.shape[0]
    tm = min(512, t)
    row = lambda i: (i, 0)
    return pl.pallas_call(
        functools.partial(_route_kernel, tm=tm),
        grid=(t // tm,),
        in_specs=[pl.BlockSpec((tm, N_EXPERTS), row), pl.BlockSpec((1, N_EXPERTS), lambda i: (0, 0))],
        out_specs=[pl.BlockSpec((tm, TOP_K), row)] * 3 + [pl.BlockSpec((1, N_EXPERTS), lambda i: (0, 0))],
        out_shape=[jax.ShapeDtypeStruct((t, TOP_K), I32), jax.ShapeDtypeStruct((t, TOP_K), F32),
                   jax.ShapeDtypeStruct((t, TOP_K), I32), jax.ShapeDtypeStruct((1, N_EXPERTS), I32)],
        scratch_shapes=[pltpu.VMEM((1, N_EXPERTS), F32)],
        compiler_params=_cparams(("arbitrary",)),
        name="route",
    )(logits, rb)


def _n_blocks_max(n_assign):
    return -(-(n_assign + N_EXPERTS * (EXPERT_BLOCK - 1)) // EXPERT_BLOCK)


def _plan_kernel(cnt_ref, ps_ref, blk_ref, nv_ref, *, nbp):
    cnt = cnt_ref[...]
    nb = lax.shift_right_logical(cnt + (EXPERT_BLOCK - 1), 8).astype(F32)
    nb8 = jnp.broadcast_to(nb, (SUBLANES, N_EXPERTS))
    nb_hi, nb_lo = _split2(nb8)
    ii = lax.broadcasted_iota(I32, (N_EXPERTS, N_EXPERTS), 0)
    jj = lax.broadcasted_iota(I32, (N_EXPERTS, N_EXPERTS), 1)
    upto = jnp.where(ii <= jj, 1.0, 0.0).astype(BF16)
    cum = (_dot(nb_hi, upto) + _dot(nb_lo, upto))[0:1, :]
    ps_ref[...] = ((cum - nb) * float(EXPERT_BLOCK)).astype(I32)
    bi = lax.broadcasted_iota(I32, (nbp, N_EXPERTS), 0).astype(F32)
    owner = jnp.sum(jnp.where(cum <= bi, 1.0, 0.0), axis=-1, keepdims=True)
    blk_ref[...] = jnp.minimum(owner, float(N_EXPERTS - 1)).astype(I32)
    nv_ref[...] = cum[:, N_EXPERTS - 1:N_EXPERTS].astype(I32)


def _plan(cnt, nbp):
    return pl.pallas_call(
        functools.partial(_plan_kernel, nbp=nbp),
        out_shape=[jax.ShapeDtypeStruct((1, N_EXPERTS), I32), jax.ShapeDtypeStruct((nbp, 1), I32),
                   jax.ShapeDtypeStruct((1, 1), I32)],
        name="plan",
    )(cnt)


def _row_copy(src, src_row, dst, dst_row, sem):
    return pltpu.make_async_copy(src.at[pl.ds(pl.multiple_of(src_row * ROW_TILES, ROW_TILES), ROW_TILES)],
                                 dst.at[pl.ds(pl.multiple_of(dst_row * ROW_TILES, ROW_TILES), ROW_TILES)], sem)


def _dispatch_kernel(e_ref, r_ref, ps_ref, cnt_ref, h_ref, xs_ref, zero_ref, sem, zsem, *, tm):
    step = pl.program_id(0)
    pad_sizes = [1 << p for p in range(EXPERT_BLOCK.bit_length() - 2, -1, -1)]

    def pad_copy(start_row, size):
        return pltpu.make_async_copy(
            zero_ref.at[pl.ds(0, size * ROW_TILES)],
            xs_ref.at[pl.ds(pl.multiple_of(start_row * ROW_TILES, ROW_TILES), size * ROW_TILES)], zsem)

    @pl.when(step == 0)
    def _():
        zero_ref[...] = jnp.zeros_like(zero_ref)
        for wait in (False, True):
            def pad_body(e, carry, wait=wait):
                cnt = cnt_ref[e]
                pad = (-cnt) & (EXPERT_BLOCK - 1)
                row = ps_ref[e] + cnt
                for size in pad_sizes:
                    @pl.when((pad & size) != 0)
                    def _(row=row, size=size):
                        cp = pad_copy(row, size)
                        cp.wait() if wait else cp.start()
                    row = row + (pad & size)
                return carry
            lax.fori_loop(0, N_EXPERTS, pad_body, 0)

    def copy(j):
        dst = ps_ref[e_ref[j]] + r_ref[j]
        return _row_copy(h_ref, step * tm + j // TOP_K, xs_ref, dst, sem)

    def start_body(j, carry):
        copy(j).start()
        return carry

    def wait_body(j, carry):
        copy(j).wait()
        return carry

    lax.fori_loop(0, tm * TOP_K, start_body, 0, unroll=8)
    lax.fori_loop(0, tm * TOP_K, wait_body, 0, unroll=8)


def _dispatch(e_flat, r_flat, pstart, cnt, hrows, n_rows):
    t = e_flat.shape[0] // TOP_K
    tm = min(512, t)
    smem_blk = pl.BlockSpec((tm * TOP_K,), lambda i: (i,), memory_space=pltpu.SMEM)
    smem_all = pl.BlockSpec((N_EXPERTS,), lambda i: (0,), memory_space=pltpu.SMEM)
    return pl.pallas_call(
        functools.partial(_dispatch_kernel, tm=tm),
        grid=(t // tm,),
        in_specs=[smem_blk, smem_blk, smem_all, smem_all, pl.BlockSpec(memory_space=pl.ANY)],
        out_specs=pl.BlockSpec(memory_space=pl.ANY),
        out_shape=jax.ShapeDtypeStruct((n_rows * ROW_TILES, LANES), F32),
        scratch_shapes=[pltpu.VMEM((EXPERT_BLOCK // 2 * ROW_TILES, LANES), F32),
                        pltpu.SemaphoreType.DMA, pltpu.SemaphoreType.DMA],
        compiler_params=_cparams(("arbitrary",)),
        name="dispatch",
    )(e_flat, r_flat, pstart, cnt, hrows)


def _experts_kernel(blk_ref, nv_ref, xs_ref, wg_ref, wu_ref, wd_ref, ys_ref, wgb_ref, wub_ref, wdb_ref):
    i = pl.program_id(0)
    nv = nv_ref[0]

    @pl.when(i < nv)
    def _():
        prev = blk_ref[jnp.maximum(i - 1, 0)]

        @pl.when((i == 0) | (blk_ref[i] != prev))
        def _():
            wgb_ref[...] = wg_ref[0].astype(BF16)
            wub_ref[...] = wu_ref[0].astype(BF16)
            wdb_ref[...] = wd_ref[0].astype(BF16)

        xb = jnp.concatenate(
            [xs_ref[pl.ds(s, EXPERT_BLOCK, stride=ROW_TILES), :] for s in range(ROW_TILES)], axis=1).astype(BF16)
        hid = _silu(_dot(xb, wgb_ref[...])) * _dot(xb, wub_ref[...])
        y = _dot(hid.astype(BF16), wdb_ref[...])
        for s in range(ROW_TILES):
            ys_ref[pl.ds(s, EXPERT_BLOCK, stride=ROW_TILES), :] = y[:, LANES * s:LANES * (s + 1)]


def _experts(blk, nv, xs, wg, wu, wd, nbp):
    rows = EXPERT_BLOCK * ROW_TILES
    cur = lambda i, blk_ref, nv_ref: jnp.minimum(i, nv_ref[0] - 1)
    xmap = lambda i, blk_ref, nv_ref: (cur(i, blk_ref, nv_ref), 0)
    wmap = lambda i, blk_ref, nv_ref: (blk_ref[cur(i, blk_ref, nv_ref)], 0, 0)
    return pl.pallas_call(
        _experts_kernel,
        grid_spec=pltpu.PrefetchScalarGridSpec(
            num_scalar_prefetch=2,
            grid=(nbp,),
            in_specs=[pl.BlockSpec((rows, LANES), xmap),
                      pl.BlockSpec((1, D_MODEL, D_EXPERT), wmap),
                      pl.BlockSpec((1, D_MODEL, D_EXPERT), wmap),
                      pl.BlockSpec((1, D_EXPERT, D_MODEL), wmap)],
            out_specs=pl.BlockSpec((rows, LANES), xmap),
            scratch_shapes=[pltpu.VMEM((D_MODEL, D_EXPERT), BF16), pltpu.VMEM((D_MODEL, D_EXPERT), BF16),
                            pltpu.VMEM((D_EXPERT, D_MODEL), BF16)]),
        out_shape=jax.ShapeDtypeStruct(xs.shape, F32),
        compiler_params=_cparams(("arbitrary",)),
        name="experts",
    )(blk, nv, xs, wg, wu, wd)


def _combine_kernel(e_ref, r_ref, w_ref, ps_ref, ys_ref, base_ref, g2_ref, fg_ref, out_ref,
                    buf_ref, acc_ref, sem, *, tm):
    def copy(j):
        src = ps_ref[e_ref[j]] + r_ref[j]
        return _row_copy(ys_ref, src, buf_ref, j, sem)

    def start_body(j, carry):
        copy(j).start()
        return carry

    def wait_body(j, carry):
        copy(j).wait()
        return carry

    lax.fori_loop(0, tm * TOP_K, start_body, 0, unroll=8)
    lax.fori_loop(0, tm * TOP_K, wait_body, 0, unroll=8)

    def token_body(t, carry):
        j0 = t * TOP_K
        acc = jnp.zeros((ROW_TILES, LANES), F32)
        for k in range(TOP_K):
            acc = acc + w_ref[j0 + k] * buf_ref[pl.ds(pl.multiple_of((j0 + k) * ROW_TILES, ROW_TILES), ROW_TILES), :]
        acc_ref[pl.ds(pl.multiple_of(t * ROW_TILES, ROW_TILES), ROW_TILES), :] = acc
        return carry

    lax.fori_loop(0, tm, token_body, 0)
    routed = jnp.concatenate([acc_ref[pl.ds(s, tm, stride=ROW_TILES), :] for s in range(ROW_TILES)], axis=1)
    out_ref[0] = _rms(base_ref[0] + g2_ref[0] * routed, fg_ref[...])


def _combine(e_flat, r_flat, w_flat, pstart, ys, base, g2, fg):
    b, l, _ = base.shape
    tm = 256
    nt = l // tm
    smem_blk = pl.BlockSpec((tm * TOP_K,), lambda bi, i: (bi * nt + i,), memory_space=pltpu.SMEM)
    return pl.pallas_call(
        functools.partial(_combine_kernel, tm=tm),
        grid=(b, nt),
        in_specs=[smem_blk, smem_blk, smem_blk,
                  pl.BlockSpec((N_EXPERTS,), lambda bi, i: (0,), memory_space=pltpu.SMEM),
                  pl.BlockSpec(memory_space=pl.ANY),
                  pl.BlockSpec((1, tm, D_MODEL), lambda bi, i: (bi, i, 0)),
                  pl.BlockSpec((1, 1, D_MODEL), lambda bi, i: (bi, 0, 0)),
                  pl.BlockSpec((1, D_MODEL), lambda bi, i: (0, 0))],
        out_specs=pl.BlockSpec((1, tm, D_MODEL), lambda bi, i: (bi, i, 0)),
        out_shape=jax.ShapeDtypeStruct((b, l, D_MODEL), F32),
        scratch_shapes=[pltpu.VMEM((tm * TOP_K * ROW_TILES, LANES), F32),
                        pltpu.VMEM((tm * ROW_TILES, LANES), F32),
                        pltpu.SemaphoreType.DMA],
        compiler_params=_cparams(("arbitrary", "arbitrary")),
        name="combine",
    )(e_flat, r_flat, w_flat, pstart, ys, base, g2, fg)


def _mixer_inputs(h, shift, scale, gain, wm, wl, wa, ba):
    return _proj(h, shift, scale, gain, wm, wl, wa, ba)


def kernel(x, c, ctx, c_ctx, ada_w, ada_b, norm1_g, norm2_g, w_in, gla_wa_f, gla_ba_f, gla_wa_b, gla_ba_b, gla_norm_g, s5_lam_re_f, s5_lam_im_f, s5_log_step_f, s5_lam_re_b, s5_lam_im_b, s5_log_step_b, s5_b_re, s5_b_im, s5_c_re_f, s5_c_im_f, s5_c_re_b, s5_c_im_b, s5_d, s5_glu_w, s5_glu_b, w_out, router_w, router_b, exp_w_gate, exp_w_up, exp_w_down, sh_w_gate, sh_w_up, sh_w_down, final_norm_g):
    b, l, d = x.shape
    i = 0

    rows = -(-(b + 1) // SUBLANES) * SUBLANES
    cs = jnp.zeros((rows, d), F32).at[:b].set(c).at[b].set(c_ctx)
    mod = _adaln(cs, ada_w[i], ada_b[i][None, :])
    sh1, sc1, g1, sh2, sc2, g2 = [mod[:b, d * j:d * (j + 1)][:, None, :] for j in range(6)]
    csh1, csc1 = [jnp.broadcast_to(mod[b, d * j:d * (j + 1)][None, None, :], (b, 1, d)) for j in range(2)]

    w = w_in[i]
    o1, o2, o3, o4, o5, o6 = 256, 512, 1024, 1536, 1552, 1568
    wm = jnp.concatenate([w[:, :o4], w[:, o6:]], axis=1).astype(BF16)
    wl = jnp.zeros((d, LANES), F32).at[:, :2 * GLA_GATE_RANK].set(w[:, o4:o6]).astype(BF16)
    wa = jnp.zeros((LANES, 2 * GLA_DK), F32)
    wa = wa.at[:GLA_GATE_RANK, :GLA_DK].set(gla_wa_f[i]).at[GLA_GATE_RANK:2 * GLA_GATE_RANK, GLA_DK:].set(gla_wa_b[i])
    wa = wa.astype(BF16)
    ba = jnp.concatenate([gla_ba_f[i], gla_ba_b[i]])[None, :]
    n1 = norm1_g[i][None, :]

    pcols = jnp.stack([s5_lam_re_f[i], s5_lam_im_f[i],
                       jnp.broadcast_to(s5_log_step_f[i][:, None], (S5_GROUPS, S5_STATE)),
                       s5_lam_re_b[i], s5_lam_im_b[i],
                       jnp.broadcast_to(s5_log_step_b[i][:, None], (S5_GROUPS, S5_STATE)),
                       jnp.zeros((S5_GROUPS, S5_STATE), F32), jnp.zeros((S5_GROUPS, S5_STATE), F32)], axis=-1)
    prows = pcols.transpose(0, 2, 1)
    tile_b = lambda t: jnp.tile(t, (1, 1, S5_CHUNK))
    tile_c = lambda t: jnp.tile(t.transpose(0, 2, 1), (1, 1, S5_CHUNK))
    m_op, wt_op, v_op, ab4 = _s5gen(pcols, prows, tile_b(s5_b_re[i]), tile_b(s5_b_im[i]),
                                    tile_c(s5_c_re_f[i]), tile_c(s5_c_im_f[i]),
                                    tile_c(s5_c_re_b[i]), tile_c(s5_c_im_b[i]))
    ab = ab4.reshape(S5_GROUPS, 1, 4 * S5_STATE)

    cq, ck, cv, _, cu, claf, clab = _proj(ctx, csh1, csc1, n1, wm, wl, wa, ba)
    zero_state = jnp.zeros((b, GLA_HEADS, GLA_DV_HEAD, LANES), F32)
    gsf, gsb = _gla(cq, ck, cv, claf, clab, zero_state, zero_state, with_output=False)
    (x0,) = _s5(_to_chunk_vectors(cu), None, wt_op, None, ab, jnp.zeros((S5_GROUPS, b, S5_VEC), F32), b,
                with_output=False)

    q, k, v, go, u, laf, lab = _proj(x, sh1, sc1, n1, wm, wl, wa, ba)
    o, _, _ = _gla(q, k, v, laf, lab, gsf, gsb, with_output=True)
    yvec, _ = _s5(_to_chunk_vectors(u), m_op, wt_op, v_op, ab, x0, b, with_output=True)
    y = _from_chunk_vectors(yvec, b)

    rw_hi = router_w[i].astype(BF16)
    rw_lo = (router_w[i] - rw_hi.astype(F32)).astype(BF16)
    base, hrows, logits = _post(
        x, o, go, y, u, g1, sh2, sc2, g2,
        gla_norm_g[i][None, :], s5_d[i][None, :], s5_glu_w[i].astype(BF16),
        s5_glu_b[i][None, :], w_out[i].astype(BF16), norm2_g[i][None, :], rw_hi, rw_lo,
        jnp.concatenate([sh_w_gate[i], sh_w_up[i]], axis=1).astype(BF16), sh_w_down[i].astype(BF16))

    t = b * l
    top_e, wts, rank, cnt = _route(logits.reshape(t, N_EXPERTS), router_b[i][None, :])
    nbp = -(-_n_blocks_max(t * TOP_K) // SUBLANES) * SUBLANES
    pstart, blk, nv = _plan(cnt, nbp)
    e_flat, r_flat, w_flat = top_e.reshape(-1), rank.reshape(-1), wts.reshape(-1)
    ps_flat, cnt_flat = pstart.reshape(-1), cnt.reshape(-1)
    xs = _dispatch(e_flat, r_flat, ps_flat, cnt_flat, hrows, nbp * EXPERT_BLOCK)
    ys = _experts(blk.reshape(-1), nv.reshape(-1), xs, exp_w_gate[i], exp_w_up[i], exp_w_down[i], nbp)
    return _combine(e_flat, r_flat, w_flat, ps_flat, ys, base, g2, final_norm_g[None, :])
```

```python
import functools

import jax
import jax.numpy as jnp
from jax import lax
from jax.experimental import pallas as pl
from jax.experimental.pallas import tpu as pltpu

F32 = jnp.float32
BF16 = jnp.bfloat16
I32 = jnp.int32

D_MODEL = 1024
GLA_HEADS = 4
GLA_DK_HEAD = 64
GLA_DV_HEAD = 128
GLA_DK = 256
GLA_DV = 512
GLA_GATE_RANK = 16
GLA_GATE_TAU = 16.0
GLA_CHUNK = 64
D_S5 = 512
S5_GROUP_CH = 16
S5_GROUPS = 32
S5_STATE = 64
S5_CHUNK = 16
S5_VEC = S5_CHUNK * S5_GROUP_CH
N_EXPERTS = 256
TOP_K = 8
N_EXPERT_GROUPS = 8
TOPK_GROUPS = 4
D_EXPERT = 256
D_SHARED = 256
ROUTE_SCALE = 2.5
EPS = 1e-6

LANES = 128
SUBLANES = 8
ROW_TILES = D_MODEL // LANES
EXPERT_BLOCK = 256
VMEM_LIMIT = 56 * 1024 * 1024


def _cparams(sem):
    return pltpu.CompilerParams(dimension_semantics=sem, vmem_limit_bytes=VMEM_LIMIT)


def _dot(a, b):
    return jnp.dot(a, b, preferred_element_type=F32)


def _dot_nt(a, b):
    return lax.dot_general(a, b, (((1,), (1,)), ((), ())), preferred_element_type=F32)


def _dot_tn(a, b):
    return lax.dot_general(a, b, (((0,), (0,)), ((), ())), preferred_element_type=F32)


def _split2(x):
    hi = x.astype(BF16)
    lo = (x - hi.astype(F32)).astype(BF16)
    return hi, lo


def _dot3(a, b_hi, b_lo):
    a_hi, a_lo = _split2(a)
    return _dot(a_hi, b_hi) + (_dot(a_hi, b_lo) + _dot(a_lo, b_hi))


def _silu(x):
    return x * jax.nn.sigmoid(x)


def _rms(x, g):
    return x * lax.rsqrt(jnp.mean(x * x, axis=-1, keepdims=True) + EPS) * g


def _adaln_kernel(c_ref, w_ref, b_ref, o_ref):
    s = _silu(c_ref[...])
    w_hi, w_lo = _split2(w_ref[...])
    o_ref[...] = _dot3(s, w_hi, w_lo) + b_ref[...]


def _adaln(cs, w, b):
    rows, n = cs.shape[0], w.shape[1]
    tn = 1024
    return pl.pallas_call(
        _adaln_kernel,
        grid=(n // tn,),
        in_specs=[pl.BlockSpec((rows, D_MODEL), lambda j: (0, 0)),
                  pl.BlockSpec((D_MODEL, tn), lambda j: (0, j)),
                  pl.BlockSpec((1, tn), lambda j: (0, j))],
        out_specs=pl.BlockSpec((rows, tn), lambda j: (0, j)),
        out_shape=jax.ShapeDtypeStruct((rows, n), F32),
        compiler_params=_cparams(("arbitrary",)),
        name="adaln",
    )(cs, w, b)


def _proj_kernel(x_ref, sh_ref, sc_ref, g_ref, wm_ref, wl_ref, wa_ref, ba_ref,
                 q_ref, k_ref, v_ref, go_ref, u_ref, laf_ref, lab_ref):
    h = _rms(x_ref[0], g_ref[...]) * (1.0 + sc_ref[0]) + sh_ref[0]
    hb = h.astype(BF16)
    q_ref[0] = _dot(hb, wm_ref[:, 0:256]) * (GLA_DK_HEAD ** -0.5)
    k_ref[0] = _dot(hb, wm_ref[:, 256:512])
    v_ref[0] = _dot(hb, wm_ref[:, 512:1024])
    go_ref[0] = _dot(hb, wm_ref[:, 1024:1536])
    u_ref[0] = _dot(hb, wm_ref[:, 1536:2048])
    lr = _dot(hb, wl_ref[...])
    pre = _dot(lr.astype(BF16), wa_ref[...]) + ba_ref[...]
    la = (jnp.minimum(pre, 0.0) - jnp.log1p(jnp.exp(-jnp.abs(pre)))) * (1.0 / GLA_GATE_TAU)
    laf_ref[0] = la[:, 0:GLA_DK]
    lab_ref[0] = la[:, GLA_DK:2 * GLA_DK]


def _proj(x, shift, scale, gain, wm, wl, wa, ba):
    b, l, _ = x.shape
    tm = min(512, l)
    row = lambda bi, i: (bi, i, 0)
    mod = lambda bi, i: (bi, 0, 0)
    full = lambda bi, i: (0, 0)
    widths = (GLA_DK, GLA_DK, GLA_DV, GLA_DV, D_S5, GLA_DK, GLA_DK)
    return pl.pallas_call(
        _proj_kernel,
        grid=(b, l // tm),
        in_specs=[pl.BlockSpec((1, tm, D_MODEL), row),
                  pl.BlockSpec((1, 1, D_MODEL), mod),
                  pl.BlockSpec((1, 1, D_MODEL), mod),
                  pl.BlockSpec((1, D_MODEL), full),
                  pl.BlockSpec(wm.shape, full),
                  pl.BlockSpec(wl.shape, full),
                  pl.BlockSpec(wa.shape, full),
                  pl.BlockSpec(ba.shape, full)],
        out_specs=[pl.BlockSpec((1, tm, w), row) for w in widths],
        out_shape=[jax.ShapeDtypeStruct((b, l, w), F32) for w in widths],
        compiler_params=_cparams(("arbitrary", "arbitrary")),
        name="proj",
    )(x, shift, scale, gain, wm, wl, wa, ba)


def _gla_kernel(*refs, n_chunks, with_output):
    if with_output:
        q_ref, k_ref, v_ref, laf_ref, lab_ref, s0f_ref, s0b_ref, o_ref, sf_ref, sb_ref, st_ref = refs
    else:
        q_ref, k_ref, v_ref, laf_ref, lab_ref, s0f_ref, s0b_ref, sf_ref, sb_ref, st_ref = refs
        o_ref = None
    c = GLA_CHUNK
    row = lax.broadcasted_iota(I32, (c, c), 0)
    col = lax.broadcasted_iota(I32, (c, c), 1)
    lane = lax.broadcasted_iota(I32, (c, LANES), 1)
    for direction in (0, 1):
        la_ref = laf_ref if direction == 0 else lab_ref
        s0_ref = s0f_ref if direction == 0 else s0b_ref
        sfin_ref = sf_ref if direction == 0 else sb_ref
        tri = (row >= col) if direction == 0 else (row <= col)
        trib = jnp.where(tri, 1.0, 0.0).astype(BF16)
        st_ref[...] = s0_ref[0]

        def body(ci, carry, direction=direction, la_ref=la_ref, tri=tri, trib=trib):
            idx = ci if direction == 0 else n_chunks - 1 - ci
            r0 = pl.multiple_of(idx * c, c)
            q = q_ref[0, pl.ds(r0, c), :]
            k = k_ref[0, pl.ds(r0, c), :]
            v = v_ref[0, pl.ds(r0, c), :]
            la_hi, la_lo = _split2(la_ref[0, pl.ds(r0, c), :])
            cum = _dot(trib, la_hi) + _dot(trib, la_lo)
            tot = cum[c - 1:c, :] if direction == 0 else cum[0:1, :]
            qd = q * jnp.exp(cum)
            ki = k * jnp.exp(-cum)
            ks = k * jnp.exp(tot - cum)
            dec = jnp.exp(tot)
            for h in range(GLA_HEADS):
                pair = slice(LANES * (h // 2), LANES * (h // 2) + LANES)
                own = (lane >= GLA_DK_HEAD * (h % 2)) & (lane < GLA_DK_HEAD * (h % 2) + GLA_DK_HEAD)
                vb = v[:, GLA_DV_HEAD * h:GLA_DV_HEAD * (h + 1)].astype(BF16)
                st = st_ref[h]
                if with_output:
                    qb = qd[:, pair].astype(BF16)
                    kib = jnp.where(own, ki[:, pair], 0.0).astype(BF16)
                    sc = jnp.where(tri, _dot_nt(qb, kib), 0.0)
                    o = _dot(sc.astype(BF16), vb) + _dot_nt(qb, st.astype(BF16))
                    osl = (0, pl.ds(r0, c), slice(GLA_DV_HEAD * h, GLA_DV_HEAD * (h + 1)))
                    if direction == 0:
                        o_ref[osl] = o
                    else:
                        o_ref[osl] = o_ref[osl] + o
                ksb = jnp.where(own, ks[:, pair], 0.0).astype(BF16)
                st_ref[h] = st * dec[:, pair] + _dot_tn(vb, ksb)
            return carry

        lax.fori_loop(0, n_chunks, body, 0)
        sfin_ref[0] = st_ref[...]


def _gla(q, k, v, laf, lab, s0f, s0b, with_output):
    b, l, _ = q.shape
    n_chunks = l // GLA_CHUNK
    seq = lambda bi: (bi, 0, 0)
    st = lambda bi: (bi, 0, 0, 0)
    st_shape = (b, GLA_HEADS, GLA_DV_HEAD, LANES)
    st_spec = pl.BlockSpec((1, GLA_HEADS, GLA_DV_HEAD, LANES), st)
    out_specs = [st_spec, st_spec]
    out_shape = [jax.ShapeDtypeStruct(st_shape, F32)] * 2
    if with_output:
        out_specs = [pl.BlockSpec((1, l, GLA_DV), seq)] + out_specs
        out_shape = [jax.ShapeDtypeStruct((b, l, GLA_DV), F32)] + out_shape
    return pl.pallas_call(
        functools.partial(_gla_kernel, n_chunks=n_chunks, with_output=with_output),
        grid=(b,),
        in_specs=[pl.BlockSpec((1, l, GLA_DK), seq),
                  pl.BlockSpec((1, l, GLA_DK), seq),
                  pl.BlockSpec((1, l, GLA_DV), seq),
                  pl.BlockSpec((1, l, GLA_DK), seq),
                  pl.BlockSpec((1, l, GLA_DK), seq),
                  st_spec, st_spec],
        out_specs=out_specs,
        out_shape=out_shape,
        scratch_shapes=[pltpu.VMEM((GLA_HEADS, GLA_DV_HEAD, LANES), F32)],
        compiler_params=_cparams(("arbitrary",)),
        name="gla_out" if with_output else "gla_ctx",
    )(q, k, v, laf, lab, s0f, s0b)


def _s5gen_kernel(pc_ref, pr_ref, btr_ref, bti_ref, ctrf_ref, ctif_ref, ctrb_ref, ctib_ref,
                  m_ref, wt_ref, v_ref, ab_ref):
    pc = pc_ref[0]
    blk = lax.shift_right_logical(lax.broadcasted_iota(I32, (1, S5_VEC), 1), 4).astype(F32)
    lane = lax.broadcasted_iota(I32, (S5_GROUP_CH, S5_VEC), 1)
    n = float(S5_CHUNK)

    def cmul(ar, ai, br, bi):
        return ar * br - ai * bi, ar * bi + ai * br

    kcat = []
    for d in (0, 1):
        lre, lim, ls = pc[:, 3 * d:3 * d + 1], pc[:, 3 * d + 1:3 * d + 2], pc[:, 3 * d + 2:3 * d + 3]
        ctr = (ctrf_ref if d == 0 else ctrb_ref)[0]
        cti = (ctif_ref if d == 0 else ctib_ref)[0]
        step = jnp.exp(ls)
        mag = jnp.exp(lre * step)
        a_re = mag * jnp.cos(lim * step)
        a_im = mag * jnp.sin(lim * step)
        den = lre * lre + lim * lim
        f_re = ((a_re - 1.0) * lre + a_im * lim) / den
        f_im = (a_im * lre - (a_re - 1.0) * lim) / den
        bb_re, bb_im = cmul(f_re, f_im, btr_ref[0], bti_ref[0])

        def powers(e, lre=lre, lim=lim, step=step):
            m = jnp.exp(lre * step * e)
            ang = lim * step * e
            return m * jnp.cos(ang), m * jnp.sin(ang)

        w_re, w_im = cmul(*powers((n - 1.0 - blk) if d == 0 else blk), bb_re, bb_im)
        wt_ref[0, S5_STATE * d:S5_STATE * (d + 1), :] = w_re
        wt_ref[0, 2 * S5_STATE + S5_STATE * d:2 * S5_STATE + S5_STATE * (d + 1), :] = w_im
        c_re, c_im = cmul(*powers((blk + 1.0) if d == 0 else (n - blk)), ctr, cti)
        v_ref[0, S5_STATE * d:S5_STATE * (d + 1), :] = c_re
        v_ref[0, 2 * S5_STATE + S5_STATE * d:2 * S5_STATE + S5_STATE * (d + 1), :] = -c_im
        e_re, e_im = cmul(*powers(blk if d == 0 else (n - 1.0 - blk)), ctr, cti)
        b16r_hi, b16r_lo = _split2(bb_re[:, 0:S5_GROUP_CH])
        b16i_hi, b16i_lo = _split2(bb_im[:, 0:S5_GROUP_CH])
        er_hi, er_lo = _split2(e_re)
        ei_hi, ei_lo = _split2(e_im)
        kr = _dot_tn(b16r_hi, er_hi) + (_dot_tn(b16r_hi, er_lo) + _dot_tn(b16r_lo, er_hi))
        ki = _dot_tn(b16i_hi, ei_hi) + (_dot_tn(b16i_hi, ei_lo) + _dot_tn(b16i_lo, ei_hi))
        kcat.append(kr - ki)

    for s in range(S5_CHUNK):
        sh_f = S5_GROUP_CH * s
        fwd = kcat[0] if s == 0 else pltpu.roll(kcat[0], sh_f, 1)
        fwd = jnp.where(lane >= sh_f, fwd, 0.0)
        sh_b = S5_VEC - S5_GROUP_CH * (S5_CHUNK - 1 - s)
        bwd = kcat[1] if sh_b == S5_VEC else pltpu.roll(kcat[1], sh_b, 1)
        bwd = jnp.where(lane < S5_GROUP_CH * (s + 1), bwd, 0.0)
        m_ref[0, S5_GROUP_CH * s:S5_GROUP_CH * (s + 1), :] = fwd + bwd

    pr = pr_ref[0]
    for d in (0, 1):
        lre, lim, ls = pr[3 * d:3 * d + 1, :], pr[3 * d + 1:3 * d + 2, :], pr[3 * d + 2:3 * d + 3, :]
        stp = jnp.exp(ls) * n
        mg = jnp.exp(lre * stp)
        ab_ref[0, d:d + 1, :] = mg * jnp.cos(lim * stp)
        ab_ref[0, 2 + d:3 + d, :] = mg * jnp.sin(lim * stp)


def _s5gen(pc, pr, btr, bti, ctrf, ctif, ctrb, ctib):
    g = pc.shape[0]
    blk3 = lambda shape: pl.BlockSpec((1,) + shape, lambda i: (i, 0, 0))
    big = (S5_STATE, S5_VEC)
    sq = (S5_VEC, S5_VEC)
    return pl.pallas_call(
        _s5gen_kernel,
        grid=(g,),
        in_specs=[blk3((S5_STATE, 8)), blk3((8, S5_STATE))] + [blk3(big)] * 6,
        out_specs=[blk3(sq), blk3(sq), blk3(sq), blk3((4, S5_STATE))],
        out_shape=[jax.ShapeDtypeStruct((g,) + sq, F32)] * 3 + [jax.ShapeDtypeStruct((g, 4, S5_STATE), F32)],
        compiler_params=_cparams(("arbitrary",)),
        name="s5gen",
    )(pc, pr, btr, bti, ctrf, ctif, ctrb, ctib)


def _s5_kernel(*refs, n_chunks, nb, with_output):
    if with_output:
        u_ref, m_ref, wt_ref, v_ref, ab_ref, x0_ref, y_ref, xf_ref, z_ref, cin_ref = refs
    else:
        u_ref, wt_ref, ab_ref, x0_ref, xf_ref, z_ref = refs
    ub = u_ref[0].astype(BF16)
    z_ref[...] = _dot_nt(ub, wt_ref[0].astype(BF16))
    ab = ab_ref[0]
    ar, ai = ab[:, 0:LANES], ab[:, LANES:2 * LANES]
    is_f = lax.broadcasted_iota(I32, (nb, LANES), 1) < S5_STATE
    x0 = x0_ref[0]

    def body(i, carry):
        xr, xi = carry
        rf = pl.multiple_of(i * nb, nb)
        rb = pl.multiple_of((n_chunks - 1 - i) * nb, nb)
        if with_output:
            cin_ref[pl.ds(rf, nb), 0:64] = xr[:, 0:64]
            cin_ref[pl.ds(rf, nb), 128:192] = xi[:, 0:64]
            cin_ref[pl.ds(rb, nb), 64:128] = xr[:, 64:128]
            cin_ref[pl.ds(rb, nb), 192:256] = xi[:, 64:128]
        zf = z_ref[pl.ds(rf, nb), :]
        zb = z_ref[pl.ds(rb, nb), :]
        zr = jnp.where(is_f, zf[:, 0:LANES], zb[:, 0:LANES])
        zi = jnp.where(is_f, zf[:, LANES:2 * LANES], zb[:, LANES:2 * LANES])
        return ar * xr - ai * xi + zr, ar * xi + ai * xr + zi

    xr, xi = lax.fori_loop(0, n_chunks, body, (x0[:, 0:LANES], x0[:, LANES:2 * LANES]))
    xf_ref[0, :, 0:LANES] = xr
    xf_ref[0, :, LANES:2 * LANES] = xi
    if with_output:
        y_ref[0] = _dot(ub, m_ref[0].astype(BF16)) + _dot(cin_ref[...].astype(BF16), v_ref[0].astype(BF16))


def _s5(uvec, m, wt, v, ab, x0, nb, with_output):
    g, rows, _ = uvec.shape
    n_chunks = rows // nb
    blk3 = lambda shape: pl.BlockSpec((1,) + shape, lambda i: (i, 0, 0))
    sq = (S5_VEC, S5_VEC)
    st = (nb, S5_VEC)
    if with_output:
        args = (uvec, m, wt, v, ab, x0)
        in_specs = [blk3((rows, S5_VEC)), blk3(sq), blk3(sq), blk3(sq), blk3((1, S5_VEC)), blk3(st)]
        out_specs = [blk3((rows, S5_VEC)), blk3(st)]
        out_shape = [jax.ShapeDtypeStruct((g, rows, S5_VEC), F32), jax.ShapeDtypeStruct((g,) + st, F32)]
        scratch = [pltpu.VMEM((rows, S5_VEC), F32), pltpu.VMEM((rows, S5_VEC), F32)]
    else:
        args = (uvec, wt, ab, x0)
        in_specs = [blk3((rows, S5_VEC)), blk3(sq), blk3((1, S5_VEC)), blk3(st)]
        out_specs = [blk3(st)]
        out_shape = [jax.ShapeDtypeStruct((g,) + st, F32)]
        scratch = [pltpu.VMEM((rows, S5_VEC), F32)]
    return pl.pallas_call(
        functools.partial(_s5_kernel, n_chunks=n_chunks, nb=nb, with_output=with_output),
        grid=(g,),
        in_specs=in_specs,
        out_specs=out_specs,
        out_shape=out_shape,
        scratch_shapes=scratch,
        compiler_params=_cparams(("arbitrary",)),
        name="s5_out" if with_output else "s5_ctx",
    )(*args)


def _to_chunk_vectors(u):
    b, l, _ = u.shape
    n = l // S5_CHUNK
    t = u.reshape(b, n, S5_CHUNK, S5_GROUPS, S5_GROUP_CH).transpose(3, 1, 0, 2, 4)
    return t.reshape(S5_GROUPS, n * b, S5_VEC)


def _from_chunk_vectors(y, b):
    g, rows, _ = y.shape
    n = rows // b
    t = y.reshape(g, n, b, S5_CHUNK, S5_GROUP_CH).transpose(2, 1, 3, 0, 4)
    return t.reshape(b, n * S5_CHUNK, D_S5)


def _post_kernel(x_ref, o_ref, go_ref, y_ref, u_ref, g1_ref, sh2_ref, sc2_ref, g2_ref,
                 gn_ref, d_ref, gw_ref, gb_ref, wo_ref, n2_ref, rwh_ref, rwl_ref, sgu_ref, sd_ref,
                 base_ref, hrow_ref, lg_ref, *, tm):
    o = o_ref[0]
    gn = gn_ref[...]
    heads = [_rms(o[:, GLA_DV_HEAD * h:GLA_DV_HEAD * (h + 1)], gn) for h in range(GLA_HEADS)]
    gla_out = jnp.concatenate(heads, axis=1) * _silu(go_ref[0])
    yy = y_ref[0] + d_ref[...] * u_ref[0]
    z = 0.5 * yy * (1.0 + jnp.tanh(0.7978845608028654 * (yy + 0.044715 * (yy * yy * yy))))
    s5_out = z * jax.nn.sigmoid(_dot(z.astype(BF16), gw_ref[...]) + gb_ref[...])
    mix = jnp.concatenate([gla_out, s5_out], axis=1).astype(BF16)
    x1 = x_ref[0] + g1_ref[0] * _dot(mix, wo_ref[...])
    h2 = _rms(x1, n2_ref[...]) * (1.0 + sc2_ref[0]) + sh2_ref[0]
    lg_ref[0] = _dot3(h2, rwh_ref[...], rwl_ref[...])
    hb = h2.astype(BF16)
    gu = _dot(hb, sgu_ref[...])
    hid = _silu(gu[:, 0:D_SHARED]) * gu[:, D_SHARED:2 * D_SHARED]
    base_ref[0] = x1 + g2_ref[0] * _dot(hid.astype(BF16), sd_ref[...])
    for s in range(ROW_TILES):
        hrow_ref[pl.ds(s, tm, stride=ROW_TILES), :] = h2[:, LANES * s:LANES * (s + 1)]


def _post(x, o, go, y, u, g1, sh2, sc2, g2, gn, d, gw, gb, wo, n2, rwh, rwl, sgu, sd):
    b, l, _ = x.shape
    tm = 256
    nt = l // tm
    row = lambda bi, i: (bi, i, 0)
    mod = lambda bi, i: (bi, 0, 0)
    full = lambda bi, i: (0, 0)
    ws = (gn, d, gw, gb, wo, n2, rwh, rwl, sgu, sd)
    return pl.pallas_call(
        functools.partial(_post_kernel, tm=tm),
        grid=(b, nt),
        in_specs=[pl.BlockSpec((1, tm, D_MODEL), row)]
                 + [pl.BlockSpec((1, tm, 512), row)] * 4
                 + [pl.BlockSpec((1, 1, D_MODEL), mod)] * 4
                 + [pl.BlockSpec(w.shape, full) for w in ws],
        out_specs=[pl.BlockSpec((1, tm, D_MODEL), row),
                   pl.BlockSpec((tm * ROW_TILES, LANES), lambda bi, i: (bi * nt + i, 0)),
                   pl.BlockSpec((1, tm, N_EXPERTS), row)],
        out_shape=[jax.ShapeDtypeStruct((b, l, D_MODEL), F32),
                   jax.ShapeDtypeStruct((b * l * ROW_TILES, LANES), F32),
                   jax.ShapeDtypeStruct((b, l, N_EXPERTS), F32)],
        compiler_params=_cparams(("arbitrary", "arbitrary")),
        name="post",
    )(x, o, go, y, u, g1, sh2, sc2, g2, *ws)


def _route_kernel(lg_ref, rb_ref, e_ref, w_ref, r_ref, cnt_ref, run_ref, *, tm):
    @pl.when(pl.program_id(0) == 0)
    def _():
        run_ref[...] = jnp.zeros_like(run_ref)

    neg = -jnp.inf
    s = jax.nn.sigmoid(lg_ref[...])
    biased = s + rb_ref[...]
    lane = lax.broadcasted_iota(I32, (tm, N_EXPERTS), 1)
    grp = lax.shift_right_logical(lane, 5)

    def first_max(m):
        mx = jnp.max(m, axis=-1, keepdims=True)
        ix = jnp.min(jnp.where(m == mx, lane, N_EXPERTS), axis=-1, keepdims=True)
        return mx, ix

    gs = []
    for g in range(N_EXPERT_GROUPS):
        m = jnp.where(grp == g, biased, neg)
        m1, i1 = first_max(m)
        m2 = jnp.max(jnp.where(lane == i1, neg, m), axis=-1, keepdims=True)
        gs.append(m1 + m2)
    emask = jnp.zeros((tm, N_EXPERTS), F32)
    for g in range(N_EXPERT_GROUPS):
        ahead = jnp.zeros((tm, 1), F32)
        for j in range(N_EXPERT_GROUPS):
            if j < g:
                ahead = ahead + jnp.where(gs[j] >= gs[g], 1.0, 0.0)
            elif j > g:
                ahead = ahead + jnp.where(gs[j] > gs[g], 1.0, 0.0)
        emask = emask + jnp.where(grp == g, jnp.where(ahead < float(TOPK_GROUPS), 1.0, 0.0), 0.0)
    masked = jnp.where(emask > 0.5, biased, neg)

    onehot = jnp.zeros((tm, N_EXPERTS), F32)
    ids, ws = [], []
    for _ in range(TOP_K):
        _, ik = first_max(masked)
        hit = lane == ik
        ids.append(ik)
        ws.append(jnp.sum(jnp.where(hit, s, 0.0), axis=-1, keepdims=True))
        onehot = onehot + jnp.where(hit, 1.0, 0.0)
        masked = jnp.where(hit, neg, masked)
    wsum = ws[0]
    for k in range(1, TOP_K):
        wsum = wsum + ws[k]

    rr = lax.broadcasted_iota(I32, (tm, tm), 0)
    cc = lax.broadcasted_iota(I32, (tm, tm), 1)
    before = jnp.where(rr > cc, 1.0, 0.0).astype(BF16)
    pos = _dot(before, onehot.astype(BF16)) + run_ref[...]
    lane8 = lax.broadcasted_iota(I32, (tm, TOP_K), 1)
    e_out = jnp.zeros((tm, TOP_K), I32)
    w_out = jnp.zeros((tm, TOP_K), F32)
    r_out = jnp.zeros((tm, TOP_K), F32)
    for k in range(TOP_K):
        rk = jnp.sum(jnp.where(lane == ids[k], pos, 0.0), axis=-1, keepdims=True)
        e_out = jnp.where(lane8 == k, ids[k], e_out)
        w_out = jnp.where(lane8 == k, ws[k] / wsum * ROUTE_SCALE, w_out)
        r_out = jnp.where(lane8 == k, rk, r_out)
    e_ref[...] = e_out
    w_ref[...] = w_out
    r_ref[...] = r_out.astype(I32)
    run = run_ref[...] + jnp.sum(onehot, axis=0, keepdims=True)
    run_ref[...] = run
    cnt_ref[...] = run.astype(I32)


def _route(logits, rb):
    t = logits.shape[0]
    tm = min(512, t)
    row = lambda i: (i, 0)
    return pl.pallas_call(
        functools.partial(_route_kernel, tm=tm),
        grid=(t // tm,),
        in_specs=[pl.BlockSpec((tm, N_EXPERTS), row), pl.BlockSpec((1, N_EXPERTS), lambda i: (0, 0))],
        out_specs=[pl.BlockSpec((tm, TOP_K), row)] * 3 + [pl.BlockSpec((1, N_EXPERTS), lambda i: (0, 0))],
        out_shape=[jax.ShapeDtypeStruct((t, TOP_K), I32), jax.ShapeDtypeStruct((t, TOP_K), F32),
                   jax.ShapeDtypeStruct((t, TOP_K), I32), jax.ShapeDtypeStruct((1, N_EXPERTS), I32)],
        scratch_shapes=[pltpu.VMEM((1, N_EXPERTS), F32)],
        compiler_params=_cparams(("arbitrary",)),
        name="route",
    )(logits, rb)


def _n_blocks_max(n_assign):
    return -(-(n_assign + N_EXPERTS * (EXPERT_BLOCK - 1)) // EXPERT_BLOCK)


def _plan_kernel(cnt_ref, ps_ref, blk_ref, nv_ref, *, nbp):
    cnt = cnt_ref[...]
    nb = lax.shift_right_logical(cnt + (EXPERT_BLOCK - 1), 8).astype(F32)
    nb8 = jnp.broadcast_to(nb, (SUBLANES, N_EXPERTS))
    nb_hi, nb_lo = _split2(nb8)
    ii = lax.broadcasted_iota(I32, (N_EXPERTS, N_EXPERTS), 0)
    jj = lax.broadcasted_iota(I32, (N_EXPERTS, N_EXPERTS), 1)
    upto = jnp.where(ii <= jj, 1.0, 0.0).astype(BF16)
    cum = (_dot(nb_hi, upto) + _dot(nb_lo, upto))[0:1, :]
    ps_ref[...] = ((cum - nb) * float(EXPERT_BLOCK)).astype(I32)
    bi = lax.broadcasted_iota(I32, (nbp, N_EXPERTS), 0).astype(F32)
    owner = jnp.sum(jnp.where(cum <= bi, 1.0, 0.0), axis=-1, keepdims=True)
    blk_ref[...] = jnp.minimum(owner, float(N_EXPERTS - 1)).astype(I32)
    nv_ref[...] = cum[:, N_EXPERTS - 1:N_EXPERTS].astype(I32)


def _plan(cnt, nbp):
    return pl.pallas_call(
        functools.partial(_plan_kernel, nbp=nbp),
        out_shape=[jax.ShapeDtypeStruct((1, N_EXPERTS), I32), jax.ShapeDtypeStruct((nbp, 1), I32),
                   jax.ShapeDtypeStruct((1, 1), I32)],
        name="plan",
    )(cnt)


def _row_copy(src, src_row, dst, dst_row, sem):
    return pltpu.make_async_copy(src.at[pl.ds(pl.multiple_of(src_row * ROW_TILES, ROW_TILES), ROW_TILES)],
                                 dst.at[pl.ds(pl.multiple_of(dst_row * ROW_TILES, ROW_TILES), ROW_TILES)], sem)


def _dispatch_kernel(e_ref, r_ref, ps_ref, cnt_ref, h_ref, xs_ref, zero_ref, sem, zsem, *, tm):
    step = pl.program_id(0)
    pad_sizes = [1 << p for p in range(EXPERT_BLOCK.bit_length() - 2, -1, -1)]

    def pad_copy(start_row, size):
        return pltpu.make_async_copy(
            zero_ref.at[pl.ds(0, size * ROW_TILES)],
            xs_ref.at[pl.ds(pl.multiple_of(start_row * ROW_TILES, ROW_TILES), size * ROW_TILES)], zsem)

    @pl.when(step == 0)
    def _():
        zero_ref[...] = jnp.zeros_like(zero_ref)
        for wait in (False, True):
            def pad_body(e, carry, wait=wait):
                cnt = cnt_ref[e]
                pad = (-cnt) & (EXPERT_BLOCK - 1)
                row = ps_ref[e] + cnt
                for size in pad_sizes:
                    @pl.when((pad & size) != 0)
                    def _(row=row, size=size):
                        cp = pad_copy(row, size)
                        cp.wait() if wait else cp.start()
                    row = row + (pad & size)
                return carry
            lax.fori_loop(0, N_EXPERTS, pad_body, 0)

    def copy(j):
        dst = ps_ref[e_ref[j]] + r_ref[j]
        return _row_copy(h_ref, lax.shift_right_logical(j, 3), xs_ref, dst, sem)

    def start_body(j, carry):
        copy(j).start()
        return carry

    def wait_body(j, carry):
        copy(j).wait()
        return carry

    lax.fori_loop(0, tm * TOP_K, start_body, 0, unroll=8)
    lax.fori_loop(0, tm * TOP_K, wait_body, 0, unroll=8)


def _dispatch(e_flat, r_flat, pstart, cnt, hrows, n_rows):
    t = e_flat.shape[0] // TOP_K
    tm = min(512, t)
    smem_blk = pl.BlockSpec((tm * TOP_K,), lambda i: (i,), memory_space=pltpu.SMEM)
    smem_all = pl.BlockSpec((N_EXPERTS,), lambda i: (0,), memory_space=pltpu.SMEM)
    return pl.pallas_call(
        functools.partial(_dispatch_kernel, tm=tm),
        grid=(t // tm,),
        in_specs=[smem_blk, smem_blk, smem_all, smem_all,
                  pl.BlockSpec((tm * ROW_TILES, LANES), lambda i: (i, 0))],
        out_specs=pl.BlockSpec(memory_space=pl.ANY),
        out_shape=jax.ShapeDtypeStruct((n_rows * ROW_TILES, LANES), F32),
        scratch_shapes=[pltpu.VMEM((EXPERT_BLOCK // 2 * ROW_TILES, LANES), F32),
                        pltpu.SemaphoreType.DMA, pltpu.SemaphoreType.DMA],
        compiler_params=_cparams(("arbitrary",)),
        name="dispatch",
    )(e_flat, r_flat, pstart, cnt, hrows)


def _experts_kernel(blk_ref, nv_ref, xs_ref, wg_ref, wu_ref, wd_ref, ys_ref, wgb_ref, wub_ref, wdb_ref):
    i = pl.program_id(0)
    nv = nv_ref[0]

    @pl.when(i < nv)
    def _():
        prev = blk_ref[jnp.maximum(i - 1, 0)]

        @pl.when((i == 0) | (blk_ref[i] != prev))
        def _():
            wgb_ref[...] = wg_ref[0].astype(BF16)
            wub_ref[...] = wu_ref[0].astype(BF16)
            wdb_ref[...] = wd_ref[0].astype(BF16)

        xb = jnp.concatenate(
            [xs_ref[pl.ds(s, EXPERT_BLOCK, stride=ROW_TILES), :] for s in range(ROW_TILES)], axis=1).astype(BF16)
        hid = _silu(_dot(xb, wgb_ref[...])) * _dot(xb, wub_ref[...])
        y = _dot(hid.astype(BF16), wdb_ref[...])
        for s in range(ROW_TILES):
            ys_ref[pl.ds(s, EXPERT_BLOCK, stride=ROW_TILES), :] = y[:, LANES * s:LANES * (s + 1)]


def _experts(blk, nv, xs, wg, wu, wd, nbp):
    rows = EXPERT_BLOCK * ROW_TILES
    cur = lambda i, blk_ref, nv_ref: jnp.minimum(i, nv_ref[0] - 1)
    xmap = lambda i, blk_ref, nv_ref: (cur(i, blk_ref, nv_ref), 0)
    wmap = lambda i, blk_ref, nv_ref: (blk_ref[cur(i, blk_ref, nv_ref)], 0, 0)
    return pl.pallas_call(
        _experts_kernel,
        grid_spec=pltpu.PrefetchScalarGridSpec(
            num_scalar_prefetch=2,
            grid=(nbp,),
            in_specs=[pl.BlockSpec((rows, LANES), xmap),
                      pl.BlockSpec((1, D_MODEL, D_EXPERT), wmap),
                      pl.BlockSpec((1, D_MODEL, D_EXPERT), wmap),
                      pl.BlockSpec((1, D_EXPERT, D_MODEL), wmap)],
            out_specs=pl.BlockSpec((rows, LANES), xmap),
            scratch_shapes=[pltpu.VMEM((D_MODEL, D_EXPERT), BF16), pltpu.VMEM((D_MODEL, D_EXPERT), BF16),
                            pltpu.VMEM((D_EXPERT, D_MODEL), BF16)]),
        out_shape=jax.ShapeDtypeStruct(xs.shape, F32),
        compiler_params=_cparams(("arbitrary",)),
        name="experts",
    )(blk, nv, xs, wg, wu, wd)


def _combine_kernel(e_ref, r_ref, w_ref, ps_ref, ys_ref, base_ref, g2_ref, fg_ref, out_ref,
                    buf_ref, acc_ref, sem, *, tm):
    def copy(j):
        src = ps_ref[e_ref[j]] + r_ref[j]
        return _row_copy(ys_ref, src, buf_ref, j, sem)

    def start_body(j, carry):
        copy(j).start()
        return carry

    def wait_body(j, carry):
        copy(j).wait()
        return carry

    lax.fori_loop(0, tm * TOP_K, start_body, 0, unroll=8)
    lax.fori_loop(0, tm * TOP_K, wait_body, 0, unroll=8)

    def token_body(t, carry):
        j0 = t * TOP_K
        acc = jnp.zeros((ROW_TILES, LANES), F32)
        for k in range(TOP_K):
            acc = acc + w_ref[j0 + k] * buf_ref[pl.ds(pl.multiple_of((j0 + k) * ROW_TILES, ROW_TILES), ROW_TILES), :]
        acc_ref[pl.ds(pl.multiple_of(t * ROW_TILES, ROW_TILES), ROW_TILES), :] = acc
        return carry

    lax.fori_loop(0, tm, token_body, 0)
    routed = jnp.concatenate([acc_ref[pl.ds(s, tm, stride=ROW_TILES), :] for s in range(ROW_TILES)], axis=1)
    out_ref[0] = _rms(base_ref[0] + g2_ref[0] * routed, fg_ref[...])


def _combine(e_flat, r_flat, w_flat, pstart, ys, base, g2, fg):
    b, l, _ = base.shape
    tm = 256
    nt = l // tm
    smem_blk = pl.BlockSpec((tm * TOP_K,), lambda bi, i: (bi * nt + i,), memory_space=pltpu.SMEM)
    return pl.pallas_call(
        functools.partial(_combine_kernel, tm=tm),
        grid=(b, nt),
        in_specs=[smem_blk, smem_blk, smem_blk,
                  pl.BlockSpec((N_EXPERTS,), lambda bi, i: (0,), memory_space=pltpu.SMEM),
                  pl.BlockSpec(memory_space=pl.ANY),
                  pl.BlockSpec((1, tm, D_MODEL), lambda bi, i: (bi, i, 0)),
                  pl.BlockSpec((1, 1, D_MODEL), lambda bi, i: (bi, 0, 0)),
                  pl.BlockSpec((1, D_MODEL), lambda bi, i: (0, 0))],
        out_specs=pl.BlockSpec((1, tm, D_MODEL), lambda bi, i: (bi, i, 0)),
        out_shape=jax.ShapeDtypeStruct((b, l, D_MODEL), F32),
        scratch_shapes=[pltpu.VMEM((tm * TOP_K * ROW_TILES, LANES), F32),
                        pltpu.VMEM((tm * ROW_TILES, LANES), F32),
                        pltpu.SemaphoreType.DMA],
        compiler_params=_cparams(("arbitrary", "arbitrary")),
        name="combine",
    )(e_flat, r_flat, w_flat, pstart, ys, base, g2, fg)


def _mixer_inputs(h, shift, scale, gain, wm, wl, wa, ba):
    return _proj(h, shift, scale, gain, wm, wl, wa, ba)


def kernel(x, c, ctx, c_ctx, ada_w, ada_b, norm1_g, norm2_g, w_in, gla_wa_f, gla_ba_f, gla_wa_b, gla_ba_b, gla_norm_g, s5_lam_re_f, s5_lam_im_f, s5_log_step_f, s5_lam_re_b, s5_lam_im_b, s5_log_step_b, s5_b_re, s5_b_im, s5_c_re_f, s5_c_im_f, s5_c_re_b, s5_c_im_b, s5_d, s5_glu_w, s5_glu_b, w_out, router_w, router_b, exp_w_gate, exp_w_up, exp_w_down, sh_w_gate, sh_w_up, sh_w_down, final_norm_g):
    b, l, d = x.shape
    i = 0

    rows = -(-(b + 1) // SUBLANES) * SUBLANES
    cs = jnp.zeros((rows, d), F32).at[:b].set(c).at[b].set(c_ctx)
    mod = _adaln(cs, ada_w[i], ada_b[i][None, :])
    sh1, sc1, g1, sh2, sc2, g2 = [mod[:b, d * j:d * (j + 1)][:, None, :] for j in range(6)]
    csh1, csc1 = [jnp.broadcast_to(mod[b, d * j:d * (j + 1)][None, None, :], (b, 1, d)) for j in range(2)]

    w = w_in[i]
    o1, o2, o3, o4, o5, o6 = 256, 512, 1024, 1536, 1552, 1568
    wm = jnp.concatenate([w[:, :o4], w[:, o6:]], axis=1).astype(BF16)
    wl = jnp.zeros((d, LANES), F32).at[:, :2 * GLA_GATE_RANK].set(w[:, o4:o6]).astype(BF16)
    wa = jnp.zeros((LANES, 2 * GLA_DK), F32)
    wa = wa.at[:GLA_GATE_RANK, :GLA_DK].set(gla_wa_f[i]).at[GLA_GATE_RANK:2 * GLA_GATE_RANK, GLA_DK:].set(gla_wa_b[i])
    wa = wa.astype(BF16)
    ba = jnp.concatenate([gla_ba_f[i], gla_ba_b[i]])[None, :]
    n1 = norm1_g[i][None, :]

    pcols = jnp.stack([s5_lam_re_f[i], s5_lam_im_f[i],
                       jnp.broadcast_to(s5_log_step_f[i][:, None], (S5_GROUPS, S5_STATE)),
                       s5_lam_re_b[i], s5_lam_im_b[i],
                       jnp.broadcast_to(s5_log_step_b[i][:, None], (S5_GROUPS, S5_STATE)),
                       jnp.zeros((S5_GROUPS, S5_STATE), F32), jnp.zeros((S5_GROUPS, S5_STATE), F32)], axis=-1)
    prows = pcols.transpose(0, 2, 1)
    tile_b = lambda t: jnp.tile(t, (1, 1, S5_CHUNK))
    tile_c = lambda t: jnp.tile(t.transpose(0, 2, 1), (1, 1, S5_CHUNK))
    m_op, wt_op, v_op, ab4 = _s5gen(pcols, prows, tile_b(s5_b_re[i]), tile_b(s5_b_im[i]),
                                    tile_c(s5_c_re_f[i]), tile_c(s5_c_im_f[i]),
                                    tile_c(s5_c_re_b[i]), tile_c(s5_c_im_b[i]))
    ab = ab4.reshape(S5_GROUPS, 1, 4 * S5_STATE)

    cq, ck, cv, _, cu, claf, clab = _proj(ctx, csh1, csc1, n1, wm, wl, wa, ba)
    zero_state = jnp.zeros((b, GLA_HEADS, GLA_DV_HEAD, LANES), F32)
    gsf, gsb = _gla(cq, ck, cv, claf, clab, zero_state, zero_state, with_output=False)
    (x0,) = _s5(_to_chunk_vectors(cu), None, wt_op, None, ab, jnp.zeros((S5_GROUPS, b, S5_VEC), F32), b,
                with_output=False)

    q, k, v, go, u, laf, lab = _proj(x, sh1, sc1, n1, wm, wl, wa, ba)
    o, _, _ = _gla(q, k, v, laf, lab, gsf, gsb, with_output=True)
    yvec, _ = _s5(_to_chunk_vectors(u), m_op, wt_op, v_op, ab, x0, b, with_output=True)
    y = _from_chunk_vectors(yvec, b)

    rw_hi = router_w[i].astype(BF16)
    rw_lo = (router_w[i] - rw_hi.astype(F32)).astype(BF16)
    base, hrows, logits = _post(
        x, o, go, y, u, g1, sh2, sc2, g2,
        gla_norm_g[i][None, :], s5_d[i][None, :], s5_glu_w[i].astype(BF16),
        s5_glu_b[i][None, :], w_out[i].astype(BF16), norm2_g[i][None, :], rw_hi, rw_lo,
        jnp.concatenate([sh_w_gate[i], sh_w_up[i]], axis=1).astype(BF16), sh_w_down[i].astype(BF16))

    t = b * l
    top_e, wts, rank, cnt = _route(logits.reshape(t, N_EXPERTS), router_b[i][None, :])
    nbp = -(-_n_blocks_max(t * TOP_K) // SUBLANES) * SUBLANES
    pstart, blk, nv = _plan(cnt, nbp)
    e_flat, r_flat, w_flat = top_e.reshape(-1), rank.reshape(-1), wts.reshape(-1)
    ps_flat, cnt_flat = pstart.reshape(-1), cnt.reshape(-1)
    xs = _dispatch(e_flat, r_flat, ps_flat, cnt_flat, hrows, nbp * EXPERT_BLOCK)
    assert exp_w_gate.shape[0] == 1, "single-layer block"
    ys = _experts(blk.reshape(-1), nv.reshape(-1), xs, exp_w_gate.reshape(exp_w_gate.shape[1:]),
                  exp_w_up.reshape(exp_w_up.shape[1:]), exp_w_down.reshape(exp_w_down.shape[1:]), nbp)
    return _combine(e_flat, r_flat, w_flat, ps_flat, ys, base, g2, final_norm_g[None, :])
```

```python
import functools

import jax
import jax.numpy as jnp
from jax import lax
from jax.experimental import pallas as pl
from jax.experimental.pallas import tpu as pltpu

F32 = jnp.float32
BF16 = jnp.bfloat16
I32 = jnp.int32

D_MODEL = 1024
GLA_HEADS = 4
GLA_DK_HEAD = 64
GLA_DV_HEAD = 128
GLA_DK = 256
GLA_DV = 512
GLA_GATE_RANK = 16
GLA_GATE_TAU = 16.0
GLA_CHUNK = 64
D_S5 = 512
S5_GROUP_CH = 16
S5_GROUPS = 32
S5_STATE = 64
S5_CHUNK = 16
S5_VEC = S5_CHUNK * S5_GROUP_CH
N_EXPERTS = 256
TOP_K = 8
N_EXPERT_GROUPS = 8
TOPK_GROUPS = 4
D_EXPERT = 256
D_SHARED = 256
ROUTE_SCALE = 2.5
EPS = 1e-6

LANES = 128
SUBLANES = 8
ROW_TILES = D_MODEL // LANES
EXPERT_BLOCK = 256
VMEM_LIMIT = 56 * 1024 * 1024


def _cparams(sem):
    return pltpu.CompilerParams(dimension_semantics=sem, vmem_limit_bytes=VMEM_LIMIT)


def _dot(a, b):
    return jnp.dot(a, b, preferred_element_type=F32)


def _dot_nt(a, b):
    return lax.dot_general(a, b, (((1,), (1,)), ((), ())), preferred_element_type=F32)


def _dot_tn(a, b):
    return lax.dot_general(a, b, (((0,), (0,)), ((), ())), preferred_element_type=F32)


def _split2(x):
    hi = x.astype(BF16)
    lo = (x - hi.astype(F32)).astype(BF16)
    return hi, lo


def _dot3(a, b_hi, b_lo):
    a_hi, a_lo = _split2(a)
    return _dot(a_hi, b_hi) + (_dot(a_hi, b_lo) + _dot(a_lo, b_hi))


def _silu(x):
    return x * jax.nn.sigmoid(x)


def _rms(x, g):
    return x * lax.rsqrt(jnp.mean(x * x, axis=-1, keepdims=True) + EPS) * g


def _adaln_kernel(c_ref, w_ref, b_ref, o_ref):
    s = _silu(c_ref[...])
    w_hi, w_lo = _split2(w_ref[...])
    o_ref[...] = _dot3(s, w_hi, w_lo) + b_ref[...]


def _adaln(cs, w, b):
    rows, n = cs.shape[0], w.shape[1]
    tn = 1024
    return pl.pallas_call(
        _adaln_kernel,
        grid=(n // tn,),
        in_specs=[pl.BlockSpec((rows, D_MODEL), lambda j: (0, 0)),
                  pl.BlockSpec((D_MODEL, tn), lambda j: (0, j)),
                  pl.BlockSpec((1, tn), lambda j: (0, j))],
        out_specs=pl.BlockSpec((rows, tn), lambda j: (0, j)),
        out_shape=jax.ShapeDtypeStruct((rows, n), F32),
        compiler_params=_cparams(("arbitrary",)),
        name="adaln",
    )(cs, w, b)


def _group_lane_masks(rows):
    grp = lax.shift_right_logical(lax.broadcasted_iota(I32, (rows, LANES), 1), 4)
    return [grp == j for j in range(LANES // S5_GROUP_CH)]


def _move_group(x, src, dst):
    shift = ((dst - src) * S5_GROUP_CH) % LANES
    return pltpu.roll(x, shift, 1) if shift else x


def _proj_kernel(x_ref, sh_ref, sc_ref, g_ref, wm_ref, wl_ref, wa_ref, ba_ref,
                 q_ref, k_ref, v_ref, go_ref, u_ref, laf_ref, lab_ref, uv_ref, ut_ref, *, tm):
    h = _rms(x_ref[0], g_ref[...]) * (1.0 + sc_ref[0]) + sh_ref[0]
    hb = h.astype(BF16)
    q_ref[0] = _dot(hb, wm_ref[:, 0:256]) * (GLA_DK_HEAD ** -0.5)
    k_ref[0] = _dot(hb, wm_ref[:, 256:512])
    v_ref[0] = _dot(hb, wm_ref[:, 512:1024])
    go_ref[0] = _dot(hb, wm_ref[:, 1024:1536])
    u = _dot(hb, wm_ref[:, 1536:2048])
    u_ref[0] = u
    for t in range(D_S5 // LANES):
        ut_ref[t] = u[:, LANES * t:LANES * (t + 1)]
    lr = _dot(hb, wl_ref[...])
    pre = _dot(lr.astype(BF16), wa_ref[...]) + ba_ref[...]
    la = (jnp.minimum(pre, 0.0) - jnp.log1p(jnp.exp(-jnp.abs(pre)))) * (1.0 / GLA_GATE_TAU)
    laf_ref[0] = la[:, 0:GLA_DK]
    lab_ref[0] = la[:, GLA_DK:2 * GLA_DK]
    nc = tm // S5_CHUNK
    gpt = LANES // S5_GROUP_CH
    masks = _group_lane_masks(nc)
    for t in range(D_S5 // LANES):
        steps = [ut_ref[t, pl.ds(s, nc, stride=S5_CHUNK), :] for s in range(S5_CHUNK)]
        for gl in range(gpt):
            for half in range(S5_VEC // LANES):
                acc = None
                for j in range(gpt):
                    piece = _move_group(steps[half * gpt + j], gl, j)
                    acc = piece if acc is None else jnp.where(masks[j], piece, acc)
                uv_ref[t * gpt + gl, :, LANES * half:LANES * (half + 1)] = acc.astype(BF16)


def _proj(x, shift, scale, gain, wm, wl, wa, ba):
    b, l, _ = x.shape
    tm = min(512, l)
    nt = l // tm
    row = lambda bi, i: (bi, i, 0)
    mod = lambda bi, i: (bi, 0, 0)
    full = lambda bi, i: (0, 0)
    widths = (GLA_DK, GLA_DK, GLA_DV, GLA_DV, D_S5, GLA_DK, GLA_DK)
    return pl.pallas_call(
        functools.partial(_proj_kernel, tm=tm),
        grid=(b, nt),
        in_specs=[pl.BlockSpec((1, tm, D_MODEL), row),
                  pl.BlockSpec((1, 1, D_MODEL), mod),
                  pl.BlockSpec((1, 1, D_MODEL), mod),
                  pl.BlockSpec((1, D_MODEL), full),
                  pl.BlockSpec(wm.shape, full),
                  pl.BlockSpec(wl.shape, full),
                  pl.BlockSpec(wa.shape, full),
                  pl.BlockSpec(ba.shape, full)],
        out_specs=[pl.BlockSpec((1, tm, w), row) for w in widths]
                  + [pl.BlockSpec((S5_GROUPS, tm // S5_CHUNK, S5_VEC), lambda bi, i: (0, bi * nt + i, 0))],
        out_shape=[jax.ShapeDtypeStruct((b, l, w), F32) for w in widths]
                  + [jax.ShapeDtypeStruct((S5_GROUPS, b * l // S5_CHUNK, S5_VEC), BF16)],
        scratch_shapes=[pltpu.VMEM((D_S5 // LANES, tm, LANES), F32)],
        compiler_params=_cparams(("arbitrary", "arbitrary")),
        name="proj",
    )(x, shift, scale, gain, wm, wl, wa, ba)


def _gla_kernel(*refs, n_chunks, with_output):
    if with_output:
        q_ref, k_ref, v_ref, laf_ref, lab_ref, s0f_ref, s0b_ref, o_ref, sf_ref, sb_ref, st_ref = refs
    else:
        q_ref, k_ref, v_ref, laf_ref, lab_ref, s0f_ref, s0b_ref, sf_ref, sb_ref, st_ref = refs
        o_ref = None
    c = GLA_CHUNK
    row = lax.broadcasted_iota(I32, (c, c), 0)
    col = lax.broadcasted_iota(I32, (c, c), 1)
    lane = lax.broadcasted_iota(I32, (c, LANES), 1)
    for direction in (0, 1):
        la_ref = laf_ref if direction == 0 else lab_ref
        s0_ref = s0f_ref if direction == 0 else s0b_ref
        sfin_ref = sf_ref if direction == 0 else sb_ref
        tri = (row >= col) if direction == 0 else (row <= col)
        trib = jnp.where(tri, 1.0, 0.0).astype(BF16)
        st_ref[...] = s0_ref[0]

        def body(ci, carry, direction=direction, la_ref=la_ref, tri=tri, trib=trib):
            idx = ci if direction == 0 else n_chunks - 1 - ci
            r0 = pl.multiple_of(idx * c, c)
            q = q_ref[0, pl.ds(r0, c), :]
            k = k_ref[0, pl.ds(r0, c), :]
            v = v_ref[0, pl.ds(r0, c), :]
            la_hi, la_lo = _split2(la_ref[0, pl.ds(r0, c), :])
            cum = _dot(trib, la_hi) + _dot(trib, la_lo)
            tot = cum[c - 1:c, :] if direction == 0 else cum[0:1, :]
            qd = q * jnp.exp(cum)
            ki = k * jnp.exp(-cum)
            ks = k * jnp.exp(tot - cum)
            dec = jnp.exp(tot)
            for h in range(GLA_HEADS):
                pair = slice(LANES * (h // 2), LANES * (h // 2) + LANES)
                own = (lane >= GLA_DK_HEAD * (h % 2)) & (lane < GLA_DK_HEAD * (h % 2) + GLA_DK_HEAD)
                vb = v[:, GLA_DV_HEAD * h:GLA_DV_HEAD * (h + 1)].astype(BF16)
                st = st_ref[h]
                if with_output:
                    qb = qd[:, pair].astype(BF16)
                    kib = jnp.where(own, ki[:, pair], 0.0).astype(BF16)
                    sc = jnp.where(tri, _dot_nt(qb, kib), 0.0)
                    o = _dot(sc.astype(BF16), vb) + _dot_nt(qb, st.astype(BF16))
                    osl = (0, pl.ds(r0, c), slice(GLA_DV_HEAD * h, GLA_DV_HEAD * (h + 1)))
                    if direction == 0:
                        o_ref[osl] = o
                    else:
                        o_ref[osl] = o_ref[osl] + o
                ksb = jnp.where(own, ks[:, pair], 0.0).astype(BF16)
                st_ref[h] = st * dec[:, pair] + _dot_tn(vb, ksb)
            return carry

        lax.fori_loop(0, n_chunks, body, 0)
        sfin_ref[0] = st_ref[...]


def _gla(q, k, v, laf, lab, s0f, s0b, with_output):
    b, l, _ = q.shape
    n_chunks = l // GLA_CHUNK
    seq = lambda bi: (bi, 0, 0)
    st = lambda bi: (bi, 0, 0, 0)
    st_shape = (b, GLA_HEADS, GLA_DV_HEAD, LANES)
    st_spec = pl.BlockSpec((1, GLA_HEADS, GLA_DV_HEAD, LANES), st)
    out_specs = [st_spec, st_spec]
    out_shape = [jax.ShapeDtypeStruct(st_shape, F32)] * 2
    if with_output:
        out_specs = [pl.BlockSpec((1, l, GLA_DV), seq)] + out_specs
        out_shape = [jax.ShapeDtypeStruct((b, l, GLA_DV), F32)] + out_shape
    return pl.pallas_call(
        functools.partial(_gla_kernel, n_chunks=n_chunks, with_output=with_output),
        grid=(b,),
        in_specs=[pl.BlockSpec((1, l, GLA_DK), seq),
                  pl.BlockSpec((1, l, GLA_DK), seq),
                  pl.BlockSpec((1, l, GLA_DV), seq),
                  pl.BlockSpec((1, l, GLA_DK), seq),
                  pl.BlockSpec((1, l, GLA_DK), seq),
                  st_spec, st_spec],
        out_specs=out_specs,
        out_shape=out_shape,
        scratch_shapes=[pltpu.VMEM((GLA_HEADS, GLA_DV_HEAD, LANES), F32)],
        compiler_params=_cparams(("arbitrary",)),
        name="gla_out" if with_output else "gla_ctx",
    )(q, k, v, laf, lab, s0f, s0b)


def _s5gen_kernel(pc_ref, pr_ref, btr_ref, bti_ref, ctrf_ref, ctif_ref, ctrb_ref, ctib_ref,
                  m_ref, wt_ref, v_ref, ab_ref):
    pc = pc_ref[0]
    blk = lax.shift_right_logical(lax.broadcasted_iota(I32, (1, S5_VEC), 1), 4).astype(F32)
    lane = lax.broadcasted_iota(I32, (S5_GROUP_CH, S5_VEC), 1)
    n = float(S5_CHUNK)

    def cmul(ar, ai, br, bi):
        return ar * br - ai * bi, ar * bi + ai * br

    kcat = []
    for d in (0, 1):
        lre, lim, ls = pc[:, 3 * d:3 * d + 1], pc[:, 3 * d + 1:3 * d + 2], pc[:, 3 * d + 2:3 * d + 3]
        ctr = (ctrf_ref if d == 0 else ctrb_ref)[0]
        cti = (ctif_ref if d == 0 else ctib_ref)[0]
        step = jnp.exp(ls)
        mag = jnp.exp(lre * step)
        a_re = mag * jnp.cos(lim * step)
        a_im = mag * jnp.sin(lim * step)
        den = lre * lre + lim * lim
        f_re = ((a_re - 1.0) * lre + a_im * lim) / den
        f_im = (a_im * lre - (a_re - 1.0) * lim) / den
        bb_re, bb_im = cmul(f_re, f_im, btr_ref[0], bti_ref[0])

        def powers(e, lre=lre, lim=lim, step=step):
            m = jnp.exp(lre * step * e)
            ang = lim * step * e
            return m * jnp.cos(ang), m * jnp.sin(ang)

        w_re, w_im = cmul(*powers((n - 1.0 - blk) if d == 0 else blk), bb_re, bb_im)
        wt_ref[0, S5_STATE * d:S5_STATE * (d + 1), :] = w_re
        wt_ref[0, 2 * S5_STATE + S5_STATE * d:2 * S5_STATE + S5_STATE * (d + 1), :] = w_im
        c_re, c_im = cmul(*powers((blk + 1.0) if d == 0 else (n - blk)), ctr, cti)
        v_ref[0, S5_STATE * d:S5_STATE * (d + 1), :] = c_re
        v_ref[0, 2 * S5_STATE + S5_STATE * d:2 * S5_STATE + S5_STATE * (d + 1), :] = -c_im
        e_re, e_im = cmul(*powers(blk if d == 0 else (n - 1.0 - blk)), ctr, cti)
        b16r_hi, b16r_lo = _split2(bb_re[:, 0:S5_GROUP_CH])
        b16i_hi, b16i_lo = _split2(bb_im[:, 0:S5_GROUP_CH])
        er_hi, er_lo = _split2(e_re)
        ei_hi, ei_lo = _split2(e_im)
        kr = _dot_tn(b16r_hi, er_hi) + (_dot_tn(b16r_hi, er_lo) + _dot_tn(b16r_lo, er_hi))
        ki = _dot_tn(b16i_hi, ei_hi) + (_dot_tn(b16i_hi, ei_lo) + _dot_tn(b16i_lo, ei_hi))
        kcat.append(kr - ki)

    for s in range(S5_CHUNK):
        sh_f = S5_GROUP_CH * s
        fwd = kcat[0] if s == 0 else pltpu.roll(kcat[0], sh_f, 1)
        fwd = jnp.where(lane >= sh_f, fwd, 0.0)
        sh_b = S5_VEC - S5_GROUP_CH * (S5_CHUNK - 1 - s)
        bwd = kcat[1] if sh_b == S5_VEC else pltpu.roll(kcat[1], sh_b, 1)
        bwd = jnp.where(lane < S5_GROUP_CH * (s + 1), bwd, 0.0)
        m_ref[0, S5_GROUP_CH * s:S5_GROUP_CH * (s + 1), :] = fwd + bwd

    pr = pr_ref[0]
    for d in (0, 1):
        lre, lim, ls = pr[3 * d:3 * d + 1, :], pr[3 * d + 1:3 * d + 2, :], pr[3 * d + 2:3 * d + 3, :]
        stp = jnp.exp(ls) * n
        mg = jnp.exp(lre * stp)
        ab_ref[0, d:d + 1, :] = mg * jnp.cos(lim * stp)
        ab_ref[0, 2 + d:3 + d, :] = mg * jnp.sin(lim * stp)


def _s5gen(pc, pr, btr, bti, ctrf, ctif, ctrb, ctib):
    g = pc.shape[0]
    blk3 = lambda shape: pl.BlockSpec((1,) + shape, lambda i: (i, 0, 0))
    big = (S5_STATE, S5_VEC)
    sq = (S5_VEC, S5_VEC)
    return pl.pallas_call(
        _s5gen_kernel,
        grid=(g,),
        in_specs=[blk3((S5_STATE, 8)), blk3((8, S5_STATE))] + [blk3(big)] * 6,
        out_specs=[blk3(sq), blk3(sq), blk3(sq), blk3((4, S5_STATE))],
        out_shape=[jax.ShapeDtypeStruct((g,) + sq, F32)] * 3 + [jax.ShapeDtypeStruct((g, 4, S5_STATE), F32)],
        compiler_params=_cparams(("arbitrary",)),
        name="s5gen",
    )(pc, pr, btr, bti, ctrf, ctif, ctrb, ctib)


def _s5_kernel(*refs, n_chunks, nb, with_output):
    if with_output:
        u_ref, m_ref, wt_ref, v_ref, ab_ref, x0_ref, y_ref, xf_ref, z_ref, cin_ref = refs
    else:
        u_ref, wt_ref, ab_ref, x0_ref, xf_ref, z_ref = refs
    wtb = wt_ref[0].astype(BF16)
    for bi in range(nb):
        z = _dot_nt(u_ref[0, bi * n_chunks:(bi + 1) * n_chunks, :], wtb)
        z_ref[0, pl.ds(bi, n_chunks, stride=nb), :] = z[:, 0:LANES]
        z_ref[1, pl.ds(bi, n_chunks, stride=nb), :] = z[:, LANES:2 * LANES]
    ab = ab_ref[0]
    ar, ai = ab[:, 0:LANES], ab[:, LANES:2 * LANES]
    is_f = lax.broadcasted_iota(I32, (nb, LANES), 1) < S5_STATE
    x0 = x0_ref[0]

    def body(i, carry):
        xr, xi = carry
        rf = pl.multiple_of(i * nb, nb)
        rb = pl.multiple_of((n_chunks - 1 - i) * nb, nb)
        if with_output:
            cin_ref[0, pl.ds(rf, nb), 0:S5_STATE] = xr[:, 0:S5_STATE]
            cin_ref[1, pl.ds(rf, nb), 0:S5_STATE] = xi[:, 0:S5_STATE]
            cin_ref[0, pl.ds(rb, nb), S5_STATE:LANES] = xr[:, S5_STATE:LANES]
            cin_ref[1, pl.ds(rb, nb), S5_STATE:LANES] = xi[:, S5_STATE:LANES]
        zr = jnp.where(is_f, z_ref[0, pl.ds(rf, nb), :], z_ref[0, pl.ds(rb, nb), :])
        zi = jnp.where(is_f, z_ref[1, pl.ds(rf, nb), :], z_ref[1, pl.ds(rb, nb), :])
        return ar * xr - ai * xi + zr, ar * xi + ai * xr + zi

    xr, xi = lax.fori_loop(0, n_chunks, body, (x0[:, 0:LANES], x0[:, LANES:2 * LANES]))
    xf_ref[0, :, 0:LANES] = xr
    xf_ref[0, :, LANES:2 * LANES] = xi
    if with_output:
        mb = m_ref[0].astype(BF16)
        vb = v_ref[0].astype(BF16)
        for bi in range(nb):
            rows = slice(bi * n_chunks, (bi + 1) * n_chunks)
            carried = jnp.concatenate([cin_ref[0, pl.ds(bi, n_chunks, stride=nb), :],
                                       cin_ref[1, pl.ds(bi, n_chunks, stride=nb), :]], axis=1).astype(BF16)
            y_ref[0, rows, :] = _dot(u_ref[0, rows, :], mb) + _dot(carried, vb)


def _s5(uvec, m, wt, v, ab, x0, nb, with_output):
    g, rows, _ = uvec.shape
    n_chunks = rows // nb
    blk3 = lambda shape: pl.BlockSpec((1,) + shape, lambda i: (i, 0, 0))
    sq = (S5_VEC, S5_VEC)
    st = (nb, S5_VEC)
    if with_output:
        args = (uvec, m, wt, v, ab, x0)
        in_specs = [blk3((rows, S5_VEC)), blk3(sq), blk3(sq), blk3(sq), blk3((1, S5_VEC)), blk3(st)]
        out_specs = [blk3((rows, S5_VEC)), blk3(st)]
        out_shape = [jax.ShapeDtypeStruct((g, rows, S5_VEC), F32), jax.ShapeDtypeStruct((g,) + st, F32)]
        scratch = [pltpu.VMEM((S5_VEC // LANES, rows, LANES), F32), pltpu.VMEM((S5_VEC // LANES, rows, LANES), F32)]
    else:
        args = (uvec, wt, ab, x0)
        in_specs = [blk3((rows, S5_VEC)), blk3(sq), blk3((1, S5_VEC)), blk3(st)]
        out_specs = [blk3(st)]
        out_shape = [jax.ShapeDtypeStruct((g,) + st, F32)]
        scratch = [pltpu.VMEM((S5_VEC // LANES, rows, LANES), F32)]
    return pl.pallas_call(
        functools.partial(_s5_kernel, n_chunks=n_chunks, nb=nb, with_output=with_output),
        grid=(g,),
        in_specs=in_specs,
        out_specs=out_specs,
        out_shape=out_shape,
        scratch_shapes=scratch,
        compiler_params=_cparams(("arbitrary",)),
        name="s5_out" if with_output else "s5_ctx",
    )(*args)


def _post_kernel(x_ref, o_ref, go_ref, u_ref, yv_ref, g1_ref, sh2_ref, sc2_ref, g2_ref,
                 gn_ref, d_ref, gw_ref, gb_ref, wo_ref, n2_ref, rwh_ref, rwl_ref, sgu_ref, sd_ref,
                 base_ref, hrow_ref, lg_ref, y_ref, *, tm):
    nc = tm // S5_CHUNK
    gpt = LANES // S5_GROUP_CH
    masks = _group_lane_masks(nc)
    for s in range(S5_CHUNK):
        half, j = divmod(s, gpt)
        for t in range(D_S5 // LANES):
            acc = None
            for gl in range(gpt):
                piece = _move_group(yv_ref[t * gpt + gl, :, LANES * half:LANES * (half + 1)], j, gl)
                acc = piece if acc is None else jnp.where(masks[gl], piece, acc)
            y_ref[t, pl.ds(s, nc, stride=S5_CHUNK), :] = acc
    o = o_ref[0]
    gn = gn_ref[...]
    heads = [_rms(o[:, GLA_DV_HEAD * h:GLA_DV_HEAD * (h + 1)], gn) for h in range(GLA_HEADS)]
    gla_out = jnp.concatenate(heads, axis=1) * _silu(go_ref[0])
    yy = jnp.concatenate([y_ref[t] for t in range(D_S5 // LANES)], axis=1) + d_ref[...] * u_ref[0]
    z = 0.5 * yy * (1.0 + jnp.tanh(0.7978845608028654 * (yy + 0.044715 * (yy * yy * yy))))
    s5_out = z * jax.nn.sigmoid(_dot(z.astype(BF16), gw_ref[...]) + gb_ref[...])
    mix = jnp.concatenate([gla_out, s5_out], axis=1).astype(BF16)
    x1 = x_ref[0] + g1_ref[0] * _dot(mix, wo_ref[...])
    h2 = _rms(x1, n2_ref[...]) * (1.0 + sc2_ref[0]) + sh2_ref[0]
    lg_ref[0] = _dot3(h2, rwh_ref[...], rwl_ref[...])
    hb = h2.astype(BF16)
    gu = _dot(hb, sgu_ref[...])
    hid = _silu(gu[:, 0:D_SHARED]) * gu[:, D_SHARED:2 * D_SHARED]
    base_ref[0] = x1 + g2_ref[0] * _dot(hid.astype(BF16), sd_ref[...])
    for s in range(ROW_TILES):
        hrow_ref[pl.ds(s, tm, stride=ROW_TILES), :] = h2[:, LANES * s:LANES * (s + 1)]


def _post(x, o, go, u, yvec, g1, sh2, sc2, g2, gn, d, gw, gb, wo, n2, rwh, rwl, sgu, sd):
    b, l, _ = x.shape
    tm = 256
    nt = l // tm
    row = lambda bi, i: (bi, i, 0)
    mod = lambda bi, i: (bi, 0, 0)
    full = lambda bi, i: (0, 0)
    ws = (gn, d, gw, gb, wo, n2, rwh, rwl, sgu, sd)
    return pl.pallas_call(
        functools.partial(_post_kernel, tm=tm),
        grid=(b, nt),
        in_specs=[pl.BlockSpec((1, tm, D_MODEL), row)]
                 + [pl.BlockSpec((1, tm, 512), row)] * 3
                 + [pl.BlockSpec((S5_GROUPS, tm // S5_CHUNK, S5_VEC), lambda bi, i: (0, bi * nt + i, 0))]
                 + [pl.BlockSpec((1, 1, D_MODEL), mod)] * 4
                 + [pl.BlockSpec(w.shape, full) for w in ws],
        out_specs=[pl.BlockSpec((1, tm, D_MODEL), row),
                   pl.BlockSpec((tm * ROW_TILES, LANES), lambda bi, i: (bi * nt + i, 0)),
                   pl.BlockSpec((1, tm, N_EXPERTS), row)],
        out_shape=[jax.ShapeDtypeStruct((b, l, D_MODEL), F32),
                   jax.ShapeDtypeStruct((b * l * ROW_TILES, LANES), F32),
                   jax.ShapeDtypeStruct((b, l, N_EXPERTS), F32)],
        scratch_shapes=[pltpu.VMEM((D_S5 // LANES, tm, LANES), F32)],
        compiler_params=_cparams(("arbitrary", "arbitrary")),
        name="post",
    )(x, o, go, u, yvec, g1, sh2, sc2, g2, *ws)


def _route_kernel(lg_ref, rb_ref, e_ref, w_ref, r_ref, cnt_ref, run_ref, *, tm):
    @pl.when(pl.program_id(0) == 0)
    def _():
        run_ref[...] = jnp.zeros_like(run_ref)

    neg = -jnp.inf
    gsz = N_EXPERTS // N_EXPERT_GROUPS
    s = jax.nn.sigmoid(lg_ref[...].T)
    biased = s + rb_ref[...]
    row = lax.broadcasted_iota(I32, (N_EXPERTS, tm), 0).astype(F32)

    def first_max(m, idx):
        mx = jnp.max(m, axis=0, keepdims=True)
        ix = jnp.min(jnp.where(m == mx, idx, float(N_EXPERTS)), axis=0, keepdims=True)
        return mx, ix

    grow = lax.broadcasted_iota(I32, (gsz, tm), 0).astype(F32)
    gs = []
    for g in range(N_EXPERT_GROUPS):
        m, idx = biased[gsz * g:gsz * (g + 1), :], grow + float(gsz * g)
        m1, i1 = first_max(m, idx)
        gs.append(m1 + jnp.max(jnp.where(idx == i1, neg, m), axis=0, keepdims=True))
    kept = []
    for g in range(N_EXPERT_GROUPS):
        ahead = jnp.zeros((1, tm), F32)
        for j in range(N_EXPERT_GROUPS):
            if j < g:
                ahead = ahead + jnp.where(gs[j] >= gs[g], 1.0, 0.0)
            elif j > g:
                ahead = ahead + jnp.where(gs[j] > gs[g], 1.0, 0.0)
        kept.append(jnp.where(ahead < float(TOPK_GROUPS), biased[gsz * g:gsz * (g + 1), :], neg))
    masked = jnp.concatenate(kept, axis=0)

    onehot = jnp.zeros((N_EXPERTS, tm), F32)
    ids, ws = [], []
    for _ in range(TOP_K):
        _, ik = first_max(masked, row)
        hit = row == ik
        ids.append(ik)
        ws.append(jnp.sum(jnp.where(hit, s, 0.0), axis=0, keepdims=True))
        onehot = onehot + jnp.where(hit, 1.0, 0.0)
        masked = jnp.where(hit, neg, masked)
    wsum = ws[0]
    for k in range(1, TOP_K):
        wsum = wsum + ws[k]

    ss = lax.broadcasted_iota(I32, (tm, tm), 0)
    tt = lax.broadcasted_iota(I32, (tm, tm), 1)
    earlier = jnp.where(ss < tt, 1.0, 0.0).astype(BF16)
    ohb = onehot.astype(BF16)
    run = run_ref[...]
    pos = _dot(ohb, earlier) + run[:, 0:1]
    e_ref[...] = jnp.concatenate(ids, axis=0).astype(I32)
    w_ref[...] = jnp.concatenate([w / wsum * ROUTE_SCALE for w in ws], axis=0)
    r_ref[...] = jnp.concatenate(
        [jnp.sum(jnp.where(row == ids[k], pos, 0.0), axis=0, keepdims=True) for k in range(TOP_K)], axis=0).astype(I32)
    run = run + _dot(ohb, jnp.ones((tm, LANES), BF16))
    run_ref[...] = run
    cnt_ref[...] = run.astype(I32)


def _route(logits, rb):
    t = logits.shape[0]
    tm = min(512, t)
    col = lambda i: (0, i)
    fixed = lambda i: (0, 0)
    return pl.pallas_call(
        functools.partial(_route_kernel, tm=tm),
        grid=(t // tm,),
        in_specs=[pl.BlockSpec((tm, N_EXPERTS), lambda i: (i, 0)), pl.BlockSpec((N_EXPERTS, 1), fixed)],
        out_specs=[pl.BlockSpec((TOP_K, tm), col)] * 3 + [pl.BlockSpec((N_EXPERTS, LANES), fixed)],
        out_shape=[jax.ShapeDtypeStruct((TOP_K, t), I32), jax.ShapeDtypeStruct((TOP_K, t), F32),
                   jax.ShapeDtypeStruct((TOP_K, t), I32), jax.ShapeDtypeStruct((N_EXPERTS, LANES), I32)],
        scratch_shapes=[pltpu.VMEM((N_EXPERTS, LANES), F32)],
        compiler_params=_cparams(("arbitrary",)),
        name="route",
    )(logits, rb)


def _n_blocks_max(n_assign):
    return -(-(n_assign + N_EXPERTS * (EXPERT_BLOCK - 1)) // EXPERT_BLOCK)


def _plan_kernel(cnt_ref, ps_ref, blk_ref, nv_ref, *, nbp):
    cnt = cnt_ref[...]
    nb = lax.shift_right_logical(cnt + (EXPERT_BLOCK - 1), 8).astype(F32)
    nb8 = jnp.broadcast_to(nb, (SUBLANES, N_EXPERTS))
    nb_hi, nb_lo = _split2(nb8)
    ii = lax.broadcasted_iota(I32, (N_EXPERTS, N_EXPERTS), 0)
    jj = lax.broadcasted_iota(I32, (N_EXPERTS, N_EXPERTS), 1)
    upto = jnp.where(ii <= jj, 1.0, 0.0).astype(BF16)
    cum = (_dot(nb_hi, upto) + _dot(nb_lo, upto))[0:1, :]
    ps_ref[...] = ((cum - nb) * float(EXPERT_BLOCK)).astype(I32)
    bi = lax.broadcasted_iota(I32, (nbp, N_EXPERTS), 0).astype(F32)
    owner = jnp.sum(jnp.where(cum <= bi, 1.0, 0.0), axis=-1, keepdims=True)
    blk_ref[...] = jnp.minimum(owner, float(N_EXPERTS - 1)).astype(I32)
    nv_ref[...] = cum[:, N_EXPERTS - 1:N_EXPERTS].astype(I32)


def _plan(cnt, nbp):
    return pl.pallas_call(
        functools.partial(_plan_kernel, nbp=nbp),
        out_shape=[jax.ShapeDtypeStruct((1, N_EXPERTS), I32), jax.ShapeDtypeStruct((nbp, 1), I32),
                   jax.ShapeDtypeStruct((1, 1), I32)],
        name="plan",
    )(cnt)


def _row_copy(src, src_row, dst, dst_row, sem):
    return pltpu.make_async_copy(src.at[pl.ds(pl.multiple_of(src_row * ROW_TILES, ROW_TILES), ROW_TILES)],
                                 dst.at[pl.ds(pl.multiple_of(dst_row * ROW_TILES, ROW_TILES), ROW_TILES)], sem)


def _start_rows(n, copy):
    def body(g, carry):
        for k in range(TOP_K):
            copy(g * TOP_K + k).start(priority=k % 2)
        return carry

    lax.fori_loop(0, n // TOP_K, body, 0)


def _wait_rows(n, copy):
    def body(j, carry):
        copy(j).wait()
        return carry

    lax.fori_loop(0, n, body, 0, unroll=8)


def _dispatch_kernel(e_ref, r_ref, ps_ref, cnt_ref, h_ref, xs_ref, zero_ref, sem, zsem, *, tm):
    step = pl.program_id(0)
    pad_sizes = [1 << p for p in range(EXPERT_BLOCK.bit_length() - 2, -1, -1)]

    def pad_copy(start_row, size):
        return pltpu.make_async_copy(
            zero_ref.at[pl.ds(0, size * ROW_TILES)],
            xs_ref.at[pl.ds(pl.multiple_of(start_row * ROW_TILES, ROW_TILES), size * ROW_TILES)], zsem)

    @pl.when(step == 0)
    def _():
        zero_ref[...] = jnp.zeros_like(zero_ref)
        for wait in (False, True):
            def pad_body(e, carry, wait=wait):
                cnt = cnt_ref[e]
                pad = (-cnt) & (EXPERT_BLOCK - 1)
                row = ps_ref[e] + cnt
                for size in pad_sizes:
                    @pl.when((pad & size) != 0)
                    def _(row=row, size=size):
                        cp = pad_copy(row, size)
                        cp.wait() if wait else cp.start()
                    row = row + (pad & size)
                return carry
            lax.fori_loop(0, N_EXPERTS, pad_body, 0)

    def copy(j):
        dst = ps_ref[e_ref[j]] + r_ref[j]
        return _row_copy(h_ref, lax.shift_right_logical(j, 3), xs_ref, dst, sem)

    _start_rows(tm * TOP_K, copy)
    _wait_rows(tm * TOP_K, copy)


def _dispatch(e_flat, r_flat, pstart, cnt, hrows, n_rows):
    t = e_flat.shape[0] // TOP_K
    tm = min(512, t)
    smem_blk = pl.BlockSpec((tm * TOP_K,), lambda i: (i,), memory_space=pltpu.SMEM)
    smem_all = pl.BlockSpec((N_EXPERTS,), lambda i: (0,), memory_space=pltpu.SMEM)
    return pl.pallas_call(
        functools.partial(_dispatch_kernel, tm=tm),
        grid=(t // tm,),
        in_specs=[smem_blk, smem_blk, smem_all, smem_all,
                  pl.BlockSpec((tm * ROW_TILES, LANES), lambda i: (i, 0))],
        out_specs=pl.BlockSpec(memory_space=pl.ANY),
        out_shape=jax.ShapeDtypeStruct((n_rows * ROW_TILES, LANES), F32),
        scratch_shapes=[pltpu.VMEM((EXPERT_BLOCK // 2 * ROW_TILES, LANES), F32),
                        pltpu.SemaphoreType.DMA, pltpu.SemaphoreType.DMA],
        compiler_params=_cparams(("arbitrary",)),
        name="dispatch",
    )(e_flat, r_flat, pstart, cnt, hrows)


def _experts_kernel(blk_ref, nv_ref, xs_ref, wg_ref, wu_ref, wd_ref, ys_ref, wgb_ref, wub_ref, wdb_ref):
    i = pl.program_id(0)
    nv = nv_ref[0]

    @pl.when(i < nv)
    def _():
        prev = blk_ref[jnp.maximum(i - 1, 0)]

        @pl.when((i == 0) | (blk_ref[i] != prev))
        def _():
            wgb_ref[...] = wg_ref[0].astype(BF16)
            wub_ref[...] = wu_ref[0].astype(BF16)
            wdb_ref[...] = wd_ref[0].astype(BF16)

        xb = jnp.concatenate(
            [xs_ref[pl.ds(s, EXPERT_BLOCK, stride=ROW_TILES), :] for s in range(ROW_TILES)], axis=1).astype(BF16)
        hid = _silu(_dot(xb, wgb_ref[...])) * _dot(xb, wub_ref[...])
        y = _dot(hid.astype(BF16), wdb_ref[...])
        for s in range(ROW_TILES):
            ys_ref[pl.ds(s, EXPERT_BLOCK, stride=ROW_TILES), :] = y[:, LANES * s:LANES * (s + 1)]


def _experts(blk, nv, xs, wg, wu, wd, nbp):
    rows = EXPERT_BLOCK * ROW_TILES
    cur = lambda i, blk_ref, nv_ref: jnp.minimum(i, nv_ref[0] - 1)
    xmap = lambda i, blk_ref, nv_ref: (cur(i, blk_ref, nv_ref), 0)
    wmap = lambda i, blk_ref, nv_ref: (blk_ref[cur(i, blk_ref, nv_ref)], 0, 0)
    return pl.pallas_call(
        _experts_kernel,
        grid_spec=pltpu.PrefetchScalarGridSpec(
            num_scalar_prefetch=2,
            grid=(nbp,),
            in_specs=[pl.BlockSpec((rows, LANES), xmap),
                      pl.BlockSpec((1, D_MODEL, D_EXPERT), wmap),
                      pl.BlockSpec((1, D_MODEL, D_EXPERT), wmap),
                      pl.BlockSpec((1, D_EXPERT, D_MODEL), wmap)],
            out_specs=pl.BlockSpec((rows, LANES), xmap),
            scratch_shapes=[pltpu.VMEM((D_MODEL, D_EXPERT), BF16), pltpu.VMEM((D_MODEL, D_EXPERT), BF16),
                            pltpu.VMEM((D_EXPERT, D_MODEL), BF16)]),
        out_shape=jax.ShapeDtypeStruct(xs.shape, F32),
        compiler_params=_cparams(("arbitrary",)),
        name="experts",
    )(blk, nv, xs, wg, wu, wd)


def _combine_kernel(e_ref, r_ref, w_ref, ps_ref, ys_ref, base_ref, g2_ref, fg_ref, out_ref,
                    buf_ref, acc_ref, sem, *, tm):
    def copy(j):
        src = ps_ref[e_ref[j]] + r_ref[j]
        return _row_copy(ys_ref, src, buf_ref, j, sem)

    _start_rows(tm * TOP_K, copy)
    _wait_rows(tm * TOP_K, copy)

    def token_body(t, carry):
        j0 = t * TOP_K
        acc = jnp.zeros((ROW_TILES, LANES), F32)
        for k in range(TOP_K):
            acc = acc + w_ref[j0 + k] * buf_ref[pl.ds(pl.multiple_of((j0 + k) * ROW_TILES, ROW_TILES), ROW_TILES), :]
        acc_ref[pl.ds(pl.multiple_of(t * ROW_TILES, ROW_TILES), ROW_TILES), :] = acc
        return carry

    lax.fori_loop(0, tm, token_body, 0)
    routed = jnp.concatenate([acc_ref[pl.ds(s, tm, stride=ROW_TILES), :] for s in range(ROW_TILES)], axis=1)
    out_ref[0] = _rms(base_ref[0] + g2_ref[0] * routed, fg_ref[...])


def _combine(e_flat, r_flat, w_flat, pstart, ys, base, g2, fg):
    b, l, _ = base.shape
    tm = 256
    nt = l // tm
    smem_blk = pl.BlockSpec((tm * TOP_K,), lambda bi, i: (bi * nt + i,), memory_space=pltpu.SMEM)
    return pl.pallas_call(
        functools.partial(_combine_kernel, tm=tm),
        grid=(b, nt),
        in_specs=[smem_blk, smem_blk, smem_blk,
                  pl.BlockSpec((N_EXPERTS,), lambda bi, i: (0,), memory_space=pltpu.SMEM),
                  pl.BlockSpec(memory_space=pl.ANY),
                  pl.BlockSpec((1, tm, D_MODEL), lambda bi, i: (bi, i, 0)),
                  pl.BlockSpec((1, 1, D_MODEL), lambda bi, i: (bi, 0, 0)),
                  pl.BlockSpec((1, D_MODEL), lambda bi, i: (0, 0))],
        out_specs=pl.BlockSpec((1, tm, D_MODEL), lambda bi, i: (bi, i, 0)),
        out_shape=jax.ShapeDtypeStruct((b, l, D_MODEL), F32),
        scratch_shapes=[pltpu.VMEM((tm * TOP_K * ROW_TILES, LANES), F32),
                        pltpu.VMEM((tm * ROW_TILES, LANES), F32),
                        pltpu.SemaphoreType.DMA],
        compiler_params=_cparams(("arbitrary", "arbitrary")),
        name="combine",
    )(e_flat, r_flat, w_flat, pstart, ys, base, g2, fg)


def _mixer_inputs(h, shift, scale, gain, wm, wl, wa, ba):
    return _proj(h, shift, scale, gain, wm, wl, wa, ba)


def kernel(x, c, ctx, c_ctx, ada_w, ada_b, norm1_g, norm2_g, w_in, gla_wa_f, gla_ba_f, gla_wa_b, gla_ba_b, gla_norm_g, s5_lam_re_f, s5_lam_im_f, s5_log_step_f, s5_lam_re_b, s5_lam_im_b, s5_log_step_b, s5_b_re, s5_b_im, s5_c_re_f, s5_c_im_f, s5_c_re_b, s5_c_im_b, s5_d, s5_glu_w, s5_glu_b, w_out, router_w, router_b, exp_w_gate, exp_w_up, exp_w_down, sh_w_gate, sh_w_up, sh_w_down, final_norm_g):
    b, l, d = x.shape
    i = 0

    rows = -(-(b + 1) // SUBLANES) * SUBLANES
    cs = jnp.zeros((rows, d), F32).at[:b].set(c).at[b].set(c_ctx)
    mod = _adaln(cs, ada_w[i], ada_b[i][None, :])
    sh1, sc1, g1, sh2, sc2, g2 = [mod[:b, d * j:d * (j + 1)][:, None, :] for j in range(6)]
    csh1, csc1 = [jnp.broadcast_to(mod[b, d * j:d * (j + 1)][None, None, :], (b, 1, d)) for j in range(2)]

    w = w_in[i]
    o1, o2, o3, o4, o5, o6 = 256, 512, 1024, 1536, 1552, 1568
    wm = jnp.concatenate([w[:, :o4], w[:, o6:]], axis=1).astype(BF16)
    wl = jnp.zeros((d, LANES), F32).at[:, :2 * GLA_GATE_RANK].set(w[:, o4:o6]).astype(BF16)
    wa = jnp.zeros((LANES, 2 * GLA_DK), F32)
    wa = wa.at[:GLA_GATE_RANK, :GLA_DK].set(gla_wa_f[i]).at[GLA_GATE_RANK:2 * GLA_GATE_RANK, GLA_DK:].set(gla_wa_b[i])
    wa = wa.astype(BF16)
    ba = jnp.concatenate([gla_ba_f[i], gla_ba_b[i]])[None, :]
    n1 = norm1_g[i][None, :]

    pcols = jnp.stack([s5_lam_re_f[i], s5_lam_im_f[i],
                       jnp.broadcast_to(s5_log_step_f[i][:, None], (S5_GROUPS, S5_STATE)),
                       s5_lam_re_b[i], s5_lam_im_b[i],
                       jnp.broadcast_to(s5_log_step_b[i][:, None], (S5_GROUPS, S5_STATE)),
                       jnp.zeros((S5_GROUPS, S5_STATE), F32), jnp.zeros((S5_GROUPS, S5_STATE), F32)], axis=-1)
    prows = pcols.transpose(0, 2, 1)
    tile_b = lambda t: jnp.tile(t, (1, 1, S5_CHUNK))
    tile_c = lambda t: jnp.tile(t.transpose(0, 2, 1), (1, 1, S5_CHUNK))
    m_op, wt_op, v_op, ab4 = _s5gen(pcols, prows, tile_b(s5_b_re[i]), tile_b(s5_b_im[i]),
                                    tile_c(s5_c_re_f[i]), tile_c(s5_c_im_f[i]),
                                    tile_c(s5_c_re_b[i]), tile_c(s5_c_im_b[i]))
    ab = ab4.reshape(S5_GROUPS, 1, 4 * S5_STATE)

    cq, ck, cv, _, _, claf, clab, cuv = _proj(ctx, csh1, csc1, n1, wm, wl, wa, ba)
    zero_state = jnp.zeros((b, GLA_HEADS, GLA_DV_HEAD, LANES), F32)
    gsf, gsb = _gla(cq, ck, cv, claf, clab, zero_state, zero_state, with_output=False)
    (x0,) = _s5(cuv, None, wt_op, None, ab, jnp.zeros((S5_GROUPS, b, S5_VEC), F32), b, with_output=False)

    q, k, v, go, u, laf, lab, uv = _proj(x, sh1, sc1, n1, wm, wl, wa, ba)
    o, _, _ = _gla(q, k, v, laf, lab, gsf, gsb, with_output=True)
    yvec, _ = _s5(uv, m_op, wt_op, v_op, ab, x0, b, with_output=True)

    rw_hi = router_w[i].astype(BF16)
    rw_lo = (router_w[i] - rw_hi.astype(F32)).astype(BF16)
    base, hrows, logits = _post(
        x, o, go, u, yvec, g1, sh2, sc2, g2,
        gla_norm_g[i][None, :], s5_d[i][None, :], s5_glu_w[i].astype(BF16),
        s5_glu_b[i][None, :], w_out[i].astype(BF16), norm2_g[i][None, :], rw_hi, rw_lo,
        jnp.concatenate([sh_w_gate[i], sh_w_up[i]], axis=1).astype(BF16), sh_w_down[i].astype(BF16))

    t = b * l
    top_e, wts, rank, cnt = _route(logits.reshape(t, N_EXPERTS), router_b[i][:, None])
    cnt_flat = cnt[:, 0]
    nbp = -(-_n_blocks_max(t * TOP_K) // SUBLANES) * SUBLANES
    pstart, blk, nv = _plan(cnt_flat[None, :], nbp)
    e_flat, r_flat, w_flat = top_e.T.reshape(-1), rank.T.reshape(-1), wts.T.reshape(-1)
    ps_flat = pstart.reshape(-1)
    xs = _dispatch(e_flat, r_flat, ps_flat, cnt_flat, hrows, nbp * EXPERT_BLOCK)
    assert exp_w_gate.shape[0] == 1, "single-layer block"
    ys = _experts(blk.reshape(-1), nv.reshape(-1), xs, exp_w_gate.reshape(exp_w_gate.shape[1:]),
                  exp_w_up.reshape(exp_w_up.shape[1:]), exp_w_down.reshape(exp_w_down.shape[1:]), nbp)
    return _combine(e_flat, r_flat, w_flat, ps_flat, ys, base, g2, final_norm_g[None, :])
```

```python
import functools

import jax
import jax.numpy as jnp
from jax import lax
from jax.experimental import pallas as pl
from jax.experimental.pallas import tpu as pltpu

F32 = jnp.float32
BF16 = jnp.bfloat16
I32 = jnp.int32

D_MODEL = 1024
GLA_HEADS = 4
GLA_DK_HEAD = 64
GLA_DV_HEAD = 128
GLA_DK = 256
GLA_DV = 512
GLA_GATE_RANK = 16
GLA_GATE_TAU = 16.0
GLA_CHUNK = 64
D_S5 = 512
S5_GROUP_CH = 16
S5_GROUPS = 32
S5_STATE = 64
S5_CHUNK = 16
S5_VEC = S5_CHUNK * S5_GROUP_CH
N_EXPERTS = 256
TOP_K = 8
N_EXPERT_GROUPS = 8
TOPK_GROUPS = 4
D_EXPERT = 256
D_SHARED = 256
ROUTE_SCALE = 2.5
EPS = 1e-6

LANES = 128
SUBLANES = 8
ROW_TILES = D_MODEL // LANES
EXPERT_BLOCK = 256
VMEM_LIMIT = 56 * 1024 * 1024


def _cparams(sem):
    return pltpu.CompilerParams(dimension_semantics=sem, vmem_limit_bytes=VMEM_LIMIT)


def _dot(a, b):
    return jnp.dot(a, b, preferred_element_type=F32)


def _dot_nt(a, b):
    return lax.dot_general(a, b, (((1,), (1,)), ((), ())), preferred_element_type=F32)


def _dot_tn(a, b):
    return lax.dot_general(a, b, (((0,), (0,)), ((), ())), preferred_element_type=F32)


def _split2(x):
    hi = x.astype(BF16)
    lo = (x - hi.astype(F32)).astype(BF16)
    return hi, lo


def _dot3(a, b_hi, b_lo):
    a_hi, a_lo = _split2(a)
    return _dot(a_hi, b_hi) + (_dot(a_hi, b_lo) + _dot(a_lo, b_hi))


def _silu(x):
    return x * jax.nn.sigmoid(x)


def _rms(x, g):
    return x * lax.rsqrt(jnp.mean(x * x, axis=-1, keepdims=True) + EPS) * g


def _adaln_kernel(c_ref, w_ref, b_ref, o_ref):
    s = _silu(c_ref[...])
    w_hi, w_lo = _split2(w_ref[...])
    o_ref[...] = _dot3(s, w_hi, w_lo) + b_ref[...]


def _adaln(cs, w, b):
    rows, n = cs.shape[0], w.shape[1]
    tn = 1024
    return pl.pallas_call(
        _adaln_kernel,
        grid=(n // tn,),
        in_specs=[pl.BlockSpec((rows, D_MODEL), lambda j: (0, 0)),
                  pl.BlockSpec((D_MODEL, tn), lambda j: (0, j)),
                  pl.BlockSpec((1, tn), lambda j: (0, j))],
        out_specs=pl.BlockSpec((rows, tn), lambda j: (0, j)),
        out_shape=jax.ShapeDtypeStruct((rows, n), F32),
        compiler_params=_cparams(("arbitrary",)),
        name="adaln",
    )(cs, w, b)


def _group_lane_masks(rows):
    grp = lax.shift_right_logical(lax.broadcasted_iota(I32, (rows, LANES), 1), 4)
    return [grp == j for j in range(LANES // S5_GROUP_CH)]


def _move_group(x, src, dst):
    shift = ((dst - src) * S5_GROUP_CH) % LANES
    return pltpu.roll(x, shift, 1) if shift else x


def _proj_kernel(x_ref, sh_ref, sc_ref, g_ref, wm_ref, wl_ref, wa_ref, ba_ref,
                 q_ref, k_ref, v_ref, go_ref, u_ref, laf_ref, lab_ref, uv_ref, ut_ref, *, tm):
    h = _rms(x_ref[0], g_ref[...]) * (1.0 + sc_ref[0]) + sh_ref[0]
    hb = h.astype(BF16)
    q_ref[0] = _dot(hb, wm_ref[:, 0:256]) * (GLA_DK_HEAD ** -0.5)
    k_ref[0] = _dot(hb, wm_ref[:, 256:512])
    v_ref[0] = _dot(hb, wm_ref[:, 512:1024])
    go_ref[0] = _dot(hb, wm_ref[:, 1024:1536])
    u = _dot(hb, wm_ref[:, 1536:2048])
    u_ref[0] = u
    for t in range(D_S5 // LANES):
        ut_ref[t] = u[:, LANES * t:LANES * (t + 1)]
    lr = _dot(hb, wl_ref[...])
    pre = _dot(lr.astype(BF16), wa_ref[...]) + ba_ref[...]
    la = (jnp.minimum(pre, 0.0) - jnp.log1p(jnp.exp(-jnp.abs(pre)))) * (1.0 / GLA_GATE_TAU)
    laf_ref[0] = la[:, 0:GLA_DK]
    lab_ref[0] = la[:, GLA_DK:2 * GLA_DK]
    nc = tm // S5_CHUNK
    gpt = LANES // S5_GROUP_CH
    masks = _group_lane_masks(nc)
    for t in range(D_S5 // LANES):
        steps = [ut_ref[t, pl.ds(s, nc, stride=S5_CHUNK), :] for s in range(S5_CHUNK)]
        for gl in range(gpt):
            for half in range(S5_VEC // LANES):
                acc = None
                for j in range(gpt):
                    piece = _move_group(steps[half * gpt + j], gl, j)
                    acc = piece if acc is None else jnp.where(masks[j], piece, acc)
                uv_ref[t * gpt + gl, :, LANES * half:LANES * (half + 1)] = acc.astype(BF16)


def _proj(x, shift, scale, gain, wm, wl, wa, ba):
    b, l, _ = x.shape
    tm = min(512, l)
    nt = l // tm
    row = lambda bi, i: (bi, i, 0)
    mod = lambda bi, i: (bi, 0, 0)
    full = lambda bi, i: (0, 0)
    widths = (GLA_DK, GLA_DK, GLA_DV, GLA_DV, D_S5, GLA_DK, GLA_DK)
    return pl.pallas_call(
        functools.partial(_proj_kernel, tm=tm),
        grid=(b, nt),
        in_specs=[pl.BlockSpec((1, tm, D_MODEL), row),
                  pl.BlockSpec((1, 1, D_MODEL), mod),
                  pl.BlockSpec((1, 1, D_MODEL), mod),
                  pl.BlockSpec((1, D_MODEL), full),
                  pl.BlockSpec(wm.shape, full),
                  pl.BlockSpec(wl.shape, full),
                  pl.BlockSpec(wa.shape, full),
                  pl.BlockSpec(ba.shape, full)],
        out_specs=[pl.BlockSpec((1, tm, w), row) for w in widths]
                  + [pl.BlockSpec((S5_GROUPS, tm // S5_CHUNK, S5_VEC), lambda bi, i: (0, bi * nt + i, 0))],
        out_shape=[jax.ShapeDtypeStruct((b, l, w), F32) for w in widths]
                  + [jax.ShapeDtypeStruct((S5_GROUPS, b * l // S5_CHUNK, S5_VEC), BF16)],
        scratch_shapes=[pltpu.VMEM((D_S5 // LANES, tm, LANES), F32)],
        compiler_params=_cparams(("arbitrary", "arbitrary")),
        name="proj",
    )(x, shift, scale, gain, wm, wl, wa, ba)


def _gla_kernel(*refs, n_chunks, with_output):
    if with_output:
        q_ref, k_ref, v_ref, laf_ref, lab_ref, s0f_ref, s0b_ref, o_ref, sf_ref, sb_ref, st_ref = refs
    else:
        q_ref, k_ref, v_ref, laf_ref, lab_ref, s0f_ref, s0b_ref, sf_ref, sb_ref, st_ref = refs
        o_ref = None
    c = GLA_CHUNK
    row = lax.broadcasted_iota(I32, (c, c), 0)
    col = lax.broadcasted_iota(I32, (c, c), 1)
    lane = lax.broadcasted_iota(I32, (c, LANES), 1)
    for direction in (0, 1):
        la_ref = laf_ref if direction == 0 else lab_ref
        s0_ref = s0f_ref if direction == 0 else s0b_ref
        sfin_ref = sf_ref if direction == 0 else sb_ref
        tri = (row >= col) if direction == 0 else (row <= col)
        trib = jnp.where(tri, 1.0, 0.0).astype(BF16)
        st_ref[...] = s0_ref[0]

        def body(ci, carry, direction=direction, la_ref=la_ref, tri=tri, trib=trib):
            idx = ci if direction == 0 else n_chunks - 1 - ci
            r0 = pl.multiple_of(idx * c, c)
            q = q_ref[0, pl.ds(r0, c), :]
            k = k_ref[0, pl.ds(r0, c), :]
            v = v_ref[0, pl.ds(r0, c), :]
            la_hi, la_lo = _split2(la_ref[0, pl.ds(r0, c), :])
            cum = _dot(trib, la_hi) + _dot(trib, la_lo)
            tot = cum[c - 1:c, :] if direction == 0 else cum[0:1, :]
            qd = q * jnp.exp(cum)
            ki = k * jnp.exp(-cum)
            ks = k * jnp.exp(tot - cum)
            dec = jnp.exp(tot)
            for h in range(GLA_HEADS):
                pair = slice(LANES * (h // 2), LANES * (h // 2) + LANES)
                own = (lane >= GLA_DK_HEAD * (h % 2)) & (lane < GLA_DK_HEAD * (h % 2) + GLA_DK_HEAD)
                vb = v[:, GLA_DV_HEAD * h:GLA_DV_HEAD * (h + 1)].astype(BF16)
                st = st_ref[h]
                if with_output:
                    qb = qd[:, pair].astype(BF16)
                    kib = jnp.where(own, ki[:, pair], 0.0).astype(BF16)
                    sc = jnp.where(tri, _dot_nt(qb, kib), 0.0)
                    o = _dot(sc.astype(BF16), vb) + _dot_nt(qb, st.astype(BF16))
                    osl = (0, pl.ds(r0, c), slice(GLA_DV_HEAD * h, GLA_DV_HEAD * (h + 1)))
                    if direction == 0:
                        o_ref[osl] = o
                    else:
                        o_ref[osl] = o_ref[osl] + o
                ksb = jnp.where(own, ks[:, pair], 0.0).astype(BF16)
                st_ref[h] = st * dec[:, pair] + _dot_tn(vb, ksb)
            return carry

        lax.fori_loop(0, n_chunks, body, 0)
        sfin_ref[0] = st_ref[...]


def _gla(q, k, v, laf, lab, s0f, s0b, with_output):
    b, l, _ = q.shape
    n_chunks = l // GLA_CHUNK
    seq = lambda bi: (bi, 0, 0)
    st = lambda bi: (bi, 0, 0, 0)
    st_shape = (b, GLA_HEADS, GLA_DV_HEAD, LANES)
    st_spec = pl.BlockSpec((1, GLA_HEADS, GLA_DV_HEAD, LANES), st)
    out_specs = [st_spec, st_spec]
    out_shape = [jax.ShapeDtypeStruct(st_shape, F32)] * 2
    if with_output:
        out_specs = [pl.BlockSpec((1, l, GLA_DV), seq)] + out_specs
        out_shape = [jax.ShapeDtypeStruct((b, l, GLA_DV), F32)] + out_shape
    return pl.pallas_call(
        functools.partial(_gla_kernel, n_chunks=n_chunks, with_output=with_output),
        grid=(b,),
        in_specs=[pl.BlockSpec((1, l, GLA_DK), seq),
                  pl.BlockSpec((1, l, GLA_DK), seq),
                  pl.BlockSpec((1, l, GLA_DV), seq),
                  pl.BlockSpec((1, l, GLA_DK), seq),
                  pl.BlockSpec((1, l, GLA_DK), seq),
                  st_spec, st_spec],
        out_specs=out_specs,
        out_shape=out_shape,
        scratch_shapes=[pltpu.VMEM((GLA_HEADS, GLA_DV_HEAD, LANES), F32)],
        compiler_params=_cparams(("arbitrary",)),
        name="gla_out" if with_output else "gla_ctx",
    )(q, k, v, laf, lab, s0f, s0b)


def _s5gen_kernel(pc_ref, pr_ref, btr_ref, bti_ref, ctrf_ref, ctif_ref, ctrb_ref, ctib_ref,
                  m_ref, wt_ref, v_ref, ab_ref):
    pc = pc_ref[0]
    blk = lax.shift_right_logical(lax.broadcasted_iota(I32, (1, S5_VEC), 1), 4).astype(F32)
    lane = lax.broadcasted_iota(I32, (S5_GROUP_CH, S5_VEC), 1)
    n = float(S5_CHUNK)

    def cmul(ar, ai, br, bi):
        return ar * br - ai * bi, ar * bi + ai * br

    kcat = []
    for d in (0, 1):
        lre, lim, ls = pc[:, 3 * d:3 * d + 1], pc[:, 3 * d + 1:3 * d + 2], pc[:, 3 * d + 2:3 * d + 3]
        ctr = (ctrf_ref if d == 0 else ctrb_ref)[0]
        cti = (ctif_ref if d == 0 else ctib_ref)[0]
        step = jnp.exp(ls)
        mag = jnp.exp(lre * step)
        a_re = mag * jnp.cos(lim * step)
        a_im = mag * jnp.sin(lim * step)
        den = lre * lre + lim * lim
        f_re = ((a_re - 1.0) * lre + a_im * lim) / den
        f_im = (a_im * lre - (a_re - 1.0) * lim) / den
        bb_re, bb_im = cmul(f_re, f_im, btr_ref[0], bti_ref[0])

        def powers(e, lre=lre, lim=lim, step=step):
            m = jnp.exp(lre * step * e)
            ang = lim * step * e
            return m * jnp.cos(ang), m * jnp.sin(ang)

        w_re, w_im = cmul(*powers((n - 1.0 - blk) if d == 0 else blk), bb_re, bb_im)
        wt_ref[0, S5_STATE * d:S5_STATE * (d + 1), :] = w_re
        wt_ref[0, 2 * S5_STATE + S5_STATE * d:2 * S5_STATE + S5_STATE * (d + 1), :] = w_im
        c_re, c_im = cmul(*powers((blk + 1.0) if d == 0 else (n - blk)), ctr, cti)
        v_ref[0, S5_STATE * d:S5_STATE * (d + 1), :] = c_re
        v_ref[0, 2 * S5_STATE + S5_STATE * d:2 * S5_STATE + S5_STATE * (d + 1), :] = -c_im
        e_re, e_im = cmul(*powers(blk if d == 0 else (n - 1.0 - blk)), ctr, cti)
        b16r_hi, b16r_lo = _split2(bb_re[:, 0:S5_GROUP_CH])
        b16i_hi, b16i_lo = _split2(bb_im[:, 0:S5_GROUP_CH])
        er_hi, er_lo = _split2(e_re)
        ei_hi, ei_lo = _split2(e_im)
        kr = _dot_tn(b16r_hi, er_hi) + (_dot_tn(b16r_hi, er_lo) + _dot_tn(b16r_lo, er_hi))
        ki = _dot_tn(b16i_hi, ei_hi) + (_dot_tn(b16i_hi, ei_lo) + _dot_tn(b16i_lo, ei_hi))
        kcat.append(kr - ki)

    for s in range(S5_CHUNK):
        sh_f = S5_GROUP_CH * s
        fwd = kcat[0] if s == 0 else pltpu.roll(kcat[0], sh_f, 1)
        fwd = jnp.where(lane >= sh_f, fwd, 0.0)
        sh_b = S5_VEC - S5_GROUP_CH * (S5_CHUNK - 1 - s)
        bwd = kcat[1] if sh_b == S5_VEC else pltpu.roll(kcat[1], sh_b, 1)
        bwd = jnp.where(lane < S5_GROUP_CH * (s + 1), bwd, 0.0)
        m_ref[0, S5_GROUP_CH * s:S5_GROUP_CH * (s + 1), :] = fwd + bwd

    pr = pr_ref[0]
    for d in (0, 1):
        lre, lim, ls = pr[3 * d:3 * d + 1, :], pr[3 * d + 1:3 * d + 2, :], pr[3 * d + 2:3 * d + 3, :]
        stp = jnp.exp(ls) * n
        mg = jnp.exp(lre * stp)
        ab_ref[0, d:d + 1, :] = mg * jnp.cos(lim * stp)
        ab_ref[0, 2 + d:3 + d, :] = mg * jnp.sin(lim * stp)


def _s5gen(pc, pr, btr, bti, ctrf, ctif, ctrb, ctib):
    g = pc.shape[0]
    blk3 = lambda shape: pl.BlockSpec((1,) + shape, lambda i: (i, 0, 0))
    big = (S5_STATE, S5_VEC)
    sq = (S5_VEC, S5_VEC)
    return pl.pallas_call(
        _s5gen_kernel,
        grid=(g,),
        in_specs=[blk3((S5_STATE, 8)), blk3((8, S5_STATE))] + [blk3(big)] * 6,
        out_specs=[blk3(sq), blk3(sq), blk3(sq), blk3((4, S5_STATE))],
        out_shape=[jax.ShapeDtypeStruct((g,) + sq, F32)] * 3 + [jax.ShapeDtypeStruct((g, 4, S5_STATE), F32)],
        compiler_params=_cparams(("arbitrary",)),
        name="s5gen",
    )(pc, pr, btr, bti, ctrf, ctif, ctrb, ctib)


def _s5_kernel(*refs, n_chunks, nb, with_output):
    if with_output:
        u_ref, m_ref, wt_ref, v_ref, ab_ref, x0_ref, y_ref, xf_ref, z_ref, cin_ref = refs
    else:
        u_ref, wt_ref, ab_ref, x0_ref, xf_ref, z_ref = refs
    wtb = wt_ref[0].astype(BF16)
    for bi in range(nb):
        z = _dot_nt(u_ref[0, bi * n_chunks:(bi + 1) * n_chunks, :], wtb)
        z_ref[0, pl.ds(bi, n_chunks, stride=nb), :] = z[:, 0:LANES]
        z_ref[1, pl.ds(bi, n_chunks, stride=nb), :] = z[:, LANES:2 * LANES]
    ab = ab_ref[0]
    ar, ai = ab[:, 0:LANES], ab[:, LANES:2 * LANES]
    is_f = lax.broadcasted_iota(I32, (nb, LANES), 1) < S5_STATE
    x0 = x0_ref[0]

    def body(i, carry):
        xr, xi = carry
        rf = pl.multiple_of(i * nb, nb)
        rb = pl.multiple_of((n_chunks - 1 - i) * nb, nb)
        if with_output:
            cin_ref[0, pl.ds(rf, nb), 0:S5_STATE] = xr[:, 0:S5_STATE]
            cin_ref[1, pl.ds(rf, nb), 0:S5_STATE] = xi[:, 0:S5_STATE]
            cin_ref[0, pl.ds(rb, nb), S5_STATE:LANES] = xr[:, S5_STATE:LANES]
            cin_ref[1, pl.ds(rb, nb), S5_STATE:LANES] = xi[:, S5_STATE:LANES]
        zr = jnp.where(is_f, z_ref[0, pl.ds(rf, nb), :], z_ref[0, pl.ds(rb, nb), :])
        zi = jnp.where(is_f, z_ref[1, pl.ds(rf, nb), :], z_ref[1, pl.ds(rb, nb), :])
        return ar * xr - ai * xi + zr, ar * xi + ai * xr + zi

    xr, xi = lax.fori_loop(0, n_chunks, body, (x0[:, 0:LANES], x0[:, LANES:2 * LANES]))
    xf_ref[0, :, 0:LANES] = xr
    xf_ref[0, :, LANES:2 * LANES] = xi
    if with_output:
        mb = m_ref[0].astype(BF16)
        vb = v_ref[0].astype(BF16)
        for bi in range(nb):
            rows = slice(bi * n_chunks, (bi + 1) * n_chunks)
            carried = jnp.concatenate([cin_ref[0, pl.ds(bi, n_chunks, stride=nb), :],
                                       cin_ref[1, pl.ds(bi, n_chunks, stride=nb), :]], axis=1).astype(BF16)
            y_ref[0, rows, :] = _dot(u_ref[0, rows, :], mb) + _dot(carried, vb)


def _s5(uvec, m, wt, v, ab, x0, nb, with_output):
    g, rows, _ = uvec.shape
    n_chunks = rows // nb
    blk3 = lambda shape: pl.BlockSpec((1,) + shape, lambda i: (i, 0, 0))
    sq = (S5_VEC, S5_VEC)
    st = (nb, S5_VEC)
    if with_output:
        args = (uvec, m, wt, v, ab, x0)
        in_specs = [blk3((rows, S5_VEC)), blk3(sq), blk3(sq), blk3(sq), blk3((1, S5_VEC)), blk3(st)]
        out_specs = [blk3((rows, S5_VEC)), blk3(st)]
        out_shape = [jax.ShapeDtypeStruct((g, rows, S5_VEC), F32), jax.ShapeDtypeStruct((g,) + st, F32)]
        scratch = [pltpu.VMEM((S5_VEC // LANES, rows, LANES), F32), pltpu.VMEM((S5_VEC // LANES, rows, LANES), F32)]
    else:
        args = (uvec, wt, ab, x0)
        in_specs = [blk3((rows, S5_VEC)), blk3(sq), blk3((1, S5_VEC)), blk3(st)]
        out_specs = [blk3(st)]
        out_shape = [jax.ShapeDtypeStruct((g,) + st, F32)]
        scratch = [pltpu.VMEM((S5_VEC // LANES, rows, LANES), F32)]
    return pl.pallas_call(
        functools.partial(_s5_kernel, n_chunks=n_chunks, nb=nb, with_output=with_output),
        grid=(g,),
        in_specs=in_specs,
        out_specs=out_specs,
        out_shape=out_shape,
        scratch_shapes=scratch,
        compiler_params=_cparams(("arbitrary",)),
        name="s5_out" if with_output else "s5_ctx",
    )(*args)


def _post_kernel(x_ref, o_ref, go_ref, u_ref, yv_ref, g1_ref, sh2_ref, sc2_ref, g2_ref,
                 gn_ref, d_ref, gw_ref, gb_ref, wo_ref, n2_ref, rwh_ref, rwl_ref, sgu_ref, sd_ref,
                 base_ref, hrow_ref, lg_ref, y_ref, *, tm):
    nc = tm // S5_CHUNK
    gpt = LANES // S5_GROUP_CH
    masks = _group_lane_masks(nc)
    for s in range(S5_CHUNK):
        half, j = divmod(s, gpt)
        for t in range(D_S5 // LANES):
            acc = None
            for gl in range(gpt):
                piece = _move_group(yv_ref[t * gpt + gl, :, LANES * half:LANES * (half + 1)], j, gl)
                acc = piece if acc is None else jnp.where(masks[gl], piece, acc)
            y_ref[t, pl.ds(s, nc, stride=S5_CHUNK), :] = acc
    o = o_ref[0]
    gn = gn_ref[...]
    heads = [_rms(o[:, GLA_DV_HEAD * h:GLA_DV_HEAD * (h + 1)], gn) for h in range(GLA_HEADS)]
    gla_out = jnp.concatenate(heads, axis=1) * _silu(go_ref[0])
    yy = jnp.concatenate([y_ref[t] for t in range(D_S5 // LANES)], axis=1) + d_ref[...] * u_ref[0]
    z = 0.5 * yy * (1.0 + jnp.tanh(0.7978845608028654 * (yy + 0.044715 * (yy * yy * yy))))
    s5_out = z * jax.nn.sigmoid(_dot(z.astype(BF16), gw_ref[...]) + gb_ref[...])
    mix = jnp.concatenate([gla_out, s5_out], axis=1).astype(BF16)
    x1 = x_ref[0] + g1_ref[0] * _dot(mix, wo_ref[...])
    h2 = _rms(x1, n2_ref[...]) * (1.0 + sc2_ref[0]) + sh2_ref[0]
    lg_ref[0] = _dot3(h2, rwh_ref[...], rwl_ref[...])
    hb = h2.astype(BF16)
    gu = _dot(hb, sgu_ref[...])
    hid = _silu(gu[:, 0:D_SHARED]) * gu[:, D_SHARED:2 * D_SHARED]
    base_ref[0] = x1 + g2_ref[0] * _dot(hid.astype(BF16), sd_ref[...])
    for s in range(ROW_TILES):
        hrow_ref[pl.ds(s, tm, stride=ROW_TILES), :] = h2[:, LANES * s:LANES * (s + 1)]


def _post(x, o, go, u, yvec, g1, sh2, sc2, g2, gn, d, gw, gb, wo, n2, rwh, rwl, sgu, sd):
    b, l, _ = x.shape
    tm = 256
    nt = l // tm
    row = lambda bi, i: (bi, i, 0)
    mod = lambda bi, i: (bi, 0, 0)
    full = lambda bi, i: (0, 0)
    ws = (gn, d, gw, gb, wo, n2, rwh, rwl, sgu, sd)
    return pl.pallas_call(
        functools.partial(_post_kernel, tm=tm),
        grid=(b, nt),
        in_specs=[pl.BlockSpec((1, tm, D_MODEL), row)]
                 + [pl.BlockSpec((1, tm, 512), row)] * 3
                 + [pl.BlockSpec((S5_GROUPS, tm // S5_CHUNK, S5_VEC), lambda bi, i: (0, bi * nt + i, 0))]
                 + [pl.BlockSpec((1, 1, D_MODEL), mod)] * 4
                 + [pl.BlockSpec(w.shape, full) for w in ws],
        out_specs=[pl.BlockSpec((1, tm, D_MODEL), row),
                   pl.BlockSpec((tm * ROW_TILES, LANES), lambda bi, i: (bi * nt + i, 0)),
                   pl.BlockSpec((1, tm, N_EXPERTS), row)],
        out_shape=[jax.ShapeDtypeStruct((b, l, D_MODEL), F32),
                   jax.ShapeDtypeStruct((b * l * ROW_TILES, LANES), F32),
                   jax.ShapeDtypeStruct((b, l, N_EXPERTS), F32)],
        scratch_shapes=[pltpu.VMEM((D_S5 // LANES, tm, LANES), F32)],
        compiler_params=_cparams(("arbitrary", "arbitrary")),
        name="post",
    )(x, o, go, u, yvec, g1, sh2, sc2, g2, *ws)


def _route_kernel(lg_ref, rb_ref, w_ref, p_ref, rb4_ref, tc_ref, cnt_ref, run_ref, *, tm):
    @pl.when(pl.program_id(0) == 0)
    def _():
        run_ref[...] = jnp.zeros_like(run_ref)

    neg = -jnp.inf
    gsz = N_EXPERTS // N_EXPERT_GROUPS
    s = jax.nn.sigmoid(lg_ref[...].T)
    biased = s + rb_ref[...]
    row = lax.broadcasted_iota(I32, (N_EXPERTS, tm), 0).astype(F32)

    def first_max(m, idx):
        mx = jnp.max(m, axis=0, keepdims=True)
        ix = jnp.min(jnp.where(m == mx, idx, float(N_EXPERTS)), axis=0, keepdims=True)
        return mx, ix

    grow = lax.broadcasted_iota(I32, (gsz, tm), 0).astype(F32)
    gs = []
    for g in range(N_EXPERT_GROUPS):
        m, idx = biased[gsz * g:gsz * (g + 1), :], grow + float(gsz * g)
        m1, i1 = first_max(m, idx)
        gs.append(m1 + jnp.max(jnp.where(idx == i1, neg, m), axis=0, keepdims=True))
    kept = []
    for g in range(N_EXPERT_GROUPS):
        ahead = jnp.zeros((1, tm), F32)
        for j in range(N_EXPERT_GROUPS):
            if j < g:
                ahead = ahead + jnp.where(gs[j] >= gs[g], 1.0, 0.0)
            elif j > g:
                ahead = ahead + jnp.where(gs[j] > gs[g], 1.0, 0.0)
        kept.append(jnp.where(ahead < float(TOPK_GROUPS), biased[gsz * g:gsz * (g + 1), :], neg))
    masked = jnp.concatenate(kept, axis=0)

    onehot = jnp.zeros((N_EXPERTS, tm), F32)
    ids, ws = [], []
    for _ in range(TOP_K):
        _, ik = first_max(masked, row)
        hit = row == ik
        ids.append(ik)
        ws.append(jnp.sum(jnp.where(hit, s, 0.0), axis=0, keepdims=True))
        onehot = onehot + jnp.where(hit, 1.0, 0.0)
        masked = jnp.where(hit, neg, masked)
    wsum = ws[0]
    for k in range(1, TOP_K):
        wsum = wsum + ws[k]

    ss = lax.broadcasted_iota(I32, (tm, tm), 0)
    tt = lax.broadcasted_iota(I32, (tm, tm), 1)
    earlier = jnp.where(ss < tt, 1.0, 0.0).astype(BF16)
    ohb = onehot.astype(BF16)
    tcnt = _dot(ohb, jnp.ones((tm, LANES), BF16))
    ee = lax.broadcasted_iota(I32, (N_EXPERTS, N_EXPERTS), 0)
    ff = lax.broadcasted_iota(I32, (N_EXPERTS, N_EXPERTS), 1)
    below = jnp.where(ff < ee, 1.0, 0.0).astype(BF16)
    t_hi, t_lo = _split2(tcnt)
    toff = _dot(below, t_hi) + _dot(below, t_lo)
    lpos = _dot(ohb, earlier) + toff[:, 0:1]
    w_ref[...] = jnp.concatenate([w / wsum * ROUTE_SCALE for w in ws], axis=0)
    p_ref[...] = jnp.concatenate(
        [jnp.sum(jnp.where(row == ids[k], lpos, 0.0), axis=0, keepdims=True) for k in range(TOP_K)], axis=0).astype(I32)
    run = run_ref[...]
    rb4_ref[0] = run.astype(I32)
    tc_ref[0] = tcnt.astype(I32)
    run = run + tcnt
    run_ref[...] = run
    cnt_ref[...] = run.astype(I32)


def _route(logits, rb, tm):
    t = logits.shape[0]
    col = lambda i: (0, i)
    fixed = lambda i: (0, 0)
    tile = lambda i: (i, 0, 0)
    per_tile = jax.ShapeDtypeStruct((t // tm, N_EXPERTS, LANES), I32)
    return pl.pallas_call(
        functools.partial(_route_kernel, tm=tm),
        grid=(t // tm,),
        in_specs=[pl.BlockSpec((tm, N_EXPERTS), lambda i: (i, 0)), pl.BlockSpec((N_EXPERTS, 1), fixed)],
        out_specs=[pl.BlockSpec((TOP_K, tm), col)] * 2 + [pl.BlockSpec((1, N_EXPERTS, LANES), tile)] * 2
                  + [pl.BlockSpec((N_EXPERTS, LANES), fixed)],
        out_shape=[jax.ShapeDtypeStruct((TOP_K, t), F32), jax.ShapeDtypeStruct((TOP_K, t), I32), per_tile, per_tile,
                   jax.ShapeDtypeStruct((N_EXPERTS, LANES), I32)],
        scratch_shapes=[pltpu.VMEM((N_EXPERTS, LANES), F32)],
        compiler_params=_cparams(("arbitrary",)),
        name="route",
    )(logits, rb)


def _n_blocks_max(n_assign):
    return -(-(n_assign + N_EXPERTS * (EXPERT_BLOCK - 1)) // EXPERT_BLOCK)


def _plan_kernel(cnt_ref, ps_ref, blk_ref, nv_ref, *, nbp):
    cnt = cnt_ref[...]
    nb = lax.shift_right_logical(cnt + (EXPERT_BLOCK - 1), 8).astype(F32)
    nb8 = jnp.broadcast_to(nb, (SUBLANES, N_EXPERTS))
    nb_hi, nb_lo = _split2(nb8)
    ii = lax.broadcasted_iota(I32, (N_EXPERTS, N_EXPERTS), 0)
    jj = lax.broadcasted_iota(I32, (N_EXPERTS, N_EXPERTS), 1)
    upto = jnp.where(ii <= jj, 1.0, 0.0).astype(BF16)
    cum = (_dot(nb_hi, upto) + _dot(nb_lo, upto))[0:1, :]
    ps_ref[...] = ((cum - nb) * float(EXPERT_BLOCK)).astype(I32)
    bi = lax.broadcasted_iota(I32, (nbp, N_EXPERTS), 0).astype(F32)
    owner = jnp.sum(jnp.where(cum <= bi, 1.0, 0.0), axis=-1, keepdims=True)
    blk_ref[...] = jnp.minimum(owner, float(N_EXPERTS - 1)).astype(I32)
    nv_ref[...] = cum[:, N_EXPERTS - 1:N_EXPERTS].astype(I32)


def _plan(cnt, nbp):
    return pl.pallas_call(
        functools.partial(_plan_kernel, nbp=nbp),
        out_shape=[jax.ShapeDtypeStruct((1, N_EXPERTS), I32), jax.ShapeDtypeStruct((nbp, 1), I32),
                   jax.ShapeDtypeStruct((1, 1), I32)],
        name="plan",
    )(cnt)


def _segment_copies(tm, tc_ref, rb_ref, ps_ref, make):
    sizes = [1 << p for p in range(tm.bit_length() - 1, -1, -1)]

    def body(e, local):
        cnt = tc_ref[e]
        glob = ps_ref[e] + rb_ref[e]
        loc = local
        for size in sizes:
            @pl.when((cnt & size) != 0)
            def _(loc=loc, glob=glob, size=size):
                make(loc, glob, size).start()
            loc = loc + (cnt & size)
            glob = glob + (cnt & size)
        return local + cnt

    lax.fori_loop(0, N_EXPERTS, body, 0)


def _rows(ref, row, n):
    return ref.at[pl.ds(pl.multiple_of(row * ROW_TILES, ROW_TILES), n * ROW_TILES)]


def _dispatch_kernel(lp_ref, tc_ref, rb_ref, ps_ref, cnt_ref, nv_ref, h_ref, xs_ref, sorted_ref, zero_ref, sem, zsem,
                     *, tm, nbp):
    step = pl.program_id(0)
    half = EXPERT_BLOCK // 2
    pad_sizes = [1 << p for p in range(half.bit_length() - 1, -1, -1)]

    @pl.when(step == 0)
    def _():
        zero_ref[...] = jnp.zeros_like(zero_ref)
        for wait in (False, True):
            def tail_body(bi, carry, wait=wait):
                for part in range(EXPERT_BLOCK // half):
                    cp = pltpu.make_async_copy(zero_ref, _rows(xs_ref, bi * EXPERT_BLOCK + part * half, half), zsem)
                    cp.wait() if wait else cp.start()
                return carry
            lax.fori_loop(nv_ref[0], nbp, tail_body, 0)

        for wait in (False, True):
            def pad_body(e, carry, wait=wait):
                cnt = cnt_ref[e]
                pad = (-cnt) & (EXPERT_BLOCK - 1)
                row = ps_ref[e] + cnt
                for size in pad_sizes:
                    @pl.when((pad & size) != 0)
                    def _(row=row, size=size):
                        cp = pltpu.make_async_copy(zero_ref.at[pl.ds(0, size * ROW_TILES)], _rows(xs_ref, row, size), zsem)
                        cp.wait() if wait else cp.start()
                    row = row + (pad & size)
                return carry
            lax.fori_loop(0, N_EXPERTS, pad_body, 0)

    def token_body(t, carry):
        row = h_ref[pl.ds(pl.multiple_of(t * ROW_TILES, ROW_TILES), ROW_TILES), :]
        for k in range(TOP_K):
            p = lp_ref[t * TOP_K + k]
            sorted_ref[pl.ds(pl.multiple_of(p * ROW_TILES, ROW_TILES), ROW_TILES), :] = row
        return carry

    lax.fori_loop(0, tm, token_body, 0)
    _segment_copies(tm, tc_ref, rb_ref, ps_ref,
                    lambda loc, glob, n: pltpu.make_async_copy(_rows(sorted_ref, loc, n), _rows(xs_ref, glob, n), sem))
    pltpu.make_async_copy(sorted_ref, _rows(xs_ref, 0, tm * TOP_K), sem).wait()


def _dispatch(lp_flat, tcnt, runb, pstart, cnt, nv, hrows, nbp, tm):
    t = lp_flat.shape[0] // TOP_K
    per_tile = pl.BlockSpec((N_EXPERTS,), lambda i: (i,), memory_space=pltpu.SMEM)
    smem_all = pl.BlockSpec((N_EXPERTS,), lambda i: (0,), memory_space=pltpu.SMEM)
    return pl.pallas_call(
        functools.partial(_dispatch_kernel, tm=tm, nbp=nbp),
        grid=(t // tm,),
        in_specs=[pl.BlockSpec((tm * TOP_K,), lambda i: (i,), memory_space=pltpu.SMEM),
                  per_tile, per_tile, smem_all, smem_all,
                  pl.BlockSpec((1,), lambda i: (0,), memory_space=pltpu.SMEM),
                  pl.BlockSpec((tm * ROW_TILES, LANES), lambda i: (i, 0))],
        out_specs=pl.BlockSpec(memory_space=pl.ANY),
        out_shape=jax.ShapeDtypeStruct((nbp * EXPERT_BLOCK * ROW_TILES, LANES), F32),
        scratch_shapes=[pltpu.VMEM((tm * TOP_K * ROW_TILES, LANES), F32),
                        pltpu.VMEM((EXPERT_BLOCK // 2 * ROW_TILES, LANES), F32),
                        pltpu.SemaphoreType.DMA, pltpu.SemaphoreType.DMA],
        compiler_params=_cparams(("arbitrary",)),
        name="dispatch",
    )(lp_flat, tcnt, runb, pstart, cnt, nv, hrows)


def _experts_kernel(blk_ref, nv_ref, xs_ref, wg_ref, wu_ref, wd_ref, ys_ref, wgb_ref, wub_ref, wdb_ref):
    i = pl.program_id(0)
    nv = nv_ref[0]

    @pl.when(i >= nv)
    def _():
        ys_ref[...] = jnp.zeros_like(ys_ref)

    @pl.when(i < nv)
    def _():
        prev = blk_ref[jnp.maximum(i - 1, 0)]

        @pl.when((i == 0) | (blk_ref[i] != prev))
        def _():
            wgb_ref[...] = wg_ref[0].astype(BF16)
            wub_ref[...] = wu_ref[0].astype(BF16)
            wdb_ref[...] = wd_ref[0].astype(BF16)

        xb = jnp.concatenate(
            [xs_ref[pl.ds(s, EXPERT_BLOCK, stride=ROW_TILES), :] for s in range(ROW_TILES)], axis=1).astype(BF16)
        hid = _silu(_dot(xb, wgb_ref[...])) * _dot(xb, wub_ref[...])
        y = _dot(hid.astype(BF16), wdb_ref[...])
        for s in range(ROW_TILES):
            ys_ref[pl.ds(s, EXPERT_BLOCK, stride=ROW_TILES), :] = y[:, LANES * s:LANES * (s + 1)]


def _experts(blk, nv, xs, wg, wu, wd, nbp):
    rows = EXPERT_BLOCK * ROW_TILES
    cur = lambda i, blk_ref, nv_ref: jnp.minimum(i, nv_ref[0] - 1)
    xmap = lambda i, blk_ref, nv_ref: (cur(i, blk_ref, nv_ref), 0)
    wmap = lambda i, blk_ref, nv_ref: (blk_ref[cur(i, blk_ref, nv_ref)], 0, 0)
    return pl.pallas_call(
        _experts_kernel,
        grid_spec=pltpu.PrefetchScalarGridSpec(
            num_scalar_prefetch=2,
            grid=(nbp,),
            in_specs=[pl.BlockSpec((rows, LANES), xmap),
                      pl.BlockSpec((1, D_MODEL, D_EXPERT), wmap),
                      pl.BlockSpec((1, D_MODEL, D_EXPERT), wmap),
                      pl.BlockSpec((1, D_EXPERT, D_MODEL), wmap)],
            out_specs=pl.BlockSpec((rows, LANES), lambda i, blk_ref, nv_ref: (i, 0)),
            scratch_shapes=[pltpu.VMEM((D_MODEL, D_EXPERT), BF16), pltpu.VMEM((D_MODEL, D_EXPERT), BF16),
                            pltpu.VMEM((D_EXPERT, D_MODEL), BF16)]),
        out_shape=jax.ShapeDtypeStruct(xs.shape, F32),
        compiler_params=_cparams(("arbitrary",)),
        name="experts",
    )(blk, nv, xs, wg, wu, wd)


def _combine_kernel(lp_ref, w_ref, tc_ref, rb_ref, ps_ref, ys_ref, base_ref, g2_ref, fg_ref, out_ref,
                    buf_ref, acc_ref, sem, *, tm):
    _segment_copies(tm, tc_ref, rb_ref, ps_ref,
                    lambda loc, glob, n: pltpu.make_async_copy(_rows(ys_ref, glob, n), _rows(buf_ref, loc, n), sem))
    pltpu.make_async_copy(_rows(ys_ref, 0, tm * TOP_K), buf_ref, sem).wait()

    def token_body(t, carry):
        j0 = t * TOP_K
        acc = jnp.zeros((ROW_TILES, LANES), F32)
        for k in range(TOP_K):
            p = lp_ref[j0 + k]
            acc = acc + w_ref[j0 + k] * buf_ref[pl.ds(pl.multiple_of(p * ROW_TILES, ROW_TILES), ROW_TILES), :]
        acc_ref[pl.ds(pl.multiple_of(t * ROW_TILES, ROW_TILES), ROW_TILES), :] = acc
        return carry

    lax.fori_loop(0, tm, token_body, 0)
    routed = jnp.concatenate([acc_ref[pl.ds(s, tm, stride=ROW_TILES), :] for s in range(ROW_TILES)], axis=1)
    out_ref[0] = _rms(base_ref[0] + g2_ref[0] * routed, fg_ref[...])


def _combine(lp_flat, w_flat, tcnt, runb, pstart, ys, base, g2, fg, tm):
    b, l, _ = base.shape
    nt = l // tm
    flat = lambda bi, i: (bi * nt + i,)
    smem_blk = pl.BlockSpec((tm * TOP_K,), flat, memory_space=pltpu.SMEM)
    per_tile = pl.BlockSpec((N_EXPERTS,), flat, memory_space=pltpu.SMEM)
    return pl.pallas_call(
        functools.partial(_combine_kernel, tm=tm),
        grid=(b, nt),
        in_specs=[smem_blk, smem_blk, per_tile, per_tile,
                  pl.BlockSpec((N_EXPERTS,), lambda bi, i: (0,), memory_space=pltpu.SMEM),
                  pl.BlockSpec(memory_space=pl.ANY),
                  pl.BlockSpec((1, tm, D_MODEL), lambda bi, i: (bi, i, 0)),
                  pl.BlockSpec((1, 1, D_MODEL), lambda bi, i: (bi, 0, 0)),
                  pl.BlockSpec((1, D_MODEL), lambda bi, i: (0, 0))],
        out_specs=pl.BlockSpec((1, tm, D_MODEL), lambda bi, i: (bi, i, 0)),
        out_shape=jax.ShapeDtypeStruct((b, l, D_MODEL), F32),
        scratch_shapes=[pltpu.VMEM((tm * TOP_K * ROW_TILES, LANES), F32),
                        pltpu.VMEM((tm * ROW_TILES, LANES), F32),
                        pltpu.SemaphoreType.DMA],
        compiler_params=_cparams(("arbitrary", "arbitrary")),
        name="combine",
    )(lp_flat, w_flat, tcnt, runb, pstart, ys, base, g2, fg)


def _mixer_inputs(h, shift, scale, gain, wm, wl, wa, ba):
    return _proj(h, shift, scale, gain, wm, wl, wa, ba)


def kernel(x, c, ctx, c_ctx, ada_w, ada_b, norm1_g, norm2_g, w_in, gla_wa_f, gla_ba_f, gla_wa_b, gla_ba_b, gla_norm_g, s5_lam_re_f, s5_lam_im_f, s5_log_step_f, s5_lam_re_b, s5_lam_im_b, s5_log_step_b, s5_b_re, s5_b_im, s5_c_re_f, s5_c_im_f, s5_c_re_b, s5_c_im_b, s5_d, s5_glu_w, s5_glu_b, w_out, router_w, router_b, exp_w_gate, exp_w_up, exp_w_down, sh_w_gate, sh_w_up, sh_w_down, final_norm_g):
    b, l, d = x.shape
    i = 0

    rows = -(-(b + 1) // SUBLANES) * SUBLANES
    cs = jnp.zeros((rows, d), F32).at[:b].set(c).at[b].set(c_ctx)
    mod = _adaln(cs, ada_w[i], ada_b[i][None, :])
    sh1, sc1, g1, sh2, sc2, g2 = [mod[:b, d * j:d * (j + 1)][:, None, :] for j in range(6)]
    csh1, csc1 = [jnp.broadcast_to(mod[b, d * j:d * (j + 1)][None, None, :], (b, 1, d)) for j in range(2)]

    w = w_in[i]
    o1, o2, o3, o4, o5, o6 = 256, 512, 1024, 1536, 1552, 1568
    wm = jnp.concatenate([w[:, :o4], w[:, o6:]], axis=1).astype(BF16)
    wl = jnp.zeros((d, LANES), F32).at[:, :2 * GLA_GATE_RANK].set(w[:, o4:o6]).astype(BF16)
    wa = jnp.zeros((LANES, 2 * GLA_DK), F32)
    wa = wa.at[:GLA_GATE_RANK, :GLA_DK].set(gla_wa_f[i]).at[GLA_GATE_RANK:2 * GLA_GATE_RANK, GLA_DK:].set(gla_wa_b[i])
    wa = wa.astype(BF16)
    ba = jnp.concatenate([gla_ba_f[i], gla_ba_b[i]])[None, :]
    n1 = norm1_g[i][None, :]

    pcols = jnp.stack([s5_lam_re_f[i], s5_lam_im_f[i],
                       jnp.broadcast_to(s5_log_step_f[i][:, None], (S5_GROUPS, S5_STATE)),
                       s5_lam_re_b[i], s5_lam_im_b[i],
                       jnp.broadcast_to(s5_log_step_b[i][:, None], (S5_GROUPS, S5_STATE)),
                       jnp.zeros((S5_GROUPS, S5_STATE), F32), jnp.zeros((S5_GROUPS, S5_STATE), F32)], axis=-1)
    prows = pcols.transpose(0, 2, 1)
    tile_b = lambda t: jnp.tile(t, (1, 1, S5_CHUNK))
    tile_c = lambda t: jnp.tile(t.transpose(0, 2, 1), (1, 1, S5_CHUNK))
    m_op, wt_op, v_op, ab4 = _s5gen(pcols, prows, tile_b(s5_b_re[i]), tile_b(s5_b_im[i]),
                                    tile_c(s5_c_re_f[i]), tile_c(s5_c_im_f[i]),
                                    tile_c(s5_c_re_b[i]), tile_c(s5_c_im_b[i]))
    ab = ab4.reshape(S5_GROUPS, 1, 4 * S5_STATE)

    cq, ck, cv, _, _, claf, clab, cuv = _proj(ctx, csh1, csc1, n1, wm, wl, wa, ba)
    zero_state = jnp.zeros((b, GLA_HEADS, GLA_DV_HEAD, LANES), F32)
    gsf, gsb = _gla(cq, ck, cv, claf, clab, zero_state, zero_state, with_output=False)
    (x0,) = _s5(cuv, None, wt_op, None, ab, jnp.zeros((S5_GROUPS, b, S5_VEC), F32), b, with_output=False)

    q, k, v, go, u, laf, lab, uv = _proj(x, sh1, sc1, n1, wm, wl, wa, ba)
    o, _, _ = _gla(q, k, v, laf, lab, gsf, gsb, with_output=True)
    yvec, _ = _s5(uv, m_op, wt_op, v_op, ab, x0, b, with_output=True)

    rw_hi = router_w[i].astype(BF16)
    rw_lo = (router_w[i] - rw_hi.astype(F32)).astype(BF16)
    base, hrows, logits = _post(
        x, o, go, u, yvec, g1, sh2, sc2, g2,
        gla_norm_g[i][None, :], s5_d[i][None, :], s5_glu_w[i].astype(BF16),
        s5_glu_b[i][None, :], w_out[i].astype(BF16), norm2_g[i][None, :], rw_hi, rw_lo,
        jnp.concatenate([sh_w_gate[i], sh_w_up[i]], axis=1).astype(BF16), sh_w_down[i].astype(BF16))

    t = b * l
    tile = min(512, l)
    wts, lpos, runb, tcnt, cnt = _route(logits.reshape(t, N_EXPERTS), router_b[i][:, None], tile)
    cnt_flat = cnt[:, 0]
    runb_flat, tcnt_flat = runb[:, :, 0].reshape(-1), tcnt[:, :, 0].reshape(-1)
    nbp = -(-_n_blocks_max(t * TOP_K) // SUBLANES) * SUBLANES
    pstart, blk, nv = _plan(cnt_flat[None, :], nbp)
    lp_flat, w_flat = lpos.T.reshape(-1), wts.T.reshape(-1)
    ps_flat = pstart.reshape(-1)
    xs = _dispatch(lp_flat, tcnt_flat, runb_flat, ps_flat, cnt_flat, nv.reshape(-1), hrows, nbp, tile)
    assert exp_w_gate.shape[0] == 1, "single-layer block"
    ys = _experts(blk.reshape(-1), nv.reshape(-1), xs, exp_w_gate.reshape(exp_w_gate.shape[1:]),
                  exp_w_up.reshape(exp_w_up.shape[1:]), exp_w_down.reshape(exp_w_down.shape[1:]), nbp)
    return _combine(lp_flat, w_flat, tcnt_flat, runb_flat, ps_flat, ys, base, g2, final_norm_g[None, :], tile)
```

```python
import functools

import jax
import jax.numpy as jnp
from jax import lax
from jax.experimental import pallas as pl
from jax.experimental.pallas import tpu as pltpu

F32 = jnp.float32
BF16 = jnp.bfloat16
I32 = jnp.int32

D_MODEL = 1024
GLA_HEADS = 4
GLA_DK_HEAD = 64
GLA_DV_HEAD = 128
GLA_DK = 256
GLA_DV = 512
GLA_GATE_RANK = 16
GLA_GATE_TAU = 16.0
GLA_CHUNK = 64
D_S5 = 512
S5_GROUP_CH = 16
S5_GROUPS = 32
S5_STATE = 64
S5_CHUNK = 16
S5_VEC = S5_CHUNK * S5_GROUP_CH
N_EXPERTS = 256
TOP_K = 8
N_EXPERT_GROUPS = 8
TOPK_GROUPS = 4
D_EXPERT = 256
D_SHARED = 256
ROUTE_SCALE = 2.5
EPS = 1e-6

LANES = 128
SUBLANES = 8
ROW_TILES = D_MODEL // LANES
EXPERT_BLOCK = 256
VMEM_LIMIT = 56 * 1024 * 1024


def _cparams(sem):
    return pltpu.CompilerParams(dimension_semantics=sem, vmem_limit_bytes=VMEM_LIMIT)


def _dot(a, b):
    return jnp.dot(a, b, preferred_element_type=F32)


def _dot_nt(a, b):
    return lax.dot_general(a, b, (((1,), (1,)), ((), ())), preferred_element_type=F32)


def _dot_tn(a, b):
    return lax.dot_general(a, b, (((0,), (0,)), ((), ())), preferred_element_type=F32)


def _split2(x):
    hi = x.astype(BF16)
    lo = (x - hi.astype(F32)).astype(BF16)
    return hi, lo


def _dot3(a, b_hi, b_lo):
    a_hi, a_lo = _split2(a)
    return _dot(a_hi, b_hi) + (_dot(a_hi, b_lo) + _dot(a_lo, b_hi))


def _silu(x):
    return x * jax.nn.sigmoid(x)


def _rms(x, g):
    return x * lax.rsqrt(jnp.mean(x * x, axis=-1, keepdims=True) + EPS) * g


def _adaln_kernel(c_ref, w_ref, b_ref, o_ref):
    s = _silu(c_ref[...])
    w_hi, w_lo = _split2(w_ref[...])
    o_ref[...] = _dot3(s, w_hi, w_lo) + b_ref[...]


def _adaln(cs, w, b):
    rows, n = cs.shape[0], w.shape[1]
    tn = 1024
    return pl.pallas_call(
        _adaln_kernel,
        grid=(n // tn,),
        in_specs=[pl.BlockSpec((rows, D_MODEL), lambda j: (0, 0)),
                  pl.BlockSpec((D_MODEL, tn), lambda j: (0, j)),
                  pl.BlockSpec((1, tn), lambda j: (0, j))],
        out_specs=pl.BlockSpec((rows, tn), lambda j: (0, j)),
        out_shape=jax.ShapeDtypeStruct((rows, n), F32),
        compiler_params=_cparams(("arbitrary",)),
        name="adaln",
    )(cs, w, b)


def _group_lane_masks(rows):
    grp = lax.shift_right_logical(lax.broadcasted_iota(I32, (rows, LANES), 1), 4)
    return [grp == j for j in range(LANES // S5_GROUP_CH)]


def _move_group(x, src, dst):
    shift = ((dst - src) * S5_GROUP_CH) % LANES
    return pltpu.roll(x, shift, 1) if shift else x


def _proj_kernel(x_ref, sh_ref, sc_ref, g_ref, wm_ref, wl_ref, wa_ref, ba_ref,
                 q_ref, k_ref, v_ref, go_ref, u_ref, laf_ref, lab_ref, uv_ref, ut_ref, *, tm):
    h = _rms(x_ref[0], g_ref[...]) * (1.0 + sc_ref[0]) + sh_ref[0]
    hb = h.astype(BF16)
    q_ref[0] = _dot(hb, wm_ref[:, 0:256]) * (GLA_DK_HEAD ** -0.5)
    k_ref[0] = _dot(hb, wm_ref[:, 256:512])
    v_ref[0] = _dot(hb, wm_ref[:, 512:1024])
    go_ref[0] = _dot(hb, wm_ref[:, 1024:1536])
    u = _dot(hb, wm_ref[:, 1536:2048])
    u_ref[0] = u
    for t in range(D_S5 // LANES):
        ut_ref[t] = u[:, LANES * t:LANES * (t + 1)]
    lr = _dot(hb, wl_ref[...])
    pre = _dot(lr.astype(BF16), wa_ref[...]) + ba_ref[...]
    la = (jnp.minimum(pre, 0.0) - jnp.log1p(jnp.exp(-jnp.abs(pre)))) * (1.0 / GLA_GATE_TAU)
    laf_ref[0] = la[:, 0:GLA_DK]
    lab_ref[0] = la[:, GLA_DK:2 * GLA_DK]
    nc = tm // S5_CHUNK
    gpt = LANES // S5_GROUP_CH
    masks = _group_lane_masks(nc)
    for t in range(D_S5 // LANES):
        steps = [ut_ref[t, pl.ds(s, nc, stride=S5_CHUNK), :] for s in range(S5_CHUNK)]
        for gl in range(gpt):
            for half in range(S5_VEC // LANES):
                acc = None
                for j in range(gpt):
                    piece = _move_group(steps[half * gpt + j], gl, j)
                    acc = piece if acc is None else jnp.where(masks[j], piece, acc)
                uv_ref[t * gpt + gl, :, LANES * half:LANES * (half + 1)] = acc.astype(BF16)


def _proj(x, shift, scale, gain, wm, wl, wa, ba):
    b, l, _ = x.shape
    tm = min(512, l)
    nt = l // tm
    row = lambda bi, i: (bi, i, 0)
    mod = lambda bi, i: (bi, 0, 0)
    full = lambda bi, i: (0, 0)
    widths = (GLA_DK, GLA_DK, GLA_DV, GLA_DV, D_S5, GLA_DK, GLA_DK)
    return pl.pallas_call(
        functools.partial(_proj_kernel, tm=tm),
        grid=(b, nt),
        in_specs=[pl.BlockSpec((1, tm, D_MODEL), row),
                  pl.BlockSpec((1, 1, D_MODEL), mod),
                  pl.BlockSpec((1, 1, D_MODEL), mod),
                  pl.BlockSpec((1, D_MODEL), full),
                  pl.BlockSpec(wm.shape, full),
                  pl.BlockSpec(wl.shape, full),
                  pl.BlockSpec(wa.shape, full),
                  pl.BlockSpec(ba.shape, full)],
        out_specs=[pl.BlockSpec((1, tm, w), row) for w in widths]
                  + [pl.BlockSpec((S5_GROUPS, tm // S5_CHUNK, S5_VEC), lambda bi, i: (0, bi * nt + i, 0))],
        out_shape=[jax.ShapeDtypeStruct((b, l, w), F32) for w in widths]
                  + [jax.ShapeDtypeStruct((S5_GROUPS, b * l // S5_CHUNK, S5_VEC), BF16)],
        scratch_shapes=[pltpu.VMEM((D_S5 // LANES, tm, LANES), F32)],
        compiler_params=_cparams(("arbitrary", "arbitrary")),
        name="proj",
    )(x, shift, scale, gain, wm, wl, wa, ba)


def _gla_kernel(*refs, n_chunks, with_output):
    if with_output:
        q_ref, k_ref, v_ref, laf_ref, lab_ref, s0f_ref, s0b_ref, o_ref, sf_ref, sb_ref, st_ref, ob_ref = refs
    else:
        q_ref, k_ref, v_ref, laf_ref, lab_ref, s0f_ref, s0b_ref, sf_ref, sb_ref, st_ref = refs
        o_ref = ob_ref = None
    c = GLA_CHUNK
    row = lax.broadcasted_iota(I32, (c, c), 0)
    col = lax.broadcasted_iota(I32, (c, c), 1)
    lane = lax.broadcasted_iota(I32, (c, LANES), 1)
    st_ref[0] = s0f_ref[0]
    st_ref[1] = s0b_ref[0]

    def chunk(direction, idx):
        la_ref = laf_ref if direction == 0 else lab_ref
        tri = (row >= col) if direction == 0 else (row <= col)
        trib = jnp.where(tri, 1.0, 0.0).astype(BF16)
        r0 = pl.multiple_of(idx * c, c)
        q = q_ref[0, pl.ds(r0, c), :]
        k = k_ref[0, pl.ds(r0, c), :]
        v = v_ref[0, pl.ds(r0, c), :]
        la_hi, la_lo = _split2(la_ref[0, pl.ds(r0, c), :])
        cum = _dot(trib, la_hi) + _dot(trib, la_lo)
        tot = cum[c - 1:c, :] if direction == 0 else cum[0:1, :]
        qd = q * jnp.exp(cum)
        ki = k * jnp.exp(-cum)
        ks = k * jnp.exp(tot - cum)
        dec = jnp.exp(tot)
        for h in range(GLA_HEADS):
            pair = slice(LANES * (h // 2), LANES * (h // 2) + LANES)
            own = (lane >= GLA_DK_HEAD * (h % 2)) & (lane < GLA_DK_HEAD * (h % 2) + GLA_DK_HEAD)
            vb = v[:, GLA_DV_HEAD * h:GLA_DV_HEAD * (h + 1)].astype(BF16)
            st = st_ref[direction, h]
            if with_output:
                qb = qd[:, pair].astype(BF16)
                kib = jnp.where(own, ki[:, pair], 0.0).astype(BF16)
                sc = jnp.where(tri, _dot_nt(qb, kib), 0.0)
                o = _dot(sc.astype(BF16), vb) + _dot_nt(qb, st.astype(BF16))
                cols = slice(GLA_DV_HEAD * h, GLA_DV_HEAD * (h + 1))
                if direction == 0:
                    o_ref[0, pl.ds(r0, c), cols] = o
                else:
                    ob_ref[pl.ds(r0, c), cols] = o
            ksb = jnp.where(own, ks[:, pair], 0.0).astype(BF16)
            st_ref[direction, h] = st * dec[:, pair] + _dot_tn(vb, ksb)

    def body(ci, carry):
        chunk(0, ci)
        chunk(1, n_chunks - 1 - ci)
        return carry

    lax.fori_loop(0, n_chunks, body, 0)
    sf_ref[0] = st_ref[0]
    sb_ref[0] = st_ref[1]
    if with_output:
        o_ref[0] = o_ref[0] + ob_ref[...]


def _gla(q, k, v, laf, lab, s0f, s0b, with_output):
    b, l, _ = q.shape
    n_chunks = l // GLA_CHUNK
    seq = lambda bi: (bi, 0, 0)
    st = lambda bi: (bi, 0, 0, 0)
    st_shape = (b, GLA_HEADS, GLA_DV_HEAD, LANES)
    st_spec = pl.BlockSpec((1, GLA_HEADS, GLA_DV_HEAD, LANES), st)
    out_specs = [st_spec, st_spec]
    out_shape = [jax.ShapeDtypeStruct(st_shape, F32)] * 2
    if with_output:
        out_specs = [pl.BlockSpec((1, l, GLA_DV), seq)] + out_specs
        out_shape = [jax.ShapeDtypeStruct((b, l, GLA_DV), F32)] + out_shape
    return pl.pallas_call(
        functools.partial(_gla_kernel, n_chunks=n_chunks, with_output=with_output),
        grid=(b,),
        in_specs=[pl.BlockSpec((1, l, GLA_DK), seq),
                  pl.BlockSpec((1, l, GLA_DK), seq),
                  pl.BlockSpec((1, l, GLA_DV), seq),
                  pl.BlockSpec((1, l, GLA_DK), seq),
                  pl.BlockSpec((1, l, GLA_DK), seq),
                  st_spec, st_spec],
        out_specs=out_specs,
        out_shape=out_shape,
        scratch_shapes=[pltpu.VMEM((2, GLA_HEADS, GLA_DV_HEAD, LANES), F32)]
                       + ([pltpu.VMEM((l, GLA_DV), F32)] if with_output else []),
        compiler_params=_cparams(("arbitrary",)),
        name="gla_out" if with_output else "gla_ctx",
    )(q, k, v, laf, lab, s0f, s0b)


def _s5gen_kernel(pc_ref, pr_ref, btr_ref, bti_ref, ctrf_ref, ctif_ref, ctrb_ref, ctib_ref,
                  m_ref, wt_ref, v_ref, ab_ref):
    pc = pc_ref[0]
    blk = lax.shift_right_logical(lax.broadcasted_iota(I32, (1, S5_VEC), 1), 4).astype(F32)
    lane = lax.broadcasted_iota(I32, (S5_GROUP_CH, S5_VEC), 1)
    n = float(S5_CHUNK)

    def cmul(ar, ai, br, bi):
        return ar * br - ai * bi, ar * bi + ai * br

    kcat = []
    for d in (0, 1):
        lre, lim, ls = pc[:, 3 * d:3 * d + 1], pc[:, 3 * d + 1:3 * d + 2], pc[:, 3 * d + 2:3 * d + 3]
        ctr = (ctrf_ref if d == 0 else ctrb_ref)[0]
        cti = (ctif_ref if d == 0 else ctib_ref)[0]
        step = jnp.exp(ls)
        mag = jnp.exp(lre * step)
        a_re = mag * jnp.cos(lim * step)
        a_im = mag * jnp.sin(lim * step)
        den = lre * lre + lim * lim
        f_re = ((a_re - 1.0) * lre + a_im * lim) / den
        f_im = (a_im * lre - (a_re - 1.0) * lim) / den
        bb_re, bb_im = cmul(f_re, f_im, btr_ref[0], bti_ref[0])

        def powers(e, lre=lre, lim=lim, step=step):
            m = jnp.exp(lre * step * e)
            ang = lim * step * e
            return m * jnp.cos(ang), m * jnp.sin(ang)

        w_re, w_im = cmul(*powers((n - 1.0 - blk) if d == 0 else blk), bb_re, bb_im)
        wt_ref[0, S5_STATE * d:S5_STATE * (d + 1), :] = w_re
        wt_ref[0, 2 * S5_STATE + S5_STATE * d:2 * S5_STATE + S5_STATE * (d + 1), :] = w_im
        c_re, c_im = cmul(*powers((blk + 1.0) if d == 0 else (n - blk)), ctr, cti)
        v_ref[0, S5_STATE * d:S5_STATE * (d + 1), :] = c_re
        v_ref[0, 2 * S5_STATE + S5_STATE * d:2 * S5_STATE + S5_STATE * (d + 1), :] = -c_im
        e_re, e_im = cmul(*powers(blk if d == 0 else (n - 1.0 - blk)), ctr, cti)
        b16r_hi, b16r_lo = _split2(bb_re[:, 0:S5_GROUP_CH])
        b16i_hi, b16i_lo = _split2(bb_im[:, 0:S5_GROUP_CH])
        er_hi, er_lo = _split2(e_re)
        ei_hi, ei_lo = _split2(e_im)
        kr = _dot_tn(b16r_hi, er_hi) + (_dot_tn(b16r_hi, er_lo) + _dot_tn(b16r_lo, er_hi))
        ki = _dot_tn(b16i_hi, ei_hi) + (_dot_tn(b16i_hi, ei_lo) + _dot_tn(b16i_lo, ei_hi))
        kcat.append(kr - ki)

    for s in range(S5_CHUNK):
        sh_f = S5_GROUP_CH * s
        fwd = kcat[0] if s == 0 else pltpu.roll(kcat[0], sh_f, 1)
        fwd = jnp.where(lane >= sh_f, fwd, 0.0)
        sh_b = S5_VEC - S5_GROUP_CH * (S5_CHUNK - 1 - s)
        bwd = kcat[1] if sh_b == S5_VEC else pltpu.roll(kcat[1], sh_b, 1)
        bwd = jnp.where(lane < S5_GROUP_CH * (s + 1), bwd, 0.0)
        m_ref[0, S5_GROUP_CH * s:S5_GROUP_CH * (s + 1), :] = fwd + bwd

    pr = pr_ref[0]
    for d in (0, 1):
        lre, lim, ls = pr[3 * d:3 * d + 1, :], pr[3 * d + 1:3 * d + 2, :], pr[3 * d + 2:3 * d + 3, :]
        stp = jnp.exp(ls) * n
        mg = jnp.exp(lre * stp)
        ab_ref[0, d:d + 1, :] = mg * jnp.cos(lim * stp)
        ab_ref[0, 2 + d:3 + d, :] = mg * jnp.sin(lim * stp)


def _s5gen(pc, pr, btr, bti, ctrf, ctif, ctrb, ctib):
    g = pc.shape[0]
    blk3 = lambda shape: pl.BlockSpec((1,) + shape, lambda i: (i, 0, 0))
    big = (S5_STATE, S5_VEC)
    sq = (S5_VEC, S5_VEC)
    return pl.pallas_call(
        _s5gen_kernel,
        grid=(g,),
        in_specs=[blk3((S5_STATE, 8)), blk3((8, S5_STATE))] + [blk3(big)] * 6,
        out_specs=[blk3(sq), blk3(sq), blk3(sq), blk3((4, S5_STATE))],
        out_shape=[jax.ShapeDtypeStruct((g,) + sq, F32)] * 3 + [jax.ShapeDtypeStruct((g, 4, S5_STATE), F32)],
        compiler_params=_cparams(("arbitrary",)),
        name="s5gen",
    )(pc, pr, btr, bti, ctrf, ctif, ctrb, ctib)


def _s5_kernel(*refs, n_chunks, nb, with_output):
    if with_output:
        u_ref, m_ref, wt_ref, v_ref, ab_ref, x0_ref, y_ref, xf_ref, z_ref, cin_ref = refs
    else:
        u_ref, wt_ref, ab_ref, x0_ref, xf_ref, z_ref = refs
    wtb = wt_ref[0].astype(BF16)
    for bi in range(nb):
        z = _dot_nt(u_ref[0, bi * n_chunks:(bi + 1) * n_chunks, :], wtb)
        z_ref[0, pl.ds(bi, n_chunks, stride=nb), :] = z[:, 0:LANES]
        z_ref[1, pl.ds(bi, n_chunks, stride=nb), :] = z[:, LANES:2 * LANES]
    ab = ab_ref[0]
    ar, ai = ab[:, 0:LANES], ab[:, LANES:2 * LANES]
    is_f = lax.broadcasted_iota(I32, (nb, LANES), 1) < S5_STATE
    x0 = x0_ref[0]

    def body(i, carry):
        xr, xi = carry
        rf = pl.multiple_of(i * nb, nb)
        rb = pl.multiple_of((n_chunks - 1 - i) * nb, nb)
        if with_output:
            cin_ref[0, pl.ds(rf, nb), 0:S5_STATE] = xr[:, 0:S5_STATE]
            cin_ref[1, pl.ds(rf, nb), 0:S5_STATE] = xi[:, 0:S5_STATE]
            cin_ref[0, pl.ds(rb, nb), S5_STATE:LANES] = xr[:, S5_STATE:LANES]
            cin_ref[1, pl.ds(rb, nb), S5_STATE:LANES] = xi[:, S5_STATE:LANES]
        zr = jnp.where(is_f, z_ref[0, pl.ds(rf, nb), :], z_ref[0, pl.ds(rb, nb), :])
        zi = jnp.where(is_f, z_ref[1, pl.ds(rf, nb), :], z_ref[1, pl.ds(rb, nb), :])
        return ar * xr - ai * xi + zr, ar * xi + ai * xr + zi

    xr, xi = lax.fori_loop(0, n_chunks, body, (x0[:, 0:LANES], x0[:, LANES:2 * LANES]))
    xf_ref[0, :, 0:LANES] = xr
    xf_ref[0, :, LANES:2 * LANES] = xi
    if with_output:
        mb = m_ref[0].astype(BF16)
        vb = v_ref[0].astype(BF16)
        for bi in range(nb):
            rows = slice(bi * n_chunks, (bi + 1) * n_chunks)
            carried = jnp.concatenate([cin_ref[0, pl.ds(bi, n_chunks, stride=nb), :],
                                       cin_ref[1, pl.ds(bi, n_chunks, stride=nb), :]], axis=1).astype(BF16)
            y_ref[0, rows, :] = _dot(u_ref[0, rows, :], mb) + _dot(carried, vb)


def _s5(uvec, m, wt, v, ab, x0, nb, with_output):
    g, rows, _ = uvec.shape
    n_chunks = rows // nb
    blk3 = lambda shape: pl.BlockSpec((1,) + shape, lambda i: (i, 0, 0))
    sq = (S5_VEC, S5_VEC)
    st = (nb, S5_VEC)
    if with_output:
        args = (uvec, m, wt, v, ab, x0)
        in_specs = [blk3((rows, S5_VEC)), blk3(sq), blk3(sq), blk3(sq), blk3((1, S5_VEC)), blk3(st)]
        out_specs = [blk3((rows, S5_VEC)), blk3(st)]
        out_shape = [jax.ShapeDtypeStruct((g, rows, S5_VEC), F32), jax.ShapeDtypeStruct((g,) + st, F32)]
        scratch = [pltpu.VMEM((S5_VEC // LANES, rows, LANES), F32), pltpu.VMEM((S5_VEC // LANES, rows, LANES), F32)]
    else:
        args = (uvec, wt, ab, x0)
        in_specs = [blk3((rows, S5_VEC)), blk3(sq), blk3((1, S5_VEC)), blk3(st)]
        out_specs = [blk3(st)]
        out_shape = [jax.ShapeDtypeStruct((g,) + st, F32)]
        scratch = [pltpu.VMEM((S5_VEC // LANES, rows, LANES), F32)]
    return pl.pallas_call(
        functools.partial(_s5_kernel, n_chunks=n_chunks, nb=nb, with_output=with_output),
        grid=(g,),
        in_specs=in_specs,
        out_specs=out_specs,
        out_shape=out_shape,
        scratch_shapes=scratch,
        compiler_params=_cparams(("arbitrary",)),
        name="s5_out" if with_output else "s5_ctx",
    )(*args)


def _post_kernel(x_ref, o_ref, go_ref, u_ref, yv_ref, g1_ref, sh2_ref, sc2_ref, g2_ref,
                 gn_ref, d_ref, gw_ref, gb_ref, wo_ref, n2_ref, rwh_ref, rwl_ref, sgu_ref, sd_ref,
                 base_ref, hrow_ref, lg_ref, y_ref, *, tm):
    nc = tm // S5_CHUNK
    gpt = LANES // S5_GROUP_CH
    masks = _group_lane_masks(nc)
    for s in range(S5_CHUNK):
        half, j = divmod(s, gpt)
        for t in range(D_S5 // LANES):
            acc = None
            for gl in range(gpt):
                piece = _move_group(yv_ref[t * gpt + gl, :, LANES * half:LANES * (half + 1)], j, gl)
                acc = piece if acc is None else jnp.where(masks[gl], piece, acc)
            y_ref[t, pl.ds(s, nc, stride=S5_CHUNK), :] = acc
    o = o_ref[0]
    gn = gn_ref[...]
    heads = [_rms(o[:, GLA_DV_HEAD * h:GLA_DV_HEAD * (h + 1)], gn) for h in range(GLA_HEADS)]
    gla_out = jnp.concatenate(heads, axis=1) * _silu(go_ref[0])
    yy = jnp.concatenate([y_ref[t] for t in range(D_S5 // LANES)], axis=1) + d_ref[...] * u_ref[0]
    z = 0.5 * yy * (1.0 + jnp.tanh(0.7978845608028654 * (yy + 0.044715 * (yy * yy * yy))))
    s5_out = z * jax.nn.sigmoid(_dot(z.astype(BF16), gw_ref[...]) + gb_ref[...])
    mix = jnp.concatenate([gla_out, s5_out], axis=1).astype(BF16)
    x1 = x_ref[0] + g1_ref[0] * _dot(mix, wo_ref[...])
    h2 = _rms(x1, n2_ref[...]) * (1.0 + sc2_ref[0]) + sh2_ref[0]
    lg_ref[0] = _dot3(h2, rwh_ref[...], rwl_ref[...])
    hb = h2.astype(BF16)
    gu = _dot(hb, sgu_ref[...])
    hid = _silu(gu[:, 0:D_SHARED]) * gu[:, D_SHARED:2 * D_SHARED]
    base_ref[0] = x1 + g2_ref[0] * _dot(hid.astype(BF16), sd_ref[...])
    for s in range(ROW_TILES):
        hrow_ref[pl.ds(s, tm, stride=ROW_TILES), :] = h2[:, LANES * s:LANES * (s + 1)]


def _post(x, o, go, u, yvec, g1, sh2, sc2, g2, gn, d, gw, gb, wo, n2, rwh, rwl, sgu, sd):
    b, l, _ = x.shape
    tm = 256
    nt = l // tm
    row = lambda bi, i: (bi, i, 0)
    mod = lambda bi, i: (bi, 0, 0)
    full = lambda bi, i: (0, 0)
    ws = (gn, d, gw, gb, wo, n2, rwh, rwl, sgu, sd)
    return pl.pallas_call(
        functools.partial(_post_kernel, tm=tm),
        grid=(b, nt),
        in_specs=[pl.BlockSpec((1, tm, D_MODEL), row)]
                 + [pl.BlockSpec((1, tm, 512), row)] * 3
                 + [pl.BlockSpec((S5_GROUPS, tm // S5_CHUNK, S5_VEC), lambda bi, i: (0, bi * nt + i, 0))]
                 + [pl.BlockSpec((1, 1, D_MODEL), mod)] * 4
                 + [pl.BlockSpec(w.shape, full) for w in ws],
        out_specs=[pl.BlockSpec((1, tm, D_MODEL), row),
                   pl.BlockSpec((tm * ROW_TILES, LANES), lambda bi, i: (bi * nt + i, 0)),
                   pl.BlockSpec((1, tm, N_EXPERTS), row)],
        out_shape=[jax.ShapeDtypeStruct((b, l, D_MODEL), F32),
                   jax.ShapeDtypeStruct((b * l * ROW_TILES, LANES), F32),
                   jax.ShapeDtypeStruct((b, l, N_EXPERTS), F32)],
        scratch_shapes=[pltpu.VMEM((D_S5 // LANES, tm, LANES), F32)],
        compiler_params=_cparams(("arbitrary", "arbitrary")),
        name="post",
    )(x, o, go, u, yvec, g1, sh2, sc2, g2, *ws)


def _route_kernel(lg_ref, rb_ref, w_ref, p_ref, rb4_ref, tc_ref, cnt_ref, run_ref, *, tm):
    @pl.when(pl.program_id(0) == 0)
    def _():
        run_ref[...] = jnp.zeros_like(run_ref)

    neg = -jnp.inf
    gsz = N_EXPERTS // N_EXPERT_GROUPS
    s = jax.nn.sigmoid(lg_ref[...].T)
    biased = s + rb_ref[...]
    row = lax.broadcasted_iota(I32, (N_EXPERTS, tm), 0).astype(F32)

    def first_max(m, idx):
        mx = jnp.max(m, axis=0, keepdims=True)
        ix = jnp.min(jnp.where(m == mx, idx, float(N_EXPERTS)), axis=0, keepdims=True)
        return mx, ix

    grow = lax.broadcasted_iota(I32, (gsz, tm), 0).astype(F32)
    gs = []
    for g in range(N_EXPERT_GROUPS):
        m, idx = biased[gsz * g:gsz * (g + 1), :], grow + float(gsz * g)
        m1, i1 = first_max(m, idx)
        gs.append(m1 + jnp.max(jnp.where(idx == i1, neg, m), axis=0, keepdims=True))
    kept = []
    for g in range(N_EXPERT_GROUPS):
        ahead = jnp.zeros((1, tm), F32)
        for j in range(N_EXPERT_GROUPS):
            if j < g:
                ahead = ahead + jnp.where(gs[j] >= gs[g], 1.0, 0.0)
            elif j > g:
                ahead = ahead + jnp.where(gs[j] > gs[g], 1.0, 0.0)
        kept.append(jnp.where(ahead < float(TOPK_GROUPS), biased[gsz * g:gsz * (g + 1), :], neg))
    masked = jnp.concatenate(kept, axis=0)

    onehot = jnp.zeros((N_EXPERTS, tm), F32)
    ids, ws = [], []
    for _ in range(TOP_K):
        _, ik = first_max(masked, row)
        hit = row == ik
        ids.append(ik)
        ws.append(jnp.sum(jnp.where(hit, s, 0.0), axis=0, keepdims=True))
        onehot = onehot + jnp.where(hit, 1.0, 0.0)
        masked = jnp.where(hit, neg, masked)
    wsum = ws[0]
    for k in range(1, TOP_K):
        wsum = wsum + ws[k]

    ss = lax.broadcasted_iota(I32, (tm, tm), 0)
    tt = lax.broadcasted_iota(I32, (tm, tm), 1)
    earlier = jnp.where(ss < tt, 1.0, 0.0).astype(BF16)
    ohb = onehot.astype(BF16)
    tcnt = _dot(ohb, jnp.ones((tm, LANES), BF16))
    ee = lax.broadcasted_iota(I32, (N_EXPERTS, N_EXPERTS), 0)
    ff = lax.broadcasted_iota(I32, (N_EXPERTS, N_EXPERTS), 1)
    below = jnp.where(ff < ee, 1.0, 0.0).astype(BF16)
    t_hi, t_lo = _split2(tcnt)
    toff = _dot(below, t_hi) + _dot(below, t_lo)
    lpos = _dot(ohb, earlier) + toff[:, 0:1]
    w_ref[...] = jnp.concatenate([w / wsum * ROUTE_SCALE for w in ws], axis=0)
    p_ref[...] = jnp.concatenate(
        [jnp.sum(jnp.where(row == ids[k], lpos, 0.0), axis=0, keepdims=True) for k in range(TOP_K)], axis=0).astype(I32)
    run = run_ref[...]
    rb4_ref[0] = run.astype(I32)
    tc_ref[0] = tcnt.astype(I32)
    run = run + tcnt
    run_ref[...] = run
    cnt_ref[...] = run.astype(I32)


def _route(logits, rb, tm):
    t = logits.shape[0]
    col = lambda i: (0, i)
    fixed = lambda i: (0, 0)
    tile = lambda i: (i, 0, 0)
    per_tile = jax.ShapeDtypeStruct((t // tm, N_EXPERTS, LANES), I32)
    return pl.pallas_call(
        functools.partial(_route_kernel, tm=tm),
        grid=(t // tm,),
        in_specs=[pl.BlockSpec((tm, N_EXPERTS), lambda i: (i, 0)), pl.BlockSpec((N_EXPERTS, 1), fixed)],
        out_specs=[pl.BlockSpec((TOP_K, tm), col)] * 2 + [pl.BlockSpec((1, N_EXPERTS, LANES), tile)] * 2
                  + [pl.BlockSpec((N_EXPERTS, LANES), fixed)],
        out_shape=[jax.ShapeDtypeStruct((TOP_K, t), F32), jax.ShapeDtypeStruct((TOP_K, t), I32), per_tile, per_tile,
                   jax.ShapeDtypeStruct((N_EXPERTS, LANES), I32)],
        scratch_shapes=[pltpu.VMEM((N_EXPERTS, LANES), F32)],
        compiler_params=_cparams(("arbitrary",)),
        name="route",
    )(logits, rb)


def _n_blocks_max(n_assign):
    return -(-(n_assign + N_EXPERTS * (EXPERT_BLOCK - 1)) // EXPERT_BLOCK)


def _plan_kernel(cnt_ref, ps_ref, blk_ref, nv_ref, *, nbp):
    cnt = cnt_ref[...]
    nb = lax.shift_right_logical(cnt + (EXPERT_BLOCK - 1), 8).astype(F32)
    nb8 = jnp.broadcast_to(nb, (SUBLANES, N_EXPERTS))
    nb_hi, nb_lo = _split2(nb8)
    ii = lax.broadcasted_iota(I32, (N_EXPERTS, N_EXPERTS), 0)
    jj = lax.broadcasted_iota(I32, (N_EXPERTS, N_EXPERTS), 1)
    upto = jnp.where(ii <= jj, 1.0, 0.0).astype(BF16)
    cum = (_dot(nb_hi, upto) + _dot(nb_lo, upto))[0:1, :]
    ps_ref[...] = ((cum - nb) * float(EXPERT_BLOCK)).astype(I32)
    bi = lax.broadcasted_iota(I32, (nbp, N_EXPERTS), 0).astype(F32)
    owner = jnp.sum(jnp.where(cum <= bi, 1.0, 0.0), axis=-1, keepdims=True)
    blk_ref[...] = jnp.minimum(owner, float(N_EXPERTS - 1)).astype(I32)
    nv_ref[...] = cum[:, N_EXPERTS - 1:N_EXPERTS].astype(I32)


def _plan(cnt, nbp):
    return pl.pallas_call(
        functools.partial(_plan_kernel, nbp=nbp),
        out_shape=[jax.ShapeDtypeStruct((1, N_EXPERTS), I32), jax.ShapeDtypeStruct((nbp, 1), I32),
                   jax.ShapeDtypeStruct((1, 1), I32)],
        name="plan",
    )(cnt)


def _segment_copies(tc_ref, rb_ref, ps_ref, make):
    def body(e, local):
        cnt = tc_ref[e]

        @pl.when(cnt > 0)
        def _():
            make(local, ps_ref[e] + rb_ref[e], cnt).start()

        return local + cnt

    lax.fori_loop(0, N_EXPERTS, body, 0)


def _rows(ref, row, n):
    return ref.at[pl.ds(pl.multiple_of(row * ROW_TILES, ROW_TILES), n * ROW_TILES)]


def _dispatch_kernel(lp_ref, tc_ref, rb_ref, ps_ref, cnt_ref, nv_ref, h_ref, xs_ref, sorted_ref, zero_ref, sem, zsem,
                     *, tm, nbp):
    step = pl.program_id(0)

    @pl.when(step == 0)
    def _():
        zero_ref[...] = jnp.zeros_like(zero_ref)
        for wait in (False, True):
            def tail_body(bi, carry, wait=wait):
                cp = pltpu.make_async_copy(zero_ref, _rows(xs_ref, bi * EXPERT_BLOCK, EXPERT_BLOCK), zsem)
                cp.wait() if wait else cp.start()
                return carry
            lax.fori_loop(nv_ref[0], nbp, tail_body, 0)

            def pad_body(e, carry, wait=wait):
                cnt = cnt_ref[e]
                pad = (-cnt) & (EXPERT_BLOCK - 1)

                @pl.when(pad > 0)
                def _():
                    cp = pltpu.make_async_copy(_rows(zero_ref, 0, pad), _rows(xs_ref, ps_ref[e] + cnt, pad), zsem)
                    cp.wait() if wait else cp.start()

                return carry
            lax.fori_loop(0, N_EXPERTS, pad_body, 0)

    def token_body(t, carry):
        row = h_ref[pl.ds(pl.multiple_of(t * ROW_TILES, ROW_TILES), ROW_TILES), :]
        for k in range(TOP_K):
            p = lp_ref[t * TOP_K + k]
            sorted_ref[pl.ds(pl.multiple_of(p * ROW_TILES, ROW_TILES), ROW_TILES), :] = row
        return carry

    lax.fori_loop(0, tm, token_body, 0)
    _segment_copies(tc_ref, rb_ref, ps_ref,
                    lambda loc, glob, n: pltpu.make_async_copy(_rows(sorted_ref, loc, n), _rows(xs_ref, glob, n), sem))
    pltpu.make_async_copy(sorted_ref, _rows(xs_ref, 0, tm * TOP_K), sem).wait()


def _dispatch(lp_flat, tcnt, runb, pstart, cnt, nv, hrows, nbp, tm):
    t = lp_flat.shape[0] // TOP_K
    per_tile = pl.BlockSpec((N_EXPERTS,), lambda i: (i,), memory_space=pltpu.SMEM)
    smem_all = pl.BlockSpec((N_EXPERTS,), lambda i: (0,), memory_space=pltpu.SMEM)
    return pl.pallas_call(
        functools.partial(_dispatch_kernel, tm=tm, nbp=nbp),
        grid=(t // tm,),
        in_specs=[pl.BlockSpec((tm * TOP_K,), lambda i: (i,), memory_space=pltpu.SMEM),
                  per_tile, per_tile, smem_all, smem_all,
                  pl.BlockSpec((1,), lambda i: (0,), memory_space=pltpu.SMEM),
                  pl.BlockSpec((tm * ROW_TILES, LANES), lambda i: (i, 0))],
        out_specs=pl.BlockSpec(memory_space=pl.ANY),
        out_shape=jax.ShapeDtypeStruct((nbp * EXPERT_BLOCK * ROW_TILES, LANES), F32),
        scratch_shapes=[pltpu.VMEM((tm * TOP_K * ROW_TILES, LANES), F32),
                        pltpu.VMEM((EXPERT_BLOCK * ROW_TILES, LANES), F32),
                        pltpu.SemaphoreType.DMA, pltpu.SemaphoreType.DMA],
        compiler_params=_cparams(("arbitrary",)),
        name="dispatch",
    )(lp_flat, tcnt, runb, pstart, cnt, nv, hrows)


def _experts_kernel(blk_ref, nv_ref, xs_ref, wg_ref, wu_ref, wd_ref, ys_ref, wgb_ref, wub_ref, wdb_ref):
    i = pl.program_id(0)
    nv = nv_ref[0]

    @pl.when(i >= nv)
    def _():
        ys_ref[...] = jnp.zeros_like(ys_ref)

    @pl.when(i < nv)
    def _():
        prev = blk_ref[jnp.maximum(i - 1, 0)]

        @pl.when((i == 0) | (blk_ref[i] != prev))
        def _():
            wgb_ref[...] = wg_ref[0].astype(BF16)
            wub_ref[...] = wu_ref[0].astype(BF16)
            wdb_ref[...] = wd_ref[0].astype(BF16)

        xb = jnp.concatenate(
            [xs_ref[pl.ds(s, EXPERT_BLOCK, stride=ROW_TILES), :] for s in range(ROW_TILES)], axis=1).astype(BF16)
        hid = _silu(_dot(xb, wgb_ref[...])) * _dot(xb, wub_ref[...])
        y = _dot(hid.astype(BF16), wdb_ref[...])
        for s in range(ROW_TILES):
            ys_ref[pl.ds(s, EXPERT_BLOCK, stride=ROW_TILES), :] = y[:, LANES * s:LANES * (s + 1)]


def _experts(blk, nv, xs, wg, wu, wd, nbp):
    rows = EXPERT_BLOCK * ROW_TILES
    cur = lambda i, blk_ref, nv_ref: jnp.minimum(i, nv_ref[0] - 1)
    xmap = lambda i, blk_ref, nv_ref: (cur(i, blk_ref, nv_ref), 0)
    wmap = lambda i, blk_ref, nv_ref: (blk_ref[cur(i, blk_ref, nv_ref)], 0, 0)
    return pl.pallas_call(
        _experts_kernel,
        grid_spec=pltpu.PrefetchScalarGridSpec(
            num_scalar_prefetch=2,
            grid=(nbp,),
            in_specs=[pl.BlockSpec((rows, LANES), xmap),
                      pl.BlockSpec((1, D_MODEL, D_EXPERT), wmap),
                      pl.BlockSpec((1, D_MODEL, D_EXPERT), wmap),
                      pl.BlockSpec((1, D_EXPERT, D_MODEL), wmap)],
            out_specs=pl.BlockSpec((rows, LANES), lambda i, blk_ref, nv_ref: (i, 0)),
            scratch_shapes=[pltpu.VMEM((D_MODEL, D_EXPERT), BF16), pltpu.VMEM((D_MODEL, D_EXPERT), BF16),
                            pltpu.VMEM((D_EXPERT, D_MODEL), BF16)]),
        out_shape=jax.ShapeDtypeStruct(xs.shape, F32),
        compiler_params=_cparams(("arbitrary",)),
        name="experts",
    )(blk, nv, xs, wg, wu, wd)


def _combine_kernel(lp_ref, w_ref, tc_ref, rb_ref, ps_ref, ys_ref, base_ref, g2_ref, fg_ref, out_ref,
                    buf_ref, acc_ref, sem, *, tm):
    _segment_copies(tc_ref, rb_ref, ps_ref,
                    lambda loc, glob, n: pltpu.make_async_copy(_rows(ys_ref, glob, n), _rows(buf_ref, loc, n), sem))
    pltpu.make_async_copy(_rows(ys_ref, 0, tm * TOP_K), buf_ref, sem).wait()

    def token_body(t, carry):
        j0 = t * TOP_K
        acc = jnp.zeros((ROW_TILES, LANES), F32)
        for k in range(TOP_K):
            p = lp_ref[j0 + k]
            acc = acc + w_ref[j0 + k] * buf_ref[pl.ds(pl.multiple_of(p * ROW_TILES, ROW_TILES), ROW_TILES), :]
        acc_ref[pl.ds(pl.multiple_of(t * ROW_TILES, ROW_TILES), ROW_TILES), :] = acc
        return carry

    lax.fori_loop(0, tm, token_body, 0)
    routed = jnp.concatenate([acc_ref[pl.ds(s, tm, stride=ROW_TILES), :] for s in range(ROW_TILES)], axis=1)
    out_ref[0] = _rms(base_ref[0] + g2_ref[0] * routed, fg_ref[...])


def _combine(lp_flat, w_flat, tcnt, runb, pstart, ys, base, g2, fg, tm):
    b, l, _ = base.shape
    nt = l // tm
    flat = lambda bi, i: (bi * nt + i,)
    smem_blk = pl.BlockSpec((tm * TOP_K,), flat, memory_space=pltpu.SMEM)
    per_tile = pl.BlockSpec((N_EXPERTS,), flat, memory_space=pltpu.SMEM)
    return pl.pallas_call(
        functools.partial(_combine_kernel, tm=tm),
        grid=(b, nt),
        in_specs=[smem_blk, smem_blk, per_tile, per_tile,
                  pl.BlockSpec((N_EXPERTS,), lambda bi, i: (0,), memory_space=pltpu.SMEM),
                  pl.BlockSpec(memory_space=pl.ANY),
                  pl.BlockSpec((1, tm, D_MODEL), lambda bi, i: (bi, i, 0)),
                  pl.BlockSpec((1, 1, D_MODEL), lambda bi, i: (bi, 0, 0)),
                  pl.BlockSpec((1, D_MODEL), lambda bi, i: (0, 0))],
        out_specs=pl.BlockSpec((1, tm, D_MODEL), lambda bi, i: (bi, i, 0)),
        out_shape=jax.ShapeDtypeStruct((b, l, D_MODEL), F32),
        scratch_shapes=[pltpu.VMEM((tm * TOP_K * ROW_TILES, LANES), F32),
                        pltpu.VMEM((tm * ROW_TILES, LANES), F32),
                        pltpu.SemaphoreType.DMA],
        compiler_params=_cparams(("arbitrary", "arbitrary")),
        name="combine",
    )(lp_flat, w_flat, tcnt, runb, pstart, ys, base, g2, fg)


def _mixer_inputs(h, shift, scale, gain, wm, wl, wa, ba):
    return _proj(h, shift, scale, gain, wm, wl, wa, ba)


def kernel(x, c, ctx, c_ctx, ada_w, ada_b, norm1_g, norm2_g, w_in, gla_wa_f, gla_ba_f, gla_wa_b, gla_ba_b, gla_norm_g, s5_lam_re_f, s5_lam_im_f, s5_log_step_f, s5_lam_re_b, s5_lam_im_b, s5_log_step_b, s5_b_re, s5_b_im, s5_c_re_f, s5_c_im_f, s5_c_re_b, s5_c_im_b, s5_d, s5_glu_w, s5_glu_b, w_out, router_w, router_b, exp_w_gate, exp_w_up, exp_w_down, sh_w_gate, sh_w_up, sh_w_down, final_norm_g):
    b, l, d = x.shape
    i = 0

    rows = -(-(b + 1) // SUBLANES) * SUBLANES
    cs = jnp.zeros((rows, d), F32).at[:b].set(c).at[b].set(c_ctx)
    mod = _adaln(cs, ada_w[i], ada_b[i][None, :])
    sh1, sc1, g1, sh2, sc2, g2 = [mod[:b, d * j:d * (j + 1)][:, None, :] for j in range(6)]
    csh1, csc1 = [jnp.broadcast_to(mod[b, d * j:d * (j + 1)][None, None, :], (b, 1, d)) for j in range(2)]

    w = w_in[i]
    o1, o2, o3, o4, o5, o6 = 256, 512, 1024, 1536, 1552, 1568
    wm = jnp.concatenate([w[:, :o4], w[:, o6:]], axis=1).astype(BF16)
    wl = jnp.zeros((d, LANES), F32).at[:, :2 * GLA_GATE_RANK].set(w[:, o4:o6]).astype(BF16)
    wa = jnp.zeros((LANES, 2 * GLA_DK), F32)
    wa = wa.at[:GLA_GATE_RANK, :GLA_DK].set(gla_wa_f[i]).at[GLA_GATE_RANK:2 * GLA_GATE_RANK, GLA_DK:].set(gla_wa_b[i])
    wa = wa.astype(BF16)
    ba = jnp.concatenate([gla_ba_f[i], gla_ba_b[i]])[None, :]
    n1 = norm1_g[i][None, :]

    pcols = jnp.stack([s5_lam_re_f[i], s5_lam_im_f[i],
                       jnp.broadcast_to(s5_log_step_f[i][:, None], (S5_GROUPS, S5_STATE)),
                       s5_lam_re_b[i], s5_lam_im_b[i],
                       jnp.broadcast_to(s5_log_step_b[i][:, None], (S5_GROUPS, S5_STATE)),
                       jnp.zeros((S5_GROUPS, S5_STATE), F32), jnp.zeros((S5_GROUPS, S5_STATE), F32)], axis=-1)
    prows = pcols.transpose(0, 2, 1)
    tile_b = lambda t: jnp.tile(t, (1, 1, S5_CHUNK))
    tile_c = lambda t: jnp.tile(t.transpose(0, 2, 1), (1, 1, S5_CHUNK))
    m_op, wt_op, v_op, ab4 = _s5gen(pcols, prows, tile_b(s5_b_re[i]), tile_b(s5_b_im[i]),
                                    tile_c(s5_c_re_f[i]), tile_c(s5_c_im_f[i]),
                                    tile_c(s5_c_re_b[i]), tile_c(s5_c_im_b[i]))
    ab = ab4.reshape(S5_GROUPS, 1, 4 * S5_STATE)

    cq, ck, cv, _, _, claf, clab, cuv = _proj(ctx, csh1, csc1, n1, wm, wl, wa, ba)
    zero_state = jnp.zeros((b, GLA_HEADS, GLA_DV_HEAD, LANES), F32)
    gsf, gsb = _gla(cq, ck, cv, claf, clab, zero_state, zero_state, with_output=False)
    (x0,) = _s5(cuv, None, wt_op, None, ab, jnp.zeros((S5_GROUPS, b, S5_VEC), F32), b, with_output=False)

    q, k, v, go, u, laf, lab, uv = _proj(x, sh1, sc1, n1, wm, wl, wa, ba)
    o, _, _ = _gla(q, k, v, laf, lab, gsf, gsb, with_output=True)
    yvec, _ = _s5(uv, m_op, wt_op, v_op, ab, x0, b, with_output=True)

    rw_hi = router_w[i].astype(BF16)
    rw_lo = (router_w[i] - rw_hi.astype(F32)).astype(BF16)
    base, hrows, logits = _post(
        x, o, go, u, yvec, g1, sh2, sc2, g2,
        gla_norm_g[i][None, :], s5_d[i][None, :], s5_glu_w[i].astype(BF16),
        s5_glu_b[i][None, :], w_out[i].astype(BF16), norm2_g[i][None, :], rw_hi, rw_lo,
        jnp.concatenate([sh_w_gate[i], sh_w_up[i]], axis=1).astype(BF16), sh_w_down[i].astype(BF16))

    t = b * l
    tile = min(512, l)
    wts, lpos, runb, tcnt, cnt = _route(logits.reshape(t, N_EXPERTS), router_b[i][:, None], tile)
    cnt_flat = cnt[:, 0]
    runb_flat, tcnt_flat = runb[:, :, 0].reshape(-1), tcnt[:, :, 0].reshape(-1)
    nbp = -(-_n_blocks_max(t * TOP_K) // SUBLANES) * SUBLANES
    pstart, blk, nv = _plan(cnt_flat[None, :], nbp)
    lp_flat, w_flat = lpos.T.reshape(-1), wts.T.reshape(-1)
    ps_flat = pstart.reshape(-1)
    xs = _dispatch(lp_flat, tcnt_flat, runb_flat, ps_flat, cnt_flat, nv.reshape(-1), hrows, nbp, tile)
    assert exp_w_gate.shape[0] == 1, "single-layer block"
    ys = _experts(blk.reshape(-1), nv.reshape(-1), xs, exp_w_gate.reshape(exp_w_gate.shape[1:]),
                  exp_w_up.reshape(exp_w_up.shape[1:]), exp_w_down.reshape(exp_w_down.shape[1:]), nbp)
    return _combine(lp_flat, w_flat, tcnt_flat, runb_flat, ps_flat, ys, base, g2, final_norm_g[None, :], tile)
```

```python
import functools

import jax
import jax.numpy as jnp
from jax import lax
from jax.experimental import pallas as pl
from jax.experimental.pallas import tpu as pltpu

F32 = jnp.float32
BF16 = jnp.bfloat16
I32 = jnp.int32

D_MODEL = 1024
GLA_HEADS = 4
GLA_DK_HEAD = 64
GLA_DV_HEAD = 128
GLA_DK = 256
GLA_DV = 512
GLA_GATE_RANK = 16
GLA_GATE_TAU = 16.0
GLA_CHUNK = 64
D_S5 = 512
S5_GROUP_CH = 16
S5_GROUPS = 32
S5_STATE = 64
S5_CHUNK = 16
S5_VEC = S5_CHUNK * S5_GROUP_CH
N_EXPERTS = 256
TOP_K = 8
N_EXPERT_GROUPS = 8
TOPK_GROUPS = 4
D_EXPERT = 256
D_SHARED = 256
ROUTE_SCALE = 2.5
EPS = 1e-6

LANES = 128
SUBLANES = 8
ROW_TILES = D_MODEL // LANES
EXPERT_BLOCK = 256
VMEM_LIMIT = 56 * 1024 * 1024


def _cparams(sem):
    return pltpu.CompilerParams(dimension_semantics=sem, vmem_limit_bytes=VMEM_LIMIT)


def _dot(a, b):
    return jnp.dot(a, b, preferred_element_type=F32)


def _dot_nt(a, b):
    return lax.dot_general(a, b, (((1,), (1,)), ((), ())), preferred_element_type=F32)


def _dot_tn(a, b):
    return lax.dot_general(a, b, (((0,), (0,)), ((), ())), preferred_element_type=F32)


def _split2(x):
    hi = x.astype(BF16)
    lo = (x - hi.astype(F32)).astype(BF16)
    return hi, lo


def _dot3(a, b_hi, b_lo):
    a_hi, a_lo = _split2(a)
    return _dot(a_hi, b_hi) + (_dot(a_hi, b_lo) + _dot(a_lo, b_hi))


def _silu(x):
    return x * jax.nn.sigmoid(x)


def _rms(x, g):
    return x * lax.rsqrt(jnp.mean(x * x, axis=-1, keepdims=True) + EPS) * g


def _adaln_kernel(c_ref, w_ref, b_ref, o_ref):
    s = _silu(c_ref[...])
    w_hi, w_lo = _split2(w_ref[...])
    o_ref[...] = _dot3(s, w_hi, w_lo) + b_ref[...]


def _adaln(cs, w, b):
    rows, n = cs.shape[0], w.shape[1]
    tn = 1024
    return pl.pallas_call(
        _adaln_kernel,
        grid=(n // tn,),
        in_specs=[pl.BlockSpec((rows, D_MODEL), lambda j: (0, 0)),
                  pl.BlockSpec((D_MODEL, tn), lambda j: (0, j)),
                  pl.BlockSpec((1, tn), lambda j: (0, j))],
        out_specs=pl.BlockSpec((rows, tn), lambda j: (0, j)),
        out_shape=jax.ShapeDtypeStruct((rows, n), F32),
        compiler_params=_cparams(("arbitrary",)),
        name="adaln",
    )(cs, w, b)


def _group_lane_masks(rows):
    grp = lax.shift_right_logical(lax.broadcasted_iota(I32, (rows, LANES), 1), 4)
    return [grp == j for j in range(LANES // S5_GROUP_CH)]


def _move_group(x, src, dst):
    shift = ((dst - src) * S5_GROUP_CH) % LANES
    return pltpu.roll(x, shift, 1) if shift else x


def _proj_kernel(x_ref, sh_ref, sc_ref, g_ref, wm_ref, wl_ref, wa_ref, ba_ref,
                 q_ref, k_ref, v_ref, go_ref, u_ref, laf_ref, lab_ref, uv_ref, ut_ref, *, tm):
    h = _rms(x_ref[0], g_ref[...]) * (1.0 + sc_ref[0]) + sh_ref[0]
    hb = h.astype(BF16)
    q_ref[0] = _dot(hb, wm_ref[:, 0:256]) * (GLA_DK_HEAD ** -0.5)
    k_ref[0] = _dot(hb, wm_ref[:, 256:512])
    v_ref[0] = _dot(hb, wm_ref[:, 512:1024])
    go_ref[0] = _dot(hb, wm_ref[:, 1024:1536])
    u = _dot(hb, wm_ref[:, 1536:2048])
    u_ref[0] = u
    for t in range(D_S5 // LANES):
        ut_ref[t] = u[:, LANES * t:LANES * (t + 1)]
    lr = _dot(hb, wl_ref[...])
    pre = _dot(lr.astype(BF16), wa_ref[...]) + ba_ref[...]
    la = (jnp.minimum(pre, 0.0) - jnp.log1p(jnp.exp(-jnp.abs(pre)))) * (1.0 / GLA_GATE_TAU)
    laf_ref[0] = la[:, 0:GLA_DK]
    lab_ref[0] = la[:, GLA_DK:2 * GLA_DK]
    nc = tm // S5_CHUNK
    gpt = LANES // S5_GROUP_CH
    masks = _group_lane_masks(nc)
    for t in range(D_S5 // LANES):
        steps = [ut_ref[t, pl.ds(s, nc, stride=S5_CHUNK), :] for s in range(S5_CHUNK)]
        for gl in range(gpt):
            for half in range(S5_VEC // LANES):
                acc = None
                for j in range(gpt):
                    piece = _move_group(steps[half * gpt + j], gl, j)
                    acc = piece if acc is None else jnp.where(masks[j], piece, acc)
                uv_ref[t * gpt + gl, :, LANES * half:LANES * (half + 1)] = acc.astype(BF16)


def _proj(x, shift, scale, gain, wm, wl, wa, ba):
    b, l, _ = x.shape
    tm = min(512, l)
    nt = l // tm
    row = lambda bi, i: (bi, i, 0)
    mod = lambda bi, i: (bi, 0, 0)
    full = lambda bi, i: (0, 0)
    widths = (GLA_DK, GLA_DK, GLA_DV, GLA_DV, D_S5, GLA_DK, GLA_DK)
    return pl.pallas_call(
        functools.partial(_proj_kernel, tm=tm),
        grid=(b, nt),
        in_specs=[pl.BlockSpec((1, tm, D_MODEL), row),
                  pl.BlockSpec((1, 1, D_MODEL), mod),
                  pl.BlockSpec((1, 1, D_MODEL), mod),
                  pl.BlockSpec((1, D_MODEL), full),
                  pl.BlockSpec(wm.shape, full),
                  pl.BlockSpec(wl.shape, full),
                  pl.BlockSpec(wa.shape, full),
                  pl.BlockSpec(ba.shape, full)],
        out_specs=[pl.BlockSpec((1, tm, w), row) for w in widths]
                  + [pl.BlockSpec((S5_GROUPS, tm // S5_CHUNK, S5_VEC), lambda bi, i: (0, bi * nt + i, 0))],
        out_shape=[jax.ShapeDtypeStruct((b, l, w), F32) for w in widths]
                  + [jax.ShapeDtypeStruct((S5_GROUPS, b * l // S5_CHUNK, S5_VEC), BF16)],
        scratch_shapes=[pltpu.VMEM((D_S5 // LANES, tm, LANES), F32)],
        compiler_params=_cparams(("arbitrary", "arbitrary")),
        name="proj",
    )(x, shift, scale, gain, wm, wl, wa, ba)


def _gla_kernel(*refs, n_chunks, with_output):
    if with_output:
        q_ref, k_ref, v_ref, laf_ref, lab_ref, s0f_ref, s0b_ref, o_ref, sf_ref, sb_ref, st_ref, ob_ref = refs
    else:
        q_ref, k_ref, v_ref, laf_ref, lab_ref, s0f_ref, s0b_ref, sf_ref, sb_ref, st_ref = refs
        o_ref = ob_ref = None
    c = GLA_CHUNK
    row = lax.broadcasted_iota(I32, (c, c), 0)
    col = lax.broadcasted_iota(I32, (c, c), 1)
    lane = lax.broadcasted_iota(I32, (c, LANES), 1)
    st_ref[0] = s0f_ref[0]
    st_ref[1] = s0b_ref[0]

    def chunk(direction, idx):
        la_ref = laf_ref if direction == 0 else lab_ref
        tri = (row >= col) if direction == 0 else (row <= col)
        trib = jnp.where(tri, 1.0, 0.0).astype(BF16)
        r0 = pl.multiple_of(idx * c, c)
        q = q_ref[0, pl.ds(r0, c), :]
        k = k_ref[0, pl.ds(r0, c), :]
        v = v_ref[0, pl.ds(r0, c), :]
        la_hi, la_lo = _split2(la_ref[0, pl.ds(r0, c), :])
        cum = _dot(trib, la_hi) + _dot(trib, la_lo)
        tot = cum[c - 1:c, :] if direction == 0 else cum[0:1, :]
        qd = q * jnp.exp(cum)
        ki = k * jnp.exp(-cum)
        ks = k * jnp.exp(tot - cum)
        dec = jnp.exp(tot)
        for h in range(GLA_HEADS):
            pair = slice(LANES * (h // 2), LANES * (h // 2) + LANES)
            own = (lane >= GLA_DK_HEAD * (h % 2)) & (lane < GLA_DK_HEAD * (h % 2) + GLA_DK_HEAD)
            vb = v[:, GLA_DV_HEAD * h:GLA_DV_HEAD * (h + 1)].astype(BF16)
            st = st_ref[direction, h]
            if with_output:
                qb = qd[:, pair].astype(BF16)
                kib = jnp.where(own, ki[:, pair], 0.0).astype(BF16)
                sc = jnp.where(tri, _dot_nt(qb, kib), 0.0)
                o = _dot(sc.astype(BF16), vb) + _dot_nt(qb, st.astype(BF16))
                cols = slice(GLA_DV_HEAD * h, GLA_DV_HEAD * (h + 1))
                if direction == 0:
                    o_ref[0, pl.ds(r0, c), cols] = o
                else:
                    ob_ref[pl.ds(r0, c), cols] = o
            ksb = jnp.where(own, ks[:, pair], 0.0).astype(BF16)
            st_ref[direction, h] = st * dec[:, pair] + _dot_tn(vb, ksb)

    def body(ci, carry):
        chunk(0, ci)
        chunk(1, n_chunks - 1 - ci)
        return carry

    lax.fori_loop(0, n_chunks, body, 0)
    sf_ref[0] = st_ref[0]
    sb_ref[0] = st_ref[1]
    if with_output:
        o_ref[0] = o_ref[0] + ob_ref[...]


def _gla(q, k, v, laf, lab, s0f, s0b, with_output):
    b, l, _ = q.shape
    n_chunks = l // GLA_CHUNK
    seq = lambda bi: (bi, 0, 0)
    st = lambda bi: (bi, 0, 0, 0)
    st_shape = (b, GLA_HEADS, GLA_DV_HEAD, LANES)
    st_spec = pl.BlockSpec((1, GLA_HEADS, GLA_DV_HEAD, LANES), st)
    out_specs = [st_spec, st_spec]
    out_shape = [jax.ShapeDtypeStruct(st_shape, F32)] * 2
    if with_output:
        out_specs = [pl.BlockSpec((1, l, GLA_DV), seq)] + out_specs
        out_shape = [jax.ShapeDtypeStruct((b, l, GLA_DV), F32)] + out_shape
    return pl.pallas_call(
        functools.partial(_gla_kernel, n_chunks=n_chunks, with_output=with_output),
        grid=(b,),
        in_specs=[pl.BlockSpec((1, l, GLA_DK), seq),
                  pl.BlockSpec((1, l, GLA_DK), seq),
                  pl.BlockSpec((1, l, GLA_DV), seq),
                  pl.BlockSpec((1, l, GLA_DK), seq),
                  pl.BlockSpec((1, l, GLA_DK), seq),
                  st_spec, st_spec],
        out_specs=out_specs,
        out_shape=out_shape,
        scratch_shapes=[pltpu.VMEM((2, GLA_HEADS, GLA_DV_HEAD, LANES), F32)]
                       + ([pltpu.VMEM((l, GLA_DV), F32)] if with_output else []),
        compiler_params=_cparams(("arbitrary",)),
        name="gla_out" if with_output else "gla_ctx",
    )(q, k, v, laf, lab, s0f, s0b)


def _s5gen_kernel(pc_ref, pr_ref, btr_ref, bti_ref, ctrf_ref, ctif_ref, ctrb_ref, ctib_ref,
                  m_ref, wt_ref, v_ref, ab_ref):
    pc = pc_ref[0]
    blk = lax.shift_right_logical(lax.broadcasted_iota(I32, (1, S5_VEC), 1), 4).astype(F32)
    lane = lax.broadcasted_iota(I32, (S5_GROUP_CH, S5_VEC), 1)
    n = float(S5_CHUNK)

    def cmul(ar, ai, br, bi):
        return ar * br - ai * bi, ar * bi + ai * br

    kcat = []
    for d in (0, 1):
        lre, lim, ls = pc[:, 3 * d:3 * d + 1], pc[:, 3 * d + 1:3 * d + 2], pc[:, 3 * d + 2:3 * d + 3]
        ctr = (ctrf_ref if d == 0 else ctrb_ref)[0]
        cti = (ctif_ref if d == 0 else ctib_ref)[0]
        step = jnp.exp(ls)
        mag = jnp.exp(lre * step)
        a_re = mag * jnp.cos(lim * step)
        a_im = mag * jnp.sin(lim * step)
        den = lre * lre + lim * lim
        f_re = ((a_re - 1.0) * lre + a_im * lim) / den
        f_im = (a_im * lre - (a_re - 1.0) * lim) / den
        bb_re, bb_im = cmul(f_re, f_im, btr_ref[0], bti_ref[0])

        def powers(e, lre=lre, lim=lim, step=step):
            m = jnp.exp(lre * step * e)
            ang = lim * step * e
            return m * jnp.cos(ang), m * jnp.sin(ang)

        w_re, w_im = cmul(*powers((n - 1.0 - blk) if d == 0 else blk), bb_re, bb_im)
        wt_ref[0, S5_STATE * d:S5_STATE * (d + 1), :] = w_re
        wt_ref[0, 2 * S5_STATE + S5_STATE * d:2 * S5_STATE + S5_STATE * (d + 1), :] = w_im
        c_re, c_im = cmul(*powers((blk + 1.0) if d == 0 else (n - blk)), ctr, cti)
        v_ref[0, S5_STATE * d:S5_STATE * (d + 1), :] = c_re
        v_ref[0, 2 * S5_STATE + S5_STATE * d:2 * S5_STATE + S5_STATE * (d + 1), :] = -c_im
        e_re, e_im = cmul(*powers(blk if d == 0 else (n - 1.0 - blk)), ctr, cti)
        b16r_hi, b16r_lo = _split2(bb_re[:, 0:S5_GROUP_CH])
        b16i_hi, b16i_lo = _split2(bb_im[:, 0:S5_GROUP_CH])
        er_hi, er_lo = _split2(e_re)
        ei_hi, ei_lo = _split2(e_im)
        kr = _dot_tn(b16r_hi, er_hi) + (_dot_tn(b16r_hi, er_lo) + _dot_tn(b16r_lo, er_hi))
        ki = _dot_tn(b16i_hi, ei_hi) + (_dot_tn(b16i_hi, ei_lo) + _dot_tn(b16i_lo, ei_hi))
        kcat.append(kr - ki)

    for s in range(S5_CHUNK):
        sh_f = S5_GROUP_CH * s
        fwd = kcat[0] if s == 0 else pltpu.roll(kcat[0], sh_f, 1)
        fwd = jnp.where(lane >= sh_f, fwd, 0.0)
        sh_b = S5_VEC - S5_GROUP_CH * (S5_CHUNK - 1 - s)
        bwd = kcat[1] if sh_b == S5_VEC else pltpu.roll(kcat[1], sh_b, 1)
        bwd = jnp.where(lane < S5_GROUP_CH * (s + 1), bwd, 0.0)
        m_ref[0, S5_GROUP_CH * s:S5_GROUP_CH * (s + 1), :] = fwd + bwd

    pr = pr_ref[0]
    for d in (0, 1):
        lre, lim, ls = pr[3 * d:3 * d + 1, :], pr[3 * d + 1:3 * d + 2, :], pr[3 * d + 2:3 * d + 3, :]
        stp = jnp.exp(ls) * n
        mg = jnp.exp(lre * stp)
        ab_ref[0, d:d + 1, :] = mg * jnp.cos(lim * stp)
        ab_ref[0, 2 + d:3 + d, :] = mg * jnp.sin(lim * stp)


def _s5gen(pc, pr, btr, bti, ctrf, ctif, ctrb, ctib):
    g = pc.shape[0]
    blk3 = lambda shape: pl.BlockSpec((1,) + shape, lambda i: (i, 0, 0))
    big = (S5_STATE, S5_VEC)
    sq = (S5_VEC, S5_VEC)
    return pl.pallas_call(
        _s5gen_kernel,
        grid=(g,),
        in_specs=[blk3((S5_STATE, 8)), blk3((8, S5_STATE))] + [blk3(big)] * 6,
        out_specs=[blk3(sq), blk3(sq), blk3(sq), blk3((4, S5_STATE))],
        out_shape=[jax.ShapeDtypeStruct((g,) + sq, F32)] * 3 + [jax.ShapeDtypeStruct((g, 4, S5_STATE), F32)],
        compiler_params=_cparams(("arbitrary",)),
        name="s5gen",
    )(pc, pr, btr, bti, ctrf, ctif, ctrb, ctib)


def _s5_kernel(*refs, n_chunks, nb, with_output):
    if with_output:
        u_ref, m_ref, wt_ref, v_ref, ab_ref, x0_ref, y_ref, xf_ref, z_ref, cin_ref = refs
    else:
        u_ref, wt_ref, ab_ref, x0_ref, xf_ref, z_ref = refs
    wtb = wt_ref[0].astype(BF16)
    for bi in range(nb):
        z = _dot_nt(u_ref[0, bi * n_chunks:(bi + 1) * n_chunks, :], wtb)
        z_ref[0, pl.ds(bi, n_chunks, stride=nb), :] = z[:, 0:LANES]
        z_ref[1, pl.ds(bi, n_chunks, stride=nb), :] = z[:, LANES:2 * LANES]
    ab = ab_ref[0]
    ar, ai = ab[:, 0:LANES], ab[:, LANES:2 * LANES]
    is_f = lax.broadcasted_iota(I32, (nb, LANES), 1) < S5_STATE
    x0 = x0_ref[0]

    def body(i, carry):
        xr, xi = carry
        rf = pl.multiple_of(i * nb, nb)
        rb = pl.multiple_of((n_chunks - 1 - i) * nb, nb)
        if with_output:
            cin_ref[0, pl.ds(rf, nb), 0:S5_STATE] = xr[:, 0:S5_STATE]
            cin_ref[1, pl.ds(rf, nb), 0:S5_STATE] = xi[:, 0:S5_STATE]
            cin_ref[0, pl.ds(rb, nb), S5_STATE:LANES] = xr[:, S5_STATE:LANES]
            cin_ref[1, pl.ds(rb, nb), S5_STATE:LANES] = xi[:, S5_STATE:LANES]
        zr = jnp.where(is_f, z_ref[0, pl.ds(rf, nb), :], z_ref[0, pl.ds(rb, nb), :])
        zi = jnp.where(is_f, z_ref[1, pl.ds(rf, nb), :], z_ref[1, pl.ds(rb, nb), :])
        return ar * xr - ai * xi + zr, ar * xi + ai * xr + zi

    xr, xi = lax.fori_loop(0, n_chunks, body, (x0[:, 0:LANES], x0[:, LANES:2 * LANES]))
    xf_ref[0, :, 0:LANES] = xr
    xf_ref[0, :, LANES:2 * LANES] = xi
    if with_output:
        mb = m_ref[0].astype(BF16)
        vb = v_ref[0].astype(BF16)
        for bi in range(nb):
            rows = slice(bi * n_chunks, (bi + 1) * n_chunks)
            carried = jnp.concatenate([cin_ref[0, pl.ds(bi, n_chunks, stride=nb), :],
                                       cin_ref[1, pl.ds(bi, n_chunks, stride=nb), :]], axis=1).astype(BF16)
            y_ref[0, rows, :] = _dot(u_ref[0, rows, :], mb) + _dot(carried, vb)


def _s5(uvec, m, wt, v, ab, x0, nb, with_output):
    g, rows, _ = uvec.shape
    n_chunks = rows // nb
    blk3 = lambda shape: pl.BlockSpec((1,) + shape, lambda i: (i, 0, 0))
    sq = (S5_VEC, S5_VEC)
    st = (nb, S5_VEC)
    if with_output:
        args = (uvec, m, wt, v, ab, x0)
        in_specs = [blk3((rows, S5_VEC)), blk3(sq), blk3(sq), blk3(sq), blk3((1, S5_VEC)), blk3(st)]
        out_specs = [blk3((rows, S5_VEC)), blk3(st)]
        out_shape = [jax.ShapeDtypeStruct((g, rows, S5_VEC), F32), jax.ShapeDtypeStruct((g,) + st, F32)]
        scratch = [pltpu.VMEM((S5_VEC // LANES, rows, LANES), F32), pltpu.VMEM((S5_VEC // LANES, rows, LANES), F32)]
    else:
        args = (uvec, wt, ab, x0)
        in_specs = [blk3((rows, S5_VEC)), blk3(sq), blk3((1, S5_VEC)), blk3(st)]
        out_specs = [blk3(st)]
        out_shape = [jax.ShapeDtypeStruct((g,) + st, F32)]
        scratch = [pltpu.VMEM((S5_VEC // LANES, rows, LANES), F32)]
    return pl.pallas_call(
        functools.partial(_s5_kernel, n_chunks=n_chunks, nb=nb, with_output=with_output),
        grid=(g,),
        in_specs=in_specs,
        out_specs=out_specs,
        out_shape=out_shape,
        scratch_shapes=scratch,
        compiler_params=_cparams(("arbitrary",)),
        name="s5_out" if with_output else "s5_ctx",
    )(*args)


def _post_kernel(x_ref, o_ref, go_ref, u_ref, yv_ref, g1_ref, sh2_ref, sc2_ref, g2_ref,
                 gn_ref, d_ref, gw_ref, gb_ref, wo_ref, n2_ref, rwh_ref, rwl_ref, sgu_ref, sd_ref,
                 base_ref, hrow_ref, lg_ref, y_ref, *, tm):
    nc = tm // S5_CHUNK
    gpt = LANES // S5_GROUP_CH
    masks = _group_lane_masks(nc)
    for s in range(S5_CHUNK):
        half, j = divmod(s, gpt)
        for t in range(D_S5 // LANES):
            acc = None
            for gl in range(gpt):
                piece = _move_group(yv_ref[t * gpt + gl, :, LANES * half:LANES * (half + 1)], j, gl)
                acc = piece if acc is None else jnp.where(masks[gl], piece, acc)
            y_ref[t, pl.ds(s, nc, stride=S5_CHUNK), :] = acc
    o = o_ref[0]
    gn = gn_ref[...]
    heads = [_rms(o[:, GLA_DV_HEAD * h:GLA_DV_HEAD * (h + 1)], gn) for h in range(GLA_HEADS)]
    gla_out = jnp.concatenate(heads, axis=1) * _silu(go_ref[0])
    yy = jnp.concatenate([y_ref[t] for t in range(D_S5 // LANES)], axis=1) + d_ref[...] * u_ref[0]
    z = 0.5 * yy * (1.0 + jnp.tanh(0.7978845608028654 * (yy + 0.044715 * (yy * yy * yy))))
    s5_out = z * jax.nn.sigmoid(_dot(z.astype(BF16), gw_ref[...]) + gb_ref[...])
    mix = jnp.concatenate([gla_out, s5_out], axis=1).astype(BF16)
    x1 = x_ref[0] + g1_ref[0] * _dot(mix, wo_ref[...])
    h2 = _rms(x1, n2_ref[...]) * (1.0 + sc2_ref[0]) + sh2_ref[0]
    lg_ref[0] = _dot3(h2, rwh_ref[...], rwl_ref[...])
    hb = h2.astype(BF16)
    gu = _dot(hb, sgu_ref[...])
    hid = _silu(gu[:, 0:D_SHARED]) * gu[:, D_SHARED:2 * D_SHARED]
    base_ref[0] = x1 + g2_ref[0] * _dot(hid.astype(BF16), sd_ref[...])
    for s in range(ROW_TILES):
        hrow_ref[pl.ds(s, tm, stride=ROW_TILES), :] = h2[:, LANES * s:LANES * (s + 1)]


def _post(x, o, go, u, yvec, g1, sh2, sc2, g2, gn, d, gw, gb, wo, n2, rwh, rwl, sgu, sd):
    b, l, _ = x.shape
    tm = 256
    nt = l // tm
    row = lambda bi, i: (bi, i, 0)
    mod = lambda bi, i: (bi, 0, 0)
    full = lambda bi, i: (0, 0)
    ws = (gn, d, gw, gb, wo, n2, rwh, rwl, sgu, sd)
    return pl.pallas_call(
        functools.partial(_post_kernel, tm=tm),
        grid=(b, nt),
        in_specs=[pl.BlockSpec((1, tm, D_MODEL), row)]
                 + [pl.BlockSpec((1, tm, 512), row)] * 3
                 + [pl.BlockSpec((S5_GROUPS, tm // S5_CHUNK, S5_VEC), lambda bi, i: (0, bi * nt + i, 0))]
                 + [pl.BlockSpec((1, 1, D_MODEL), mod)] * 4
                 + [pl.BlockSpec(w.shape, full) for w in ws],
        out_specs=[pl.BlockSpec((1, tm, D_MODEL), row),
                   pl.BlockSpec((tm * ROW_TILES, LANES), lambda bi, i: (bi * nt + i, 0)),
                   pl.BlockSpec((1, tm, N_EXPERTS), row)],
        out_shape=[jax.ShapeDtypeStruct((b, l, D_MODEL), F32),
                   jax.ShapeDtypeStruct((b * l * ROW_TILES, LANES), F32),
                   jax.ShapeDtypeStruct((b, l, N_EXPERTS), F32)],
        scratch_shapes=[pltpu.VMEM((D_S5 // LANES, tm, LANES), F32)],
        compiler_params=_cparams(("arbitrary", "arbitrary")),
        name="post",
    )(x, o, go, u, yvec, g1, sh2, sc2, g2, *ws)


def _route_kernel(lg_ref, rb_ref, w_ref, p_ref, rb4_ref, tc_ref, cnt_ref, run_ref, *, tm):
    @pl.when(pl.program_id(0) == 0)
    def _():
        run_ref[...] = jnp.zeros_like(run_ref)

    neg = -jnp.inf
    gsz = N_EXPERTS // N_EXPERT_GROUPS
    s = jax.nn.sigmoid(lg_ref[...].T)
    biased = s + rb_ref[...]
    row = lax.broadcasted_iota(I32, (N_EXPERTS, tm), 0).astype(F32)

    def first_max(m, idx):
        mx = jnp.max(m, axis=0, keepdims=True)
        ix = jnp.min(jnp.where(m == mx, idx, float(N_EXPERTS)), axis=0, keepdims=True)
        return mx, ix

    grow = lax.broadcasted_iota(I32, (gsz, tm), 0).astype(F32)
    gs = []
    for g in range(N_EXPERT_GROUPS):
        m, idx = biased[gsz * g:gsz * (g + 1), :], grow + float(gsz * g)
        m1, i1 = first_max(m, idx)
        gs.append(m1 + jnp.max(jnp.where(idx == i1, neg, m), axis=0, keepdims=True))
    kept = []
    for g in range(N_EXPERT_GROUPS):
        ahead = jnp.zeros((1, tm), F32)
        for j in range(N_EXPERT_GROUPS):
            if j < g:
                ahead = ahead + jnp.where(gs[j] >= gs[g], 1.0, 0.0)
            elif j > g:
                ahead = ahead + jnp.where(gs[j] > gs[g], 1.0, 0.0)
        kept.append(jnp.where(ahead < float(TOPK_GROUPS), biased[gsz * g:gsz * (g + 1), :], neg))
    masked = jnp.concatenate(kept, axis=0)

    onehot = jnp.zeros((N_EXPERTS, tm), F32)
    ids, ws = [], []
    for _ in range(TOP_K):
        _, ik = first_max(masked, row)
        hit = row == ik
        ids.append(ik)
        ws.append(jnp.sum(jnp.where(hit, s, 0.0), axis=0, keepdims=True))
        onehot = onehot + jnp.where(hit, 1.0, 0.0)
        masked = jnp.where(hit, neg, masked)
    wsum = ws[0]
    for k in range(1, TOP_K):
        wsum = wsum + ws[k]

    ss = lax.broadcasted_iota(I32, (tm, tm), 0)
    tt = lax.broadcasted_iota(I32, (tm, tm), 1)
    earlier = jnp.where(ss < tt, 1.0, 0.0).astype(BF16)
    ohb = onehot.astype(BF16)
    tcnt = _dot(ohb, jnp.ones((tm, LANES), BF16))
    ee = lax.broadcasted_iota(I32, (N_EXPERTS, N_EXPERTS), 0)
    ff = lax.broadcasted_iota(I32, (N_EXPERTS, N_EXPERTS), 1)
    below = jnp.where(ff < ee, 1.0, 0.0).astype(BF16)
    t_hi, t_lo = _split2(tcnt)
    toff = _dot(below, t_hi) + _dot(below, t_lo)
    lpos = _dot(ohb, earlier) + toff[:, 0:1]
    w_ref[...] = jnp.concatenate([w / wsum * ROUTE_SCALE for w in ws], axis=0)
    p_ref[...] = jnp.concatenate(
        [jnp.sum(jnp.where(row == ids[k], lpos, 0.0), axis=0, keepdims=True) for k in range(TOP_K)], axis=0).astype(I32)
    run = run_ref[...]
    rb4_ref[0] = run.astype(I32)
    tc_ref[0] = tcnt.astype(I32)
    run = run + tcnt
    run_ref[...] = run
    cnt_ref[...] = run.astype(I32)


def _route(logits, rb, tm):
    t = logits.shape[0]
    col = lambda i: (0, i)
    fixed = lambda i: (0, 0)
    tile = lambda i: (i, 0, 0)
    per_tile = jax.ShapeDtypeStruct((t // tm, N_EXPERTS, LANES), I32)
    return pl.pallas_call(
        functools.partial(_route_kernel, tm=tm),
        grid=(t // tm,),
        in_specs=[pl.BlockSpec((tm, N_EXPERTS), lambda i: (i, 0)), pl.BlockSpec((N_EXPERTS, 1), fixed)],
        out_specs=[pl.BlockSpec((TOP_K, tm), col)] * 2 + [pl.BlockSpec((1, N_EXPERTS, LANES), tile)] * 2
                  + [pl.BlockSpec((N_EXPERTS, LANES), fixed)],
        out_shape=[jax.ShapeDtypeStruct((TOP_K, t), F32), jax.ShapeDtypeStruct((TOP_K, t), I32), per_tile, per_tile,
                   jax.ShapeDtypeStruct((N_EXPERTS, LANES), I32)],
        scratch_shapes=[pltpu.VMEM((N_EXPERTS, LANES), F32)],
        compiler_params=_cparams(("arbitrary",)),
        name="route",
    )(logits, rb)


def _n_blocks_max(n_assign):
    return -(-(n_assign + N_EXPERTS * (EXPERT_BLOCK - 1)) // EXPERT_BLOCK)


def _plan_kernel(cnt_ref, ps_ref, blk_ref, nv_ref, *, nbp):
    cnt = cnt_ref[...]
    nb = lax.shift_right_logical(cnt + (EXPERT_BLOCK - 1), 8).astype(F32)
    nb8 = jnp.broadcast_to(nb, (SUBLANES, N_EXPERTS))
    nb_hi, nb_lo = _split2(nb8)
    ii = lax.broadcasted_iota(I32, (N_EXPERTS, N_EXPERTS), 0)
    jj = lax.broadcasted_iota(I32, (N_EXPERTS, N_EXPERTS), 1)
    upto = jnp.where(ii <= jj, 1.0, 0.0).astype(BF16)
    cum = (_dot(nb_hi, upto) + _dot(nb_lo, upto))[0:1, :]
    ps_ref[...] = ((cum - nb) * float(EXPERT_BLOCK)).astype(I32)
    bi = lax.broadcasted_iota(I32, (nbp, N_EXPERTS), 0).astype(F32)
    owner = jnp.sum(jnp.where(cum <= bi, 1.0, 0.0), axis=-1, keepdims=True)
    blk_ref[...] = jnp.minimum(owner, float(N_EXPERTS - 1)).astype(I32)
    nv_ref[...] = cum[:, N_EXPERTS - 1:N_EXPERTS].astype(I32)


def _plan(cnt, nbp):
    return pl.pallas_call(
        functools.partial(_plan_kernel, nbp=nbp),
        out_shape=[jax.ShapeDtypeStruct((1, N_EXPERTS), I32), jax.ShapeDtypeStruct((nbp, 1), I32),
                   jax.ShapeDtypeStruct((1, 1), I32)],
        name="plan",
    )(cnt)


def _segment_copies(tc_ref, rb_ref, ps_ref, make):
    def body(e, local):
        cnt = tc_ref[e]

        @pl.when(cnt > 0)
        def _():
            make(local, ps_ref[e] + rb_ref[e], cnt).start()

        return local + cnt

    lax.fori_loop(0, N_EXPERTS, body, 0)


def _rows(ref, row, n):
    return ref.at[pl.ds(pl.multiple_of(row * ROW_TILES, ROW_TILES), n * ROW_TILES)]


def _dispatch_kernel(lp_ref, tc_ref, rb_ref, ps_ref, cnt_ref, nv_ref, h_ref, xs_ref, sorted_ref, zero_ref, sems, zsem,
                     *, tm, nbp):
    step = pl.program_id(0)

    @pl.when(step == 0)
    def _():
        zero_ref[...] = jnp.zeros_like(zero_ref)
        for wait in (False, True):
            def tail_body(bi, carry, wait=wait):
                cp = pltpu.make_async_copy(zero_ref, _rows(xs_ref, bi * EXPERT_BLOCK, EXPERT_BLOCK), zsem)
                cp.wait() if wait else cp.start()
                return carry
            lax.fori_loop(nv_ref[0], nbp, tail_body, 0)

            def pad_body(e, carry, wait=wait):
                cnt = cnt_ref[e]
                pad = (-cnt) & (EXPERT_BLOCK - 1)

                @pl.when(pad > 0)
                def _():
                    cp = pltpu.make_async_copy(_rows(zero_ref, 0, pad), _rows(xs_ref, ps_ref[e] + cnt, pad), zsem)
                    cp.wait() if wait else cp.start()

                return carry
            lax.fori_loop(0, N_EXPERTS, pad_body, 0)

    slot = step % 2
    mine = sorted_ref.at[slot]

    def token_body(t, carry):
        row = h_ref[pl.ds(pl.multiple_of(t * ROW_TILES, ROW_TILES), ROW_TILES), :]
        for k in range(TOP_K):
            p = lp_ref[t * TOP_K + k]
            mine[pl.ds(pl.multiple_of(p * ROW_TILES, ROW_TILES), ROW_TILES), :] = row
        return carry

    lax.fori_loop(0, tm, token_body, 0)
    _segment_copies(tc_ref, rb_ref, ps_ref,
                    lambda loc, glob, n: pltpu.make_async_copy(_rows(mine, loc, n), _rows(xs_ref, glob, n), sems.at[slot]))

    def drain(s):
        pltpu.make_async_copy(sorted_ref.at[s], _rows(xs_ref, 0, tm * TOP_K), sems.at[s]).wait()

    @pl.when(step > 0)
    def _():
        drain(1 - slot)

    @pl.when(step == pl.num_programs(0) - 1)
    def _():
        drain(slot)


def _dispatch(lp_flat, tcnt, runb, pstart, cnt, nv, hrows, nbp, tm):
    t = lp_flat.shape[0] // TOP_K
    per_tile = pl.BlockSpec((N_EXPERTS,), lambda i: (i,), memory_space=pltpu.SMEM)
    smem_all = pl.BlockSpec((N_EXPERTS,), lambda i: (0,), memory_space=pltpu.SMEM)
    return pl.pallas_call(
        functools.partial(_dispatch_kernel, tm=tm, nbp=nbp),
        grid=(t // tm,),
        in_specs=[pl.BlockSpec((tm * TOP_K,), lambda i: (i,), memory_space=pltpu.SMEM),
                  per_tile, per_tile, smem_all, smem_all,
                  pl.BlockSpec((1,), lambda i: (0,), memory_space=pltpu.SMEM),
                  pl.BlockSpec((tm * ROW_TILES, LANES), lambda i: (i, 0))],
        out_specs=pl.BlockSpec(memory_space=pl.ANY),
        out_shape=jax.ShapeDtypeStruct((nbp * EXPERT_BLOCK * ROW_TILES, LANES), F32),
        scratch_shapes=[pltpu.VMEM((2, tm * TOP_K * ROW_TILES, LANES), F32),
                        pltpu.VMEM((EXPERT_BLOCK * ROW_TILES, LANES), F32),
                        pltpu.SemaphoreType.DMA((2,)), pltpu.SemaphoreType.DMA],
        compiler_params=_cparams(("arbitrary",)),
        name="dispatch",
    )(lp_flat, tcnt, runb, pstart, cnt, nv, hrows)


def _experts_kernel(blk_ref, nv_ref, xs_ref, wg_ref, wu_ref, wd_ref, ys_ref, wgb_ref, wub_ref, wdb_ref):
    i = pl.program_id(0)
    nv = nv_ref[0]

    @pl.when(i >= nv)
    def _():
        ys_ref[...] = jnp.zeros_like(ys_ref)

    @pl.when(i < nv)
    def _():
        prev = blk_ref[jnp.maximum(i - 1, 0)]

        @pl.when((i == 0) | (blk_ref[i] != prev))
        def _():
            wgb_ref[...] = wg_ref[0].astype(BF16)
            wub_ref[...] = wu_ref[0].astype(BF16)
            wdb_ref[...] = wd_ref[0].astype(BF16)

        xb = jnp.concatenate(
            [xs_ref[pl.ds(s, EXPERT_BLOCK, stride=ROW_TILES), :] for s in range(ROW_TILES)], axis=1).astype(BF16)
        hid = _silu(_dot(xb, wgb_ref[...])) * _dot(xb, wub_ref[...])
        y = _dot(hid.astype(BF16), wdb_ref[...])
        for s in range(ROW_TILES):
            ys_ref[pl.ds(s, EXPERT_BLOCK, stride=ROW_TILES), :] = y[:, LANES * s:LANES * (s + 1)]


def _experts(blk, nv, xs, wg, wu, wd, nbp):
    rows = EXPERT_BLOCK * ROW_TILES
    cur = lambda i, blk_ref, nv_ref: jnp.minimum(i, nv_ref[0] - 1)
    xmap = lambda i, blk_ref, nv_ref: (cur(i, blk_ref, nv_ref), 0)
    wmap = lambda i, blk_ref, nv_ref: (blk_ref[cur(i, blk_ref, nv_ref)], 0, 0)
    return pl.pallas_call(
        _experts_kernel,
        grid_spec=pltpu.PrefetchScalarGridSpec(
            num_scalar_prefetch=2,
            grid=(nbp,),
            in_specs=[pl.BlockSpec((rows, LANES), xmap),
                      pl.BlockSpec((1, D_MODEL, D_EXPERT), wmap),
                      pl.BlockSpec((1, D_MODEL, D_EXPERT), wmap),
                      pl.BlockSpec((1, D_EXPERT, D_MODEL), wmap)],
            out_specs=pl.BlockSpec((rows, LANES), lambda i, blk_ref, nv_ref: (i, 0)),
            scratch_shapes=[pltpu.VMEM((D_MODEL, D_EXPERT), BF16), pltpu.VMEM((D_MODEL, D_EXPERT), BF16),
                            pltpu.VMEM((D_EXPERT, D_MODEL), BF16)]),
        out_shape=jax.ShapeDtypeStruct(xs.shape, F32),
        compiler_params=_cparams(("arbitrary",)),
        name="experts",
    )(blk, nv, xs, wg, wu, wd)


def _combine_kernel(lp_ref, w_ref, tc_ref, rb_ref, tcn_ref, rbn_ref, ps_ref, ys_ref, base_ref, g2_ref, fg_ref, out_ref,
                    buf_ref, acc_ref, sems, *, tm):
    step = pl.program_id(0) * pl.num_programs(1) + pl.program_id(1)
    last = pl.num_programs(0) * pl.num_programs(1) - 1
    slot = step % 2

    def fetch(tcnt_ref, runb_ref, s):
        _segment_copies(tcnt_ref, runb_ref, ps_ref,
                        lambda loc, glob, n: pltpu.make_async_copy(_rows(ys_ref, glob, n), _rows(buf_ref.at[s], loc, n),
                                                                   sems.at[s]))

    @pl.when(step == 0)
    def _():
        fetch(tc_ref, rb_ref, slot)

    @pl.when(step < last)
    def _():
        fetch(tcn_ref, rbn_ref, 1 - slot)

    mine = buf_ref.at[slot]
    pltpu.make_async_copy(_rows(ys_ref, 0, tm * TOP_K), mine, sems.at[slot]).wait()

    def token_body(t, carry):
        j0 = t * TOP_K
        acc = jnp.zeros((ROW_TILES, LANES), F32)
        for k in range(TOP_K):
            p = lp_ref[j0 + k]
            acc = acc + w_ref[j0 + k] * mine[pl.ds(pl.multiple_of(p * ROW_TILES, ROW_TILES), ROW_TILES), :]
        acc_ref[pl.ds(pl.multiple_of(t * ROW_TILES, ROW_TILES), ROW_TILES), :] = acc
        return carry

    lax.fori_loop(0, tm, token_body, 0)
    routed = jnp.concatenate([acc_ref[pl.ds(s, tm, stride=ROW_TILES), :] for s in range(ROW_TILES)], axis=1)
    out_ref[0] = _rms(base_ref[0] + g2_ref[0] * routed, fg_ref[...])


def _combine(lp_flat, w_flat, tcnt, runb, pstart, ys, base, g2, fg, tm):
    b, l, _ = base.shape
    nt = l // tm
    flat = lambda bi, i: (bi * nt + i,)
    following = lambda bi, i: (jnp.minimum(bi * nt + i + 1, b * nt - 1),)
    smem_blk = pl.BlockSpec((tm * TOP_K,), flat, memory_space=pltpu.SMEM)
    per_tile = pl.BlockSpec((N_EXPERTS,), flat, memory_space=pltpu.SMEM)
    next_tile = pl.BlockSpec((N_EXPERTS,), following, memory_space=pltpu.SMEM)
    return pl.pallas_call(
        functools.partial(_combine_kernel, tm=tm),
        grid=(b, nt),
        in_specs=[smem_blk, smem_blk, per_tile, per_tile, next_tile, next_tile,
                  pl.BlockSpec((N_EXPERTS,), lambda bi, i: (0,), memory_space=pltpu.SMEM),
                  pl.BlockSpec(memory_space=pl.ANY),
                  pl.BlockSpec((1, tm, D_MODEL), lambda bi, i: (bi, i, 0)),
                  pl.BlockSpec((1, 1, D_MODEL), lambda bi, i: (bi, 0, 0)),
                  pl.BlockSpec((1, D_MODEL), lambda bi, i: (0, 0))],
        out_specs=pl.BlockSpec((1, tm, D_MODEL), lambda bi, i: (bi, i, 0)),
        out_shape=jax.ShapeDtypeStruct((b, l, D_MODEL), F32),
        scratch_shapes=[pltpu.VMEM((2, tm * TOP_K * ROW_TILES, LANES), F32),
                        pltpu.VMEM((tm * ROW_TILES, LANES), F32),
                        pltpu.SemaphoreType.DMA((2,))],
        compiler_params=_cparams(("arbitrary", "arbitrary")),
        name="combine",
    )(lp_flat, w_flat, tcnt, runb, tcnt, runb, pstart, ys, base, g2, fg)


def _mixer_inputs(h, shift, scale, gain, wm, wl, wa, ba):
    return _proj(h, shift, scale, gain, wm, wl, wa, ba)


def kernel(x, c, ctx, c_ctx, ada_w, ada_b, norm1_g, norm2_g, w_in, gla_wa_f, gla_ba_f, gla_wa_b, gla_ba_b, gla_norm_g, s5_lam_re_f, s5_lam_im_f, s5_log_step_f, s5_lam_re_b, s5_lam_im_b, s5_log_step_b, s5_b_re, s5_b_im, s5_c_re_f, s5_c_im_f, s5_c_re_b, s5_c_im_b, s5_d, s5_glu_w, s5_glu_b, w_out, router_w, router_b, exp_w_gate, exp_w_up, exp_w_down, sh_w_gate, sh_w_up, sh_w_down, final_norm_g):
    b, l, d = x.shape
    i = 0

    rows = -(-(b + 1) // SUBLANES) * SUBLANES
    cs = jnp.zeros((rows, d), F32).at[:b].set(c).at[b].set(c_ctx)
    mod = _adaln(cs, ada_w[i], ada_b[i][None, :])
    sh1, sc1, g1, sh2, sc2, g2 = [mod[:b, d * j:d * (j + 1)][:, None, :] for j in range(6)]
    csh1, csc1 = [jnp.broadcast_to(mod[b, d * j:d * (j + 1)][None, None, :], (b, 1, d)) for j in range(2)]

    w = w_in[i]
    o1, o2, o3, o4, o5, o6 = 256, 512, 1024, 1536, 1552, 1568
    wm = jnp.concatenate([w[:, :o4], w[:, o6:]], axis=1).astype(BF16)
    wl = jnp.zeros((d, LANES), F32).at[:, :2 * GLA_GATE_RANK].set(w[:, o4:o6]).astype(BF16)
    wa = jnp.zeros((LANES, 2 * GLA_DK), F32)
    wa = wa.at[:GLA_GATE_RANK, :GLA_DK].set(gla_wa_f[i]).at[GLA_GATE_RANK:2 * GLA_GATE_RANK, GLA_DK:].set(gla_wa_b[i])
    wa = wa.astype(BF16)
    ba = jnp.concatenate([gla_ba_f[i], gla_ba_b[i]])[None, :]
    n1 = norm1_g[i][None, :]

    pcols = jnp.stack([s5_lam_re_f[i], s5_lam_im_f[i],
                       jnp.broadcast_to(s5_log_step_f[i][:, None], (S5_GROUPS, S5_STATE)),
                       s5_lam_re_b[i], s5_lam_im_b[i],
                       jnp.broadcast_to(s5_log_step_b[i][:, None], (S5_GROUPS, S5_STATE)),
                       jnp.zeros((S5_GROUPS, S5_STATE), F32), jnp.zeros((S5_GROUPS, S5_STATE), F32)], axis=-1)
    prows = pcols.transpose(0, 2, 1)
    tile_b = lambda t: jnp.tile(t, (1, 1, S5_CHUNK))
    tile_c = lambda t: jnp.tile(t.transpose(0, 2, 1), (1, 1, S5_CHUNK))
    m_op, wt_op, v_op, ab4 = _s5gen(pcols, prows, tile_b(s5_b_re[i]), tile_b(s5_b_im[i]),
                                    tile_c(s5_c_re_f[i]), tile_c(s5_c_im_f[i]),
                                    tile_c(s5_c_re_b[i]), tile_c(s5_c_im_b[i]))
    ab = ab4.reshape(S5_GROUPS, 1, 4 * S5_STATE)

    cq, ck, cv, _, _, claf, clab, cuv = _proj(ctx, csh1, csc1, n1, wm, wl, wa, ba)
    zero_state = jnp.zeros((b, GLA_HEADS, GLA_DV_HEAD, LANES), F32)
    gsf, gsb = _gla(cq, ck, cv, claf, clab, zero_state, zero_state, with_output=False)
    (x0,) = _s5(cuv, None, wt_op, None, ab, jnp.zeros((S5_GROUPS, b, S5_VEC), F32), b, with_output=False)

    q, k, v, go, u, laf, lab, uv = _proj(x, sh1, sc1, n1, wm, wl, wa, ba)
    o, _, _ = _gla(q, k, v, laf, lab, gsf, gsb, with_output=True)
    yvec, _ = _s5(uv, m_op, wt_op, v_op, ab, x0, b, with_output=True)

    rw_hi = router_w[i].astype(BF16)
    rw_lo = (router_w[i] - rw_hi.astype(F32)).astype(BF16)
    base, hrows, logits = _post(
        x, o, go, u, yvec, g1, sh2, sc2, g2,
        gla_norm_g[i][None, :], s5_d[i][None, :], s5_glu_w[i].astype(BF16),
        s5_glu_b[i][None, :], w_out[i].astype(BF16), norm2_g[i][None, :], rw_hi, rw_lo,
        jnp.concatenate([sh_w_gate[i], sh_w_up[i]], axis=1).astype(BF16), sh_w_down[i].astype(BF16))

    t = b * l
    tile = min(512, l)
    wts, lpos, runb, tcnt, cnt = _route(logits.reshape(t, N_EXPERTS), router_b[i][:, None], tile)
    cnt_flat = cnt[:, 0]
    runb_flat, tcnt_flat = runb[:, :, 0].reshape(-1), tcnt[:, :, 0].reshape(-1)
    nbp = -(-_n_blocks_max(t * TOP_K) // SUBLANES) * SUBLANES
    pstart, blk, nv = _plan(cnt_flat[None, :], nbp)
    lp_flat, w_flat = lpos.T.reshape(-1), wts.T.reshape(-1)
    ps_flat = pstart.reshape(-1)
    xs = _dispatch(lp_flat, tcnt_flat, runb_flat, ps_flat, cnt_flat, nv.reshape(-1), hrows, nbp, tile)
    assert exp_w_gate.shape[0] == 1, "single-layer block"
    ys = _experts(blk.reshape(-1), nv.reshape(-1), xs, exp_w_gate.reshape(exp_w_gate.shape[1:]),
                  exp_w_up.reshape(exp_w_up.shape[1:]), exp_w_down.reshape(exp_w_down.shape[1:]), nbp)
    return _combine(lp_flat, w_flat, tcnt_flat, runb_flat, ps_flat, ys, base, g2, final_norm_g[None, :], tile)
```

```python
import functools

import jax
import jax.numpy as jnp
from jax import lax
from jax.experimental import pallas as pl
from jax.experimental.pallas import tpu as pltpu

F32 = jnp.float32
BF16 = jnp.bfloat16
I32 = jnp.int32

D_MODEL = 1024
GLA_HEADS = 4
GLA_DK_HEAD = 64
GLA_DV_HEAD = 128
GLA_DK = 256
GLA_DV = 512
GLA_GATE_RANK = 16
GLA_GATE_TAU = 16.0
GLA_CHUNK = 64
D_S5 = 512
S5_GROUP_CH = 16
S5_GROUPS = 32
S5_STATE = 64
S5_CHUNK = 16
S5_VEC = S5_CHUNK * S5_GROUP_CH
N_EXPERTS = 256
TOP_K = 8
N_EXPERT_GROUPS = 8
TOPK_GROUPS = 4
D_EXPERT = 256
D_SHARED = 256
ROUTE_SCALE = 2.5
EPS = 1e-6

LANES = 128
SUBLANES = 8
ROW_TILES = D_MODEL // LANES
EXPERT_BLOCK = 256
VMEM_LIMIT = 56 * 1024 * 1024


def _cparams(sem):
    return pltpu.CompilerParams(dimension_semantics=sem, vmem_limit_bytes=VMEM_LIMIT)


def _dot(a, b):
    return jnp.dot(a, b, preferred_element_type=F32)


def _dot_nt(a, b):
    return lax.dot_general(a, b, (((1,), (1,)), ((), ())), preferred_element_type=F32)


def _dot_tn(a, b):
    return lax.dot_general(a, b, (((0,), (0,)), ((), ())), preferred_element_type=F32)


def _split2(x):
    hi = x.astype(BF16)
    lo = (x - hi.astype(F32)).astype(BF16)
    return hi, lo


def _dot3(a, b_hi, b_lo):
    a_hi, a_lo = _split2(a)
    return _dot(a_hi, b_hi) + (_dot(a_hi, b_lo) + _dot(a_lo, b_hi))


def _silu(x):
    return x * jax.nn.sigmoid(x)


def _rms(x, g):
    return x * lax.rsqrt(jnp.mean(x * x, axis=-1, keepdims=True) + EPS) * g


def _adaln_kernel(c_ref, w_ref, b_ref, o_ref):
    s = _silu(c_ref[...])
    w_hi, w_lo = _split2(w_ref[...])
    o_ref[...] = _dot3(s, w_hi, w_lo) + b_ref[...]


def _adaln(cs, w, b):
    rows, n = cs.shape[0], w.shape[1]
    tn = 1024
    return pl.pallas_call(
        _adaln_kernel,
        grid=(n // tn,),
        in_specs=[pl.BlockSpec((rows, D_MODEL), lambda j: (0, 0)),
                  pl.BlockSpec((D_MODEL, tn), lambda j: (0, j)),
                  pl.BlockSpec((1, tn), lambda j: (0, j))],
        out_specs=pl.BlockSpec((rows, tn), lambda j: (0, j)),
        out_shape=jax.ShapeDtypeStruct((rows, n), F32),
        compiler_params=_cparams(("arbitrary",)),
        name="adaln",
    )(cs, w, b)


def _group_lane_masks(rows):
    grp = lax.shift_right_logical(lax.broadcasted_iota(I32, (rows, LANES), 1), 4)
    return [grp == j for j in range(LANES // S5_GROUP_CH)]


def _move_group(x, src, dst):
    shift = ((dst - src) * S5_GROUP_CH) % LANES
    return pltpu.roll(x, shift, 1) if shift else x


def _proj_kernel(x_ref, sh_ref, sc_ref, g_ref, wm_ref, wl_ref, wa_ref, ba_ref,
                 q_ref, k_ref, v_ref, go_ref, u_ref, laf_ref, lab_ref, uv_ref, ut_ref, *, tm):
    h = _rms(x_ref[0], g_ref[...]) * (1.0 + sc_ref[0]) + sh_ref[0]
    hb = h.astype(BF16)
    q_ref[0] = _dot(hb, wm_ref[:, 0:256]) * (GLA_DK_HEAD ** -0.5)
    k_ref[0] = _dot(hb, wm_ref[:, 256:512])
    v_ref[0] = _dot(hb, wm_ref[:, 512:1024])
    go_ref[0] = _dot(hb, wm_ref[:, 1024:1536])
    u = _dot(hb, wm_ref[:, 1536:2048])
    u_ref[0] = u
    for t in range(D_S5 // LANES):
        ut_ref[t] = u[:, LANES * t:LANES * (t + 1)]
    lr = _dot(hb, wl_ref[...])
    pre = _dot(lr.astype(BF16), wa_ref[...]) + ba_ref[...]
    la = (jnp.minimum(pre, 0.0) - jnp.log1p(jnp.exp(-jnp.abs(pre)))) * (1.0 / GLA_GATE_TAU)
    laf_ref[0] = la[:, 0:GLA_DK]
    lab_ref[0] = la[:, GLA_DK:2 * GLA_DK]
    nc = tm // S5_CHUNK
    gpt = LANES // S5_GROUP_CH
    masks = _group_lane_masks(nc)
    for t in range(D_S5 // LANES):
        steps = [ut_ref[t, pl.ds(s, nc, stride=S5_CHUNK), :] for s in range(S5_CHUNK)]
        for gl in range(gpt):
            for half in range(S5_VEC // LANES):
                acc = None
                for j in range(gpt):
                    piece = _move_group(steps[half * gpt + j], gl, j)
                    acc = piece if acc is None else jnp.where(masks[j], piece, acc)
                uv_ref[t * gpt + gl, :, LANES * half:LANES * (half + 1)] = acc.astype(BF16)


def _proj(x, shift, scale, gain, wm, wl, wa, ba):
    b, l, _ = x.shape
    tm = min(512, l)
    nt = l // tm
    row = lambda bi, i: (bi, i, 0)
    mod = lambda bi, i: (bi, 0, 0)
    full = lambda bi, i: (0, 0)
    widths = (GLA_DK, GLA_DK, GLA_DV, GLA_DV, D_S5, GLA_DK, GLA_DK)
    return pl.pallas_call(
        functools.partial(_proj_kernel, tm=tm),
        grid=(b, nt),
        in_specs=[pl.BlockSpec((1, tm, D_MODEL), row),
                  pl.BlockSpec((1, 1, D_MODEL), mod),
                  pl.BlockSpec((1, 1, D_MODEL), mod),
                  pl.BlockSpec((1, D_MODEL), full),
                  pl.BlockSpec(wm.shape, full),
                  pl.BlockSpec(wl.shape, full),
                  pl.BlockSpec(wa.shape, full),
                  pl.BlockSpec(ba.shape, full)],
        out_specs=[pl.BlockSpec((1, tm, w), row) for w in widths]
                  + [pl.BlockSpec((S5_GROUPS, tm // S5_CHUNK, S5_VEC), lambda bi, i: (0, bi * nt + i, 0))],
        out_shape=[jax.ShapeDtypeStruct((b, l, w), F32) for w in widths]
                  + [jax.ShapeDtypeStruct((S5_GROUPS, b * l // S5_CHUNK, S5_VEC), BF16)],
        scratch_shapes=[pltpu.VMEM((D_S5 // LANES, tm, LANES), F32)],
        compiler_params=_cparams(("arbitrary", "arbitrary")),
        name="proj",
    )(x, shift, scale, gain, wm, wl, wa, ba)


def _gla_kernel(*refs, n_chunks, with_output):
    if with_output:
        q_ref, k_ref, v_ref, laf_ref, lab_ref, s0f_ref, s0b_ref, o_ref, sf_ref, sb_ref, st_ref, ob_ref = refs
    else:
        q_ref, k_ref, v_ref, laf_ref, lab_ref, s0f_ref, s0b_ref, sf_ref, sb_ref, st_ref = refs
        o_ref = ob_ref = None
    c = GLA_CHUNK
    row = lax.broadcasted_iota(I32, (c, c), 0)
    col = lax.broadcasted_iota(I32, (c, c), 1)
    lane = lax.broadcasted_iota(I32, (c, LANES), 1)
    st_ref[0] = s0f_ref[0]
    st_ref[1] = s0b_ref[0]

    def chunk(direction, idx):
        la_ref = laf_ref if direction == 0 else lab_ref
        tri = (row >= col) if direction == 0 else (row <= col)
        trib = jnp.where(tri, 1.0, 0.0).astype(BF16)
        r0 = pl.multiple_of(idx * c, c)
        q = q_ref[0, pl.ds(r0, c), :]
        k = k_ref[0, pl.ds(r0, c), :]
        v = v_ref[0, pl.ds(r0, c), :]
        la_hi, la_lo = _split2(la_ref[0, pl.ds(r0, c), :])
        cum = _dot(trib, la_hi) + _dot(trib, la_lo)
        tot = cum[c - 1:c, :] if direction == 0 else cum[0:1, :]
        qd = q * jnp.exp(cum)
        ki = k * jnp.exp(-cum)
        ks = k * jnp.exp(tot - cum)
        dec = jnp.exp(tot)
        for h in range(GLA_HEADS):
            pair = slice(LANES * (h // 2), LANES * (h // 2) + LANES)
            own = (lane >= GLA_DK_HEAD * (h % 2)) & (lane < GLA_DK_HEAD * (h % 2) + GLA_DK_HEAD)
            vb = v[:, GLA_DV_HEAD * h:GLA_DV_HEAD * (h + 1)].astype(BF16)
            st = st_ref[direction, h]
            if with_output:
                qb = qd[:, pair].astype(BF16)
                kib = jnp.where(own, ki[:, pair], 0.0).astype(BF16)
                sc = jnp.where(tri, _dot_nt(qb, kib), 0.0)
                o = _dot(sc.astype(BF16), vb) + _dot_nt(qb, st.astype(BF16))
                cols = slice(GLA_DV_HEAD * h, GLA_DV_HEAD * (h + 1))
                if direction == 0:
                    o_ref[0, pl.ds(r0, c), cols] = o
                else:
                    ob_ref[pl.ds(r0, c), cols] = o
            ksb = jnp.where(own, ks[:, pair], 0.0).astype(BF16)
            st_ref[direction, h] = st * dec[:, pair] + _dot_tn(vb, ksb)

    def body(ci, carry):
        chunk(0, ci)
        chunk(1, n_chunks - 1 - ci)
        return carry

    lax.fori_loop(0, n_chunks, body, 0)
    sf_ref[0] = st_ref[0]
    sb_ref[0] = st_ref[1]
    if with_output:
        o_ref[0] = o_ref[0] + ob_ref[...]


def _gla(q, k, v, laf, lab, s0f, s0b, with_output):
    b, l, _ = q.shape
    n_chunks = l // GLA_CHUNK
    seq = lambda bi: (bi, 0, 0)
    st = lambda bi: (bi, 0, 0, 0)
    st_shape = (b, GLA_HEADS, GLA_DV_HEAD, LANES)
    st_spec = pl.BlockSpec((1, GLA_HEADS, GLA_DV_HEAD, LANES), st)
    out_specs = [st_spec, st_spec]
    out_shape = [jax.ShapeDtypeStruct(st_shape, F32)] * 2
    if with_output:
        out_specs = [pl.BlockSpec((1, l, GLA_DV), seq)] + out_specs
        out_shape = [jax.ShapeDtypeStruct((b, l, GLA_DV), F32)] + out_shape
    return pl.pallas_call(
        functools.partial(_gla_kernel, n_chunks=n_chunks, with_output=with_output),
        grid=(b,),
        in_specs=[pl.BlockSpec((1, l, GLA_DK), seq),
                  pl.BlockSpec((1, l, GLA_DK), seq),
                  pl.BlockSpec((1, l, GLA_DV), seq),
                  pl.BlockSpec((1, l, GLA_DK), seq),
                  pl.BlockSpec((1, l, GLA_DK), seq),
                  st_spec, st_spec],
        out_specs=out_specs,
        out_shape=out_shape,
        scratch_shapes=[pltpu.VMEM((2, GLA_HEADS, GLA_DV_HEAD, LANES), F32)]
                       + ([pltpu.VMEM((l, GLA_DV), F32)] if with_output else []),
        compiler_params=_cparams(("arbitrary",)),
        name="gla_out" if with_output else "gla_ctx",
    )(q, k, v, laf, lab, s0f, s0b)


def _s5gen_kernel(pc_ref, pr_ref, btr_ref, bti_ref, ctrf_ref, ctif_ref, ctrb_ref, ctib_ref,
                  m_ref, wt_ref, v_ref, ab_ref):
    pc = pc_ref[0]
    blk = lax.shift_right_logical(lax.broadcasted_iota(I32, (1, S5_VEC), 1), 4).astype(F32)
    lane = lax.broadcasted_iota(I32, (S5_GROUP_CH, S5_VEC), 1)
    n = float(S5_CHUNK)

    def cmul(ar, ai, br, bi):
        return ar * br - ai * bi, ar * bi + ai * br

    kcat = []
    for d in (0, 1):
        lre, lim, ls = pc[:, 3 * d:3 * d + 1], pc[:, 3 * d + 1:3 * d + 2], pc[:, 3 * d + 2:3 * d + 3]
        ctr = (ctrf_ref if d == 0 else ctrb_ref)[0]
        cti = (ctif_ref if d == 0 else ctib_ref)[0]
        step = jnp.exp(ls)
        mag = jnp.exp(lre * step)
        a_re = mag * jnp.cos(lim * step)
        a_im = mag * jnp.sin(lim * step)
        den = lre * lre + lim * lim
        f_re = ((a_re - 1.0) * lre + a_im * lim) / den
        f_im = (a_im * lre - (a_re - 1.0) * lim) / den
        bb_re, bb_im = cmul(f_re, f_im, btr_ref[0], bti_ref[0])

        def powers(e, lre=lre, lim=lim, step=step):
            m = jnp.exp(lre * step * e)
            ang = lim * step * e
            return m * jnp.cos(ang), m * jnp.sin(ang)

        w_re, w_im = cmul(*powers((n - 1.0 - blk) if d == 0 else blk), bb_re, bb_im)
        wt_ref[0, S5_STATE * d:S5_STATE * (d + 1), :] = w_re
        wt_ref[0, 2 * S5_STATE + S5_STATE * d:2 * S5_STATE + S5_STATE * (d + 1), :] = w_im
        c_re, c_im = cmul(*powers((blk + 1.0) if d == 0 else (n - blk)), ctr, cti)
        v_ref[0, S5_STATE * d:S5_STATE * (d + 1), :] = c_re
        v_ref[0, 2 * S5_STATE + S5_STATE * d:2 * S5_STATE + S5_STATE * (d + 1), :] = -c_im
        e_re, e_im = cmul(*powers(blk if d == 0 else (n - 1.0 - blk)), ctr, cti)
        b16r_hi, b16r_lo = _split2(bb_re[:, 0:S5_GROUP_CH])
        b16i_hi, b16i_lo = _split2(bb_im[:, 0:S5_GROUP_CH])
        er_hi, er_lo = _split2(e_re)
        ei_hi, ei_lo = _split2(e_im)
        kr = _dot_tn(b16r_hi, er_hi) + (_dot_tn(b16r_hi, er_lo) + _dot_tn(b16r_lo, er_hi))
        ki = _dot_tn(b16i_hi, ei_hi) + (_dot_tn(b16i_hi, ei_lo) + _dot_tn(b16i_lo, ei_hi))
        kcat.append(kr - ki)

    for s in range(S5_CHUNK):
        sh_f = S5_GROUP_CH * s
        fwd = kcat[0] if s == 0 else pltpu.roll(kcat[0], sh_f, 1)
        fwd = jnp.where(lane >= sh_f, fwd, 0.0)
        sh_b = S5_VEC - S5_GROUP_CH * (S5_CHUNK - 1 - s)
        bwd = kcat[1] if sh_b == S5_VEC else pltpu.roll(kcat[1], sh_b, 1)
        bwd = jnp.where(lane < S5_GROUP_CH * (s + 1), bwd, 0.0)
        m_ref[0, S5_GROUP_CH * s:S5_GROUP_CH * (s + 1), :] = fwd + bwd

    pr = pr_ref[0]
    for d in (0, 1):
        lre, lim, ls = pr[3 * d:3 * d + 1, :], pr[3 * d + 1:3 * d + 2, :], pr[3 * d + 2:3 * d + 3, :]
        stp = jnp.exp(ls) * n
        mg = jnp.exp(lre * stp)
        ab_ref[0, d:d + 1, :] = mg * jnp.cos(lim * stp)
        ab_ref[0, 2 + d:3 + d, :] = mg * jnp.sin(lim * stp)


def _s5gen(pc, pr, btr, bti, ctrf, ctif, ctrb, ctib):
    g = pc.shape[0]
    blk3 = lambda shape: pl.BlockSpec((1,) + shape, lambda i: (i, 0, 0))
    big = (S5_STATE, S5_VEC)
    sq = (S5_VEC, S5_VEC)
    return pl.pallas_call(
        _s5gen_kernel,
        grid=(g,),
        in_specs=[blk3((S5_STATE, 8)), blk3((8, S5_STATE))] + [blk3(big)] * 6,
        out_specs=[blk3(sq), blk3(sq), blk3(sq), blk3((4, S5_STATE))],
        out_shape=[jax.ShapeDtypeStruct((g,) + sq, F32)] * 3 + [jax.ShapeDtypeStruct((g, 4, S5_STATE), F32)],
        compiler_params=_cparams(("arbitrary",)),
        name="s5gen",
    )(pc, pr, btr, bti, ctrf, ctif, ctrb, ctib)


def _s5_kernel(*refs, n_chunks, nb, with_output):
    if with_output:
        u_ref, m_ref, wt_ref, v_ref, ab_ref, x0_ref, y_ref, xf_ref, z_ref, cin_ref = refs
    else:
        u_ref, wt_ref, ab_ref, x0_ref, xf_ref, z_ref = refs
    wtb = wt_ref[0].astype(BF16)
    for bi in range(nb):
        z = _dot_nt(u_ref[0, bi * n_chunks:(bi + 1) * n_chunks, :], wtb)
        z_ref[0, pl.ds(bi, n_chunks, stride=nb), :] = z[:, 0:LANES]
        z_ref[1, pl.ds(bi, n_chunks, stride=nb), :] = z[:, LANES:2 * LANES]
    ab = ab_ref[0]
    ar, ai = ab[:, 0:LANES], ab[:, LANES:2 * LANES]
    is_f = lax.broadcasted_iota(I32, (nb, LANES), 1) < S5_STATE
    x0 = x0_ref[0]

    def body(i, carry):
        xr, xi = carry
        rf = pl.multiple_of(i * nb, nb)
        rb = pl.multiple_of((n_chunks - 1 - i) * nb, nb)
        if with_output:
            cin_ref[0, pl.ds(rf, nb), 0:S5_STATE] = xr[:, 0:S5_STATE]
            cin_ref[1, pl.ds(rf, nb), 0:S5_STATE] = xi[:, 0:S5_STATE]
            cin_ref[0, pl.ds(rb, nb), S5_STATE:LANES] = xr[:, S5_STATE:LANES]
            cin_ref[1, pl.ds(rb, nb), S5_STATE:LANES] = xi[:, S5_STATE:LANES]
        zr = jnp.where(is_f, z_ref[0, pl.ds(rf, nb), :], z_ref[0, pl.ds(rb, nb), :])
        zi = jnp.where(is_f, z_ref[1, pl.ds(rf, nb), :], z_ref[1, pl.ds(rb, nb), :])
        return ar * xr - ai * xi + zr, ar * xi + ai * xr + zi

    xr, xi = lax.fori_loop(0, n_chunks, body, (x0[:, 0:LANES], x0[:, LANES:2 * LANES]))
    xf_ref[0, :, 0:LANES] = xr
    xf_ref[0, :, LANES:2 * LANES] = xi
    if with_output:
        mb = m_ref[0].astype(BF16)
        vb = v_ref[0].astype(BF16)
        for bi in range(nb):
            rows = slice(bi * n_chunks, (bi + 1) * n_chunks)
            carried = jnp.concatenate([cin_ref[0, pl.ds(bi, n_chunks, stride=nb), :],
                                       cin_ref[1, pl.ds(bi, n_chunks, stride=nb), :]], axis=1).astype(BF16)
            y_ref[0, rows, :] = _dot(u_ref[0, rows, :], mb) + _dot(carried, vb)


def _s5(uvec, m, wt, v, ab, x0, nb, with_output):
    g, rows, _ = uvec.shape
    n_chunks = rows // nb
    blk3 = lambda shape: pl.BlockSpec((1,) + shape, lambda i: (i, 0, 0))
    sq = (S5_VEC, S5_VEC)
    st = (nb, S5_VEC)
    if with_output:
        args = (uvec, m, wt, v, ab, x0)
        in_specs = [blk3((rows, S5_VEC)), blk3(sq), blk3(sq), blk3(sq), blk3((1, S5_VEC)), blk3(st)]
        out_specs = [blk3((rows, S5_VEC)), blk3(st)]
        out_shape = [jax.ShapeDtypeStruct((g, rows, S5_VEC), F32), jax.ShapeDtypeStruct((g,) + st, F32)]
        scratch = [pltpu.VMEM((S5_VEC // LANES, rows, LANES), F32), pltpu.VMEM((S5_VEC // LANES, rows, LANES), F32)]
    else:
        args = (uvec, wt, ab, x0)
        in_specs = [blk3((rows, S5_VEC)), blk3(sq), blk3((1, S5_VEC)), blk3(st)]
        out_specs = [blk3(st)]
        out_shape = [jax.ShapeDtypeStruct((g,) + st, F32)]
        scratch = [pltpu.VMEM((S5_VEC // LANES, rows, LANES), F32)]
    return pl.pallas_call(
        functools.partial(_s5_kernel, n_chunks=n_chunks, nb=nb, with_output=with_output),
        grid=(g,),
        in_specs=in_specs,
        out_specs=out_specs,
        out_shape=out_shape,
        scratch_shapes=scratch,
        compiler_params=_cparams(("arbitrary",)),
        name="s5_out" if with_output else "s5_ctx",
    )(*args)


def _post_kernel(x_ref, o_ref, go_ref, u_ref, yv_ref, g1_ref, sh2_ref, sc2_ref, g2_ref,
                 gn_ref, d_ref, gw_ref, gb_ref, wo_ref, n2_ref, rwh_ref, rwl_ref, sgu_ref, sd_ref,
                 base_ref, hrow_ref, lg_ref, y_ref, *, tm):
    nc = tm // S5_CHUNK
    gpt = LANES // S5_GROUP_CH
    masks = _group_lane_masks(nc)
    for s in range(S5_CHUNK):
        half, j = divmod(s, gpt)
        for t in range(D_S5 // LANES):
            acc = None
            for gl in range(gpt):
                piece = _move_group(yv_ref[t * gpt + gl, :, LANES * half:LANES * (half + 1)], j, gl)
                acc = piece if acc is None else jnp.where(masks[gl], piece, acc)
            y_ref[t, pl.ds(s, nc, stride=S5_CHUNK), :] = acc
    o = o_ref[0]
    gn = gn_ref[...]
    heads = [_rms(o[:, GLA_DV_HEAD * h:GLA_DV_HEAD * (h + 1)], gn) for h in range(GLA_HEADS)]
    gla_out = jnp.concatenate(heads, axis=1) * _silu(go_ref[0])
    yy = jnp.concatenate([y_ref[t] for t in range(D_S5 // LANES)], axis=1) + d_ref[...] * u_ref[0]
    z = 0.5 * yy * (1.0 + jnp.tanh(0.7978845608028654 * (yy + 0.044715 * (yy * yy * yy))))
    s5_out = z * jax.nn.sigmoid(_dot(z.astype(BF16), gw_ref[...]) + gb_ref[...])
    mix = jnp.concatenate([gla_out, s5_out], axis=1).astype(BF16)
    x1 = x_ref[0] + g1_ref[0] * _dot(mix, wo_ref[...])
    h2 = _rms(x1, n2_ref[...]) * (1.0 + sc2_ref[0]) + sh2_ref[0]
    lg_ref[0] = _dot3(h2, rwh_ref[...], rwl_ref[...])
    hb = h2.astype(BF16)
    gu = _dot(hb, sgu_ref[...])
    hid = _silu(gu[:, 0:D_SHARED]) * gu[:, D_SHARED:2 * D_SHARED]
    base_ref[0] = x1 + g2_ref[0] * _dot(hid.astype(BF16), sd_ref[...])
    for s in range(ROW_TILES):
        hrow_ref[pl.ds(s, tm, stride=ROW_TILES), :] = h2[:, LANES * s:LANES * (s + 1)]


def _post(x, o, go, u, yvec, g1, sh2, sc2, g2, gn, d, gw, gb, wo, n2, rwh, rwl, sgu, sd):
    b, l, _ = x.shape
    tm = 256
    nt = l // tm
    row = lambda bi, i: (bi, i, 0)
    mod = lambda bi, i: (bi, 0, 0)
    full = lambda bi, i: (0, 0)
    ws = (gn, d, gw, gb, wo, n2, rwh, rwl, sgu, sd)
    return pl.pallas_call(
        functools.partial(_post_kernel, tm=tm),
        grid=(b, nt),
        in_specs=[pl.BlockSpec((1, tm, D_MODEL), row)]
                 + [pl.BlockSpec((1, tm, 512), row)] * 3
                 + [pl.BlockSpec((S5_GROUPS, tm // S5_CHUNK, S5_VEC), lambda bi, i: (0, bi * nt + i, 0))]
                 + [pl.BlockSpec((1, 1, D_MODEL), mod)] * 4
                 + [pl.BlockSpec(w.shape, full) for w in ws],
        out_specs=[pl.BlockSpec((1, tm, D_MODEL), row),
                   pl.BlockSpec((tm * ROW_TILES, LANES), lambda bi, i: (bi * nt + i, 0)),
                   pl.BlockSpec((1, tm, N_EXPERTS), row)],
        out_shape=[jax.ShapeDtypeStruct((b, l, D_MODEL), F32),
                   jax.ShapeDtypeStruct((b * l * ROW_TILES, LANES), F32),
                   jax.ShapeDtypeStruct((b, l, N_EXPERTS), F32)],
        scratch_shapes=[pltpu.VMEM((D_S5 // LANES, tm, LANES), F32)],
        compiler_params=_cparams(("arbitrary", "arbitrary")),
        name="post",
    )(x, o, go, u, yvec, g1, sh2, sc2, g2, *ws)


def _route_kernel(lg_ref, rb_ref, w_ref, p_ref, rb4_ref, tc_ref, cnt_ref, run_ref, *, tm):
    @pl.when(pl.program_id(0) == 0)
    def _():
        run_ref[...] = jnp.zeros_like(run_ref)

    neg = -jnp.inf
    gsz = N_EXPERTS // N_EXPERT_GROUPS
    s = jax.nn.sigmoid(lg_ref[...].T)
    biased = s + rb_ref[...]
    row = lax.broadcasted_iota(I32, (N_EXPERTS, tm), 0).astype(F32)

    def first_max(m, idx):
        mx = jnp.max(m, axis=0, keepdims=True)
        ix = jnp.min(jnp.where(m == mx, idx, float(N_EXPERTS)), axis=0, keepdims=True)
        return mx, ix

    grow = lax.broadcasted_iota(I32, (gsz, tm), 0).astype(F32)
    gs = []
    for g in range(N_EXPERT_GROUPS):
        m, idx = biased[gsz * g:gsz * (g + 1), :], grow + float(gsz * g)
        m1, i1 = first_max(m, idx)
        gs.append(m1 + jnp.max(jnp.where(idx == i1, neg, m), axis=0, keepdims=True))
    kept = []
    for g in range(N_EXPERT_GROUPS):
        ahead = jnp.zeros((1, tm), F32)
        for j in range(N_EXPERT_GROUPS):
            if j < g:
                ahead = ahead + jnp.where(gs[j] >= gs[g], 1.0, 0.0)
            elif j > g:
                ahead = ahead + jnp.where(gs[j] > gs[g], 1.0, 0.0)
        kept.append(jnp.where(ahead < float(TOPK_GROUPS), biased[gsz * g:gsz * (g + 1), :], neg))
    masked = jnp.concatenate(kept, axis=0)

    onehot = jnp.zeros((N_EXPERTS, tm), F32)
    ids, ws = [], []
    for _ in range(TOP_K):
        _, ik = first_max(masked, row)
        hit = row == ik
        ids.append(ik)
        ws.append(jnp.sum(jnp.where(hit, s, 0.0), axis=0, keepdims=True))
        onehot = onehot + jnp.where(hit, 1.0, 0.0)
        masked = jnp.where(hit, neg, masked)
    wsum = ws[0]
    for k in range(1, TOP_K):
        wsum = wsum + ws[k]

    ss = lax.broadcasted_iota(I32, (tm, tm), 0)
    tt = lax.broadcasted_iota(I32, (tm, tm), 1)
    earlier = jnp.where(ss < tt, 1.0, 0.0).astype(BF16)
    ohb = onehot.astype(BF16)
    tcnt = _dot(ohb, jnp.ones((tm, LANES), BF16))
    ee = lax.broadcasted_iota(I32, (N_EXPERTS, N_EXPERTS), 0)
    ff = lax.broadcasted_iota(I32, (N_EXPERTS, N_EXPERTS), 1)
    below = jnp.where(ff < ee, 1.0, 0.0).astype(BF16)
    t_hi, t_lo = _split2(tcnt)
    toff = _dot(below, t_hi) + _dot(below, t_lo)
    lpos = _dot(ohb, earlier) + toff[:, 0:1]
    w_ref[...] = jnp.concatenate([w / wsum * ROUTE_SCALE for w in ws], axis=0)
    p_ref[...] = jnp.concatenate(
        [jnp.sum(jnp.where(row == ids[k], lpos, 0.0), axis=0, keepdims=True) for k in range(TOP_K)], axis=0).astype(I32)
    run = run_ref[...]
    rb4_ref[0] = run.astype(I32)
    tc_ref[0] = tcnt.astype(I32)
    run = run + tcnt
    run_ref[...] = run
    cnt_ref[...] = run.astype(I32)


def _route(logits, rb, tm):
    t = logits.shape[0]
    col = lambda i: (0, i)
    fixed = lambda i: (0, 0)
    tile = lambda i: (i, 0, 0)
    per_tile = jax.ShapeDtypeStruct((t // tm, N_EXPERTS, LANES), I32)
    return pl.pallas_call(
        functools.partial(_route_kernel, tm=tm),
        grid=(t // tm,),
        in_specs=[pl.BlockSpec((tm, N_EXPERTS), lambda i: (i, 0)), pl.BlockSpec((N_EXPERTS, 1), fixed)],
        out_specs=[pl.BlockSpec((TOP_K, tm), col)] * 2 + [pl.BlockSpec((1, N_EXPERTS, LANES), tile)] * 2
                  + [pl.BlockSpec((N_EXPERTS, LANES), fixed)],
        out_shape=[jax.ShapeDtypeStruct((TOP_K, t), F32), jax.ShapeDtypeStruct((TOP_K, t), I32), per_tile, per_tile,
                   jax.ShapeDtypeStruct((N_EXPERTS, LANES), I32)],
        scratch_shapes=[pltpu.VMEM((N_EXPERTS, LANES), F32)],
        compiler_params=_cparams(("arbitrary",)),
        name="route",
    )(logits, rb)


def _n_blocks_max(n_assign):
    return -(-(n_assign + N_EXPERTS * (EXPERT_BLOCK - 1)) // EXPERT_BLOCK)


def _plan_kernel(cnt_ref, ps_ref, blk_ref, nv_ref, *, nbp):
    cnt = cnt_ref[...]
    nb = lax.shift_right_logical(cnt + (EXPERT_BLOCK - 1), 8).astype(F32)
    nb8 = jnp.broadcast_to(nb, (SUBLANES, N_EXPERTS))
    nb_hi, nb_lo = _split2(nb8)
    ii = lax.broadcasted_iota(I32, (N_EXPERTS, N_EXPERTS), 0)
    jj = lax.broadcasted_iota(I32, (N_EXPERTS, N_EXPERTS), 1)
    upto = jnp.where(ii <= jj, 1.0, 0.0).astype(BF16)
    cum = (_dot(nb_hi, upto) + _dot(nb_lo, upto))[0:1, :]
    ps_ref[...] = ((cum - nb) * float(EXPERT_BLOCK)).astype(I32)
    bi = lax.broadcasted_iota(I32, (nbp, N_EXPERTS), 0).astype(F32)
    owner = jnp.sum(jnp.where(cum <= bi, 1.0, 0.0), axis=-1, keepdims=True)
    blk_ref[...] = jnp.minimum(owner, float(N_EXPERTS - 1)).astype(I32)
    nv_ref[...] = cum[:, N_EXPERTS - 1:N_EXPERTS].astype(I32)


def _plan(cnt, nbp):
    return pl.pallas_call(
        functools.partial(_plan_kernel, nbp=nbp),
        out_shape=[jax.ShapeDtypeStruct((1, N_EXPERTS), I32), jax.ShapeDtypeStruct((nbp, 1), I32),
                   jax.ShapeDtypeStruct((1, 1), I32)],
        name="plan",
    )(cnt)


def _segment_copies(tc_ref, rb_ref, ps_ref, make):
    def body(e, local):
        cnt = tc_ref[e]

        @pl.when(cnt > 0)
        def _():
            make(local, ps_ref[e] + rb_ref[e], cnt).start()

        return local + cnt

    lax.fori_loop(0, N_EXPERTS, body, 0)


def _rows(ref, row, n):
    return ref.at[pl.ds(pl.multiple_of(row * ROW_TILES, ROW_TILES), n * ROW_TILES)]


def _dispatch_kernel(lp_ref, tc_ref, rb_ref, ps_ref, cnt_ref, nv_ref, h_ref, xs_ref, sorted_ref, zero_ref, sems, zsem,
                     *, tm, nbp):
    step = pl.program_id(0)

    @pl.when(step == 0)
    def _():
        zero_ref[...] = jnp.zeros_like(zero_ref)
        for wait in (False, True):
            def tail_body(bi, carry, wait=wait):
                cp = pltpu.make_async_copy(zero_ref, _rows(xs_ref, bi * EXPERT_BLOCK, EXPERT_BLOCK), zsem)
                cp.wait() if wait else cp.start()
                return carry
            lax.fori_loop(nv_ref[0], nbp, tail_body, 0)

            def pad_body(e, carry, wait=wait):
                cnt = cnt_ref[e]
                pad = (-cnt) & (EXPERT_BLOCK - 1)

                @pl.when(pad > 0)
                def _():
                    cp = pltpu.make_async_copy(_rows(zero_ref, 0, pad), _rows(xs_ref, ps_ref[e] + cnt, pad), zsem)
                    cp.wait() if wait else cp.start()

                return carry
            lax.fori_loop(0, N_EXPERTS, pad_body, 0)

    slot = step % 2
    mine = sorted_ref.at[slot]

    def token_body(t, carry):
        row = h_ref[pl.ds(pl.multiple_of(t * ROW_TILES, ROW_TILES), ROW_TILES), :]
        for k in range(TOP_K):
            p = lp_ref[t * TOP_K + k]
            mine[pl.ds(pl.multiple_of(p * ROW_TILES, ROW_TILES), ROW_TILES), :] = row
        return carry

    lax.fori_loop(0, tm, token_body, 0)
    _segment_copies(tc_ref, rb_ref, ps_ref,
                    lambda loc, glob, n: pltpu.make_async_copy(_rows(mine, loc, n), _rows(xs_ref, glob, n), sems.at[slot]))

    def drain(s):
        pltpu.make_async_copy(sorted_ref.at[s], _rows(xs_ref, 0, tm * TOP_K), sems.at[s]).wait()

    @pl.when(step > 0)
    def _():
        drain(1 - slot)

    @pl.when(step == pl.num_programs(0) - 1)
    def _():
        drain(slot)


def _dispatch(lp_flat, tcnt, runb, pstart, cnt, nv, hrows, nbp, tm):
    t = lp_flat.shape[0] // TOP_K
    per_tile = pl.BlockSpec((N_EXPERTS,), lambda i: (i,), memory_space=pltpu.SMEM)
    smem_all = pl.BlockSpec((N_EXPERTS,), lambda i: (0,), memory_space=pltpu.SMEM)
    return pl.pallas_call(
        functools.partial(_dispatch_kernel, tm=tm, nbp=nbp),
        grid=(t // tm,),
        in_specs=[pl.BlockSpec((tm * TOP_K,), lambda i: (i,), memory_space=pltpu.SMEM),
                  per_tile, per_tile, smem_all, smem_all,
                  pl.BlockSpec((1,), lambda i: (0,), memory_space=pltpu.SMEM),
                  pl.BlockSpec((tm * ROW_TILES, LANES), lambda i: (i, 0))],
        out_specs=pl.BlockSpec(memory_space=pl.ANY),
        out_shape=jax.ShapeDtypeStruct((nbp * EXPERT_BLOCK * ROW_TILES, LANES), F32),
        scratch_shapes=[pltpu.VMEM((2, tm * TOP_K * ROW_TILES, LANES), F32),
                        pltpu.VMEM((EXPERT_BLOCK * ROW_TILES, LANES), F32),
                        pltpu.SemaphoreType.DMA((2,)), pltpu.SemaphoreType.DMA],
        compiler_params=_cparams(("arbitrary",)),
        name="dispatch",
    )(lp_flat, tcnt, runb, pstart, cnt, nv, hrows)


def _experts_kernel(blk_ref, nv_ref, xs_hbm, wg_hbm, wu_hbm, wd_hbm, ys_hbm, step_ref, wgb_ref, wub_ref, wdb_ref, *, nbp):
    rows = EXPERT_BLOCK * ROW_TILES
    nv = nv_ref[0]
    cur = lambda i: jnp.minimum(i, nv - 1)
    xmap = lambda i: (cur(i), 0)
    wmap = lambda i: (blk_ref[cur(i)], 0, 0)
    step_ref[0] = 0

    def body(xs_ref, wg_ref, wu_ref, wd_ref, ys_ref):
        i = step_ref[0]
        step_ref[0] = i + 1

        @pl.when(i >= nv)
        def _():
            ys_ref[...] = jnp.zeros_like(ys_ref)

        @pl.when(i < nv)
        def _():
            prev = blk_ref[jnp.maximum(i - 1, 0)]

            @pl.when((i == 0) | (blk_ref[i] != prev))
            def _():
                wgb_ref[...] = wg_ref[0].astype(BF16)
                wub_ref[...] = wu_ref[0].astype(BF16)
                wdb_ref[...] = wd_ref[0].astype(BF16)

            xb = jnp.concatenate(
                [xs_ref[pl.ds(s, EXPERT_BLOCK, stride=ROW_TILES), :] for s in range(ROW_TILES)], axis=1).astype(BF16)
            hid = _silu(_dot(xb, wgb_ref[...])) * _dot(xb, wub_ref[...])
            y = _dot(hid.astype(BF16), wdb_ref[...])
            for s in range(ROW_TILES):
                ys_ref[pl.ds(s, EXPERT_BLOCK, stride=ROW_TILES), :] = y[:, LANES * s:LANES * (s + 1)]

    ahead = pl.Buffered(2, use_lookahead=True)
    pltpu.emit_pipeline(
        body,
        grid=(nbp,),
        in_specs=[pl.BlockSpec((rows, LANES), xmap, pipeline_mode=pl.Buffered(3)),
                  pl.BlockSpec((1, D_MODEL, D_EXPERT), wmap, pipeline_mode=ahead),
                  pl.BlockSpec((1, D_MODEL, D_EXPERT), wmap, pipeline_mode=ahead),
                  pl.BlockSpec((1, D_EXPERT, D_MODEL), wmap, pipeline_mode=ahead)],
        out_specs=[pl.BlockSpec((rows, LANES), lambda i: (i, 0))],
    )(xs_hbm, wg_hbm, wu_hbm, wd_hbm, ys_hbm)


def _experts(blk, nv, xs, wg, wu, wd, nbp):
    smem = pl.BlockSpec(memory_space=pltpu.SMEM)
    hbm = pl.BlockSpec(memory_space=pl.ANY)
    return pl.pallas_call(
        functools.partial(_experts_kernel, nbp=nbp),
        in_specs=[smem, smem, hbm, hbm, hbm, hbm],
        out_specs=hbm,
        out_shape=jax.ShapeDtypeStruct(xs.shape, F32),
        scratch_shapes=[pltpu.SMEM((1,), I32),
                        pltpu.VMEM((D_MODEL, D_EXPERT), BF16), pltpu.VMEM((D_MODEL, D_EXPERT), BF16),
                        pltpu.VMEM((D_EXPERT, D_MODEL), BF16)],
        compiler_params=pltpu.CompilerParams(vmem_limit_bytes=VMEM_LIMIT),
        name="experts",
    )(blk, nv, xs, wg, wu, wd)


def _combine_kernel(lp_ref, w_ref, tc_ref, rb_ref, tcn_ref, rbn_ref, ps_ref, ys_ref, base_ref, g2_ref, fg_ref, out_ref,
                    buf_ref, acc_ref, sems, *, tm):
    step = pl.program_id(0) * pl.num_programs(1) + pl.program_id(1)
    last = pl.num_programs(0) * pl.num_programs(1) - 1
    slot = step % 2

    def fetch(tcnt_ref, runb_ref, s):
        _segment_copies(tcnt_ref, runb_ref, ps_ref,
                        lambda loc, glob, n: pltpu.make_async_copy(_rows(ys_ref, glob, n), _rows(buf_ref.at[s], loc, n),
                                                                   sems.at[s]))

    @pl.when(step == 0)
    def _():
        fetch(tc_ref, rb_ref, slot)

    @pl.when(step < last)
    def _():
        fetch(tcn_ref, rbn_ref, 1 - slot)

    mine = buf_ref.at[slot]
    pltpu.make_async_copy(_rows(ys_ref, 0, tm * TOP_K), mine, sems.at[slot]).wait()

    def token_body(t, carry):
        j0 = t * TOP_K
        acc = jnp.zeros((ROW_TILES, LANES), F32)
        for k in range(TOP_K):
            p = lp_ref[j0 + k]
            acc = acc + w_ref[j0 + k] * mine[pl.ds(pl.multiple_of(p * ROW_TILES, ROW_TILES), ROW_TILES), :]
        acc_ref[pl.ds(pl.multiple_of(t * ROW_TILES, ROW_TILES), ROW_TILES), :] = acc
        return carry

    lax.fori_loop(0, tm, token_body, 0)
    routed = jnp.concatenate([acc_ref[pl.ds(s, tm, stride=ROW_TILES), :] for s in range(ROW_TILES)], axis=1)
    out_ref[0] = _rms(base_ref[0] + g2_ref[0] * routed, fg_ref[...])


def _combine(lp_flat, w_flat, tcnt, runb, pstart, ys, base, g2, fg, tm):
    b, l, _ = base.shape
    nt = l // tm
    flat = lambda bi, i: (bi * nt + i,)
    following = lambda bi, i: (jnp.minimum(bi * nt + i + 1, b * nt - 1),)
    smem_blk = pl.BlockSpec((tm * TOP_K,), flat, memory_space=pltpu.SMEM)
    per_tile = pl.BlockSpec((N_EXPERTS,), flat, memory_space=pltpu.SMEM)
    next_tile = pl.BlockSpec((N_EXPERTS,), following, memory_space=pltpu.SMEM)
    return pl.pallas_call(
        functools.partial(_combine_kernel, tm=tm),
        grid=(b, nt),
        in_specs=[smem_blk, smem_blk, per_tile, per_tile, next_tile, next_tile,
                  pl.BlockSpec((N_EXPERTS,), lambda bi, i: (0,), memory_space=pltpu.SMEM),
                  pl.BlockSpec(memory_space=pl.ANY),
                  pl.BlockSpec((1, tm, D_MODEL), lambda bi, i: (bi, i, 0)),
                  pl.BlockSpec((1, 1, D_MODEL), lambda bi, i: (bi, 0, 0)),
                  pl.BlockSpec((1, D_MODEL), lambda bi, i: (0, 0))],
        out_specs=pl.BlockSpec((1, tm, D_MODEL), lambda bi, i: (bi, i, 0)),
        out_shape=jax.ShapeDtypeStruct((b, l, D_MODEL), F32),
        scratch_shapes=[pltpu.VMEM((2, tm * TOP_K * ROW_TILES, LANES), F32),
                        pltpu.VMEM((tm * ROW_TILES, LANES), F32),
                        pltpu.SemaphoreType.DMA((2,))],
        compiler_params=_cparams(("arbitrary", "arbitrary")),
        name="combine",
    )(lp_flat, w_flat, tcnt, runb, tcnt, runb, pstart, ys, base, g2, fg)


def _mixer_inputs(h, shift, scale, gain, wm, wl, wa, ba):
    return _proj(h, shift, scale, gain, wm, wl, wa, ba)


def kernel(x, c, ctx, c_ctx, ada_w, ada_b, norm1_g, norm2_g, w_in, gla_wa_f, gla_ba_f, gla_wa_b, gla_ba_b, gla_norm_g, s5_lam_re_f, s5_lam_im_f, s5_log_step_f, s5_lam_re_b, s5_lam_im_b, s5_log_step_b, s5_b_re, s5_b_im, s5_c_re_f, s5_c_im_f, s5_c_re_b, s5_c_im_b, s5_d, s5_glu_w, s5_glu_b, w_out, router_w, router_b, exp_w_gate, exp_w_up, exp_w_down, sh_w_gate, sh_w_up, sh_w_down, final_norm_g):
    b, l, d = x.shape
    i = 0

    rows = -(-(b + 1) // SUBLANES) * SUBLANES
    cs = jnp.zeros((rows, d), F32).at[:b].set(c).at[b].set(c_ctx)
    mod = _adaln(cs, ada_w[i], ada_b[i][None, :])
    sh1, sc1, g1, sh2, sc2, g2 = [mod[:b, d * j:d * (j + 1)][:, None, :] for j in range(6)]
    csh1, csc1 = [jnp.broadcast_to(mod[b, d * j:d * (j + 1)][None, None, :], (b, 1, d)) for j in range(2)]

    w = w_in[i]
    o1, o2, o3, o4, o5, o6 = 256, 512, 1024, 1536, 1552, 1568
    wm = jnp.concatenate([w[:, :o4], w[:, o6:]], axis=1).astype(BF16)
    wl = jnp.zeros((d, LANES), F32).at[:, :2 * GLA_GATE_RANK].set(w[:, o4:o6]).astype(BF16)
    wa = jnp.zeros((LANES, 2 * GLA_DK), F32)
    wa = wa.at[:GLA_GATE_RANK, :GLA_DK].set(gla_wa_f[i]).at[GLA_GATE_RANK:2 * GLA_GATE_RANK, GLA_DK:].set(gla_wa_b[i])
    wa = wa.astype(BF16)
    ba = jnp.concatenate([gla_ba_f[i], gla_ba_b[i]])[None, :]
    n1 = norm1_g[i][None, :]

    pcols = jnp.stack([s5_lam_re_f[i], s5_lam_im_f[i],
                       jnp.broadcast_to(s5_log_step_f[i][:, None], (S5_GROUPS, S5_STATE)),
                       s5_lam_re_b[i], s5_lam_im_b[i],
                       jnp.broadcast_to(s5_log_step_b[i][:, None], (S5_GROUPS, S5_STATE)),
                       jnp.zeros((S5_GROUPS, S5_STATE), F32), jnp.zeros((S5_GROUPS, S5_STATE), F32)], axis=-1)
    prows = pcols.transpose(0, 2, 1)
    tile_b = lambda t: jnp.tile(t, (1, 1, S5_CHUNK))
    tile_c = lambda t: jnp.tile(t.transpose(0, 2, 1), (1, 1, S5_CHUNK))
    m_op, wt_op, v_op, ab4 = _s5gen(pcols, prows, tile_b(s5_b_re[i]), tile_b(s5_b_im[i]),
                                    tile_c(s5_c_re_f[i]), tile_c(s5_c_im_f[i]),
                                    tile_c(s5_c_re_b[i]), tile_c(s5_c_im_b[i]))
    ab = ab4.reshape(S5_GROUPS, 1, 4 * S5_STATE)

    cq, ck, cv, _, _, claf, clab, cuv = _proj(ctx, csh1, csc1, n1, wm, wl, wa, ba)
    zero_state = jnp.zeros((b, GLA_HEADS, GLA_DV_HEAD, LANES), F32)
    gsf, gsb = _gla(cq, ck, cv, claf, clab, zero_state, zero_state, with_output=False)
    (x0,) = _s5(cuv, None, wt_op, None, ab, jnp.zeros((S5_GROUPS, b, S5_VEC), F32), b, with_output=False)

    q, k, v, go, u, laf, lab, uv = _proj(x, sh1, sc1, n1, wm, wl, wa, ba)
    o, _, _ = _gla(q, k, v, laf, lab, gsf, gsb, with_output=True)
    yvec, _ = _s5(uv, m_op, wt_op, v_op, ab, x0, b, with_output=True)

    rw_hi = router_w[i].astype(BF16)
    rw_lo = (router_w[i] - rw_hi.astype(F32)).astype(BF16)
    base, hrows, logits = _post(
        x, o, go, u, yvec, g1, sh2, sc2, g2,
        gla_norm_g[i][None, :], s5_d[i][None, :], s5_glu_w[i].astype(BF16),
        s5_glu_b[i][None, :], w_out[i].astype(BF16), norm2_g[i][None, :], rw_hi, rw_lo,
        jnp.concatenate([sh_w_gate[i], sh_w_up[i]], axis=1).astype(BF16), sh_w_down[i].astype(BF16))

    t = b * l
    tile = min(512, l)
    wts, lpos, runb, tcnt, cnt = _route(logits.reshape(t, N_EXPERTS), router_b[i][:, None], tile)
    cnt_flat = cnt[:, 0]
    runb_flat, tcnt_flat = runb[:, :, 0].reshape(-1), tcnt[:, :, 0].reshape(-1)
    nbp = -(-_n_blocks_max(t * TOP_K) // SUBLANES) * SUBLANES
    pstart, blk, nv = _plan(cnt_flat[None, :], nbp)
    lp_flat, w_flat = lpos.T.reshape(-1), wts.T.reshape(-1)
    ps_flat = pstart.reshape(-1)
    xs = _dispatch(lp_flat, tcnt_flat, runb_flat, ps_flat, cnt_flat, nv.reshape(-1), hrows, nbp, tile)
    assert exp_w_gate.shape[0] == 1, "single-layer block"
    ys = _experts(blk.reshape(-1), nv.reshape(-1), xs, exp_w_gate.reshape(exp_w_gate.shape[1:]),
                  exp_w_up.reshape(exp_w_up.shape[1:]), exp_w_down.reshape(exp_w_down.shape[1:]), nbp)
    return _combine(lp_flat, w_flat, tcnt_flat, runb_flat, ps_flat, ys, base, g2, final_norm_g[None, :], tile)
```

```python
import functools

import jax
import jax.numpy as jnp
from jax import lax
from jax.experimental import pallas as pl
from jax.experimental.pallas import tpu as pltpu

F32 = jnp.float32
BF16 = jnp.bfloat16
I32 = jnp.int32

D_MODEL = 1024
GLA_HEADS = 4
GLA_DK_HEAD = 64
GLA_DV_HEAD = 128
GLA_DK = 256
GLA_DV = 512
GLA_GATE_RANK = 16
GLA_GATE_TAU = 16.0
GLA_CHUNK = 64
D_S5 = 512
S5_GROUP_CH = 16
S5_GROUPS = 32
S5_STATE = 64
S5_CHUNK = 16
S5_VEC = S5_CHUNK * S5_GROUP_CH
N_EXPERTS = 256
TOP_K = 8
N_EXPERT_GROUPS = 8
TOPK_GROUPS = 4
D_EXPERT = 256
D_SHARED = 256
ROUTE_SCALE = 2.5
EPS = 1e-6

LANES = 128
SUBLANES = 8
ROW_TILES = D_MODEL // LANES
EXPERT_BLOCK = 256
VMEM_LIMIT = 56 * 1024 * 1024


def _cparams(sem):
    return pltpu.CompilerParams(dimension_semantics=sem, vmem_limit_bytes=VMEM_LIMIT)


def _dot(a, b):
    return jnp.dot(a, b, preferred_element_type=F32)


def _dot_nt(a, b):
    return lax.dot_general(a, b, (((1,), (1,)), ((), ())), preferred_element_type=F32)


def _dot_tn(a, b):
    return lax.dot_general(a, b, (((0,), (0,)), ((), ())), preferred_element_type=F32)


def _split2(x):
    hi = x.astype(BF16)
    lo = (x - hi.astype(F32)).astype(BF16)
    return hi, lo


def _dot3(a, b_hi, b_lo):
    a_hi, a_lo = _split2(a)
    return _dot(a_hi, b_hi) + (_dot(a_hi, b_lo) + _dot(a_lo, b_hi))


def _silu(x):
    return x * jax.nn.sigmoid(x)


def _rms(x, g):
    return x * lax.rsqrt(jnp.mean(x * x, axis=-1, keepdims=True) + EPS) * g


def _adaln_kernel(c_ref, w_ref, b_ref, o_ref):
    s = _silu(c_ref[...])
    w_hi, w_lo = _split2(w_ref[...])
    o_ref[...] = _dot3(s, w_hi, w_lo) + b_ref[...]


def _adaln(cs, w, b):
    rows, n = cs.shape[0], w.shape[1]
    tn = 1024
    return pl.pallas_call(
        _adaln_kernel,
        grid=(n // tn,),
        in_specs=[pl.BlockSpec((rows, D_MODEL), lambda j: (0, 0)),
                  pl.BlockSpec((D_MODEL, tn), lambda j: (0, j)),
                  pl.BlockSpec((1, tn), lambda j: (0, j))],
        out_specs=pl.BlockSpec((rows, tn), lambda j: (0, j)),
        out_shape=jax.ShapeDtypeStruct((rows, n), F32),
        compiler_params=_cparams(("arbitrary",)),
        name="adaln",
    )(cs, w, b)


def _group_lane_masks(rows):
    grp = lax.shift_right_logical(lax.broadcasted_iota(I32, (rows, LANES), 1), 4)
    return [grp == j for j in range(LANES // S5_GROUP_CH)]


def _move_group(x, src, dst):
    shift = ((dst - src) * S5_GROUP_CH) % LANES
    return pltpu.roll(x, shift, 1) if shift else x


def _proj_kernel(x_ref, sh_ref, sc_ref, g_ref, wm_ref, wl_ref, wa_ref, ba_ref,
                 q_ref, k_ref, v_ref, go_ref, u_ref, laf_ref, lab_ref, uv_ref, ut_ref, *, tm):
    h = _rms(x_ref[0], g_ref[...]) * (1.0 + sc_ref[0]) + sh_ref[0]
    hb = h.astype(BF16)
    q_ref[0] = _dot(hb, wm_ref[:, 0:256]) * (GLA_DK_HEAD ** -0.5)
    k_ref[0] = _dot(hb, wm_ref[:, 256:512])
    v_ref[0] = _dot(hb, wm_ref[:, 512:1024])
    go_ref[0] = _dot(hb, wm_ref[:, 1024:1536])
    u = _dot(hb, wm_ref[:, 1536:2048])
    u_ref[0] = u
    for t in range(D_S5 // LANES):
        ut_ref[t] = u[:, LANES * t:LANES * (t + 1)]
    lr = _dot(hb, wl_ref[...])
    pre = _dot(lr.astype(BF16), wa_ref[...]) + ba_ref[...]
    la = (jnp.minimum(pre, 0.0) - jnp.log1p(jnp.exp(-jnp.abs(pre)))) * (1.0 / GLA_GATE_TAU)
    laf_ref[0] = la[:, 0:GLA_DK]
    lab_ref[0] = la[:, GLA_DK:2 * GLA_DK]
    nc = tm // S5_CHUNK
    gpt = LANES // S5_GROUP_CH
    masks = _group_lane_masks(nc)
    for t in range(D_S5 // LANES):
        steps = [ut_ref[t, pl.ds(s, nc, stride=S5_CHUNK), :] for s in range(S5_CHUNK)]
        for gl in range(gpt):
            for half in range(S5_VEC // LANES):
                acc = None
                for j in range(gpt):
                    piece = _move_group(steps[half * gpt + j], gl, j)
                    acc = piece if acc is None else jnp.where(masks[j], piece, acc)
                uv_ref[t * gpt + gl, :, LANES * half:LANES * (half + 1)] = acc.astype(BF16)


def _proj(x, shift, scale, gain, wm, wl, wa, ba):
    b, l, _ = x.shape
    tm = min(512, l)
    nt = l // tm
    row = lambda bi, i: (bi, i, 0)
    mod = lambda bi, i: (bi, 0, 0)
    full = lambda bi, i: (0, 0)
    widths = (GLA_DK, GLA_DK, GLA_DV, GLA_DV, D_S5, GLA_DK, GLA_DK)
    return pl.pallas_call(
        functools.partial(_proj_kernel, tm=tm),
        grid=(b, nt),
        in_specs=[pl.BlockSpec((1, tm, D_MODEL), row),
                  pl.BlockSpec((1, 1, D_MODEL), mod),
                  pl.BlockSpec((1, 1, D_MODEL), mod),
                  pl.BlockSpec((1, D_MODEL), full),
                  pl.BlockSpec(wm.shape, full),
                  pl.BlockSpec(wl.shape, full),
                  pl.BlockSpec(wa.shape, full),
                  pl.BlockSpec(ba.shape, full)],
        out_specs=[pl.BlockSpec((1, tm, w), row) for w in widths]
                  + [pl.BlockSpec((S5_GROUPS, tm // S5_CHUNK, S5_VEC), lambda bi, i: (0, bi * nt + i, 0))],
        out_shape=[jax.ShapeDtypeStruct((b, l, w), F32) for w in widths]
                  + [jax.ShapeDtypeStruct((S5_GROUPS, b * l // S5_CHUNK, S5_VEC), BF16)],
        scratch_shapes=[pltpu.VMEM((D_S5 // LANES, tm, LANES), F32)],
        compiler_params=_cparams(("arbitrary", "arbitrary")),
        name="proj",
    )(x, shift, scale, gain, wm, wl, wa, ba)


def _gla_kernel(*refs, n_chunks, with_output):
    if with_output:
        q_ref, k_ref, v_ref, laf_ref, lab_ref, s0f_ref, s0b_ref, o_ref, sf_ref, sb_ref, st_ref, ob_ref = refs
    else:
        q_ref, k_ref, v_ref, laf_ref, lab_ref, s0f_ref, s0b_ref, sf_ref, sb_ref, st_ref = refs
        o_ref = ob_ref = None
    c = GLA_CHUNK
    row = lax.broadcasted_iota(I32, (c, c), 0)
    col = lax.broadcasted_iota(I32, (c, c), 1)
    lane = lax.broadcasted_iota(I32, (c, LANES), 1)
    st_ref[0] = s0f_ref[0]
    st_ref[1] = s0b_ref[0]

    def chunk(direction, idx):
        la_ref = laf_ref if direction == 0 else lab_ref
        tri = (row >= col) if direction == 0 else (row <= col)
        trib = jnp.where(tri, 1.0, 0.0).astype(BF16)
        r0 = pl.multiple_of(idx * c, c)
        q = q_ref[0, pl.ds(r0, c), :]
        k = k_ref[0, pl.ds(r0, c), :]
        v = v_ref[0, pl.ds(r0, c), :]
        la_hi, la_lo = _split2(la_ref[0, pl.ds(r0, c), :])
        cum = _dot(trib, la_hi) + _dot(trib, la_lo)
        tot = cum[c - 1:c, :] if direction == 0 else cum[0:1, :]
        qd = q * jnp.exp(cum)
        ki = k * jnp.exp(-cum)
        ks = k * jnp.exp(tot - cum)
        dec = jnp.exp(tot)
        for h in range(GLA_HEADS):
            pair = slice(LANES * (h // 2), LANES * (h // 2) + LANES)
            own = (lane >= GLA_DK_HEAD * (h % 2)) & (lane < GLA_DK_HEAD * (h % 2) + GLA_DK_HEAD)
            vb = v[:, GLA_DV_HEAD * h:GLA_DV_HEAD * (h + 1)].astype(BF16)
            st = st_ref[direction, h]
            if with_output:
                qb = qd[:, pair].astype(BF16)
                kib = jnp.where(own, ki[:, pair], 0.0).astype(BF16)
                sc = jnp.where(tri, _dot_nt(qb, kib), 0.0)
                o = _dot(sc.astype(BF16), vb) + _dot_nt(qb, st.astype(BF16))
                cols = slice(GLA_DV_HEAD * h, GLA_DV_HEAD * (h + 1))
                if direction == 0:
                    o_ref[0, pl.ds(r0, c), cols] = o
                else:
                    ob_ref[pl.ds(r0, c), cols] = o
            ksb = jnp.where(own, ks[:, pair], 0.0).astype(BF16)
            st_ref[direction, h] = st * dec[:, pair] + _dot_tn(vb, ksb)

    def body(ci, carry):
        chunk(0, ci)
        chunk(1, n_chunks - 1 - ci)
        return carry

    lax.fori_loop(0, n_chunks, body, 0)
    sf_ref[0] = st_ref[0]
    sb_ref[0] = st_ref[1]
    if with_output:
        o_ref[0] = o_ref[0] + ob_ref[...]


def _gla(q, k, v, laf, lab, s0f, s0b, with_output):
    b, l, _ = q.shape
    n_chunks = l // GLA_CHUNK
    seq = lambda bi: (bi, 0, 0)
    st = lambda bi: (bi, 0, 0, 0)
    st_shape = (b, GLA_HEADS, GLA_DV_HEAD, LANES)
    st_spec = pl.BlockSpec((1, GLA_HEADS, GLA_DV_HEAD, LANES), st)
    out_specs = [st_spec, st_spec]
    out_shape = [jax.ShapeDtypeStruct(st_shape, F32)] * 2
    if with_output:
        out_specs = [pl.BlockSpec((1, l, GLA_DV), seq)] + out_specs
        out_shape = [jax.ShapeDtypeStruct((b, l, GLA_DV), F32)] + out_shape
    return pl.pallas_call(
        functools.partial(_gla_kernel, n_chunks=n_chunks, with_output=with_output),
        grid=(b,),
        in_specs=[pl.BlockSpec((1, l, GLA_DK), seq),
                  pl.BlockSpec((1, l, GLA_DK), seq),
                  pl.BlockSpec((1, l, GLA_DV), seq),
                  pl.BlockSpec((1, l, GLA_DK), seq),
                  pl.BlockSpec((1, l, GLA_DK), seq),
                  st_spec, st_spec],
        out_specs=out_specs,
        out_shape=out_shape,
        scratch_shapes=[pltpu.VMEM((2, GLA_HEADS, GLA_DV_HEAD, LANES), F32)]
                       + ([pltpu.VMEM((l, GLA_DV), F32)] if with_output else []),
        compiler_params=_cparams(("arbitrary",)),
        name="gla_out" if with_output else "gla_ctx",
    )(q, k, v, laf, lab, s0f, s0b)


def _s5gen_kernel(pc_ref, pr_ref, btr_ref, bti_ref, ctrf_ref, ctif_ref, ctrb_ref, ctib_ref,
                  m_ref, wt_ref, v_ref, ab_ref):
    pc = pc_ref[0]
    blk = lax.shift_right_logical(lax.broadcasted_iota(I32, (1, S5_VEC), 1), 4).astype(F32)
    lane = lax.broadcasted_iota(I32, (S5_GROUP_CH, S5_VEC), 1)
    n = float(S5_CHUNK)

    def cmul(ar, ai, br, bi):
        return ar * br - ai * bi, ar * bi + ai * br

    kcat = []
    for d in (0, 1):
        lre, lim, ls = pc[:, 3 * d:3 * d + 1], pc[:, 3 * d + 1:3 * d + 2], pc[:, 3 * d + 2:3 * d + 3]
        ctr = (ctrf_ref if d == 0 else ctrb_ref)[0]
        cti = (ctif_ref if d == 0 else ctib_ref)[0]
        step = jnp.exp(ls)
        mag = jnp.exp(lre * step)
        a_re = mag * jnp.cos(lim * step)
        a_im = mag * jnp.sin(lim * step)
        den = lre * lre + lim * lim
        f_re = ((a_re - 1.0) * lre + a_im * lim) / den
        f_im = (a_im * lre - (a_re - 1.0) * lim) / den
        bb_re, bb_im = cmul(f_re, f_im, btr_ref[0], bti_ref[0])

        def powers(e, lre=lre, lim=lim, step=step):
            m = jnp.exp(lre * step * e)
            ang = lim * step * e
            return m * jnp.cos(ang), m * jnp.sin(ang)

        w_re, w_im = cmul(*powers((n - 1.0 - blk) if d == 0 else blk), bb_re, bb_im)
        wt_ref[0, S5_STATE * d:S5_STATE * (d + 1), :] = w_re
        wt_ref[0, 2 * S5_STATE + S5_STATE * d:2 * S5_STATE + S5_STATE * (d + 1), :] = w_im
        c_re, c_im = cmul(*powers((blk + 1.0) if d == 0 else (n - blk)), ctr, cti)
        v_ref[0, S5_STATE * d:S5_STATE * (d + 1), :] = c_re
        v_ref[0, 2 * S5_STATE + S5_STATE * d:2 * S5_STATE + S5_STATE * (d + 1), :] = -c_im
        e_re, e_im = cmul(*powers(blk if d == 0 else (n - 1.0 - blk)), ctr, cti)
        b16r_hi, b16r_lo = _split2(bb_re[:, 0:S5_GROUP_CH])
        b16i_hi, b16i_lo = _split2(bb_im[:, 0:S5_GROUP_CH])
        er_hi, er_lo = _split2(e_re)
        ei_hi, ei_lo = _split2(e_im)
        kr = _dot_tn(b16r_hi, er_hi) + (_dot_tn(b16r_hi, er_lo) + _dot_tn(b16r_lo, er_hi))
        ki = _dot_tn(b16i_hi, ei_hi) + (_dot_tn(b16i_hi, ei_lo) + _dot_tn(b16i_lo, ei_hi))
        kcat.append(kr - ki)

    for s in range(S5_CHUNK):
        sh_f = S5_GROUP_CH * s
        fwd = kcat[0] if s == 0 else pltpu.roll(kcat[0], sh_f, 1)
        fwd = jnp.where(lane >= sh_f, fwd, 0.0)
        sh_b = S5_VEC - S5_GROUP_CH * (S5_CHUNK - 1 - s)
        bwd = kcat[1] if sh_b == S5_VEC else pltpu.roll(kcat[1], sh_b, 1)
        bwd = jnp.where(lane < S5_GROUP_CH * (s + 1), bwd, 0.0)
        m_ref[0, S5_GROUP_CH * s:S5_GROUP_CH * (s + 1), :] = fwd + bwd

    pr = pr_ref[0]
    for d in (0, 1):
        lre, lim, ls = pr[3 * d:3 * d + 1, :], pr[3 * d + 1:3 * d + 2, :], pr[3 * d + 2:3 * d + 3, :]
        stp = jnp.exp(ls) * n
        mg = jnp.exp(lre * stp)
        ab_ref[0, d:d + 1, :] = mg * jnp.cos(lim * stp)
        ab_ref[0, 2 + d:3 + d, :] = mg * jnp.sin(lim * stp)


def _s5gen(pc, pr, btr, bti, ctrf, ctif, ctrb, ctib):
    g = pc.shape[0]
    blk3 = lambda shape: pl.BlockSpec((1,) + shape, lambda i: (i, 0, 0))
    big = (S5_STATE, S5_VEC)
    sq = (S5_VEC, S5_VEC)
    return pl.pallas_call(
        _s5gen_kernel,
        grid=(g,),
        in_specs=[blk3((S5_STATE, 8)), blk3((8, S5_STATE))] + [blk3(big)] * 6,
        out_specs=[blk3(sq), blk3(sq), blk3(sq), blk3((4, S5_STATE))],
        out_shape=[jax.ShapeDtypeStruct((g,) + sq, F32)] * 3 + [jax.ShapeDtypeStruct((g, 4, S5_STATE), F32)],
        compiler_params=_cparams(("arbitrary",)),
        name="s5gen",
    )(pc, pr, btr, bti, ctrf, ctif, ctrb, ctib)


def _s5_kernel(*refs, n_chunks, nb, with_output):
    if with_output:
        u_ref, m_ref, wt_ref, v_ref, ab_ref, x0_ref, y_ref, xf_ref, z_ref, cin_ref = refs
    else:
        u_ref, wt_ref, ab_ref, x0_ref, xf_ref, z_ref = refs
    wtb = wt_ref[0].astype(BF16)
    for bi in range(nb):
        z = _dot_nt(u_ref[0, bi * n_chunks:(bi + 1) * n_chunks, :], wtb)
        z_ref[0, pl.ds(bi, n_chunks, stride=nb), :] = z[:, 0:LANES]
        z_ref[1, pl.ds(bi, n_chunks, stride=nb), :] = z[:, LANES:2 * LANES]
    ab = ab_ref[0]
    ar, ai = ab[:, 0:LANES], ab[:, LANES:2 * LANES]
    is_f = lax.broadcasted_iota(I32, (nb, LANES), 1) < S5_STATE
    x0 = x0_ref[0]

    def body(i, carry):
        xr, xi = carry
        rf = pl.multiple_of(i * nb, nb)
        rb = pl.multiple_of((n_chunks - 1 - i) * nb, nb)
        if with_output:
            cin_ref[0, pl.ds(rf, nb), 0:S5_STATE] = xr[:, 0:S5_STATE]
            cin_ref[1, pl.ds(rf, nb), 0:S5_STATE] = xi[:, 0:S5_STATE]
            cin_ref[0, pl.ds(rb, nb), S5_STATE:LANES] = xr[:, S5_STATE:LANES]
            cin_ref[1, pl.ds(rb, nb), S5_STATE:LANES] = xi[:, S5_STATE:LANES]
        zr = jnp.where(is_f, z_ref[0, pl.ds(rf, nb), :], z_ref[0, pl.ds(rb, nb), :])
        zi = jnp.where(is_f, z_ref[1, pl.ds(rf, nb), :], z_ref[1, pl.ds(rb, nb), :])
        return ar * xr - ai * xi + zr, ar * xi + ai * xr + zi

    xr, xi = lax.fori_loop(0, n_chunks, body, (x0[:, 0:LANES], x0[:, LANES:2 * LANES]))
    xf_ref[0, :, 0:LANES] = xr
    xf_ref[0, :, LANES:2 * LANES] = xi
    if with_output:
        mb = m_ref[0].astype(BF16)
        vb = v_ref[0].astype(BF16)
        for bi in range(nb):
            rows = slice(bi * n_chunks, (bi + 1) * n_chunks)
            carried = jnp.concatenate([cin_ref[0, pl.ds(bi, n_chunks, stride=nb), :],
                                       cin_ref[1, pl.ds(bi, n_chunks, stride=nb), :]], axis=1).astype(BF16)
            y_ref[0, rows, :] = _dot(u_ref[0, rows, :], mb) + _dot(carried, vb)


def _s5(uvec, m, wt, v, ab, x0, nb, with_output):
    g, rows, _ = uvec.shape
    n_chunks = rows // nb
    blk3 = lambda shape: pl.BlockSpec((1,) + shape, lambda i: (i, 0, 0))
    sq = (S5_VEC, S5_VEC)
    st = (nb, S5_VEC)
    if with_output:
        args = (uvec, m, wt, v, ab, x0)
        in_specs = [blk3((rows, S5_VEC)), blk3(sq), blk3(sq), blk3(sq), blk3((1, S5_VEC)), blk3(st)]
        out_specs = [blk3((rows, S5_VEC)), blk3(st)]
        out_shape = [jax.ShapeDtypeStruct((g, rows, S5_VEC), F32), jax.ShapeDtypeStruct((g,) + st, F32)]
        scratch = [pltpu.VMEM((S5_VEC // LANES, rows, LANES), F32), pltpu.VMEM((S5_VEC // LANES, rows, LANES), F32)]
    else:
        args = (uvec, wt, ab, x0)
        in_specs = [blk3((rows, S5_VEC)), blk3(sq), blk3((1, S5_VEC)), blk3(st)]
        out_specs = [blk3(st)]
        out_shape = [jax.ShapeDtypeStruct((g,) + st, F32)]
        scratch = [pltpu.VMEM((S5_VEC // LANES, rows, LANES), F32)]
    return pl.pallas_call(
        functools.partial(_s5_kernel, n_chunks=n_chunks, nb=nb, with_output=with_output),
        grid=(g,),
        in_specs=in_specs,
        out_specs=out_specs,
        out_shape=out_shape,
        scratch_shapes=scratch,
        compiler_params=_cparams(("arbitrary",)),
        name="s5_out" if with_output else "s5_ctx",
    )(*args)


def _post_kernel(x_ref, o_ref, go_ref, u_ref, yv_ref, g1_ref, sh2_ref, sc2_ref, g2_ref,
                 gn_ref, d_ref, gw_ref, gb_ref, wo_ref, n2_ref, rwh_ref, rwl_ref, sgu_ref, sd_ref,
                 base_ref, hrow_ref, lg_ref, y_ref, *, tm):
    nc = tm // S5_CHUNK
    gpt = LANES // S5_GROUP_CH
    masks = _group_lane_masks(nc)
    for s in range(S5_CHUNK):
        half, j = divmod(s, gpt)
        for t in range(D_S5 // LANES):
            acc = None
            for gl in range(gpt):
                piece = _move_group(yv_ref[t * gpt + gl, :, LANES * half:LANES * (half + 1)], j, gl)
                acc = piece if acc is None else jnp.where(masks[gl], piece, acc)
            y_ref[t, pl.ds(s, nc, stride=S5_CHUNK), :] = acc
    o = o_ref[0]
    gn = gn_ref[...]
    heads = [_rms(o[:, GLA_DV_HEAD * h:GLA_DV_HEAD * (h + 1)], gn) for h in range(GLA_HEADS)]
    gla_out = jnp.concatenate(heads, axis=1) * _silu(go_ref[0])
    yy = jnp.concatenate([y_ref[t] for t in range(D_S5 // LANES)], axis=1) + d_ref[...] * u_ref[0]
    z = 0.5 * yy * (1.0 + jnp.tanh(0.7978845608028654 * (yy + 0.044715 * (yy * yy * yy))))
    s5_out = z * jax.nn.sigmoid(_dot(z.astype(BF16), gw_ref[...]) + gb_ref[...])
    mix = jnp.concatenate([gla_out, s5_out], axis=1).astype(BF16)
    x1 = x_ref[0] + g1_ref[0] * _dot(mix, wo_ref[...])
    h2 = _rms(x1, n2_ref[...]) * (1.0 + sc2_ref[0]) + sh2_ref[0]
    lg_ref[0] = _dot3(h2, rwh_ref[...], rwl_ref[...])
    hb = h2.astype(BF16)
    gu = _dot(hb, sgu_ref[...])
    hid = _silu(gu[:, 0:D_SHARED]) * gu[:, D_SHARED:2 * D_SHARED]
    base_ref[0] = x1 + g2_ref[0] * _dot(hid.astype(BF16), sd_ref[...])
    for s in range(ROW_TILES):
        hrow_ref[pl.ds(s, tm, stride=ROW_TILES), :] = h2[:, LANES * s:LANES * (s + 1)]


def _post(x, o, go, u, yvec, g1, sh2, sc2, g2, gn, d, gw, gb, wo, n2, rwh, rwl, sgu, sd):
    b, l, _ = x.shape
    tm = 256
    nt = l // tm
    row = lambda bi, i: (bi, i, 0)
    mod = lambda bi, i: (bi, 0, 0)
    full = lambda bi, i: (0, 0)
    ws = (gn, d, gw, gb, wo, n2, rwh, rwl, sgu, sd)
    return pl.pallas_call(
        functools.partial(_post_kernel, tm=tm),
        grid=(b, nt),
        in_specs=[pl.BlockSpec((1, tm, D_MODEL), row)]
                 + [pl.BlockSpec((1, tm, 512), row)] * 3
                 + [pl.BlockSpec((S5_GROUPS, tm // S5_CHUNK, S5_VEC), lambda bi, i: (0, bi * nt + i, 0))]
                 + [pl.BlockSpec((1, 1, D_MODEL), mod)] * 4
                 + [pl.BlockSpec(w.shape, full) for w in ws],
        out_specs=[pl.BlockSpec((1, tm, D_MODEL), row),
                   pl.BlockSpec((tm * ROW_TILES, LANES), lambda bi, i: (bi * nt + i, 0)),
                   pl.BlockSpec((1, tm, N_EXPERTS), row)],
        out_shape=[jax.ShapeDtypeStruct((b, l, D_MODEL), F32),
                   jax.ShapeDtypeStruct((b * l * ROW_TILES, LANES), F32),
                   jax.ShapeDtypeStruct((b, l, N_EXPERTS), F32)],
        scratch_shapes=[pltpu.VMEM((D_S5 // LANES, tm, LANES), F32)],
        compiler_params=_cparams(("arbitrary", "arbitrary")),
        name="post",
    )(x, o, go, u, yvec, g1, sh2, sc2, g2, *ws)


def _route_kernel(lg_ref, rb_ref, w_ref, p_ref, rb4_ref, tc_ref, cnt_ref, run_ref, *, tm):
    @pl.when(pl.program_id(0) == 0)
    def _():
        run_ref[...] = jnp.zeros_like(run_ref)

    neg = -jnp.inf
    gsz = N_EXPERTS // N_EXPERT_GROUPS
    s = jax.nn.sigmoid(lg_ref[...].T)
    biased = s + rb_ref[...]
    row = lax.broadcasted_iota(I32, (N_EXPERTS, tm), 0).astype(F32)

    def first_max(m, idx):
        mx = jnp.max(m, axis=0, keepdims=True)
        ix = jnp.min(jnp.where(m == mx, idx, float(N_EXPERTS)), axis=0, keepdims=True)
        return mx, ix

    grow = lax.broadcasted_iota(I32, (gsz, tm), 0).astype(F32)
    gs = []
    for g in range(N_EXPERT_GROUPS):
        m, idx = biased[gsz * g:gsz * (g + 1), :], grow + float(gsz * g)
        m1, i1 = first_max(m, idx)
        gs.append(m1 + jnp.max(jnp.where(idx == i1, neg, m), axis=0, keepdims=True))
    kept = []
    for g in range(N_EXPERT_GROUPS):
        ahead = jnp.zeros((1, tm), F32)
        for j in range(N_EXPERT_GROUPS):
            if j < g:
                ahead = ahead + jnp.where(gs[j] >= gs[g], 1.0, 0.0)
            elif j > g:
                ahead = ahead + jnp.where(gs[j] > gs[g], 1.0, 0.0)
        kept.append(jnp.where(ahead < float(TOPK_GROUPS), biased[gsz * g:gsz * (g + 1), :], neg))
    masked = jnp.concatenate(kept, axis=0)

    onehot = jnp.zeros((N_EXPERTS, tm), F32)
    ids, ws = [], []
    for _ in range(TOP_K):
        _, ik = first_max(masked, row)
        hit = row == ik
        ids.append(ik)
        ws.append(jnp.sum(jnp.where(hit, s, 0.0), axis=0, keepdims=True))
        onehot = onehot + jnp.where(hit, 1.0, 0.0)
        masked = jnp.where(hit, neg, masked)
    wsum = ws[0]
    for k in range(1, TOP_K):
        wsum = wsum + ws[k]

    ss = lax.broadcasted_iota(I32, (tm, tm), 0)
    tt = lax.broadcasted_iota(I32, (tm, tm), 1)
    earlier = jnp.where(ss < tt, 1.0, 0.0).astype(BF16)
    ohb = onehot.astype(BF16)
    tcnt = _dot(ohb, jnp.ones((tm, LANES), BF16))
    ee = lax.broadcasted_iota(I32, (N_EXPERTS, N_EXPERTS), 0)
    ff = lax.broadcasted_iota(I32, (N_EXPERTS, N_EXPERTS), 1)
    below = jnp.where(ff < ee, 1.0, 0.0).astype(BF16)
    t_hi, t_lo = _split2(tcnt)
    toff = _dot(below, t_hi) + _dot(below, t_lo)
    lpos = (_dot(ohb, earlier) + toff[:, 0:1]) * float(ROW_TILES)
    w_ref[...] = jnp.concatenate([w / wsum * ROUTE_SCALE for w in ws], axis=0)
    p_ref[...] = jnp.concatenate(
        [jnp.sum(jnp.where(row == ids[k], lpos, 0.0), axis=0, keepdims=True) for k in range(TOP_K)], axis=0).astype(I32)
    run = run_ref[...]
    rb4_ref[0] = run.astype(I32)
    tc_ref[0] = tcnt.astype(I32)
    run = run + tcnt
    run_ref[...] = run
    cnt_ref[...] = run.astype(I32)


def _route(logits, rb, tm):
    t = logits.shape[0]
    col = lambda i: (0, i)
    fixed = lambda i: (0, 0)
    tile = lambda i: (i, 0, 0)
    per_tile = jax.ShapeDtypeStruct((t // tm, N_EXPERTS, LANES), I32)
    return pl.pallas_call(
        functools.partial(_route_kernel, tm=tm),
        grid=(t // tm,),
        in_specs=[pl.BlockSpec((tm, N_EXPERTS), lambda i: (i, 0)), pl.BlockSpec((N_EXPERTS, 1), fixed)],
        out_specs=[pl.BlockSpec((TOP_K, tm), col)] * 2 + [pl.BlockSpec((1, N_EXPERTS, LANES), tile)] * 2
                  + [pl.BlockSpec((N_EXPERTS, LANES), fixed)],
        out_shape=[jax.ShapeDtypeStruct((TOP_K, t), F32), jax.ShapeDtypeStruct((TOP_K, t), I32), per_tile, per_tile,
                   jax.ShapeDtypeStruct((N_EXPERTS, LANES), I32)],
        scratch_shapes=[pltpu.VMEM((N_EXPERTS, LANES), F32)],
        compiler_params=_cparams(("arbitrary",)),
        name="route",
    )(logits, rb)


def _n_blocks_max(n_assign):
    return -(-(n_assign + N_EXPERTS * (EXPERT_BLOCK - 1)) // EXPERT_BLOCK)


def _plan_kernel(cnt_ref, ps_ref, blk_ref, nv_ref, *, nbp):
    cnt = cnt_ref[...]
    nb = lax.shift_right_logical(cnt + (EXPERT_BLOCK - 1), 8).astype(F32)
    nb8 = jnp.broadcast_to(nb, (SUBLANES, N_EXPERTS))
    nb_hi, nb_lo = _split2(nb8)
    ii = lax.broadcasted_iota(I32, (N_EXPERTS, N_EXPERTS), 0)
    jj = lax.broadcasted_iota(I32, (N_EXPERTS, N_EXPERTS), 1)
    upto = jnp.where(ii <= jj, 1.0, 0.0).astype(BF16)
    cum = (_dot(nb_hi, upto) + _dot(nb_lo, upto))[0:1, :]
    ps_ref[...] = ((cum - nb) * float(EXPERT_BLOCK)).astype(I32)
    bi = lax.broadcasted_iota(I32, (nbp, N_EXPERTS), 0).astype(F32)
    owner = jnp.sum(jnp.where(cum <= bi, 1.0, 0.0), axis=-1, keepdims=True)
    blk_ref[...] = jnp.minimum(owner, float(N_EXPERTS - 1)).astype(I32)
    nv_ref[...] = cum[:, N_EXPERTS - 1:N_EXPERTS].astype(I32)


def _plan(cnt, nbp):
    return pl.pallas_call(
        functools.partial(_plan_kernel, nbp=nbp),
        out_shape=[jax.ShapeDtypeStruct((1, N_EXPERTS), I32), jax.ShapeDtypeStruct((nbp, 1), I32),
                   jax.ShapeDtypeStruct((1, 1), I32)],
        name="plan",
    )(cnt)


def _segment_copies(tc_ref, rb_ref, ps_ref, make):
    def body(pair, local):
        for j in range(2):
            e = 2 * pair + j
            cnt = tc_ref[e]

            @pl.when(cnt > 0)
            def _(e=e, local=local, cnt=cnt, j=j):
                make(local, ps_ref[e] + rb_ref[e], cnt).start(priority=j)

            local = local + cnt
        return local

    lax.fori_loop(0, N_EXPERTS // 2, body, 0)


def _rows(ref, row, n):
    return ref.at[pl.ds(pl.multiple_of(row * ROW_TILES, ROW_TILES), n * ROW_TILES)]


def _dispatch_kernel(lp_ref, tc_ref, rb_ref, ps_ref, cnt_ref, nv_ref, h_ref, xs_ref, sorted_ref, zero_ref, sems, zsem,
                     *, tm, nbp):
    step = pl.program_id(0)

    @pl.when(step == 0)
    def _():
        zero_ref[...] = jnp.zeros_like(zero_ref)
        for wait in (False, True):
            def tail_body(bi, carry, wait=wait):
                cp = pltpu.make_async_copy(zero_ref, _rows(xs_ref, bi * EXPERT_BLOCK, EXPERT_BLOCK), zsem)
                cp.wait() if wait else cp.start()
                return carry
            lax.fori_loop(nv_ref[0], nbp, tail_body, 0)

            def pad_body(e, carry, wait=wait):
                cnt = cnt_ref[e]
                pad = (-cnt) & (EXPERT_BLOCK - 1)

                @pl.when(pad > 0)
                def _():
                    cp = pltpu.make_async_copy(_rows(zero_ref, 0, pad), _rows(xs_ref, ps_ref[e] + cnt, pad), zsem)
                    cp.wait() if wait else cp.start()

                return carry
            lax.fori_loop(0, N_EXPERTS, pad_body, 0)

    slot = step % 2
    mine = sorted_ref.at[slot]

    def token_body(t, carry):
        row = h_ref[pl.ds(pl.multiple_of(t * ROW_TILES, ROW_TILES), ROW_TILES), :]
        for k in range(TOP_K):
            mine[pl.ds(pl.multiple_of(lp_ref[t * TOP_K + k], ROW_TILES), ROW_TILES), :] = row
        return carry

    lax.fori_loop(0, tm, token_body, 0, unroll=2)
    _segment_copies(tc_ref, rb_ref, ps_ref,
                    lambda loc, glob, n: pltpu.make_async_copy(_rows(mine, loc, n), _rows(xs_ref, glob, n), sems.at[slot]))

    def drain(s):
        pltpu.make_async_copy(sorted_ref.at[s], _rows(xs_ref, 0, tm * TOP_K), sems.at[s]).wait()

    @pl.when(step > 0)
    def _():
        drain(1 - slot)

    @pl.when(step == pl.num_programs(0) - 1)
    def _():
        drain(slot)


def _dispatch(lp_flat, tcnt, runb, pstart, cnt, nv, hrows, nbp, tm):
    t = lp_flat.shape[0] // TOP_K
    per_tile = pl.BlockSpec((N_EXPERTS,), lambda i: (i,), memory_space=pltpu.SMEM)
    smem_all = pl.BlockSpec((N_EXPERTS,), lambda i: (0,), memory_space=pltpu.SMEM)
    return pl.pallas_call(
        functools.partial(_dispatch_kernel, tm=tm, nbp=nbp),
        grid=(t // tm,),
        in_specs=[pl.BlockSpec((tm * TOP_K,), lambda i: (i,), memory_space=pltpu.SMEM),
                  per_tile, per_tile, smem_all, smem_all,
                  pl.BlockSpec((1,), lambda i: (0,), memory_space=pltpu.SMEM),
                  pl.BlockSpec((tm * ROW_TILES, LANES), lambda i: (i, 0))],
        out_specs=pl.BlockSpec(memory_space=pl.ANY),
        out_shape=jax.ShapeDtypeStruct((nbp * EXPERT_BLOCK * ROW_TILES, LANES), F32),
        scratch_shapes=[pltpu.VMEM((2, tm * TOP_K * ROW_TILES, LANES), F32),
                        pltpu.VMEM((EXPERT_BLOCK * ROW_TILES, LANES), F32),
                        pltpu.SemaphoreType.DMA((2,)), pltpu.SemaphoreType.DMA],
        compiler_params=_cparams(("arbitrary",)),
        name="dispatch",
    )(lp_flat, tcnt, runb, pstart, cnt, nv, hrows)


def _experts_kernel(blk_ref, nv_ref, xs_hbm, wg_hbm, wu_hbm, wd_hbm, ys_hbm, step_ref, wgb_ref, wub_ref, wdb_ref, *, nbp):
    rows = EXPERT_BLOCK * ROW_TILES
    nv = nv_ref[0]
    cur = lambda i: jnp.minimum(i, nv - 1)
    xmap = lambda i: (cur(i), 0)
    wmap = lambda i: (blk_ref[cur(i)], 0, 0)
    step_ref[0] = 0

    def body(xs_ref, wg_ref, wu_ref, wd_ref, ys_ref):
        i = step_ref[0]
        step_ref[0] = i + 1

        @pl.when(i >= nv)
        def _():
            ys_ref[...] = jnp.zeros_like(ys_ref)

        @pl.when(i < nv)
        def _():
            prev = blk_ref[jnp.maximum(i - 1, 0)]

            @pl.when((i == 0) | (blk_ref[i] != prev))
            def _():
                wgb_ref[...] = wg_ref[0].astype(BF16)
                wub_ref[...] = wu_ref[0].astype(BF16)
                wdb_ref[...] = wd_ref[0].astype(BF16)

            xb = jnp.concatenate(
                [xs_ref[pl.ds(s, EXPERT_BLOCK, stride=ROW_TILES), :] for s in range(ROW_TILES)], axis=1).astype(BF16)
            hid = _silu(_dot(xb, wgb_ref[...])) * _dot(xb, wub_ref[...])
            y = _dot(hid.astype(BF16), wdb_ref[...])
            for s in range(ROW_TILES):
                ys_ref[pl.ds(s, EXPERT_BLOCK, stride=ROW_TILES), :] = y[:, LANES * s:LANES * (s + 1)]

    ahead = pl.Buffered(2, use_lookahead=True)
    pltpu.emit_pipeline(
        body,
        grid=(nbp,),
        in_specs=[pl.BlockSpec((rows, LANES), xmap, pipeline_mode=pl.Buffered(3)),
                  pl.BlockSpec((1, D_MODEL, D_EXPERT), wmap, pipeline_mode=ahead),
                  pl.BlockSpec((1, D_MODEL, D_EXPERT), wmap, pipeline_mode=ahead),
                  pl.BlockSpec((1, D_EXPERT, D_MODEL), wmap, pipeline_mode=ahead)],
        out_specs=[pl.BlockSpec((rows, LANES), lambda i: (i, 0))],
    )(xs_hbm, wg_hbm, wu_hbm, wd_hbm, ys_hbm)


def _experts(blk, nv, xs, wg, wu, wd, nbp):
    smem = pl.BlockSpec(memory_space=pltpu.SMEM)
    hbm = pl.BlockSpec(memory_space=pl.ANY)
    return pl.pallas_call(
        functools.partial(_experts_kernel, nbp=nbp),
        in_specs=[smem, smem, hbm, hbm, hbm, hbm],
        out_specs=hbm,
        out_shape=jax.ShapeDtypeStruct(xs.shape, F32),
        scratch_shapes=[pltpu.SMEM((1,), I32),
                        pltpu.VMEM((D_MODEL, D_EXPERT), BF16), pltpu.VMEM((D_MODEL, D_EXPERT), BF16),
                        pltpu.VMEM((D_EXPERT, D_MODEL), BF16)],
        compiler_params=pltpu.CompilerParams(vmem_limit_bytes=VMEM_LIMIT),
        name="experts",
    )(blk, nv, xs, wg, wu, wd)


def _combine_kernel(lp_ref, w_ref, tc_ref, rb_ref, tcn_ref, rbn_ref, ps_ref, ys_ref, base_ref, g2_ref, fg_ref, out_ref,
                    buf_ref, acc_ref, sems, *, tm):
    step = pl.program_id(0) * pl.num_programs(1) + pl.program_id(1)
    last = pl.num_programs(0) * pl.num_programs(1) - 1
    slot = step % 2

    def fetch(tcnt_ref, runb_ref, s):
        _segment_copies(tcnt_ref, runb_ref, ps_ref,
                        lambda loc, glob, n: pltpu.make_async_copy(_rows(ys_ref, glob, n), _rows(buf_ref.at[s], loc, n),
                                                                   sems.at[s]))

    @pl.when(step == 0)
    def _():
        fetch(tc_ref, rb_ref, slot)

    @pl.when(step < last)
    def _():
        fetch(tcn_ref, rbn_ref, 1 - slot)

    mine = buf_ref.at[slot]
    pltpu.make_async_copy(_rows(ys_ref, 0, tm * TOP_K), mine, sems.at[slot]).wait()

    def token_body(t, carry):
        j0 = t * TOP_K
        acc = jnp.zeros((ROW_TILES, LANES), F32)
        for k in range(TOP_K):
            acc = acc + w_ref[j0 + k] * mine[pl.ds(pl.multiple_of(lp_ref[j0 + k], ROW_TILES), ROW_TILES), :]
        acc_ref[pl.ds(pl.multiple_of(t * ROW_TILES, ROW_TILES), ROW_TILES), :] = acc
        return carry

    lax.fori_loop(0, tm, token_body, 0, unroll=2)
    routed = jnp.concatenate([acc_ref[pl.ds(s, tm, stride=ROW_TILES), :] for s in range(ROW_TILES)], axis=1)
    out_ref[0] = _rms(base_ref[0] + g2_ref[0] * routed, fg_ref[...])


def _combine(lp_flat, w_flat, tcnt, runb, pstart, ys, base, g2, fg, tm):
    b, l, _ = base.shape
    nt = l // tm
    flat = lambda bi, i: (bi * nt + i,)
    following = lambda bi, i: (jnp.minimum(bi * nt + i + 1, b * nt - 1),)
    smem_blk = pl.BlockSpec((tm * TOP_K,), flat, memory_space=pltpu.SMEM)
    per_tile = pl.BlockSpec((N_EXPERTS,), flat, memory_space=pltpu.SMEM)
    next_tile = pl.BlockSpec((N_EXPERTS,), following, memory_space=pltpu.SMEM)
    return pl.pallas_call(
        functools.partial(_combine_kernel, tm=tm),
        grid=(b, nt),
        in_specs=[smem_blk, smem_blk, per_tile, per_tile, next_tile, next_tile,
                  pl.BlockSpec((N_EXPERTS,), lambda bi, i: (0,), memory_space=pltpu.SMEM),
                  pl.BlockSpec(memory_space=pl.ANY),
                  pl.BlockSpec((1, tm, D_MODEL), lambda bi, i: (bi, i, 0)),
                  pl.BlockSpec((1, 1, D_MODEL), lambda bi, i: (bi, 0, 0)),
                  pl.BlockSpec((1, D_MODEL), lambda bi, i: (0, 0))],
        out_specs=pl.BlockSpec((1, tm, D_MODEL), lambda bi, i: (bi, i, 0)),
        out_shape=jax.ShapeDtypeStruct((b, l, D_MODEL), F32),
        scratch_shapes=[pltpu.VMEM((2, tm * TOP_K * ROW_TILES, LANES), F32),
                        pltpu.VMEM((tm * ROW_TILES, LANES), F32),
                        pltpu.SemaphoreType.DMA((2,))],
        compiler_params=_cparams(("arbitrary", "arbitrary")),
        name="combine",
    )(lp_flat, w_flat, tcnt, runb, tcnt, runb, pstart, ys, base, g2, fg)


def _mixer_inputs(h, shift, scale, gain, wm, wl, wa, ba):
    return _proj(h, shift, scale, gain, wm, wl, wa, ba)


def kernel(x, c, ctx, c_ctx, ada_w, ada_b, norm1_g, norm2_g, w_in, gla_wa_f, gla_ba_f, gla_wa_b, gla_ba_b, gla_norm_g, s5_lam_re_f, s5_lam_im_f, s5_log_step_f, s5_lam_re_b, s5_lam_im_b, s5_log_step_b, s5_b_re, s5_b_im, s5_c_re_f, s5_c_im_f, s5_c_re_b, s5_c_im_b, s5_d, s5_glu_w, s5_glu_b, w_out, router_w, router_b, exp_w_gate, exp_w_up, exp_w_down, sh_w_gate, sh_w_up, sh_w_down, final_norm_g):
    b, l, d = x.shape
    i = 0

    rows = -(-(b + 1) // SUBLANES) * SUBLANES
    cs = jnp.zeros((rows, d), F32).at[:b].set(c).at[b].set(c_ctx)
    mod = _adaln(cs, ada_w[i], ada_b[i][None, :])
    sh1, sc1, g1, sh2, sc2, g2 = [mod[:b, d * j:d * (j + 1)][:, None, :] for j in range(6)]
    csh1, csc1 = [jnp.broadcast_to(mod[b, d * j:d * (j + 1)][None, None, :], (b, 1, d)) for j in range(2)]

    w = w_in[i]
    o1, o2, o3, o4, o5, o6 = 256, 512, 1024, 1536, 1552, 1568
    wm = jnp.concatenate([w[:, :o4], w[:, o6:]], axis=1).astype(BF16)
    wl = jnp.zeros((d, LANES), F32).at[:, :2 * GLA_GATE_RANK].set(w[:, o4:o6]).astype(BF16)
    wa = jnp.zeros((LANES, 2 * GLA_DK), F32)
    wa = wa.at[:GLA_GATE_RANK, :GLA_DK].set(gla_wa_f[i]).at[GLA_GATE_RANK:2 * GLA_GATE_RANK, GLA_DK:].set(gla_wa_b[i])
    wa = wa.astype(BF16)
    ba = jnp.concatenate([gla_ba_f[i], gla_ba_b[i]])[None, :]
    n1 = norm1_g[i][None, :]

    pcols = jnp.stack([s5_lam_re_f[i], s5_lam_im_f[i],
                       jnp.broadcast_to(s5_log_step_f[i][:, None], (S5_GROUPS, S5_STATE)),
                       s5_lam_re_b[i], s5_lam_im_b[i],
                       jnp.broadcast_to(s5_log_step_b[i][:, None], (S5_GROUPS, S5_STATE)),
                       jnp.zeros((S5_GROUPS, S5_STATE), F32), jnp.zeros((S5_GROUPS, S5_STATE), F32)], axis=-1)
    prows = pcols.transpose(0, 2, 1)
    tile_b = lambda t: jnp.tile(t, (1, 1, S5_CHUNK))
    tile_c = lambda t: jnp.tile(t.transpose(0, 2, 1), (1, 1, S5_CHUNK))
    m_op, wt_op, v_op, ab4 = _s5gen(pcols, prows, tile_b(s5_b_re[i]), tile_b(s5_b_im[i]),
                                    tile_c(s5_c_re_f[i]), tile_c(s5_c_im_f[i]),
                                    tile_c(s5_c_re_b[i]), tile_c(s5_c_im_b[i]))
    ab = ab4.reshape(S5_GROUPS, 1, 4 * S5_STATE)

    cq, ck, cv, _, _, claf, clab, cuv = _proj(ctx, csh1, csc1, n1, wm, wl, wa, ba)
    zero_state = jnp.zeros((b, GLA_HEADS, GLA_DV_HEAD, LANES), F32)
    gsf, gsb = _gla(cq, ck, cv, claf, clab, zero_state, zero_state, with_output=False)
    (x0,) = _s5(cuv, None, wt_op, None, ab, jnp.zeros((S5_GROUPS, b, S5_VEC), F32), b, with_output=False)

    q, k, v, go, u, laf, lab, uv = _proj(x, sh1, sc1, n1, wm, wl, wa, ba)
    o, _, _ = _gla(q, k, v, laf, lab, gsf, gsb, with_output=True)
    yvec, _ = _s5(uv, m_op, wt_op, v_op, ab, x0, b, with_output=True)

    rw_hi = router_w[i].astype(BF16)
    rw_lo = (router_w[i] - rw_hi.astype(F32)).astype(BF16)
    base, hrows, logits = _post(
        x, o, go, u, yvec, g1, sh2, sc2, g2,
        gla_norm_g[i][None, :], s5_d[i][None, :], s5_glu_w[i].astype(BF16),
        s5_glu_b[i][None, :], w_out[i].astype(BF16), norm2_g[i][None, :], rw_hi, rw_lo,
        jnp.concatenate([sh_w_gate[i], sh_w_up[i]], axis=1).astype(BF16), sh_w_down[i].astype(BF16))

    t = b * l
    tile = min(512, l)
    wts, lpos, runb, tcnt, cnt = _route(logits.reshape(t, N_EXPERTS), router_b[i][:, None], tile)
    cnt_flat = cnt[:, 0]
    runb_flat, tcnt_flat = runb[:, :, 0].reshape(-1), tcnt[:, :, 0].reshape(-1)
    nbp = -(-_n_blocks_max(t * TOP_K) // SUBLANES) * SUBLANES
    pstart, blk, nv = _plan(cnt_flat[None, :], nbp)
    lp_flat, w_flat = lpos.T.reshape(-1), wts.T.reshape(-1)
    ps_flat = pstart.reshape(-1)
    xs = _dispatch(lp_flat, tcnt_flat, runb_flat, ps_flat, cnt_flat, nv.reshape(-1), hrows, nbp, tile)
    assert exp_w_gate.shape[0] == 1, "single-layer block"
    ys = _experts(blk.reshape(-1), nv.reshape(-1), xs, exp_w_gate.reshape(exp_w_gate.shape[1:]),
                  exp_w_up.reshape(exp_w_up.shape[1:]), exp_w_down.reshape(exp_w_down.shape[1:]), nbp)
    return _combine(lp_flat, w_flat, tcnt_flat, runb_flat, ps_flat, ys, base, g2, final_norm_g[None, :], tile)
```

```python
import functools

import jax
import jax.numpy as jnp
from jax import lax
from jax.experimental import pallas as pl
from jax.experimental.pallas import tpu as pltpu

F32 = jnp.float32
BF16 = jnp.bfloat16
I32 = jnp.int32

D_MODEL = 1024
GLA_HEADS = 4
GLA_DK_HEAD = 64
GLA_DV_HEAD = 128
GLA_DK = 256
GLA_DV = 512
GLA_GATE_RANK = 16
GLA_GATE_TAU = 16.0
GLA_CHUNK = 64
D_S5 = 512
S5_GROUP_CH = 16
S5_GROUPS = 32
S5_STATE = 64
S5_CHUNK = 16
S5_VEC = S5_CHUNK * S5_GROUP_CH
N_EXPERTS = 256
TOP_K = 8
N_EXPERT_GROUPS = 8
TOPK_GROUPS = 4
D_EXPERT = 256
D_SHARED = 256
ROUTE_SCALE = 2.5
EPS = 1e-6

LANES = 128
SUBLANES = 8
ROW_TILES = D_MODEL // LANES
EXPERT_BLOCK = 256
VMEM_LIMIT = 56 * 1024 * 1024


def _cparams(sem):
    return pltpu.CompilerParams(dimension_semantics=sem, vmem_limit_bytes=VMEM_LIMIT)


def _dot(a, b):
    return jnp.dot(a, b, preferred_element_type=F32)


def _dot_nt(a, b):
    return lax.dot_general(a, b, (((1,), (1,)), ((), ())), preferred_element_type=F32)


def _dot_tn(a, b):
    return lax.dot_general(a, b, (((0,), (0,)), ((), ())), preferred_element_type=F32)


def _split2(x):
    hi = x.astype(BF16)
    lo = (x - hi.astype(F32)).astype(BF16)
    return hi, lo


def _dot3(a, b_hi, b_lo):
    a_hi, a_lo = _split2(a)
    return _dot(a_hi, b_hi) + (_dot(a_hi, b_lo) + _dot(a_lo, b_hi))


def _silu(x):
    return x * jax.nn.sigmoid(x)


def _rms(x, g):
    return x * lax.rsqrt(jnp.mean(x * x, axis=-1, keepdims=True) + EPS) * g


def _adaln_kernel(c_ref, w_ref, b_ref, o_ref):
    s = _silu(c_ref[...])
    w_hi, w_lo = _split2(w_ref[...])
    o_ref[...] = _dot3(s, w_hi, w_lo) + b_ref[...]


def _adaln(cs, w, b):
    rows, n = cs.shape[0], w.shape[1]
    tn = 1024
    return pl.pallas_call(
        _adaln_kernel,
        grid=(n // tn,),
        in_specs=[pl.BlockSpec((rows, D_MODEL), lambda j: (0, 0)),
                  pl.BlockSpec((D_MODEL, tn), lambda j: (0, j)),
                  pl.BlockSpec((1, tn), lambda j: (0, j))],
        out_specs=pl.BlockSpec((rows, tn), lambda j: (0, j)),
        out_shape=jax.ShapeDtypeStruct((rows, n), F32),
        compiler_params=_cparams(("arbitrary",)),
        name="adaln",
    )(cs, w, b)


def _group_lane_masks(rows):
    grp = lax.shift_right_logical(lax.broadcasted_iota(I32, (rows, LANES), 1), 4)
    return [grp == j for j in range(LANES // S5_GROUP_CH)]


def _move_group(x, src, dst):
    shift = ((dst - src) * S5_GROUP_CH) % LANES
    return pltpu.roll(x, shift, 1) if shift else x


def _proj_kernel(x_ref, sh_ref, sc_ref, g_ref, wm_ref, wl_ref, wa_ref, ba_ref,
                 q_ref, k_ref, v_ref, go_ref, u_ref, laf_ref, lab_ref, uv_ref, ut_ref, *, tm):
    h = _rms(x_ref[0], g_ref[...]) * (1.0 + sc_ref[0]) + sh_ref[0]
    hb = h.astype(BF16)
    q_ref[0] = _dot(hb, wm_ref[:, 0:256]) * (GLA_DK_HEAD ** -0.5)
    k_ref[0] = _dot(hb, wm_ref[:, 256:512])
    v_ref[0] = _dot(hb, wm_ref[:, 512:1024])
    go_ref[0] = _dot(hb, wm_ref[:, 1024:1536])
    u = _dot(hb, wm_ref[:, 1536:2048])
    u_ref[0] = u
    for t in range(D_S5 // LANES):
        ut_ref[t] = u[:, LANES * t:LANES * (t + 1)]
    lr = _dot(hb, wl_ref[...])
    pre = _dot(lr.astype(BF16), wa_ref[...]) + ba_ref[...]
    la = (jnp.minimum(pre, 0.0) - jnp.log1p(jnp.exp(-jnp.abs(pre)))) * (1.0 / GLA_GATE_TAU)
    laf_ref[0] = la[:, 0:GLA_DK]
    lab_ref[0] = la[:, GLA_DK:2 * GLA_DK]
    nc = tm // S5_CHUNK
    gpt = LANES // S5_GROUP_CH
    masks = _group_lane_masks(nc)
    for t in range(D_S5 // LANES):
        steps = [ut_ref[t, pl.ds(s, nc, stride=S5_CHUNK), :] for s in range(S5_CHUNK)]
        for gl in range(gpt):
            for half in range(S5_VEC // LANES):
                acc = None
                for j in range(gpt):
                    piece = _move_group(steps[half * gpt + j], gl, j)
                    acc = piece if acc is None else jnp.where(masks[j], piece, acc)
                uv_ref[t * gpt + gl, :, LANES * half:LANES * (half + 1)] = acc.astype(BF16)


def _proj(x, shift, scale, gain, wm, wl, wa, ba):
    b, l, _ = x.shape
    tm = min(512, l)
    nt = l // tm
    row = lambda bi, i: (bi, i, 0)
    mod = lambda bi, i: (bi, 0, 0)
    full = lambda bi, i: (0, 0)
    widths = (GLA_DK, GLA_DK, GLA_DV, GLA_DV, D_S5, GLA_DK, GLA_DK)
    return pl.pallas_call(
        functools.partial(_proj_kernel, tm=tm),
        grid=(b, nt),
        in_specs=[pl.BlockSpec((1, tm, D_MODEL), row),
                  pl.BlockSpec((1, 1, D_MODEL), mod),
                  pl.BlockSpec((1, 1, D_MODEL), mod),
                  pl.BlockSpec((1, D_MODEL), full),
                  pl.BlockSpec(wm.shape, full),
                  pl.BlockSpec(wl.shape, full),
                  pl.BlockSpec(wa.shape, full),
                  pl.BlockSpec(ba.shape, full)],
        out_specs=[pl.BlockSpec((1, tm, w), row) for w in widths]
                  + [pl.BlockSpec((S5_GROUPS, tm // S5_CHUNK, S5_VEC), lambda bi, i: (0, bi * nt + i, 0))],
        out_shape=[jax.ShapeDtypeStruct((b, l, w), F32) for w in widths]
                  + [jax.ShapeDtypeStruct((S5_GROUPS, b * l // S5_CHUNK, S5_VEC), BF16)],
        scratch_shapes=[pltpu.VMEM((D_S5 // LANES, tm, LANES), F32)],
        compiler_params=_cparams(("arbitrary", "arbitrary")),
        name="proj",
    )(x, shift, scale, gain, wm, wl, wa, ba)


def _gla_kernel(*refs, n_chunks, with_output):
    if with_output:
        q_ref, k_ref, v_ref, laf_ref, lab_ref, s0f_ref, s0b_ref, o_ref, sf_ref, sb_ref, st_ref, ob_ref = refs
    else:
        q_ref, k_ref, v_ref, laf_ref, lab_ref, s0f_ref, s0b_ref, sf_ref, sb_ref, st_ref = refs
        o_ref = ob_ref = None
    c = GLA_CHUNK
    row = lax.broadcasted_iota(I32, (c, c), 0)
    col = lax.broadcasted_iota(I32, (c, c), 1)
    lane = lax.broadcasted_iota(I32, (c, LANES), 1)
    st_ref[0] = s0f_ref[0]
    st_ref[1] = s0b_ref[0]

    def chunk(direction, idx):
        la_ref = laf_ref if direction == 0 else lab_ref
        tri = (row >= col) if direction == 0 else (row <= col)
        trib = jnp.where(tri, 1.0, 0.0).astype(BF16)
        r0 = pl.multiple_of(idx * c, c)
        q = q_ref[0, pl.ds(r0, c), :]
        k = k_ref[0, pl.ds(r0, c), :]
        v = v_ref[0, pl.ds(r0, c), :]
        la_hi, la_lo = _split2(la_ref[0, pl.ds(r0, c), :])
        cum = _dot(trib, la_hi) + _dot(trib, la_lo)
        tot = cum[c - 1:c, :] if direction == 0 else cum[0:1, :]
        qd = q * jnp.exp(cum)
        ki = k * jnp.exp(-cum)
        ks = k * jnp.exp(tot - cum)
        dec = jnp.exp(tot)
        for h in range(GLA_HEADS):
            pair = slice(LANES * (h // 2), LANES * (h // 2) + LANES)
            own = (lane >= GLA_DK_HEAD * (h % 2)) & (lane < GLA_DK_HEAD * (h % 2) + GLA_DK_HEAD)
            vb = v[:, GLA_DV_HEAD * h:GLA_DV_HEAD * (h + 1)].astype(BF16)
            st = st_ref[direction, h]
            if with_output:
                qb = qd[:, pair].astype(BF16)
                kib = jnp.where(own, ki[:, pair], 0.0).astype(BF16)
                sc = jnp.where(tri, _dot_nt(qb, kib), 0.0)
                o = _dot(sc.astype(BF16), vb) + _dot_nt(qb, st.astype(BF16))
                cols = slice(GLA_DV_HEAD * h, GLA_DV_HEAD * (h + 1))
                if direction == 0:
                    o_ref[0, pl.ds(r0, c), cols] = o
                else:
                    ob_ref[pl.ds(r0, c), cols] = o
            ksb = jnp.where(own, ks[:, pair], 0.0).astype(BF16)
            st_ref[direction, h] = st * dec[:, pair] + _dot_tn(vb, ksb)

    def body(ci, carry):
        chunk(0, ci)
        chunk(1, n_chunks - 1 - ci)
        return carry

    lax.fori_loop(0, n_chunks, body, 0, unroll=2)
    sf_ref[0] = st_ref[0]
    sb_ref[0] = st_ref[1]
    if with_output:
        o_ref[0] = o_ref[0] + ob_ref[...]


def _gla(q, k, v, laf, lab, s0f, s0b, with_output):
    b, l, _ = q.shape
    n_chunks = l // GLA_CHUNK
    seq = lambda bi: (bi, 0, 0)
    st = lambda bi: (bi, 0, 0, 0)
    st_shape = (b, GLA_HEADS, GLA_DV_HEAD, LANES)
    st_spec = pl.BlockSpec((1, GLA_HEADS, GLA_DV_HEAD, LANES), st)
    out_specs = [st_spec, st_spec]
    out_shape = [jax.ShapeDtypeStruct(st_shape, F32)] * 2
    if with_output:
        out_specs = [pl.BlockSpec((1, l, GLA_DV), seq)] + out_specs
        out_shape = [jax.ShapeDtypeStruct((b, l, GLA_DV), F32)] + out_shape
    return pl.pallas_call(
        functools.partial(_gla_kernel, n_chunks=n_chunks, with_output=with_output),
        grid=(b,),
        in_specs=[pl.BlockSpec((1, l, GLA_DK), seq),
                  pl.BlockSpec((1, l, GLA_DK), seq),
                  pl.BlockSpec((1, l, GLA_DV), seq),
                  pl.BlockSpec((1, l, GLA_DK), seq),
                  pl.BlockSpec((1, l, GLA_DK), seq),
                  st_spec, st_spec],
        out_specs=out_specs,
        out_shape=out_shape,
        scratch_shapes=[pltpu.VMEM((2, GLA_HEADS, GLA_DV_HEAD, LANES), F32)]
                       + ([pltpu.VMEM((l, GLA_DV), F32)] if with_output else []),
        compiler_params=_cparams(("arbitrary",)),
        name="gla_out" if with_output else "gla_ctx",
    )(q, k, v, laf, lab, s0f, s0b)


def _s5gen_kernel(pc_ref, pr_ref, btr_ref, bti_ref, ctrf_ref, ctif_ref, ctrb_ref, ctib_ref,
                  m_ref, wt_ref, v_ref, ab_ref):
    pc = pc_ref[0]
    blk = lax.shift_right_logical(lax.broadcasted_iota(I32, (1, S5_VEC), 1), 4).astype(F32)
    lane = lax.broadcasted_iota(I32, (S5_GROUP_CH, S5_VEC), 1)
    n = float(S5_CHUNK)

    def cmul(ar, ai, br, bi):
        return ar * br - ai * bi, ar * bi + ai * br

    kcat = []
    for d in (0, 1):
        lre, lim, ls = pc[:, 3 * d:3 * d + 1], pc[:, 3 * d + 1:3 * d + 2], pc[:, 3 * d + 2:3 * d + 3]
        ctr = (ctrf_ref if d == 0 else ctrb_ref)[0]
        cti = (ctif_ref if d == 0 else ctib_ref)[0]
        step = jnp.exp(ls)
        mag = jnp.exp(lre * step)
        a_re = mag * jnp.cos(lim * step)
        a_im = mag * jnp.sin(lim * step)
        den = lre * lre + lim * lim
        f_re = ((a_re - 1.0) * lre + a_im * lim) / den
        f_im = (a_im * lre - (a_re - 1.0) * lim) / den
        bb_re, bb_im = cmul(f_re, f_im, btr_ref[0], bti_ref[0])

        def powers(e, lre=lre, lim=lim, step=step):
            m = jnp.exp(lre * step * e)
            ang = lim * step * e
            return m * jnp.cos(ang), m * jnp.sin(ang)

        w_re, w_im = cmul(*powers((n - 1.0 - blk) if d == 0 else blk), bb_re, bb_im)
        wt_ref[0, S5_STATE * d:S5_STATE * (d + 1), :] = w_re
        wt_ref[0, 2 * S5_STATE + S5_STATE * d:2 * S5_STATE + S5_STATE * (d + 1), :] = w_im
        c_re, c_im = cmul(*powers((blk + 1.0) if d == 0 else (n - blk)), ctr, cti)
        v_ref[0, S5_STATE * d:S5_STATE * (d + 1), :] = c_re
        v_ref[0, 2 * S5_STATE + S5_STATE * d:2 * S5_STATE + S5_STATE * (d + 1), :] = -c_im
        e_re, e_im = cmul(*powers(blk if d == 0 else (n - 1.0 - blk)), ctr, cti)
        b16r_hi, b16r_lo = _split2(bb_re[:, 0:S5_GROUP_CH])
        b16i_hi, b16i_lo = _split2(bb_im[:, 0:S5_GROUP_CH])
        er_hi, er_lo = _split2(e_re)
        ei_hi, ei_lo = _split2(e_im)
        kr = _dot_tn(b16r_hi, er_hi) + (_dot_tn(b16r_hi, er_lo) + _dot_tn(b16r_lo, er_hi))
        ki = _dot_tn(b16i_hi, ei_hi) + (_dot_tn(b16i_hi, ei_lo) + _dot_tn(b16i_lo, ei_hi))
        kcat.append(kr - ki)

    for s in range(S5_CHUNK):
        sh_f = S5_GROUP_CH * s
        fwd = kcat[0] if s == 0 else pltpu.roll(kcat[0], sh_f, 1)
        fwd = jnp.where(lane >= sh_f, fwd, 0.0)
        sh_b = S5_VEC - S5_GROUP_CH * (S5_CHUNK - 1 - s)
        bwd = kcat[1] if sh_b == S5_VEC else pltpu.roll(kcat[1], sh_b, 1)
        bwd = jnp.where(lane < S5_GROUP_CH * (s + 1), bwd, 0.0)
        m_ref[0, S5_GROUP_CH * s:S5_GROUP_CH * (s + 1), :] = fwd + bwd

    pr = pr_ref[0]
    for d in (0, 1):
        lre, lim, ls = pr[3 * d:3 * d + 1, :], pr[3 * d + 1:3 * d + 2, :], pr[3 * d + 2:3 * d + 3, :]
        stp = jnp.exp(ls) * n
        mg = jnp.exp(lre * stp)
        ab_ref[0, d:d + 1, :] = mg * jnp.cos(lim * stp)
        ab_ref[0, 2 + d:3 + d, :] = mg * jnp.sin(lim * stp)


def _s5gen(pc, pr, btr, bti, ctrf, ctif, ctrb, ctib):
    g = pc.shape[0]
    blk3 = lambda shape: pl.BlockSpec((1,) + shape, lambda i: (i, 0, 0))
    big = (S5_STATE, S5_VEC)
    sq = (S5_VEC, S5_VEC)
    return pl.pallas_call(
        _s5gen_kernel,
        grid=(g,),
        in_specs=[blk3((S5_STATE, 8)), blk3((8, S5_STATE))] + [blk3(big)] * 6,
        out_specs=[blk3(sq), blk3(sq), blk3(sq), blk3((4, S5_STATE))],
        out_shape=[jax.ShapeDtypeStruct((g,) + sq, F32)] * 3 + [jax.ShapeDtypeStruct((g, 4, S5_STATE), F32)],
        compiler_params=_cparams(("arbitrary",)),
        name="s5gen",
    )(pc, pr, btr, bti, ctrf, ctif, ctrb, ctib)


def _s5_kernel(*refs, n_chunks, nb, with_output):
    if with_output:
        u_ref, m_ref, wt_ref, v_ref, ab_ref, x0_ref, y_ref, xf_ref, z_ref, cin_ref = refs
    else:
        u_ref, wt_ref, ab_ref, x0_ref, xf_ref, z_ref = refs
    wtb = wt_ref[0].astype(BF16)
    for bi in range(nb):
        z = _dot_nt(u_ref[0, bi * n_chunks:(bi + 1) * n_chunks, :], wtb)
        z_ref[0, pl.ds(bi, n_chunks, stride=nb), :] = z[:, 0:LANES]
        z_ref[1, pl.ds(bi, n_chunks, stride=nb), :] = z[:, LANES:2 * LANES]
    ab = ab_ref[0]
    ar, ai = ab[:, 0:LANES], ab[:, LANES:2 * LANES]
    is_f = lax.broadcasted_iota(I32, (nb, LANES), 1) < S5_STATE
    x0 = x0_ref[0]

    def body(i, carry):
        xr, xi = carry
        rf = pl.multiple_of(i * nb, nb)
        rb = pl.multiple_of((n_chunks - 1 - i) * nb, nb)
        if with_output:
            cin_ref[0, pl.ds(rf, nb), 0:S5_STATE] = xr[:, 0:S5_STATE]
            cin_ref[1, pl.ds(rf, nb), 0:S5_STATE] = xi[:, 0:S5_STATE]
            cin_ref[0, pl.ds(rb, nb), S5_STATE:LANES] = xr[:, S5_STATE:LANES]
            cin_ref[1, pl.ds(rb, nb), S5_STATE:LANES] = xi[:, S5_STATE:LANES]
        zr = jnp.where(is_f, z_ref[0, pl.ds(rf, nb), :], z_ref[0, pl.ds(rb, nb), :])
        zi = jnp.where(is_f, z_ref[1, pl.ds(rf, nb), :], z_ref[1, pl.ds(rb, nb), :])
        return ar * xr - ai * xi + zr, ar * xi + ai * xr + zi

    xr, xi = lax.fori_loop(0, n_chunks, body, (x0[:, 0:LANES], x0[:, LANES:2 * LANES]))
    xf_ref[0, :, 0:LANES] = xr
    xf_ref[0, :, LANES:2 * LANES] = xi
    if with_output:
        mb = m_ref[0].astype(BF16)
        vb = v_ref[0].astype(BF16)
        for bi in range(nb):
            rows = slice(bi * n_chunks, (bi + 1) * n_chunks)
            carried = jnp.concatenate([cin_ref[0, pl.ds(bi, n_chunks, stride=nb), :],
                                       cin_ref[1, pl.ds(bi, n_chunks, stride=nb), :]], axis=1).astype(BF16)
            y_ref[0, rows, :] = _dot(u_ref[0, rows, :], mb) + _dot(carried, vb)


def _s5(uvec, m, wt, v, ab, x0, nb, with_output):
    g, rows, _ = uvec.shape
    n_chunks = rows // nb
    blk3 = lambda shape: pl.BlockSpec((1,) + shape, lambda i: (i, 0, 0))
    sq = (S5_VEC, S5_VEC)
    st = (nb, S5_VEC)
    if with_output:
        args = (uvec, m, wt, v, ab, x0)
        in_specs = [blk3((rows, S5_VEC)), blk3(sq), blk3(sq), blk3(sq), blk3((1, S5_VEC)), blk3(st)]
        out_specs = [blk3((rows, S5_VEC)), blk3(st)]
        out_shape = [jax.ShapeDtypeStruct((g, rows, S5_VEC), F32), jax.ShapeDtypeStruct((g,) + st, F32)]
        scratch = [pltpu.VMEM((S5_VEC // LANES, rows, LANES), F32), pltpu.VMEM((S5_VEC // LANES, rows, LANES), F32)]
    else:
        args = (uvec, wt, ab, x0)
        in_specs = [blk3((rows, S5_VEC)), blk3(sq), blk3((1, S5_VEC)), blk3(st)]
        out_specs = [blk3(st)]
        out_shape = [jax.ShapeDtypeStruct((g,) + st, F32)]
        scratch = [pltpu.VMEM((S5_VEC // LANES, rows, LANES), F32)]
    return pl.pallas_call(
        functools.partial(_s5_kernel, n_chunks=n_chunks, nb=nb, with_output=with_output),
        grid=(g,),
        in_specs=in_specs,
        out_specs=out_specs,
        out_shape=out_shape,
        scratch_shapes=scratch,
        compiler_params=_cparams(("arbitrary",)),
        name="s5_out" if with_output else "s5_ctx",
    )(*args)


def _post_kernel(x_ref, o_ref, go_ref, u_ref, yv_ref, g1_ref, sh2_ref, sc2_ref, g2_ref,
                 gn_ref, d_ref, gw_ref, gb_ref, wo_ref, n2_ref, rwh_ref, rwl_ref, sgu_ref, sd_ref,
                 base_ref, hrow_ref, lg_ref, y_ref, *, tm):
    nc = tm // S5_CHUNK
    gpt = LANES // S5_GROUP_CH
    masks = _group_lane_masks(nc)
    for s in range(S5_CHUNK):
        half, j = divmod(s, gpt)
        for t in range(D_S5 // LANES):
            acc = None
            for gl in range(gpt):
                piece = _move_group(yv_ref[t * gpt + gl, :, LANES * half:LANES * (half + 1)], j, gl)
                acc = piece if acc is None else jnp.where(masks[gl], piece, acc)
            y_ref[t, pl.ds(s, nc, stride=S5_CHUNK), :] = acc
    o = o_ref[0]
    gn = gn_ref[...]
    heads = [_rms(o[:, GLA_DV_HEAD * h:GLA_DV_HEAD * (h + 1)], gn) for h in range(GLA_HEADS)]
    gla_out = jnp.concatenate(heads, axis=1) * _silu(go_ref[0])
    yy = jnp.concatenate([y_ref[t] for t in range(D_S5 // LANES)], axis=1) + d_ref[...] * u_ref[0]
    z = 0.5 * yy * (1.0 + jnp.tanh(0.7978845608028654 * (yy + 0.044715 * (yy * yy * yy))))
    s5_out = z * jax.nn.sigmoid(_dot(z.astype(BF16), gw_ref[...]) + gb_ref[...])
    mix = jnp.concatenate([gla_out, s5_out], axis=1).astype(BF16)
    x1 = x_ref[0] + g1_ref[0] * _dot(mix, wo_ref[...])
    h2 = _rms(x1, n2_ref[...]) * (1.0 + sc2_ref[0]) + sh2_ref[0]
    lg_ref[0] = _dot3(h2, rwh_ref[...], rwl_ref[...])
    hb = h2.astype(BF16)
    gu = _dot(hb, sgu_ref[...])
    hid = _silu(gu[:, 0:D_SHARED]) * gu[:, D_SHARED:2 * D_SHARED]
    base_ref[0] = x1 + g2_ref[0] * _dot(hid.astype(BF16), sd_ref[...])
    for s in range(ROW_TILES):
        hrow_ref[pl.ds(s, tm, stride=ROW_TILES), :] = h2[:, LANES * s:LANES * (s + 1)]


def _post(x, o, go, u, yvec, g1, sh2, sc2, g2, gn, d, gw, gb, wo, n2, rwh, rwl, sgu, sd):
    b, l, _ = x.shape
    tm = 256
    nt = l // tm
    row = lambda bi, i: (bi, i, 0)
    mod = lambda bi, i: (bi, 0, 0)
    full = lambda bi, i: (0, 0)
    ws = (gn, d, gw, gb, wo, n2, rwh, rwl, sgu, sd)
    return pl.pallas_call(
        functools.partial(_post_kernel, tm=tm),
        grid=(b, nt),
        in_specs=[pl.BlockSpec((1, tm, D_MODEL), row)]
                 + [pl.BlockSpec((1, tm, 512), row)] * 3
                 + [pl.BlockSpec((S5_GROUPS, tm // S5_CHUNK, S5_VEC), lambda bi, i: (0, bi * nt + i, 0))]
                 + [pl.BlockSpec((1, 1, D_MODEL), mod)] * 4
                 + [pl.BlockSpec(w.shape, full) for w in ws],
        out_specs=[pl.BlockSpec((1, tm, D_MODEL), row),
                   pl.BlockSpec((tm * ROW_TILES, LANES), lambda bi, i: (bi * nt + i, 0)),
                   pl.BlockSpec((1, tm, N_EXPERTS), row)],
        out_shape=[jax.ShapeDtypeStruct((b, l, D_MODEL), F32),
                   jax.ShapeDtypeStruct((b * l * ROW_TILES, LANES), F32),
                   jax.ShapeDtypeStruct((b, l, N_EXPERTS), F32)],
        scratch_shapes=[pltpu.VMEM((D_S5 // LANES, tm, LANES), F32)],
        compiler_params=_cparams(("arbitrary", "arbitrary")),
        name="post",
    )(x, o, go, u, yvec, g1, sh2, sc2, g2, *ws)


def _route_kernel(lg_ref, rb_ref, w_ref, p_ref, rb4_ref, tc_ref, cnt_ref, run_ref, *, tm):
    @pl.when(pl.program_id(0) == 0)
    def _():
        run_ref[...] = jnp.zeros_like(run_ref)

    neg = -jnp.inf
    gsz = N_EXPERTS // N_EXPERT_GROUPS
    s = jax.nn.sigmoid(lg_ref[...].T)
    biased = s + rb_ref[...]
    row = lax.broadcasted_iota(I32, (N_EXPERTS, tm), 0).astype(F32)

    def first_max(m, idx):
        mx = jnp.max(m, axis=0, keepdims=True)
        ix = jnp.min(jnp.where(m == mx, idx, float(N_EXPERTS)), axis=0, keepdims=True)
        return mx, ix

    grow = lax.broadcasted_iota(I32, (gsz, tm), 0).astype(F32)
    gs = []
    for g in range(N_EXPERT_GROUPS):
        m, idx = biased[gsz * g:gsz * (g + 1), :], grow + float(gsz * g)
        m1, i1 = first_max(m, idx)
        gs.append(m1 + jnp.max(jnp.where(idx == i1, neg, m), axis=0, keepdims=True))
    kept = []
    for g in range(N_EXPERT_GROUPS):
        ahead = jnp.zeros((1, tm), F32)
        for j in range(N_EXPERT_GROUPS):
            if j < g:
                ahead = ahead + jnp.where(gs[j] >= gs[g], 1.0, 0.0)
            elif j > g:
                ahead = ahead + jnp.where(gs[j] > gs[g], 1.0, 0.0)
        kept.append(jnp.where(ahead < float(TOPK_GROUPS), biased[gsz * g:gsz * (g + 1), :], neg))
    masked = jnp.concatenate(kept, axis=0)

    onehot = jnp.zeros((N_EXPERTS, tm), F32)
    ids, ws = [], []
    for _ in range(TOP_K):
        _, ik = first_max(masked, row)
        hit = row == ik
        ids.append(ik)
        ws.append(jnp.sum(jnp.where(hit, s, 0.0), axis=0, keepdims=True))
        onehot = onehot + jnp.where(hit, 1.0, 0.0)
        masked = jnp.where(hit, neg, masked)
    wsum = ws[0]
    for k in range(1, TOP_K):
        wsum = wsum + ws[k]

    ss = lax.broadcasted_iota(I32, (tm, tm), 0)
    tt = lax.broadcasted_iota(I32, (tm, tm), 1)
    earlier = jnp.where(ss < tt, 1.0, 0.0).astype(BF16)
    ohb = onehot.astype(BF16)
    tcnt = _dot(ohb, jnp.ones((tm, LANES), BF16))
    ee = lax.broadcasted_iota(I32, (N_EXPERTS, N_EXPERTS), 0)
    ff = lax.broadcasted_iota(I32, (N_EXPERTS, N_EXPERTS), 1)
    below = jnp.where(ff < ee, 1.0, 0.0).astype(BF16)
    t_hi, t_lo = _split2(tcnt)
    toff = _dot(below, t_hi) + _dot(below, t_lo)
    lpos = (_dot(ohb, earlier) + toff[:, 0:1]) * float(ROW_TILES)
    w_ref[...] = jnp.concatenate([w / wsum * ROUTE_SCALE for w in ws], axis=0)
    p_ref[...] = jnp.concatenate(
        [jnp.sum(jnp.where(row == ids[k], lpos, 0.0), axis=0, keepdims=True) for k in range(TOP_K)], axis=0).astype(I32)
    run = run_ref[...]
    rb4_ref[0] = run.astype(I32)
    tc_ref[0] = tcnt.astype(I32)
    run = run + tcnt
    run_ref[...] = run
    cnt_ref[...] = run.astype(I32)


def _route(logits, rb, tm):
    t = logits.shape[0]
    col = lambda i: (0, i)
    fixed = lambda i: (0, 0)
    tile = lambda i: (i, 0, 0)
    per_tile = jax.ShapeDtypeStruct((t // tm, N_EXPERTS, LANES), I32)
    return pl.pallas_call(
        functools.partial(_route_kernel, tm=tm),
        grid=(t // tm,),
        in_specs=[pl.BlockSpec((tm, N_EXPERTS), lambda i: (i, 0)), pl.BlockSpec((N_EXPERTS, 1), fixed)],
        out_specs=[pl.BlockSpec((TOP_K, tm), col)] * 2 + [pl.BlockSpec((1, N_EXPERTS, LANES), tile)] * 2
                  + [pl.BlockSpec((N_EXPERTS, LANES), fixed)],
        out_shape=[jax.ShapeDtypeStruct((TOP_K, t), F32), jax.ShapeDtypeStruct((TOP_K, t), I32), per_tile, per_tile,
                   jax.ShapeDtypeStruct((N_EXPERTS, LANES), I32)],
        scratch_shapes=[pltpu.VMEM((N_EXPERTS, LANES), F32)],
        compiler_params=_cparams(("arbitrary",)),
        name="route",
    )(logits, rb)


def _n_blocks_max(n_assign):
    return -(-(n_assign + N_EXPERTS * (EXPERT_BLOCK - 1)) // EXPERT_BLOCK)


def _plan_kernel(cnt_ref, ps_ref, blk_ref, nv_ref, *, nbp):
    cnt = cnt_ref[...]
    nb = lax.shift_right_logical(cnt + (EXPERT_BLOCK - 1), 8).astype(F32)
    nb8 = jnp.broadcast_to(nb, (SUBLANES, N_EXPERTS))
    nb_hi, nb_lo = _split2(nb8)
    ii = lax.broadcasted_iota(I32, (N_EXPERTS, N_EXPERTS), 0)
    jj = lax.broadcasted_iota(I32, (N_EXPERTS, N_EXPERTS), 1)
    upto = jnp.where(ii <= jj, 1.0, 0.0).astype(BF16)
    cum = (_dot(nb_hi, upto) + _dot(nb_lo, upto))[0:1, :]
    ps_ref[...] = ((cum - nb) * float(EXPERT_BLOCK)).astype(I32)
    bi = lax.broadcasted_iota(I32, (nbp, N_EXPERTS), 0).astype(F32)
    owner = jnp.sum(jnp.where(cum <= bi, 1.0, 0.0), axis=-1, keepdims=True)
    blk_ref[...] = jnp.minimum(owner, float(N_EXPERTS - 1)).astype(I32)
    nv_ref[...] = cum[:, N_EXPERTS - 1:N_EXPERTS].astype(I32)


def _plan(cnt, nbp):
    return pl.pallas_call(
        functools.partial(_plan_kernel, nbp=nbp),
        out_shape=[jax.ShapeDtypeStruct((1, N_EXPERTS), I32), jax.ShapeDtypeStruct((nbp, 1), I32),
                   jax.ShapeDtypeStruct((1, 1), I32)],
        name="plan",
    )(cnt)


def _segment_copies(tc_ref, rb_ref, ps_ref, make):
    def body(pair, local):
        for j in range(2):
            e = 2 * pair + j
            cnt = tc_ref[e]

            @pl.when(cnt > 0)
            def _(e=e, local=local, cnt=cnt, j=j):
                make(local, ps_ref[e] + rb_ref[e], cnt).start(priority=j)

            local = local + cnt
        return local

    lax.fori_loop(0, N_EXPERTS // 2, body, 0)


def _rows(ref, row, n):
    return ref.at[pl.ds(pl.multiple_of(row * ROW_TILES, ROW_TILES), n * ROW_TILES)]


def _dispatch_kernel(lp_ref, tc_ref, rb_ref, ps_ref, cnt_ref, nv_ref, h_ref, xs_ref, sorted_ref, zero_ref, sems, zsem,
                     *, tm, nbp):
    step = pl.program_id(0)

    @pl.when(step == 0)
    def _():
        zero_ref[...] = jnp.zeros_like(zero_ref)
        for wait in (False, True):
            def tail_body(bi, carry, wait=wait):
                cp = pltpu.make_async_copy(zero_ref, _rows(xs_ref, bi * EXPERT_BLOCK, EXPERT_BLOCK), zsem)
                cp.wait() if wait else cp.start()
                return carry
            lax.fori_loop(nv_ref[0], nbp, tail_body, 0)

            def pad_body(e, carry, wait=wait):
                cnt = cnt_ref[e]
                pad = (-cnt) & (EXPERT_BLOCK - 1)

                @pl.when(pad > 0)
                def _():
                    cp = pltpu.make_async_copy(_rows(zero_ref, 0, pad), _rows(xs_ref, ps_ref[e] + cnt, pad), zsem)
                    cp.wait() if wait else cp.start()

                return carry
            lax.fori_loop(0, N_EXPERTS, pad_body, 0)

    slot = step % 2
    mine = sorted_ref.at[slot]

    def token_body(t, carry):
        row = h_ref[pl.ds(pl.multiple_of(t * ROW_TILES, ROW_TILES), ROW_TILES), :]
        for k in range(TOP_K):
            mine[pl.ds(pl.multiple_of(lp_ref[t * TOP_K + k], ROW_TILES), ROW_TILES), :] = row
        return carry

    lax.fori_loop(0, tm, token_body, 0, unroll=2)
    _segment_copies(tc_ref, rb_ref, ps_ref,
                    lambda loc, glob, n: pltpu.make_async_copy(_rows(mine, loc, n), _rows(xs_ref, glob, n), sems.at[slot]))

    def drain(s):
        pltpu.make_async_copy(sorted_ref.at[s], _rows(xs_ref, 0, tm * TOP_K), sems.at[s]).wait()

    @pl.when(step > 0)
    def _():
        drain(1 - slot)

    @pl.when(step == pl.num_programs(0) - 1)
    def _():
        drain(slot)


def _dispatch(lp_flat, tcnt, runb, pstart, cnt, nv, hrows, nbp, tm):
    t = lp_flat.shape[0] // TOP_K
    per_tile = pl.BlockSpec((N_EXPERTS,), lambda i: (i,), memory_space=pltpu.SMEM)
    smem_all = pl.BlockSpec((N_EXPERTS,), lambda i: (0,), memory_space=pltpu.SMEM)
    return pl.pallas_call(
        functools.partial(_dispatch_kernel, tm=tm, nbp=nbp),
        grid=(t // tm,),
        in_specs=[pl.BlockSpec((tm * TOP_K,), lambda i: (i,), memory_space=pltpu.SMEM),
                  per_tile, per_tile, smem_all, smem_all,
                  pl.BlockSpec((1,), lambda i: (0,), memory_space=pltpu.SMEM),
                  pl.BlockSpec((tm * ROW_TILES, LANES), lambda i: (i, 0))],
        out_specs=pl.BlockSpec(memory_space=pl.ANY),
        out_shape=jax.ShapeDtypeStruct((nbp * EXPERT_BLOCK * ROW_TILES, LANES), F32),
        scratch_shapes=[pltpu.VMEM((2, tm * TOP_K * ROW_TILES, LANES), F32),
                        pltpu.VMEM((EXPERT_BLOCK * ROW_TILES, LANES), F32),
                        pltpu.SemaphoreType.DMA((2,)), pltpu.SemaphoreType.DMA],
        compiler_params=_cparams(("arbitrary",)),
        name="dispatch",
    )(lp_flat, tcnt, runb, pstart, cnt, nv, hrows)


def _experts_kernel(blk_ref, nv_ref, xs_hbm, wg_hbm, wu_hbm, wd_hbm, ys_hbm, step_ref, wgb_ref, wub_ref, wdb_ref, *, nbp):
    rows = EXPERT_BLOCK * ROW_TILES
    nv = nv_ref[0]
    cur = lambda i: jnp.minimum(i, nv - 1)
    xmap = lambda i: (cur(i), 0)
    wmap = lambda i: (blk_ref[cur(i)], 0, 0)
    step_ref[0] = 0

    def body(xs_ref, wg_ref, wu_ref, wd_ref, ys_ref):
        i = step_ref[0]
        step_ref[0] = i + 1

        @pl.when(i >= nv)
        def _():
            ys_ref[...] = jnp.zeros_like(ys_ref)

        @pl.when(i < nv)
        def _():
            prev = blk_ref[jnp.maximum(i - 1, 0)]

            @pl.when((i == 0) | (blk_ref[i] != prev))
            def _():
                wgb_ref[...] = wg_ref[0].astype(BF16)
                wub_ref[...] = wu_ref[0].astype(BF16)
                wdb_ref[...] = wd_ref[0].astype(BF16)

            xb = jnp.concatenate(
                [xs_ref[pl.ds(s, EXPERT_BLOCK, stride=ROW_TILES), :] for s in range(ROW_TILES)], axis=1).astype(BF16)
            hid = _silu(_dot(xb, wgb_ref[...])) * _dot(xb, wub_ref[...])
            y = _dot(hid.astype(BF16), wdb_ref[...])
            for s in range(ROW_TILES):
                ys_ref[pl.ds(s, EXPERT_BLOCK, stride=ROW_TILES), :] = y[:, LANES * s:LANES * (s + 1)]

    ahead = pl.Buffered(2, use_lookahead=True)
    pltpu.emit_pipeline(
        body,
        grid=(nbp,),
        in_specs=[pl.BlockSpec((rows, LANES), xmap, pipeline_mode=pl.Buffered(3)),
                  pl.BlockSpec((1, D_MODEL, D_EXPERT), wmap, pipeline_mode=ahead),
                  pl.BlockSpec((1, D_MODEL, D_EXPERT), wmap, pipeline_mode=ahead),
                  pl.BlockSpec((1, D_EXPERT, D_MODEL), wmap, pipeline_mode=ahead)],
        out_specs=[pl.BlockSpec((rows, LANES), lambda i: (i, 0))],
    )(xs_hbm, wg_hbm, wu_hbm, wd_hbm, ys_hbm)


def _experts(blk, nv, xs, wg, wu, wd, nbp):
    smem = pl.BlockSpec(memory_space=pltpu.SMEM)
    hbm = pl.BlockSpec(memory_space=pl.ANY)
    return pl.pallas_call(
        functools.partial(_experts_kernel, nbp=nbp),
        in_specs=[smem, smem, hbm, hbm, hbm, hbm],
        out_specs=hbm,
        out_shape=jax.ShapeDtypeStruct(xs.shape, F32),
        scratch_shapes=[pltpu.SMEM((1,), I32),
                        pltpu.VMEM((D_MODEL, D_EXPERT), BF16), pltpu.VMEM((D_MODEL, D_EXPERT), BF16),
                        pltpu.VMEM((D_EXPERT, D_MODEL), BF16)],
        compiler_params=pltpu.CompilerParams(vmem_limit_bytes=VMEM_LIMIT),
        name="experts",
    )(blk, nv, xs, wg, wu, wd)


def _combine_kernel(lp_ref, w_ref, tc_ref, rb_ref, tcn_ref, rbn_ref, ps_ref, ys_ref, base_ref, g2_ref, fg_ref, out_ref,
                    buf_ref, acc_ref, sems, *, tm):
    step = pl.program_id(0) * pl.num_programs(1) + pl.program_id(1)
    last = pl.num_programs(0) * pl.num_programs(1) - 1
    slot = step % 2

    def fetch(tcnt_ref, runb_ref, s):
        _segment_copies(tcnt_ref, runb_ref, ps_ref,
                        lambda loc, glob, n: pltpu.make_async_copy(_rows(ys_ref, glob, n), _rows(buf_ref.at[s], loc, n),
                                                                   sems.at[s]))

    @pl.when(step == 0)
    def _():
        fetch(tc_ref, rb_ref, slot)

    @pl.when(step < last)
    def _():
        fetch(tcn_ref, rbn_ref, 1 - slot)

    mine = buf_ref.at[slot]
    pltpu.make_async_copy(_rows(ys_ref, 0, tm * TOP_K), mine, sems.at[slot]).wait()

    def token_body(t, carry):
        j0 = t * TOP_K
        acc = jnp.zeros((ROW_TILES, LANES), F32)
        for k in range(TOP_K):
            acc = acc + w_ref[j0 + k] * mine[pl.ds(pl.multiple_of(lp_ref[j0 + k], ROW_TILES), ROW_TILES), :]
        acc_ref[pl.ds(pl.multiple_of(t * ROW_TILES, ROW_TILES), ROW_TILES), :] = acc
        return carry

    lax.fori_loop(0, tm, token_body, 0, unroll=2)
    routed = jnp.concatenate([acc_ref[pl.ds(s, tm, stride=ROW_TILES), :] for s in range(ROW_TILES)], axis=1)
    out_ref[0] = _rms(base_ref[0] + g2_ref[0] * routed, fg_ref[...])


def _combine(lp_flat, w_flat, tcnt, runb, pstart, ys, base, g2, fg, tm):
    b, l, _ = base.shape
    nt = l // tm
    flat = lambda bi, i: (bi * nt + i,)
    following = lambda bi, i: (jnp.minimum(bi * nt + i + 1, b * nt - 1),)
    smem_blk = pl.BlockSpec((tm * TOP_K,), flat, memory_space=pltpu.SMEM)
    per_tile = pl.BlockSpec((N_EXPERTS,), flat, memory_space=pltpu.SMEM)
    next_tile = pl.BlockSpec((N_EXPERTS,), following, memory_space=pltpu.SMEM)
    return pl.pallas_call(
        functools.partial(_combine_kernel, tm=tm),
        grid=(b, nt),
        in_specs=[smem_blk, smem_blk, per_tile, per_tile, next_tile, next_tile,
                  pl.BlockSpec((N_EXPERTS,), lambda bi, i: (0,), memory_space=pltpu.SMEM),
                  pl.BlockSpec(memory_space=pl.ANY),
                  pl.BlockSpec((1, tm, D_MODEL), lambda bi, i: (bi, i, 0)),
                  pl.BlockSpec((1, 1, D_MODEL), lambda bi, i: (bi, 0, 0)),
                  pl.BlockSpec((1, D_MODEL), lambda bi, i: (0, 0))],
        out_specs=pl.BlockSpec((1, tm, D_MODEL), lambda bi, i: (bi, i, 0)),
        out_shape=jax.ShapeDtypeStruct((b, l, D_MODEL), F32),
        scratch_shapes=[pltpu.VMEM((2, tm * TOP_K * ROW_TILES, LANES), F32),
                        pltpu.VMEM((tm * ROW_TILES, LANES), F32),
                        pltpu.SemaphoreType.DMA((2,))],
        compiler_params=_cparams(("arbitrary", "arbitrary")),
        name="combine",
    )(lp_flat, w_flat, tcnt, runb, tcnt, runb, pstart, ys, base, g2, fg)


def _mixer_inputs(h, shift, scale, gain, wm, wl, wa, ba):
    return _proj(h, shift, scale, gain, wm, wl, wa, ba)


def kernel(x, c, ctx, c_ctx, ada_w, ada_b, norm1_g, norm2_g, w_in, gla_wa_f, gla_ba_f, gla_wa_b, gla_ba_b, gla_norm_g, s5_lam_re_f, s5_lam_im_f, s5_log_step_f, s5_lam_re_b, s5_lam_im_b, s5_log_step_b, s5_b_re, s5_b_im, s5_c_re_f, s5_c_im_f, s5_c_re_b, s5_c_im_b, s5_d, s5_glu_w, s5_glu_b, w_out, router_w, router_b, exp_w_gate, exp_w_up, exp_w_down, sh_w_gate, sh_w_up, sh_w_down, final_norm_g):
    b, l, d = x.shape
    i = 0

    rows = -(-(b + 1) // SUBLANES) * SUBLANES
    cs = jnp.zeros((rows, d), F32).at[:b].set(c).at[b].set(c_ctx)
    mod = _adaln(cs, ada_w[i], ada_b[i][None, :])
    sh1, sc1, g1, sh2, sc2, g2 = [mod[:b, d * j:d * (j + 1)][:, None, :] for j in range(6)]
    csh1, csc1 = [jnp.broadcast_to(mod[b, d * j:d * (j + 1)][None, None, :], (b, 1, d)) for j in range(2)]

    w = w_in[i]
    o1, o2, o3, o4, o5, o6 = 256, 512, 1024, 1536, 1552, 1568
    wm = jnp.concatenate([w[:, :o4], w[:, o6:]], axis=1).astype(BF16)
    wl = jnp.zeros((d, LANES), F32).at[:, :2 * GLA_GATE_RANK].set(w[:, o4:o6]).astype(BF16)
    wa = jnp.zeros((LANES, 2 * GLA_DK), F32)
    wa = wa.at[:GLA_GATE_RANK, :GLA_DK].set(gla_wa_f[i]).at[GLA_GATE_RANK:2 * GLA_GATE_RANK, GLA_DK:].set(gla_wa_b[i])
    wa = wa.astype(BF16)
    ba = jnp.concatenate([gla_ba_f[i], gla_ba_b[i]])[None, :]
    n1 = norm1_g[i][None, :]

    pcols = jnp.stack([s5_lam_re_f[i], s5_lam_im_f[i],
                       jnp.broadcast_to(s5_log_step_f[i][:, None], (S5_GROUPS, S5_STATE)),
                       s5_lam_re_b[i], s5_lam_im_b[i],
                       jnp.broadcast_to(s5_log_step_b[i][:, None], (S5_GROUPS, S5_STATE)),
                       jnp.zeros((S5_GROUPS, S5_STATE), F32), jnp.zeros((S5_GROUPS, S5_STATE), F32)], axis=-1)
    prows = pcols.transpose(0, 2, 1)
    tile_b = lambda t: jnp.tile(t, (1, 1, S5_CHUNK))
    tile_c = lambda t: jnp.tile(t.transpose(0, 2, 1), (1, 1, S5_CHUNK))
    m_op, wt_op, v_op, ab4 = _s5gen(pcols, prows, tile_b(s5_b_re[i]), tile_b(s5_b_im[i]),
                                    tile_c(s5_c_re_f[i]), tile_c(s5_c_im_f[i]),
                                    tile_c(s5_c_re_b[i]), tile_c(s5_c_im_b[i]))
    ab = ab4.reshape(S5_GROUPS, 1, 4 * S5_STATE)

    cq, ck, cv, _, _, claf, clab, cuv = _proj(ctx, csh1, csc1, n1, wm, wl, wa, ba)
    zero_state = jnp.zeros((b, GLA_HEADS, GLA_DV_HEAD, LANES), F32)
    gsf, gsb = _gla(cq, ck, cv, claf, clab, zero_state, zero_state, with_output=False)
    (x0,) = _s5(cuv, None, wt_op, None, ab, jnp.zeros((S5_GROUPS, b, S5_VEC), F32), b, with_output=False)

    q, k, v, go, u, laf, lab, uv = _proj(x, sh1, sc1, n1, wm, wl, wa, ba)
    o, _, _ = _gla(q, k, v, laf, lab, gsf, gsb, with_output=True)
    yvec, _ = _s5(uv, m_op, wt_op, v_op, ab, x0, b, with_output=True)

    rw_hi = router_w[i].astype(BF16)
    rw_lo = (router_w[i] - rw_hi.astype(F32)).astype(BF16)
    base, hrows, logits = _post(
        x, o, go, u, yvec, g1, sh2, sc2, g2,
        gla_norm_g[i][None, :], s5_d[i][None, :], s5_glu_w[i].astype(BF16),
        s5_glu_b[i][None, :], w_out[i].astype(BF16), norm2_g[i][None, :], rw_hi, rw_lo,
        jnp.concatenate([sh_w_gate[i], sh_w_up[i]], axis=1).astype(BF16), sh_w_down[i].astype(BF16))

    t = b * l
    tile = min(512, l)
    wts, lpos, runb, tcnt, cnt = _route(logits.reshape(t, N_EXPERTS), router_b[i][:, None], tile)
    cnt_flat = cnt[:, 0]
    runb_flat, tcnt_flat = runb[:, :, 0].reshape(-1), tcnt[:, :, 0].reshape(-1)
    nbp = -(-_n_blocks_max(t * TOP_K) // SUBLANES) * SUBLANES
    pstart, blk, nv = _plan(cnt_flat[None, :], nbp)
    lp_flat, w_flat = lpos.T.reshape(-1), wts.T.reshape(-1)
    ps_flat = pstart.reshape(-1)
    xs = _dispatch(lp_flat, tcnt_flat, runb_flat, ps_flat, cnt_flat, nv.reshape(-1), hrows, nbp, tile)
    assert exp_w_gate.shape[0] == 1, "single-layer block"
    ys = _experts(blk.reshape(-1), nv.reshape(-1), xs, exp_w_gate.reshape(exp_w_gate.shape[1:]),
                  exp_w_up.reshape(exp_w_up.shape[1:]), exp_w_down.reshape(exp_w_down.shape[1:]), nbp)
    return _combine(lp_flat, w_flat, tcnt_flat, runb_flat, ps_flat, ys, base, g2, final_norm_g[None, :], tile)
```

```python
import functools

import jax
import jax.numpy as jnp
from jax import lax
from jax.experimental import pallas as pl
from jax.experimental.pallas import tpu as pltpu

F32 = jnp.float32
BF16 = jnp.bfloat16
I32 = jnp.int32

D_MODEL = 1024
GLA_HEADS = 4
GLA_DK_HEAD = 64
GLA_DV_HEAD = 128
GLA_DK = 256
GLA_DV = 512
GLA_GATE_RANK = 16
GLA_GATE_TAU = 16.0
GLA_CHUNK = 64
D_S5 = 512
S5_GROUP_CH = 16
S5_GROUPS = 32
S5_STATE = 64
S5_CHUNK = 16
S5_VEC = S5_CHUNK * S5_GROUP_CH
N_EXPERTS = 256
TOP_K = 8
N_EXPERT_GROUPS = 8
TOPK_GROUPS = 4
D_EXPERT = 256
D_SHARED = 256
ROUTE_SCALE = 2.5
EPS = 1e-6

LANES = 128
SUBLANES = 8
ROW_TILES = D_MODEL // LANES
PACK_TILES = ROW_TILES // 2
U32 = jnp.uint32
EXPERT_BLOCK = 256
VMEM_LIMIT = 56 * 1024 * 1024


def _cparams(sem):
    return pltpu.CompilerParams(dimension_semantics=sem, vmem_limit_bytes=VMEM_LIMIT)


def _dot(a, b):
    return jnp.dot(a, b, preferred_element_type=F32)


def _dot_nt(a, b):
    return lax.dot_general(a, b, (((1,), (1,)), ((), ())), preferred_element_type=F32)


def _dot_tn(a, b):
    return lax.dot_general(a, b, (((0,), (0,)), ((), ())), preferred_element_type=F32)


def _split2(x):
    hi = x.astype(BF16)
    lo = (x - hi.astype(F32)).astype(BF16)
    return hi, lo


def _dot3(a, b_hi, b_lo):
    a_hi, a_lo = _split2(a)
    return _dot(a_hi, b_hi) + (_dot(a_hi, b_lo) + _dot(a_lo, b_hi))


def _silu(x):
    return x * jax.nn.sigmoid(x)


def _rms(x, g):
    return x * lax.rsqrt(jnp.mean(x * x, axis=-1, keepdims=True) + EPS) * g


def _adaln_kernel(c_ref, w_ref, b_ref, o_ref):
    s = _silu(c_ref[...])
    w_hi, w_lo = _split2(w_ref[...])
    o_ref[...] = _dot3(s, w_hi, w_lo) + b_ref[...]


def _adaln(cs, w, b):
    rows, n = cs.shape[0], w.shape[1]
    tn = 1024
    return pl.pallas_call(
        _adaln_kernel,
        grid=(n // tn,),
        in_specs=[pl.BlockSpec((rows, D_MODEL), lambda j: (0, 0)),
                  pl.BlockSpec((D_MODEL, tn), lambda j: (0, j)),
                  pl.BlockSpec((1, tn), lambda j: (0, j))],
        out_specs=pl.BlockSpec((rows, tn), lambda j: (0, j)),
        out_shape=jax.ShapeDtypeStruct((rows, n), F32),
        compiler_params=_cparams(("arbitrary",)),
        name="adaln",
    )(cs, w, b)


def _group_lane_masks(rows):
    grp = lax.shift_right_logical(lax.broadcasted_iota(I32, (rows, LANES), 1), 4)
    return [grp == j for j in range(LANES // S5_GROUP_CH)]


def _move_group(x, src, dst):
    shift = ((dst - src) * S5_GROUP_CH) % LANES
    return pltpu.roll(x, shift, 1) if shift else x


def _proj_kernel(x_ref, sh_ref, sc_ref, g_ref, wm_ref, wl_ref, wa_ref, ba_ref,
                 q_ref, k_ref, v_ref, go_ref, u_ref, laf_ref, lab_ref, uv_ref, ut_ref, *, tm):
    h = _rms(x_ref[0], g_ref[...]) * (1.0 + sc_ref[0]) + sh_ref[0]
    hb = h.astype(BF16)
    q_ref[0] = _dot(hb, wm_ref[:, 0:256]) * (GLA_DK_HEAD ** -0.5)
    k_ref[0] = _dot(hb, wm_ref[:, 256:512])
    v_ref[0] = _dot(hb, wm_ref[:, 512:1024])
    go_ref[0] = _dot(hb, wm_ref[:, 1024:1536])
    u = _dot(hb, wm_ref[:, 1536:2048])
    u_ref[0] = u
    for t in range(D_S5 // LANES):
        ut_ref[t] = u[:, LANES * t:LANES * (t + 1)]
    lr = _dot(hb, wl_ref[...])
    pre = _dot(lr.astype(BF16), wa_ref[...]) + ba_ref[...]
    la = (jnp.minimum(pre, 0.0) - jnp.log1p(jnp.exp(-jnp.abs(pre)))) * (1.0 / GLA_GATE_TAU)
    laf_ref[0] = la[:, 0:GLA_DK]
    lab_ref[0] = la[:, GLA_DK:2 * GLA_DK]
    nc = tm // S5_CHUNK
    gpt = LANES // S5_GROUP_CH
    masks = _group_lane_masks(nc)
    for t in range(D_S5 // LANES):
        steps = [ut_ref[t, pl.ds(s, nc, stride=S5_CHUNK), :] for s in range(S5_CHUNK)]
        for gl in range(gpt):
            for half in range(S5_VEC // LANES):
                acc = None
                for j in range(gpt):
                    piece = _move_group(steps[half * gpt + j], gl, j)
                    acc = piece if acc is None else jnp.where(masks[j], piece, acc)
                uv_ref[t * gpt + gl, :, LANES * half:LANES * (half + 1)] = acc.astype(BF16)


def _proj(x, shift, scale, gain, wm, wl, wa, ba):
    b, l, _ = x.shape
    tm = min(512, l)
    nt = l // tm
    row = lambda bi, i: (bi, i, 0)
    mod = lambda bi, i: (bi, 0, 0)
    full = lambda bi, i: (0, 0)
    widths = (GLA_DK, GLA_DK, GLA_DV, GLA_DV, D_S5, GLA_DK, GLA_DK)
    return pl.pallas_call(
        functools.partial(_proj_kernel, tm=tm),
        grid=(b, nt),
        in_specs=[pl.BlockSpec((1, tm, D_MODEL), row),
                  pl.BlockSpec((1, 1, D_MODEL), mod),
                  pl.BlockSpec((1, 1, D_MODEL), mod),
                  pl.BlockSpec((1, D_MODEL), full),
                  pl.BlockSpec(wm.shape, full),
                  pl.BlockSpec(wl.shape, full),
                  pl.BlockSpec(wa.shape, full),
                  pl.BlockSpec(ba.shape, full)],
        out_specs=[pl.BlockSpec((1, tm, w), row) for w in widths]
                  + [pl.BlockSpec((S5_GROUPS, tm // S5_CHUNK, S5_VEC), lambda bi, i: (0, bi * nt + i, 0))],
        out_shape=[jax.ShapeDtypeStruct((b, l, w), F32) for w in widths]
                  + [jax.ShapeDtypeStruct((S5_GROUPS, b * l // S5_CHUNK, S5_VEC), BF16)],
        scratch_shapes=[pltpu.VMEM((D_S5 // LANES, tm, LANES), F32)],
        compiler_params=_cparams(("arbitrary", "arbitrary")),
        name="proj",
    )(x, shift, scale, gain, wm, wl, wa, ba)


def _gla_kernel(*refs, n_chunks, with_output):
    if with_output:
        q_ref, k_ref, v_ref, laf_ref, lab_ref, s0f_ref, s0b_ref, o_ref, sf_ref, sb_ref, st_ref, ob_ref = refs
    else:
        q_ref, k_ref, v_ref, laf_ref, lab_ref, s0f_ref, s0b_ref, sf_ref, sb_ref, st_ref = refs
        o_ref = ob_ref = None
    c = GLA_CHUNK
    row = lax.broadcasted_iota(I32, (c, c), 0)
    col = lax.broadcasted_iota(I32, (c, c), 1)
    lane = lax.broadcasted_iota(I32, (c, LANES), 1)
    st_ref[0] = s0f_ref[0]
    st_ref[1] = s0b_ref[0]

    def chunk(direction, idx):
        la_ref = laf_ref if direction == 0 else lab_ref
        tri = (row >= col) if direction == 0 else (row <= col)
        trib = jnp.where(tri, 1.0, 0.0).astype(BF16)
        r0 = pl.multiple_of(idx * c, c)
        q = q_ref[0, pl.ds(r0, c), :]
        k = k_ref[0, pl.ds(r0, c), :]
        v = v_ref[0, pl.ds(r0, c), :]
        la_hi, la_lo = _split2(la_ref[0, pl.ds(r0, c), :])
        cum = _dot(trib, la_hi) + _dot(trib, la_lo)
        tot = cum[c - 1:c, :] if direction == 0 else cum[0:1, :]
        qd = q * jnp.exp(cum)
        ki = k * jnp.exp(-cum)
        ks = k * jnp.exp(tot - cum)
        dec = jnp.exp(tot)
        for h in range(GLA_HEADS):
            pair = slice(LANES * (h // 2), LANES * (h // 2) + LANES)
            own = (lane >= GLA_DK_HEAD * (h % 2)) & (lane < GLA_DK_HEAD * (h % 2) + GLA_DK_HEAD)
            vb = v[:, GLA_DV_HEAD * h:GLA_DV_HEAD * (h + 1)].astype(BF16)
            st = st_ref[direction, h]
            if with_output:
                qb = qd[:, pair].astype(BF16)
                kib = jnp.where(own, ki[:, pair], 0.0).astype(BF16)
                sc = jnp.where(tri, _dot_nt(qb, kib), 0.0)
                o = _dot(sc.astype(BF16), vb) + _dot_nt(qb, st.astype(BF16))
                cols = slice(GLA_DV_HEAD * h, GLA_DV_HEAD * (h + 1))
                if direction == 0:
                    o_ref[0, pl.ds(r0, c), cols] = o
                else:
                    ob_ref[pl.ds(r0, c), cols] = o
            ksb = jnp.where(own, ks[:, pair], 0.0).astype(BF16)
            st_ref[direction, h] = st * dec[:, pair] + _dot_tn(vb, ksb)

    def body(ci, carry):
        chunk(0, ci)
        chunk(1, n_chunks - 1 - ci)
        return carry

    lax.fori_loop(0, n_chunks, body, 0, unroll=2)
    sf_ref[0] = st_ref[0]
    sb_ref[0] = st_ref[1]
    if with_output:
        o_ref[0] = o_ref[0] + ob_ref[...]


def _gla(q, k, v, laf, lab, s0f, s0b, with_output):
    b, l, _ = q.shape
    n_chunks = l // GLA_CHUNK
    seq = lambda bi: (bi, 0, 0)
    st = lambda bi: (bi, 0, 0, 0)
    st_shape = (b, GLA_HEADS, GLA_DV_HEAD, LANES)
    st_spec = pl.BlockSpec((1, GLA_HEADS, GLA_DV_HEAD, LANES), st)
    out_specs = [st_spec, st_spec]
    out_shape = [jax.ShapeDtypeStruct(st_shape, F32)] * 2
    if with_output:
        out_specs = [pl.BlockSpec((1, l, GLA_DV), seq)] + out_specs
        out_shape = [jax.ShapeDtypeStruct((b, l, GLA_DV), F32)] + out_shape
    return pl.pallas_call(
        functools.partial(_gla_kernel, n_chunks=n_chunks, with_output=with_output),
        grid=(b,),
        in_specs=[pl.BlockSpec((1, l, GLA_DK), seq),
                  pl.BlockSpec((1, l, GLA_DK), seq),
                  pl.BlockSpec((1, l, GLA_DV), seq),
                  pl.BlockSpec((1, l, GLA_DK), seq),
                  pl.BlockSpec((1, l, GLA_DK), seq),
                  st_spec, st_spec],
        out_specs=out_specs,
        out_shape=out_shape,
        scratch_shapes=[pltpu.VMEM((2, GLA_HEADS, GLA_DV_HEAD, LANES), F32)]
                       + ([pltpu.VMEM((l, GLA_DV), F32)] if with_output else []),
        compiler_params=_cparams(("arbitrary",)),
        name="gla_out" if with_output else "gla_ctx",
    )(q, k, v, laf, lab, s0f, s0b)


def _s5gen_kernel(pc_ref, pr_ref, btr_ref, bti_ref, ctrf_ref, ctif_ref, ctrb_ref, ctib_ref,
                  m_ref, wt_ref, v_ref, ab_ref):
    pc = pc_ref[0]
    blk = lax.shift_right_logical(lax.broadcasted_iota(I32, (1, S5_VEC), 1), 4).astype(F32)
    lane = lax.broadcasted_iota(I32, (S5_GROUP_CH, S5_VEC), 1)
    n = float(S5_CHUNK)

    def cmul(ar, ai, br, bi):
        return ar * br - ai * bi, ar * bi + ai * br

    kcat = []
    for d in (0, 1):
        lre, lim, ls = pc[:, 3 * d:3 * d + 1], pc[:, 3 * d + 1:3 * d + 2], pc[:, 3 * d + 2:3 * d + 3]
        ctr = (ctrf_ref if d == 0 else ctrb_ref)[0]
        cti = (ctif_ref if d == 0 else ctib_ref)[0]
        step = jnp.exp(ls)
        mag = jnp.exp(lre * step)
        a_re = mag * jnp.cos(lim * step)
        a_im = mag * jnp.sin(lim * step)
        den = lre * lre + lim * lim
        f_re = ((a_re - 1.0) * lre + a_im * lim) / den
        f_im = (a_im * lre - (a_re - 1.0) * lim) / den
        bb_re, bb_im = cmul(f_re, f_im, btr_ref[0], bti_ref[0])

        def powers(e, lre=lre, lim=lim, step=step):
            m = jnp.exp(lre * step * e)
            ang = lim * step * e
            return m * jnp.cos(ang), m * jnp.sin(ang)

        w_re, w_im = cmul(*powers((n - 1.0 - blk) if d == 0 else blk), bb_re, bb_im)
        wt_ref[0, S5_STATE * d:S5_STATE * (d + 1), :] = w_re
        wt_ref[0, 2 * S5_STATE + S5_STATE * d:2 * S5_STATE + S5_STATE * (d + 1), :] = w_im
        c_re, c_im = cmul(*powers((blk + 1.0) if d == 0 else (n - blk)), ctr, cti)
        v_ref[0, S5_STATE * d:S5_STATE * (d + 1), :] = c_re
        v_ref[0, 2 * S5_STATE + S5_STATE * d:2 * S5_STATE + S5_STATE * (d + 1), :] = -c_im
        e_re, e_im = cmul(*powers(blk if d == 0 else (n - 1.0 - blk)), ctr, cti)
        b16r_hi, b16r_lo = _split2(bb_re[:, 0:S5_GROUP_CH])
        b16i_hi, b16i_lo = _split2(bb_im[:, 0:S5_GROUP_CH])
        er_hi, er_lo = _split2(e_re)
        ei_hi, ei_lo = _split2(e_im)
        kr = _dot_tn(b16r_hi, er_hi) + (_dot_tn(b16r_hi, er_lo) + _dot_tn(b16r_lo, er_hi))
        ki = _dot_tn(b16i_hi, ei_hi) + (_dot_tn(b16i_hi, ei_lo) + _dot_tn(b16i_lo, ei_hi))
        kcat.append(kr - ki)

    for s in range(S5_CHUNK):
        sh_f = S5_GROUP_CH * s
        fwd = kcat[0] if s == 0 else pltpu.roll(kcat[0], sh_f, 1)
        fwd = jnp.where(lane >= sh_f, fwd, 0.0)
        sh_b = S5_VEC - S5_GROUP_CH * (S5_CHUNK - 1 - s)
        bwd = kcat[1] if sh_b == S5_VEC else pltpu.roll(kcat[1], sh_b, 1)
        bwd = jnp.where(lane < S5_GROUP_CH * (s + 1), bwd, 0.0)
        m_ref[0, S5_GROUP_CH * s:S5_GROUP_CH * (s + 1), :] = fwd + bwd

    pr = pr_ref[0]
    for d in (0, 1):
        lre, lim, ls = pr[3 * d:3 * d + 1, :], pr[3 * d + 1:3 * d + 2, :], pr[3 * d + 2:3 * d + 3, :]
        stp = jnp.exp(ls) * n
        mg = jnp.exp(lre * stp)
        ab_ref[0, d:d + 1, :] = mg * jnp.cos(lim * stp)
        ab_ref[0, 2 + d:3 + d, :] = mg * jnp.sin(lim * stp)


def _s5gen(pc, pr, btr, bti, ctrf, ctif, ctrb, ctib):
    g = pc.shape[0]
    blk3 = lambda shape: pl.BlockSpec((1,) + shape, lambda i: (i, 0, 0))
    big = (S5_STATE, S5_VEC)
    sq = (S5_VEC, S5_VEC)
    return pl.pallas_call(
        _s5gen_kernel,
        grid=(g,),
        in_specs=[blk3((S5_STATE, 8)), blk3((8, S5_STATE))] + [blk3(big)] * 6,
        out_specs=[blk3(sq), blk3(sq), blk3(sq), blk3((4, S5_STATE))],
        out_shape=[jax.ShapeDtypeStruct((g,) + sq, F32)] * 3 + [jax.ShapeDtypeStruct((g, 4, S5_STATE), F32)],
        compiler_params=_cparams(("arbitrary",)),
        name="s5gen",
    )(pc, pr, btr, bti, ctrf, ctif, ctrb, ctib)


def _s5_kernel(*refs, n_chunks, nb, with_output):
    if with_output:
        u_ref, m_ref, wt_ref, v_ref, ab_ref, x0_ref, y_ref, xf_ref, z_ref, cin_ref = refs
    else:
        u_ref, wt_ref, ab_ref, x0_ref, xf_ref, z_ref = refs
    wtb = wt_ref[0].astype(BF16)
    for bi in range(nb):
        z = _dot_nt(u_ref[0, bi * n_chunks:(bi + 1) * n_chunks, :], wtb)
        z_ref[0, pl.ds(bi, n_chunks, stride=nb), :] = z[:, 0:LANES]
        z_ref[1, pl.ds(bi, n_chunks, stride=nb), :] = z[:, LANES:2 * LANES]
    ab = ab_ref[0]
    ar, ai = ab[:, 0:LANES], ab[:, LANES:2 * LANES]
    is_f = lax.broadcasted_iota(I32, (nb, LANES), 1) < S5_STATE
    x0 = x0_ref[0]

    def body(i, carry):
        xr, xi = carry
        rf = pl.multiple_of(i * nb, nb)
        rb = pl.multiple_of((n_chunks - 1 - i) * nb, nb)
        if with_output:
            cin_ref[0, pl.ds(rf, nb), 0:S5_STATE] = xr[:, 0:S5_STATE]
            cin_ref[1, pl.ds(rf, nb), 0:S5_STATE] = xi[:, 0:S5_STATE]
            cin_ref[0, pl.ds(rb, nb), S5_STATE:LANES] = xr[:, S5_STATE:LANES]
            cin_ref[1, pl.ds(rb, nb), S5_STATE:LANES] = xi[:, S5_STATE:LANES]
        zr = jnp.where(is_f, z_ref[0, pl.ds(rf, nb), :], z_ref[0, pl.ds(rb, nb), :])
        zi = jnp.where(is_f, z_ref[1, pl.ds(rf, nb), :], z_ref[1, pl.ds(rb, nb), :])
        return ar * xr - ai * xi + zr, ar * xi + ai * xr + zi

    xr, xi = lax.fori_loop(0, n_chunks, body, (x0[:, 0:LANES], x0[:, LANES:2 * LANES]))
    xf_ref[0, :, 0:LANES] = xr
    xf_ref[0, :, LANES:2 * LANES] = xi
    if with_output:
        mb = m_ref[0].astype(BF16)
        vb = v_ref[0].astype(BF16)
        for bi in range(nb):
            rows = slice(bi * n_chunks, (bi + 1) * n_chunks)
            carried = jnp.concatenate([cin_ref[0, pl.ds(bi, n_chunks, stride=nb), :],
                                       cin_ref[1, pl.ds(bi, n_chunks, stride=nb), :]], axis=1).astype(BF16)
            y_ref[0, rows, :] = _dot(u_ref[0, rows, :], mb) + _dot(carried, vb)


def _s5(uvec, m, wt, v, ab, x0, nb, with_output):
    g, rows, _ = uvec.shape
    n_chunks = rows // nb
    blk3 = lambda shape: pl.BlockSpec((1,) + shape, lambda i: (i, 0, 0))
    sq = (S5_VEC, S5_VEC)
    st = (nb, S5_VEC)
    if with_output:
        args = (uvec, m, wt, v, ab, x0)
        in_specs = [blk3((rows, S5_VEC)), blk3(sq), blk3(sq), blk3(sq), blk3((1, S5_VEC)), blk3(st)]
        out_specs = [blk3((rows, S5_VEC)), blk3(st)]
        out_shape = [jax.ShapeDtypeStruct((g, rows, S5_VEC), F32), jax.ShapeDtypeStruct((g,) + st, F32)]
        scratch = [pltpu.VMEM((S5_VEC // LANES, rows, LANES), F32), pltpu.VMEM((S5_VEC // LANES, rows, LANES), F32)]
    else:
        args = (uvec, wt, ab, x0)
        in_specs = [blk3((rows, S5_VEC)), blk3(sq), blk3((1, S5_VEC)), blk3(st)]
        out_specs = [blk3(st)]
        out_shape = [jax.ShapeDtypeStruct((g,) + st, F32)]
        scratch = [pltpu.VMEM((S5_VEC // LANES, rows, LANES), F32)]
    return pl.pallas_call(
        functools.partial(_s5_kernel, n_chunks=n_chunks, nb=nb, with_output=with_output),
        grid=(g,),
        in_specs=in_specs,
        out_specs=out_specs,
        out_shape=out_shape,
        scratch_shapes=scratch,
        compiler_params=_cparams(("arbitrary",)),
        name="s5_out" if with_output else "s5_ctx",
    )(*args)


def _post_kernel(x_ref, o_ref, go_ref, u_ref, yv_ref, g1_ref, sh2_ref, sc2_ref, g2_ref,
                 gn_ref, d_ref, gw_ref, gb_ref, wo_ref, n2_ref, rwh_ref, rwl_ref, sgu_ref, sd_ref,
                 base_ref, hrow_ref, lg_ref, y_ref, *, tm):
    nc = tm // S5_CHUNK
    gpt = LANES // S5_GROUP_CH
    masks = _group_lane_masks(nc)
    for s in range(S5_CHUNK):
        half, j = divmod(s, gpt)
        for t in range(D_S5 // LANES):
            acc = None
            for gl in range(gpt):
                piece = _move_group(yv_ref[t * gpt + gl, :, LANES * half:LANES * (half + 1)], j, gl)
                acc = piece if acc is None else jnp.where(masks[gl], piece, acc)
            y_ref[t, pl.ds(s, nc, stride=S5_CHUNK), :] = acc
    o = o_ref[0]
    gn = gn_ref[...]
    heads = [_rms(o[:, GLA_DV_HEAD * h:GLA_DV_HEAD * (h + 1)], gn) for h in range(GLA_HEADS)]
    gla_out = jnp.concatenate(heads, axis=1) * _silu(go_ref[0])
    yy = jnp.concatenate([y_ref[t] for t in range(D_S5 // LANES)], axis=1) + d_ref[...] * u_ref[0]
    z = 0.5 * yy * (1.0 + jnp.tanh(0.7978845608028654 * (yy + 0.044715 * (yy * yy * yy))))
    s5_out = z * jax.nn.sigmoid(_dot(z.astype(BF16), gw_ref[...]) + gb_ref[...])
    mix = jnp.concatenate([gla_out, s5_out], axis=1).astype(BF16)
    x1 = x_ref[0] + g1_ref[0] * _dot(mix, wo_ref[...])
    h2 = _rms(x1, n2_ref[...]) * (1.0 + sc2_ref[0]) + sh2_ref[0]
    lg_ref[0] = _dot3(h2, rwh_ref[...], rwl_ref[...])
    hb = h2.astype(BF16)
    gu = _dot(hb, sgu_ref[...])
    hid = _silu(gu[:, 0:D_SHARED]) * gu[:, D_SHARED:2 * D_SHARED]
    base_ref[0] = x1 + g2_ref[0] * _dot(hid.astype(BF16), sd_ref[...])
    rounded = hb.astype(F32)
    half = D_MODEL // 2
    words = (pltpu.bitcast(rounded[:, half:], U32) & jnp.uint32(0xFFFF0000)) | (pltpu.bitcast(rounded[:, 0:half], U32) >> 16)
    for s in range(PACK_TILES):
        hrow_ref[:, s, :] = words[:, LANES * s:LANES * (s + 1)]


def _post(x, o, go, u, yvec, g1, sh2, sc2, g2, gn, d, gw, gb, wo, n2, rwh, rwl, sgu, sd):
    b, l, _ = x.shape
    tm = 256
    nt = l // tm
    row = lambda bi, i: (bi, i, 0)
    mod = lambda bi, i: (bi, 0, 0)
    full = lambda bi, i: (0, 0)
    ws = (gn, d, gw, gb, wo, n2, rwh, rwl, sgu, sd)
    return pl.pallas_call(
        functools.partial(_post_kernel, tm=tm),
        grid=(b, nt),
        in_specs=[pl.BlockSpec((1, tm, D_MODEL), row)]
                 + [pl.BlockSpec((1, tm, 512), row)] * 3
                 + [pl.BlockSpec((S5_GROUPS, tm // S5_CHUNK, S5_VEC), lambda bi, i: (0, bi * nt + i, 0))]
                 + [pl.BlockSpec((1, 1, D_MODEL), mod)] * 4
                 + [pl.BlockSpec(w.shape, full) for w in ws],
        out_specs=[pl.BlockSpec((1, tm, D_MODEL), row),
                   pl.BlockSpec((tm, PACK_TILES, LANES), lambda bi, i: (bi * nt + i, 0, 0)),
                   pl.BlockSpec((1, tm, N_EXPERTS), row)],
        out_shape=[jax.ShapeDtypeStruct((b, l, D_MODEL), F32),
                   jax.ShapeDtypeStruct((b * l, PACK_TILES, LANES), U32),
                   jax.ShapeDtypeStruct((b, l, N_EXPERTS), F32)],
        scratch_shapes=[pltpu.VMEM((D_S5 // LANES, tm, LANES), F32)],
        compiler_params=_cparams(("arbitrary", "arbitrary")),
        name="post",
    )(x, o, go, u, yvec, g1, sh2, sc2, g2, *ws)


def _route_kernel(lg_ref, rb_ref, w_ref, p_ref, rb4_ref, tc_ref, cnt_ref, run_ref, *, tm):
    @pl.when(pl.program_id(0) == 0)
    def _():
        run_ref[...] = jnp.zeros_like(run_ref)

    neg = -jnp.inf
    gsz = N_EXPERTS // N_EXPERT_GROUPS
    s = jax.nn.sigmoid(lg_ref[...].T)
    biased = s + rb_ref[...]
    row = lax.broadcasted_iota(I32, (N_EXPERTS, tm), 0).astype(F32)

    def first_max(m, idx):
        mx = jnp.max(m, axis=0, keepdims=True)
        ix = jnp.min(jnp.where(m == mx, idx, float(N_EXPERTS)), axis=0, keepdims=True)
        return mx, ix

    grow = lax.broadcasted_iota(I32, (gsz, tm), 0).astype(F32)
    gs = []
    for g in range(N_EXPERT_GROUPS):
        m, idx = biased[gsz * g:gsz * (g + 1), :], grow + float(gsz * g)
        m1, i1 = first_max(m, idx)
        gs.append(m1 + jnp.max(jnp.where(idx == i1, neg, m), axis=0, keepdims=True))
    kept = []
    for g in range(N_EXPERT_GROUPS):
        ahead = jnp.zeros((1, tm), F32)
        for j in range(N_EXPERT_GROUPS):
            if j < g:
                ahead = ahead + jnp.where(gs[j] >= gs[g], 1.0, 0.0)
            elif j > g:
                ahead = ahead + jnp.where(gs[j] > gs[g], 1.0, 0.0)
        kept.append(jnp.where(ahead < float(TOPK_GROUPS), biased[gsz * g:gsz * (g + 1), :], neg))
    masked = jnp.concatenate(kept, axis=0)

    onehot = jnp.zeros((N_EXPERTS, tm), F32)
    ids, ws = [], []
    for _ in range(TOP_K):
        _, ik = first_max(masked, row)
        hit = row == ik
        ids.append(ik)
        ws.append(jnp.sum(jnp.where(hit, s, 0.0), axis=0, keepdims=True))
        onehot = onehot + jnp.where(hit, 1.0, 0.0)
        masked = jnp.where(hit, neg, masked)
    wsum = ws[0]
    for k in range(1, TOP_K):
        wsum = wsum + ws[k]

    ss = lax.broadcasted_iota(I32, (tm, tm), 0)
    tt = lax.broadcasted_iota(I32, (tm, tm), 1)
    earlier = jnp.where(ss < tt, 1.0, 0.0).astype(BF16)
    ohb = onehot.astype(BF16)
    tcnt = _dot(ohb, jnp.ones((tm, LANES), BF16))
    ee = lax.broadcasted_iota(I32, (N_EXPERTS, N_EXPERTS), 0)
    ff = lax.broadcasted_iota(I32, (N_EXPERTS, N_EXPERTS), 1)
    below = jnp.where(ff < ee, 1.0, 0.0).astype(BF16)
    t_hi, t_lo = _split2(tcnt)
    toff = _dot(below, t_hi) + _dot(below, t_lo)
    lpos = (_dot(ohb, earlier) + toff[:, 0:1]) * float(ROW_TILES)
    w_ref[...] = jnp.concatenate([w / wsum * ROUTE_SCALE for w in ws], axis=0)
    p_ref[...] = jnp.concatenate(
        [jnp.sum(jnp.where(row == ids[k], lpos, 0.0), axis=0, keepdims=True) for k in range(TOP_K)], axis=0).astype(I32)
    run = run_ref[...]
    rb4_ref[0] = run.astype(I32)
    tc_ref[0] = tcnt.astype(I32)
    run = run + tcnt
    run_ref[...] = run
    cnt_ref[...] = run.astype(I32)


def _route(logits, rb, tm):
    t = logits.shape[0]
    col = lambda i: (0, i)
    fixed = lambda i: (0, 0)
    tile = lambda i: (i, 0, 0)
    per_tile = jax.ShapeDtypeStruct((t // tm, N_EXPERTS, LANES), I32)
    return pl.pallas_call(
        functools.partial(_route_kernel, tm=tm),
        grid=(t // tm,),
        in_specs=[pl.BlockSpec((tm, N_EXPERTS), lambda i: (i, 0)), pl.BlockSpec((N_EXPERTS, 1), fixed)],
        out_specs=[pl.BlockSpec((TOP_K, tm), col)] * 2 + [pl.BlockSpec((1, N_EXPERTS, LANES), tile)] * 2
                  + [pl.BlockSpec((N_EXPERTS, LANES), fixed)],
        out_shape=[jax.ShapeDtypeStruct((TOP_K, t), F32), jax.ShapeDtypeStruct((TOP_K, t), I32), per_tile, per_tile,
                   jax.ShapeDtypeStruct((N_EXPERTS, LANES), I32)],
        scratch_shapes=[pltpu.VMEM((N_EXPERTS, LANES), F32)],
        compiler_params=_cparams(("arbitrary",)),
        name="route",
    )(logits, rb)


def _n_blocks_max(n_assign):
    return -(-(n_assign + N_EXPERTS * (EXPERT_BLOCK - 1)) // EXPERT_BLOCK)


def _plan_kernel(cnt_ref, ps_ref, blk_ref, nv_ref, *, nbp):
    cnt = cnt_ref[...]
    nb = lax.shift_right_logical(cnt + (EXPERT_BLOCK - 1), 8).astype(F32)
    nb8 = jnp.broadcast_to(nb, (SUBLANES, N_EXPERTS))
    nb_hi, nb_lo = _split2(nb8)
    ii = lax.broadcasted_iota(I32, (N_EXPERTS, N_EXPERTS), 0)
    jj = lax.broadcasted_iota(I32, (N_EXPERTS, N_EXPERTS), 1)
    upto = jnp.where(ii <= jj, 1.0, 0.0).astype(BF16)
    cum = (_dot(nb_hi, upto) + _dot(nb_lo, upto))[0:1, :]
    ps_ref[...] = ((cum - nb) * float(EXPERT_BLOCK)).astype(I32)
    bi = lax.broadcasted_iota(I32, (nbp, N_EXPERTS), 0).astype(F32)
    owner = jnp.sum(jnp.where(cum <= bi, 1.0, 0.0), axis=-1, keepdims=True)
    blk_ref[...] = jnp.minimum(owner, float(N_EXPERTS - 1)).astype(I32)
    nv_ref[...] = cum[:, N_EXPERTS - 1:N_EXPERTS].astype(I32)


def _plan(cnt, nbp):
    return pl.pallas_call(
        functools.partial(_plan_kernel, nbp=nbp),
        out_shape=[jax.ShapeDtypeStruct((1, N_EXPERTS), I32), jax.ShapeDtypeStruct((nbp, 1), I32),
                   jax.ShapeDtypeStruct((1, 1), I32)],
        name="plan",
    )(cnt)


def _segment_copies(tc_ref, rb_ref, ps_ref, make):
    def body(pair, local):
        for j in range(2):
            e = 2 * pair + j
            cnt = tc_ref[e]

            @pl.when(cnt > 0)
            def _(e=e, local=local, cnt=cnt, j=j):
                make(local, ps_ref[e] + rb_ref[e], cnt).start(priority=j)

            local = local + cnt
        return local

    lax.fori_loop(0, N_EXPERTS // 2, body, 0)


def _rows(ref, row, n):
    return ref.at[pl.ds(pl.multiple_of(row * ROW_TILES, ROW_TILES), n * ROW_TILES)]


def _packed_rows(ref, row, n):
    return ref.at[pl.ds(row, n)]


def _dispatch_kernel(lp_ref, tc_ref, rb_ref, ps_ref, cnt_ref, nv_ref, h_ref, xs_ref, sorted_ref, zero_ref, sems, zsem,
                     *, tm, nbp):
    step = pl.program_id(0)

    @pl.when(step == 0)
    def _():
        zero_ref[...] = jnp.zeros_like(zero_ref)
        for wait in (False, True):
            def tail_body(bi, carry, wait=wait):
                cp = pltpu.make_async_copy(zero_ref, _packed_rows(xs_ref, bi * EXPERT_BLOCK, EXPERT_BLOCK), zsem)
                cp.wait() if wait else cp.start()
                return carry
            lax.fori_loop(nv_ref[0], nbp, tail_body, 0)

            def pad_body(e, carry, wait=wait):
                cnt = cnt_ref[e]
                pad = (-cnt) & (EXPERT_BLOCK - 1)

                @pl.when(pad > 0)
                def _():
                    cp = pltpu.make_async_copy(_packed_rows(zero_ref, 0, pad), _packed_rows(xs_ref, ps_ref[e] + cnt, pad), zsem)
                    cp.wait() if wait else cp.start()

                return carry
            lax.fori_loop(0, N_EXPERTS, pad_body, 0)

    slot = step % 2
    mine = sorted_ref.at[slot]

    def token_body(t, carry):
        row = h_ref[t]
        for k in range(TOP_K):
            mine[lax.shift_right_logical(lp_ref[t * TOP_K + k], 3)] = row
        return carry

    lax.fori_loop(0, tm, token_body, 0, unroll=2)
    _segment_copies(tc_ref, rb_ref, ps_ref,
                    lambda loc, glob, n: pltpu.make_async_copy(_packed_rows(mine, loc, n), _packed_rows(xs_ref, glob, n),
                                                               sems.at[slot]))

    def drain(s):
        pltpu.make_async_copy(sorted_ref.at[s], _packed_rows(xs_ref, 0, tm * TOP_K), sems.at[s]).wait()

    @pl.when(step > 0)
    def _():
        drain(1 - slot)

    @pl.when(step == pl.num_programs(0) - 1)
    def _():
        drain(slot)


def _dispatch(lp_flat, tcnt, runb, pstart, cnt, nv, hrows, nbp, tm):
    t = lp_flat.shape[0] // TOP_K
    per_tile = pl.BlockSpec((N_EXPERTS,), lambda i: (i,), memory_space=pltpu.SMEM)
    smem_all = pl.BlockSpec((N_EXPERTS,), lambda i: (0,), memory_space=pltpu.SMEM)
    return pl.pallas_call(
        functools.partial(_dispatch_kernel, tm=tm, nbp=nbp),
        grid=(t // tm,),
        in_specs=[pl.BlockSpec((tm * TOP_K,), lambda i: (i,), memory_space=pltpu.SMEM),
                  per_tile, per_tile, smem_all, smem_all,
                  pl.BlockSpec((1,), lambda i: (0,), memory_space=pltpu.SMEM),
                  pl.BlockSpec((tm, PACK_TILES, LANES), lambda i: (i, 0, 0))],
        out_specs=pl.BlockSpec(memory_space=pl.ANY),
        out_shape=jax.ShapeDtypeStruct((nbp * EXPERT_BLOCK, PACK_TILES, LANES), U32),
        scratch_shapes=[pltpu.VMEM((2, tm * TOP_K, PACK_TILES, LANES), U32),
                        pltpu.VMEM((EXPERT_BLOCK, PACK_TILES, LANES), U32),
                        pltpu.SemaphoreType.DMA((2,)), pltpu.SemaphoreType.DMA],
        compiler_params=_cparams(("arbitrary",)),
        name="dispatch",
    )(lp_flat, tcnt, runb, pstart, cnt, nv, hrows)


def _experts_kernel(blk_ref, nv_ref, xs_hbm, wg_hbm, wu_hbm, wd_hbm, ys_hbm, step_ref, wgb_ref, wub_ref, wdb_ref, *, nbp):
    rows = EXPERT_BLOCK * ROW_TILES
    nv = nv_ref[0]
    cur = lambda i: jnp.minimum(i, nv - 1)
    xmap = lambda i: (cur(i), 0, 0)
    wmap = lambda i: (blk_ref[cur(i)], 0, 0)
    step_ref[0] = 0

    def body(xs_ref, wg_ref, wu_ref, wd_ref, ys_ref):
        i = step_ref[0]
        step_ref[0] = i + 1

        @pl.when(i >= nv)
        def _():
            ys_ref[...] = jnp.zeros_like(ys_ref)

        @pl.when(i < nv)
        def _():
            prev = blk_ref[jnp.maximum(i - 1, 0)]

            @pl.when((i == 0) | (blk_ref[i] != prev))
            def _():
                wgb_ref[...] = wg_ref[0].astype(BF16)
                wub_ref[...] = wu_ref[0].astype(BF16)
                wdb_ref[...] = wd_ref[0].astype(BF16)

            words = [xs_ref[:, s, :] for s in range(PACK_TILES)]
            low = [pltpu.bitcast(w << 16, F32) for w in words]
            high = [pltpu.bitcast(w & jnp.uint32(0xFFFF0000), F32) for w in words]
            xb = jnp.concatenate(low + high, axis=1).astype(BF16)
            hid = _silu(_dot(xb, wgb_ref[...])) * _dot(xb, wub_ref[...])
            y = _dot(hid.astype(BF16), wdb_ref[...])
            for s in range(ROW_TILES):
                ys_ref[pl.ds(s, EXPERT_BLOCK, stride=ROW_TILES), :] = y[:, LANES * s:LANES * (s + 1)]

    ahead = pl.Buffered(2, use_lookahead=True)
    pltpu.emit_pipeline(
        body,
        grid=(nbp,),
        in_specs=[pl.BlockSpec((EXPERT_BLOCK, PACK_TILES, LANES), xmap, pipeline_mode=pl.Buffered(3)),
                  pl.BlockSpec((1, D_MODEL, D_EXPERT), wmap, pipeline_mode=ahead),
                  pl.BlockSpec((1, D_MODEL, D_EXPERT), wmap, pipeline_mode=ahead),
                  pl.BlockSpec((1, D_EXPERT, D_MODEL), wmap, pipeline_mode=ahead)],
        out_specs=[pl.BlockSpec((rows, LANES), lambda i: (i, 0))],
    )(xs_hbm, wg_hbm, wu_hbm, wd_hbm, ys_hbm)


def _experts(blk, nv, xs, wg, wu, wd, nbp):
    smem = pl.BlockSpec(memory_space=pltpu.SMEM)
    hbm = pl.BlockSpec(memory_space=pl.ANY)
    return pl.pallas_call(
        functools.partial(_experts_kernel, nbp=nbp),
        in_specs=[smem, smem, hbm, hbm, hbm, hbm],
        out_specs=hbm,
        out_shape=jax.ShapeDtypeStruct((xs.shape[0] * ROW_TILES, LANES), F32),
        scratch_shapes=[pltpu.SMEM((1,), I32),
                        pltpu.VMEM((D_MODEL, D_EXPERT), BF16), pltpu.VMEM((D_MODEL, D_EXPERT), BF16),
                        pltpu.VMEM((D_EXPERT, D_MODEL), BF16)],
        compiler_params=pltpu.CompilerParams(vmem_limit_bytes=VMEM_LIMIT),
        name="experts",
    )(blk, nv, xs, wg, wu, wd)


def _combine_kernel(lp_ref, w_ref, tc_ref, rb_ref, tcn_ref, rbn_ref, ps_ref, ys_ref, base_ref, g2_ref, fg_ref, out_ref,
                    buf_ref, acc_ref, sems, *, tm):
    step = pl.program_id(0) * pl.num_programs(1) + pl.program_id(1)
    last = pl.num_programs(0) * pl.num_programs(1) - 1
    slot = step % 2

    def fetch(tcnt_ref, runb_ref, s):
        _segment_copies(tcnt_ref, runb_ref, ps_ref,
                        lambda loc, glob, n: pltpu.make_async_copy(_rows(ys_ref, glob, n), _rows(buf_ref.at[s], loc, n),
                                                                   sems.at[s]))

    @pl.when(step == 0)
    def _():
        fetch(tc_ref, rb_ref, slot)

    @pl.when(step < last)
    def _():
        fetch(tcn_ref, rbn_ref, 1 - slot)

    mine = buf_ref.at[slot]
    pltpu.make_async_copy(_rows(ys_ref, 0, tm * TOP_K), mine, sems.at[slot]).wait()

    def token_body(t, carry):
        j0 = t * TOP_K
        acc = jnp.zeros((ROW_TILES, LANES), F32)
        for k in range(TOP_K):
            acc = acc + w_ref[j0 + k] * mine[pl.ds(pl.multiple_of(lp_ref[j0 + k], ROW_TILES), ROW_TILES), :]
        acc_ref[pl.ds(pl.multiple_of(t * ROW_TILES, ROW_TILES), ROW_TILES), :] = acc
        return carry

    lax.fori_loop(0, tm, token_body, 0, unroll=2)
    routed = jnp.concatenate([acc_ref[pl.ds(s, tm, stride=ROW_TILES), :] for s in range(ROW_TILES)], axis=1)
    out_ref[0] = _rms(base_ref[0] + g2_ref[0] * routed, fg_ref[...])


def _combine(lp_flat, w_flat, tcnt, runb, pstart, ys, base, g2, fg, tm):
    b, l, _ = base.shape
    nt = l // tm
    flat = lambda bi, i: (bi * nt + i,)
    following = lambda bi, i: (jnp.minimum(bi * nt + i + 1, b * nt - 1),)
    smem_blk = pl.BlockSpec((tm * TOP_K,), flat, memory_space=pltpu.SMEM)
    per_tile = pl.BlockSpec((N_EXPERTS,), flat, memory_space=pltpu.SMEM)
    next_tile = pl.BlockSpec((N_EXPERTS,), following, memory_space=pltpu.SMEM)
    return pl.pallas_call(
        functools.partial(_combine_kernel, tm=tm),
        grid=(b, nt),
        in_specs=[smem_blk, smem_blk, per_tile, per_tile, next_tile, next_tile,
                  pl.BlockSpec((N_EXPERTS,), lambda bi, i: (0,), memory_space=pltpu.SMEM),
                  pl.BlockSpec(memory_space=pl.ANY),
                  pl.BlockSpec((1, tm, D_MODEL), lambda bi, i: (bi, i, 0)),
                  pl.BlockSpec((1, 1, D_MODEL), lambda bi, i: (bi, 0, 0)),
                  pl.BlockSpec((1, D_MODEL), lambda bi, i: (0, 0))],
        out_specs=pl.BlockSpec((1, tm, D_MODEL), lambda bi, i: (bi, i, 0)),
        out_shape=jax.ShapeDtypeStruct((b, l, D_MODEL), F32),
        scratch_shapes=[pltpu.VMEM((2, tm * TOP_K * ROW_TILES, LANES), F32),
                        pltpu.VMEM((tm * ROW_TILES, LANES), F32),
                        pltpu.SemaphoreType.DMA((2,))],
        compiler_params=_cparams(("arbitrary", "arbitrary")),
        name="combine",
    )(lp_flat, w_flat, tcnt, runb, tcnt, runb, pstart, ys, base, g2, fg)


def _mixer_inputs(h, shift, scale, gain, wm, wl, wa, ba):
    return _proj(h, shift, scale, gain, wm, wl, wa, ba)


def kernel(x, c, ctx, c_ctx, ada_w, ada_b, norm1_g, norm2_g, w_in, gla_wa_f, gla_ba_f, gla_wa_b, gla_ba_b, gla_norm_g, s5_lam_re_f, s5_lam_im_f, s5_log_step_f, s5_lam_re_b, s5_lam_im_b, s5_log_step_b, s5_b_re, s5_b_im, s5_c_re_f, s5_c_im_f, s5_c_re_b, s5_c_im_b, s5_d, s5_glu_w, s5_glu_b, w_out, router_w, router_b, exp_w_gate, exp_w_up, exp_w_down, sh_w_gate, sh_w_up, sh_w_down, final_norm_g):
    b, l, d = x.shape
    i = 0

    rows = -(-(b + 1) // SUBLANES) * SUBLANES
    cs = jnp.zeros((rows, d), F32).at[:b].set(c).at[b].set(c_ctx)
    mod = _adaln(cs, ada_w[i], ada_b[i][None, :])
    sh1, sc1, g1, sh2, sc2, g2 = [mod[:b, d * j:d * (j + 1)][:, None, :] for j in range(6)]
    csh1, csc1 = [jnp.broadcast_to(mod[b, d * j:d * (j + 1)][None, None, :], (b, 1, d)) for j in range(2)]

    w = w_in[i]
    o1, o2, o3, o4, o5, o6 = 256, 512, 1024, 1536, 1552, 1568
    wm = jnp.concatenate([w[:, :o4], w[:, o6:]], axis=1).astype(BF16)
    wl = jnp.zeros((d, LANES), F32).at[:, :2 * GLA_GATE_RANK].set(w[:, o4:o6]).astype(BF16)
    wa = jnp.zeros((LANES, 2 * GLA_DK), F32)
    wa = wa.at[:GLA_GATE_RANK, :GLA_DK].set(gla_wa_f[i]).at[GLA_GATE_RANK:2 * GLA_GATE_RANK, GLA_DK:].set(gla_wa_b[i])
    wa = wa.astype(BF16)
    ba = jnp.concatenate([gla_ba_f[i], gla_ba_b[i]])[None, :]
    n1 = norm1_g[i][None, :]

    pcols = jnp.stack([s5_lam_re_f[i], s5_lam_im_f[i],
                       jnp.broadcast_to(s5_log_step_f[i][:, None], (S5_GROUPS, S5_STATE)),
                       s5_lam_re_b[i], s5_lam_im_b[i],
                       jnp.broadcast_to(s5_log_step_b[i][:, None], (S5_GROUPS, S5_STATE)),
                       jnp.zeros((S5_GROUPS, S5_STATE), F32), jnp.zeros((S5_GROUPS, S5_STATE), F32)], axis=-1)
    prows = pcols.transpose(0, 2, 1)
    tile_b = lambda t: jnp.tile(t, (1, 1, S5_CHUNK))
    tile_c = lambda t: jnp.tile(t.transpose(0, 2, 1), (1, 1, S5_CHUNK))
    m_op, wt_op, v_op, ab4 = _s5gen(pcols, prows, tile_b(s5_b_re[i]), tile_b(s5_b_im[i]),
                                    tile_c(s5_c_re_f[i]), tile_c(s5_c_im_f[i]),
                                    tile_c(s5_c_re_b[i]), tile_c(s5_c_im_b[i]))
    ab = ab4.reshape(S5_GROUPS, 1, 4 * S5_STATE)

    cq, ck, cv, _, _, claf, clab, cuv = _proj(ctx, csh1, csc1, n1, wm, wl, wa, ba)
    zero_state = jnp.zeros((b, GLA_HEADS, GLA_DV_HEAD, LANES), F32)
    gsf, gsb = _gla(cq, ck, cv, claf, clab, zero_state, zero_state, with_output=False)
    (x0,) = _s5(cuv, None, wt_op, None, ab, jnp.zeros((S5_GROUPS, b, S5_VEC), F32), b, with_output=False)

    q, k, v, go, u, laf, lab, uv = _proj(x, sh1, sc1, n1, wm, wl, wa, ba)
    o, _, _ = _gla(q, k, v, laf, lab, gsf, gsb, with_output=True)
    yvec, _ = _s5(uv, m_op, wt_op, v_op, ab, x0, b, with_output=True)

    rw_hi = router_w[i].astype(BF16)
    rw_lo = (router_w[i] - rw_hi.astype(F32)).astype(BF16)
    base, hrows, logits = _post(
        x, o, go, u, yvec, g1, sh2, sc2, g2,
        gla_norm_g[i][None, :], s5_d[i][None, :], s5_glu_w[i].astype(BF16),
        s5_glu_b[i][None, :], w_out[i].astype(BF16), norm2_g[i][None, :], rw_hi, rw_lo,
        jnp.concatenate([sh_w_gate[i], sh_w_up[i]], axis=1).astype(BF16), sh_w_down[i].astype(BF16))

    t = b * l
    tile = min(512, l)
    wts, lpos, runb, tcnt, cnt = _route(logits.reshape(t, N_EXPERTS), router_b[i][:, None], tile)
    cnt_flat = cnt[:, 0]
    runb_flat, tcnt_flat = runb[:, :, 0].reshape(-1), tcnt[:, :, 0].reshape(-1)
    nbp = -(-_n_blocks_max(t * TOP_K) // SUBLANES) * SUBLANES
    pstart, blk, nv = _plan(cnt_flat[None, :], nbp)
    lp_flat, w_flat = lpos.T.reshape(-1), wts.T.reshape(-1)
    ps_flat = pstart.reshape(-1)
    xs = _dispatch(lp_flat, tcnt_flat, runb_flat, ps_flat, cnt_flat, nv.reshape(-1), hrows, nbp, tile)
    assert exp_w_gate.shape[0] == 1, "single-layer block"
    ys = _experts(blk.reshape(-1), nv.reshape(-1), xs, exp_w_gate.reshape(exp_w_gate.shape[1:]),
                  exp_w_up.reshape(exp_w_up.shape[1:]), exp_w_down.reshape(exp_w_down.shape[1:]), nbp)
    return _combine(lp_flat, w_flat, tcnt_flat, runb_flat, ps_flat, ys, base, g2, final_norm_g[None, :], tile)
```

```python
import functools

import jax
import jax.numpy as jnp
from jax import lax
from jax.experimental import pallas as pl
from jax.experimental.pallas import tpu as pltpu

F32 = jnp.float32
BF16 = jnp.bfloat16
I32 = jnp.int32

D_MODEL = 1024
GLA_HEADS = 4
GLA_DK_HEAD = 64
GLA_DV_HEAD = 128
GLA_DK = 256
GLA_DV = 512
GLA_GATE_RANK = 16
GLA_GATE_TAU = 16.0
GLA_CHUNK = 64
D_S5 = 512
S5_GROUP_CH = 16
S5_GROUPS = 32
S5_STATE = 64
S5_CHUNK = 16
S5_VEC = S5_CHUNK * S5_GROUP_CH
N_EXPERTS = 256
TOP_K = 8
N_EXPERT_GROUPS = 8
TOPK_GROUPS = 4
D_EXPERT = 256
D_SHARED = 256
ROUTE_SCALE = 2.5
EPS = 1e-6

LANES = 128
SUBLANES = 8
ROW_TILES = D_MODEL // LANES
EXPERT_BLOCK = 256
VMEM_LIMIT = 56 * 1024 * 1024


def _cparams(sem):
    return pltpu.CompilerParams(dimension_semantics=sem, vmem_limit_bytes=VMEM_LIMIT)


def _dot(a, b):
    return jnp.dot(a, b, preferred_element_type=F32)


def _dot_nt(a, b):
    return lax.dot_general(a, b, (((1,), (1,)), ((), ())), preferred_element_type=F32)


def _dot_tn(a, b):
    return lax.dot_general(a, b, (((0,), (0,)), ((), ())), preferred_element_type=F32)


def _split2(x):
    hi = x.astype(BF16)
    lo = (x - hi.astype(F32)).astype(BF16)
    return hi, lo


def _dot3(a, b_hi, b_lo):
    a_hi, a_lo = _split2(a)
    return _dot(a_hi, b_hi) + (_dot(a_hi, b_lo) + _dot(a_lo, b_hi))


def _silu(x):
    return x * jax.nn.sigmoid(x)


def _rms(x, g):
    return x * lax.rsqrt(jnp.mean(x * x, axis=-1, keepdims=True) + EPS) * g


def _adaln_kernel(c_ref, w_ref, b_ref, o_ref):
    s = _silu(c_ref[...])
    w_hi, w_lo = _split2(w_ref[...])
    o_ref[...] = _dot3(s, w_hi, w_lo) + b_ref[...]


def _adaln(cs, w, b):
    rows, n = cs.shape[0], w.shape[1]
    tn = 1024
    return pl.pallas_call(
        _adaln_kernel,
        grid=(n // tn,),
        in_specs=[pl.BlockSpec((rows, D_MODEL), lambda j: (0, 0)),
                  pl.BlockSpec((D_MODEL, tn), lambda j: (0, j)),
                  pl.BlockSpec((1, tn), lambda j: (0, j))],
        out_specs=pl.BlockSpec((rows, tn), lambda j: (0, j)),
        out_shape=jax.ShapeDtypeStruct((rows, n), F32),
        compiler_params=_cparams(("arbitrary",)),
        name="adaln",
    )(cs, w, b)


def _group_lane_masks(rows):
    grp = lax.shift_right_logical(lax.broadcasted_iota(I32, (rows, LANES), 1), 4)
    return [grp == j for j in range(LANES // S5_GROUP_CH)]


def _move_group(x, src, dst):
    shift = ((dst - src) * S5_GROUP_CH) % LANES
    return pltpu.roll(x, shift, 1) if shift else x


def _proj_kernel(x_ref, sh_ref, sc_ref, g_ref, wm_ref, wl_ref, wa_ref, ba_ref,
                 q_ref, k_ref, v_ref, go_ref, u_ref, laf_ref, lab_ref, uv_ref, ut_ref, *, tm):
    h = _rms(x_ref[0], g_ref[...]) * (1.0 + sc_ref[0]) + sh_ref[0]
    hb = h.astype(BF16)
    q_ref[0] = _dot(hb, wm_ref[:, 0:256]) * (GLA_DK_HEAD ** -0.5)
    k_ref[0] = _dot(hb, wm_ref[:, 256:512])
    v_ref[0] = _dot(hb, wm_ref[:, 512:1024])
    go_ref[0] = _dot(hb, wm_ref[:, 1024:1536])
    u = _dot(hb, wm_ref[:, 1536:2048])
    u_ref[0] = u
    for t in range(D_S5 // LANES):
        ut_ref[t] = u[:, LANES * t:LANES * (t + 1)]
    lr = _dot(hb, wl_ref[...])
    pre = _dot(lr.astype(BF16), wa_ref[...]) + ba_ref[...]
    la = (jnp.minimum(pre, 0.0) - jnp.log1p(jnp.exp(-jnp.abs(pre)))) * (1.0 / GLA_GATE_TAU)
    laf_ref[0] = la[:, 0:GLA_DK]
    lab_ref[0] = la[:, GLA_DK:2 * GLA_DK]
    nc = tm // S5_CHUNK
    gpt = LANES // S5_GROUP_CH
    masks = _group_lane_masks(nc)
    for t in range(D_S5 // LANES):
        steps = [ut_ref[t, pl.ds(s, nc, stride=S5_CHUNK), :] for s in range(S5_CHUNK)]
        for gl in range(gpt):
            for half in range(S5_VEC // LANES):
                acc = None
                for j in range(gpt):
                    piece = _move_group(steps[half * gpt + j], gl, j)
                    acc = piece if acc is None else jnp.where(masks[j], piece, acc)
                uv_ref[t * gpt + gl, :, LANES * half:LANES * (half + 1)] = acc.astype(BF16)


def _proj(x, shift, scale, gain, wm, wl, wa, ba):
    b, l, _ = x.shape
    tm = min(512, l)
    nt = l // tm
    row = lambda bi, i: (bi, i, 0)
    mod = lambda bi, i: (bi, 0, 0)
    full = lambda bi, i: (0, 0)
    widths = (GLA_DK, GLA_DK, GLA_DV, GLA_DV, D_S5, GLA_DK, GLA_DK)
    return pl.pallas_call(
        functools.partial(_proj_kernel, tm=tm),
        grid=(b, nt),
        in_specs=[pl.BlockSpec((1, tm, D_MODEL), row),
                  pl.BlockSpec((1, 1, D_MODEL), mod),
                  pl.BlockSpec((1, 1, D_MODEL), mod),
                  pl.BlockSpec((1, D_MODEL), full),
                  pl.BlockSpec(wm.shape, full),
                  pl.BlockSpec(wl.shape, full),
                  pl.BlockSpec(wa.shape, full),
                  pl.BlockSpec(ba.shape, full)],
        out_specs=[pl.BlockSpec((1, tm, w), row) for w in widths]
                  + [pl.BlockSpec((S5_GROUPS, tm // S5_CHUNK, S5_VEC), lambda bi, i: (0, bi * nt + i, 0))],
        out_shape=[jax.ShapeDtypeStruct((b, l, w), F32) for w in widths]
                  + [jax.ShapeDtypeStruct((S5_GROUPS, b * l // S5_CHUNK, S5_VEC), BF16)],
        scratch_shapes=[pltpu.VMEM((D_S5 // LANES, tm, LANES), F32)],
        compiler_params=_cparams(("arbitrary", "arbitrary")),
        name="proj",
    )(x, shift, scale, gain, wm, wl, wa, ba)


def _gla_kernel(*refs, n_chunks, with_output):
    if with_output:
        q_ref, k_ref, v_ref, laf_ref, lab_ref, s0f_ref, s0b_ref, o_ref, sf_ref, sb_ref, st_ref, ob_ref = refs
    else:
        q_ref, k_ref, v_ref, laf_ref, lab_ref, s0f_ref, s0b_ref, sf_ref, sb_ref, st_ref = refs
        o_ref = ob_ref = None
    c = GLA_CHUNK
    row2 = lax.broadcasted_iota(I32, (c, 2 * c), 0)
    col2 = lax.broadcasted_iota(I32, (c, 2 * c), 1) & (c - 1)
    rowk = lax.broadcasted_iota(I32, (c, GLA_DK), 0)
    colk = lax.broadcasted_iota(I32, (c, GLA_DK), 1)
    k_head = lax.shift_right_logical(colk, 6)
    v_head = lax.shift_right_logical(lax.broadcasted_iota(I32, (c, GLA_DV), 1), 7)
    st_own = (lax.shift_right_logical(lax.broadcasted_iota(I32, (GLA_DV, GLA_DK), 0), 7)
              == lax.shift_right_logical(lax.broadcasted_iota(I32, (GLA_DV, GLA_DK), 1), 6))
    st_ref[0] = s0f_ref[0]
    st_ref[1] = s0b_ref[0]

    def chunk(direction, idx):
        la_ref = laf_ref if direction == 0 else lab_ref
        keep2 = (row2 >= col2) if direction == 0 else (row2 <= col2)
        keep4 = (rowk >= (colk & (c - 1))) if direction == 0 else (rowk <= (colk & (c - 1)))
        r0 = pl.multiple_of(idx * c, c)
        q = q_ref[0, pl.ds(r0, c), :]
        k = k_ref[0, pl.ds(r0, c), :]
        v = v_ref[0, pl.ds(r0, c), :]
        la_hi, la_lo = _split2(la_ref[0, pl.ds(r0, c), :])
        cum = _dot(jnp.where(keep2, 1.0, 0.0).astype(BF16), jnp.concatenate([la_hi, la_lo], axis=0))
        tot = cum[c - 1:c, :] if direction == 0 else cum[0:1, :]
        qb = (q * jnp.exp(cum)).astype(BF16)
        ki = k * jnp.exp(-cum)
        ks = k * jnp.exp(tot - cum)
        dec = jnp.exp(tot)
        vb = v.astype(BF16)
        st = st_ref[direction]
        if with_output:
            kstack = jnp.concatenate([jnp.where(k_head == h, ki, 0.0) for h in range(GLA_HEADS)], axis=0).astype(BF16)
            vstack = jnp.concatenate([jnp.where(v_head == h, v, 0.0) for h in range(GLA_HEADS)], axis=0).astype(BF16)
            sc = jnp.where(keep4, _dot_nt(qb, kstack), 0.0)
            o = _dot(sc.astype(BF16), vstack) + _dot_nt(qb, st.astype(BF16))
            if direction == 0:
                o_ref[0, pl.ds(r0, c), :] = o
            else:
                ob_ref[pl.ds(r0, c), :] = o
        st_ref[direction] = st * dec + jnp.where(st_own, _dot_tn(vb, ks.astype(BF16)), 0.0)

    def body(ci, carry):
        chunk(0, ci)
        chunk(1, n_chunks - 1 - ci)
        return carry

    lax.fori_loop(0, n_chunks, body, 0, unroll=2)
    sf_ref[0] = st_ref[0]
    sb_ref[0] = st_ref[1]
    if with_output:
        o_ref[0] = o_ref[0] + ob_ref[...]


def _gla(q, k, v, laf, lab, s0f, s0b, with_output):
    b, l, _ = q.shape
    n_chunks = l // GLA_CHUNK
    seq = lambda bi: (bi, 0, 0)
    st_shape = (b, GLA_DV, GLA_DK)
    st_spec = pl.BlockSpec((1, GLA_DV, GLA_DK), seq)
    out_specs = [st_spec, st_spec]
    out_shape = [jax.ShapeDtypeStruct(st_shape, F32)] * 2
    if with_output:
        out_specs = [pl.BlockSpec((1, l, GLA_DV), seq)] + out_specs
        out_shape = [jax.ShapeDtypeStruct((b, l, GLA_DV), F32)] + out_shape
    return pl.pallas_call(
        functools.partial(_gla_kernel, n_chunks=n_chunks, with_output=with_output),
        grid=(b,),
        in_specs=[pl.BlockSpec((1, l, GLA_DK), seq),
                  pl.BlockSpec((1, l, GLA_DK), seq),
                  pl.BlockSpec((1, l, GLA_DV), seq),
                  pl.BlockSpec((1, l, GLA_DK), seq),
                  pl.BlockSpec((1, l, GLA_DK), seq),
                  st_spec, st_spec],
        out_specs=out_specs,
        out_shape=out_shape,
        scratch_shapes=[pltpu.VMEM((2, GLA_DV, GLA_DK), F32)]
                       + ([pltpu.VMEM((l, GLA_DV), F32)] if with_output else []),
        compiler_params=_cparams(("arbitrary",)),
        name="gla_out" if with_output else "gla_ctx",
    )(q, k, v, laf, lab, s0f, s0b)


def _s5gen_kernel(pc_ref, pr_ref, btr_ref, bti_ref, ctrf_ref, ctif_ref, ctrb_ref, ctib_ref,
                  m_ref, wt_ref, v_ref, ab_ref):
    pc = pc_ref[0]
    blk = lax.shift_right_logical(lax.broadcasted_iota(I32, (1, S5_VEC), 1), 4).astype(F32)
    lane = lax.broadcasted_iota(I32, (S5_GROUP_CH, S5_VEC), 1)
    n = float(S5_CHUNK)

    def cmul(ar, ai, br, bi):
        return ar * br - ai * bi, ar * bi + ai * br

    kcat = []
    for d in (0, 1):
        lre, lim, ls = pc[:, 3 * d:3 * d + 1], pc[:, 3 * d + 1:3 * d + 2], pc[:, 3 * d + 2:3 * d + 3]
        ctr = (ctrf_ref if d == 0 else ctrb_ref)[0]
        cti = (ctif_ref if d == 0 else ctib_ref)[0]
        step = jnp.exp(ls)
        mag = jnp.exp(lre * step)
        a_re = mag * jnp.cos(lim * step)
        a_im = mag * jnp.sin(lim * step)
        den = lre * lre + lim * lim
        f_re = ((a_re - 1.0) * lre + a_im * lim) / den
        f_im = (a_im * lre - (a_re - 1.0) * lim) / den
        bb_re, bb_im = cmul(f_re, f_im, btr_ref[0], bti_ref[0])

        def powers(e, lre=lre, lim=lim, step=step):
            m = jnp.exp(lre * step * e)
            ang = lim * step * e
            return m * jnp.cos(ang), m * jnp.sin(ang)

        w_re, w_im = cmul(*powers((n - 1.0 - blk) if d == 0 else blk), bb_re, bb_im)
        wt_ref[0, S5_STATE * d:S5_STATE * (d + 1), :] = w_re
        wt_ref[0, 2 * S5_STATE + S5_STATE * d:2 * S5_STATE + S5_STATE * (d + 1), :] = w_im
        c_re, c_im = cmul(*powers((blk + 1.0) if d == 0 else (n - blk)), ctr, cti)
        v_ref[0, S5_STATE * d:S5_STATE * (d + 1), :] = c_re
        v_ref[0, 2 * S5_STATE + S5_STATE * d:2 * S5_STATE + S5_STATE * (d + 1), :] = -c_im
        e_re, e_im = cmul(*powers(blk if d == 0 else (n - 1.0 - blk)), ctr, cti)
        b16r_hi, b16r_lo = _split2(bb_re[:, 0:S5_GROUP_CH])
        b16i_hi, b16i_lo = _split2(bb_im[:, 0:S5_GROUP_CH])
        er_hi, er_lo = _split2(e_re)
        ei_hi, ei_lo = _split2(e_im)
        kr = _dot_tn(b16r_hi, er_hi) + (_dot_tn(b16r_hi, er_lo) + _dot_tn(b16r_lo, er_hi))
        ki = _dot_tn(b16i_hi, ei_hi) + (_dot_tn(b16i_hi, ei_lo) + _dot_tn(b16i_lo, ei_hi))
        kcat.append(kr - ki)

    for s in range(S5_CHUNK):
        sh_f = S5_GROUP_CH * s
        fwd = kcat[0] if s == 0 else pltpu.roll(kcat[0], sh_f, 1)
        fwd = jnp.where(lane >= sh_f, fwd, 0.0)
        sh_b = S5_VEC - S5_GROUP_CH * (S5_CHUNK - 1 - s)
        bwd = kcat[1] if sh_b == S5_VEC else pltpu.roll(kcat[1], sh_b, 1)
        bwd = jnp.where(lane < S5_GROUP_CH * (s + 1), bwd, 0.0)
        m_ref[0, S5_GROUP_CH * s:S5_GROUP_CH * (s + 1), :] = fwd + bwd

    pr = pr_ref[0]
    for d in (0, 1):
        lre, lim, ls = pr[3 * d:3 * d + 1, :], pr[3 * d + 1:3 * d + 2, :], pr[3 * d + 2:3 * d + 3, :]
        stp = jnp.exp(ls) * n
        mg = jnp.exp(lre * stp)
        ab_ref[0, d:d + 1, :] = mg * jnp.cos(lim * stp)
        ab_ref[0, 2 + d:3 + d, :] = mg * jnp.sin(lim * stp)


def _s5gen(pc, pr, btr, bti, ctrf, ctif, ctrb, ctib):
    g = pc.shape[0]
    blk3 = lambda shape: pl.BlockSpec((1,) + shape, lambda i: (i, 0, 0))
    big = (S5_STATE, S5_VEC)
    sq = (S5_VEC, S5_VEC)
    return pl.pallas_call(
        _s5gen_kernel,
        grid=(g,),
        in_specs=[blk3((S5_STATE, 8)), blk3((8, S5_STATE))] + [blk3(big)] * 6,
        out_specs=[blk3(sq), blk3(sq), blk3(sq), blk3((4, S5_STATE))],
        out_shape=[jax.ShapeDtypeStruct((g,) + sq, F32)] * 3 + [jax.ShapeDtypeStruct((g, 4, S5_STATE), F32)],
        compiler_params=_cparams(("arbitrary",)),
        name="s5gen",
    )(pc, pr, btr, bti, ctrf, ctif, ctrb, ctib)


def _s5_kernel(*refs, n_chunks, nb, with_output):
    if with_output:
        u_ref, m_ref, wt_ref, v_ref, ab_ref, x0_ref, y_ref, xf_ref, z_ref, cin_ref = refs
    else:
        u_ref, wt_ref, ab_ref, x0_ref, xf_ref, z_ref = refs
    wtb = wt_ref[0].astype(BF16)
    for bi in range(nb):
        z = _dot_nt(u_ref[0, bi * n_chunks:(bi + 1) * n_chunks, :], wtb)
        z_ref[0, pl.ds(bi, n_chunks, stride=nb), :] = z[:, 0:LANES]
        z_ref[1, pl.ds(bi, n_chunks, stride=nb), :] = z[:, LANES:2 * LANES]
    ab = ab_ref[0]
    ar, ai = ab[:, 0:LANES], ab[:, LANES:2 * LANES]
    is_f = lax.broadcasted_iota(I32, (nb, LANES), 1) < S5_STATE
    x0 = x0_ref[0]

    def body(i, carry):
        xr, xi = carry
        rf = pl.multiple_of(i * nb, nb)
        rb = pl.multiple_of((n_chunks - 1 - i) * nb, nb)
        if with_output:
            cin_ref[0, pl.ds(rf, nb), 0:S5_STATE] = xr[:, 0:S5_STATE]
            cin_ref[1, pl.ds(rf, nb), 0:S5_STATE] = xi[:, 0:S5_STATE]
            cin_ref[0, pl.ds(rb, nb), S5_STATE:LANES] = xr[:, S5_STATE:LANES]
            cin_ref[1, pl.ds(rb, nb), S5_STATE:LANES] = xi[:, S5_STATE:LANES]
        zr = jnp.where(is_f, z_ref[0, pl.ds(rf, nb), :], z_ref[0, pl.ds(rb, nb), :])
        zi = jnp.where(is_f, z_ref[1, pl.ds(rf, nb), :], z_ref[1, pl.ds(rb, nb), :])
        return ar * xr - ai * xi + zr, ar * xi + ai * xr + zi

    xr, xi = lax.fori_loop(0, n_chunks, body, (x0[:, 0:LANES], x0[:, LANES:2 * LANES]))
    xf_ref[0, :, 0:LANES] = xr
    xf_ref[0, :, LANES:2 * LANES] = xi
    if with_output:
        mb = m_ref[0].astype(BF16)
        vb = v_ref[0].astype(BF16)
        for bi in range(nb):
            rows = slice(bi * n_chunks, (bi + 1) * n_chunks)
            carried = jnp.concatenate([cin_ref[0, pl.ds(bi, n_chunks, stride=nb), :],
                                       cin_ref[1, pl.ds(bi, n_chunks, stride=nb), :]], axis=1).astype(BF16)
            y_ref[0, rows, :] = _dot(u_ref[0, rows, :], mb) + _dot(carried, vb)


def _s5(uvec, m, wt, v, ab, x0, nb, with_output):
    g, rows, _ = uvec.shape
    n_chunks = rows // nb
    blk3 = lambda shape: pl.BlockSpec((1,) + shape, lambda i: (i, 0, 0))
    sq = (S5_VEC, S5_VEC)
    st = (nb, S5_VEC)
    if with_output:
        args = (uvec, m, wt, v, ab, x0)
        in_specs = [blk3((rows, S5_VEC)), blk3(sq), blk3(sq), blk3(sq), blk3((1, S5_VEC)), blk3(st)]
        out_specs = [blk3((rows, S5_VEC)), blk3(st)]
        out_shape = [jax.ShapeDtypeStruct((g, rows, S5_VEC), F32), jax.ShapeDtypeStruct((g,) + st, F32)]
        scratch = [pltpu.VMEM((S5_VEC // LANES, rows, LANES), F32), pltpu.VMEM((S5_VEC // LANES, rows, LANES), F32)]
    else:
        args = (uvec, wt, ab, x0)
        in_specs = [blk3((rows, S5_VEC)), blk3(sq), blk3((1, S5_VEC)), blk3(st)]
        out_specs = [blk3(st)]
        out_shape = [jax.ShapeDtypeStruct((g,) + st, F32)]
        scratch = [pltpu.VMEM((S5_VEC // LANES, rows, LANES), F32)]
    return pl.pallas_call(
        functools.partial(_s5_kernel, n_chunks=n_chunks, nb=nb, with_output=with_output),
        grid=(g,),
        in_specs=in_specs,
        out_specs=out_specs,
        out_shape=out_shape,
        scratch_shapes=scratch,
        compiler_params=_cparams(("arbitrary",)),
        name="s5_out" if with_output else "s5_ctx",
    )(*args)


def _post_kernel(x_ref, o_ref, go_ref, u_ref, yv_ref, g1_ref, sh2_ref, sc2_ref, g2_ref,
                 gn_ref, d_ref, gw_ref, gb_ref, wo_ref, n2_ref, rwh_ref, rwl_ref, sgu_ref, sd_ref,
                 base_ref, hrow_ref, lg_ref, y_ref, *, tm):
    nc = tm // S5_CHUNK
    gpt = LANES // S5_GROUP_CH
    masks = _group_lane_masks(nc)
    for s in range(S5_CHUNK):
        half, j = divmod(s, gpt)
        for t in range(D_S5 // LANES):
            acc = None
            for gl in range(gpt):
                piece = _move_group(yv_ref[t * gpt + gl, :, LANES * half:LANES * (half + 1)], j, gl)
                acc = piece if acc is None else jnp.where(masks[gl], piece, acc)
            y_ref[t, pl.ds(s, nc, stride=S5_CHUNK), :] = acc
    o = o_ref[0]
    gn = gn_ref[...]
    heads = [_rms(o[:, GLA_DV_HEAD * h:GLA_DV_HEAD * (h + 1)], gn) for h in range(GLA_HEADS)]
    gla_out = jnp.concatenate(heads, axis=1) * _silu(go_ref[0])
    yy = jnp.concatenate([y_ref[t] for t in range(D_S5 // LANES)], axis=1) + d_ref[...] * u_ref[0]
    z = 0.5 * yy * (1.0 + jnp.tanh(0.7978845608028654 * (yy + 0.044715 * (yy * yy * yy))))
    s5_out = z * jax.nn.sigmoid(_dot(z.astype(BF16), gw_ref[...]) + gb_ref[...])
    mix = jnp.concatenate([gla_out, s5_out], axis=1).astype(BF16)
    x1 = x_ref[0] + g1_ref[0] * _dot(mix, wo_ref[...])
    h2 = _rms(x1, n2_ref[...]) * (1.0 + sc2_ref[0]) + sh2_ref[0]
    lg_ref[0] = _dot3(h2, rwh_ref[...], rwl_ref[...])
    hb = h2.astype(BF16)
    gu = _dot(hb, sgu_ref[...])
    hid = _silu(gu[:, 0:D_SHARED]) * gu[:, D_SHARED:2 * D_SHARED]
    base_ref[0] = x1 + g2_ref[0] * _dot(hid.astype(BF16), sd_ref[...])
    for s in range(ROW_TILES):
        hrow_ref[pl.ds(s, tm, stride=ROW_TILES), :] = h2[:, LANES * s:LANES * (s + 1)]


def _post(x, o, go, u, yvec, g1, sh2, sc2, g2, gn, d, gw, gb, wo, n2, rwh, rwl, sgu, sd):
    b, l, _ = x.shape
    tm = 256
    nt = l // tm
    row = lambda bi, i: (bi, i, 0)
    mod = lambda bi, i: (bi, 0, 0)
    full = lambda bi, i: (0, 0)
    ws = (gn, d, gw, gb, wo, n2, rwh, rwl, sgu, sd)
    return pl.pallas_call(
        functools.partial(_post_kernel, tm=tm),
        grid=(b, nt),
        in_specs=[pl.BlockSpec((1, tm, D_MODEL), row)]
                 + [pl.BlockSpec((1, tm, 512), row)] * 3
                 + [pl.BlockSpec((S5_GROUPS, tm // S5_CHUNK, S5_VEC), lambda bi, i: (0, bi * nt + i, 0))]
                 + [pl.BlockSpec((1, 1, D_MODEL), mod)] * 4
                 + [pl.BlockSpec(w.shape, full) for w in ws],
        out_specs=[pl.BlockSpec((1, tm, D_MODEL), row),
                   pl.BlockSpec((tm * ROW_TILES, LANES), lambda bi, i: (bi * nt + i, 0)),
                   pl.BlockSpec((1, tm, N_EXPERTS), row)],
        out_shape=[jax.ShapeDtypeStruct((b, l, D_MODEL), F32),
                   jax.ShapeDtypeStruct((b * l * ROW_TILES, LANES), F32),
                   jax.ShapeDtypeStruct((b, l, N_EXPERTS), F32)],
        scratch_shapes=[pltpu.VMEM((D_S5 // LANES, tm, LANES), F32)],
        compiler_params=_cparams(("arbitrary", "arbitrary")),
        name="post",
    )(x, o, go, u, yvec, g1, sh2, sc2, g2, *ws)


def _route_kernel(lg_ref, rb_ref, w_ref, p_ref, rb4_ref, tc_ref, cnt_ref, run_ref, *, tm):
    @pl.when(pl.program_id(0) == 0)
    def _():
        run_ref[...] = jnp.zeros_like(run_ref)

    neg = -jnp.inf
    gsz = N_EXPERTS // N_EXPERT_GROUPS
    s = jax.nn.sigmoid(lg_ref[...].T)
    biased = s + rb_ref[...]
    row = lax.broadcasted_iota(I32, (N_EXPERTS, tm), 0).astype(F32)

    def first_max(m, idx):
        mx = jnp.max(m, axis=0, keepdims=True)
        ix = jnp.min(jnp.where(m == mx, idx, float(N_EXPERTS)), axis=0, keepdims=True)
        return mx, ix

    grow = lax.broadcasted_iota(I32, (gsz, tm), 0).astype(F32)
    gs = []
    for g in range(N_EXPERT_GROUPS):
        m, idx = biased[gsz * g:gsz * (g + 1), :], grow + float(gsz * g)
        m1, i1 = first_max(m, idx)
        gs.append(m1 + jnp.max(jnp.where(idx == i1, neg, m), axis=0, keepdims=True))
    kept = []
    for g in range(N_EXPERT_GROUPS):
        ahead = jnp.zeros((1, tm), F32)
        for j in range(N_EXPERT_GROUPS):
            if j < g:
                ahead = ahead + jnp.where(gs[j] >= gs[g], 1.0, 0.0)
            elif j > g:
                ahead = ahead + jnp.where(gs[j] > gs[g], 1.0, 0.0)
        kept.append(jnp.where(ahead < float(TOPK_GROUPS), biased[gsz * g:gsz * (g + 1), :], neg))
    masked = jnp.concatenate(kept, axis=0)

    onehot = jnp.zeros((N_EXPERTS, tm), F32)
    ids, ws = [], []
    for _ in range(TOP_K):
        _, ik = first_max(masked, row)
        hit = row == ik
        ids.append(ik)
        ws.append(jnp.sum(jnp.where(hit, s, 0.0), axis=0, keepdims=True))
        onehot = onehot + jnp.where(hit, 1.0, 0.0)
        masked = jnp.where(hit, neg, masked)
    wsum = ws[0]
    for k in range(1, TOP_K):
        wsum = wsum + ws[k]

    ss = lax.broadcasted_iota(I32, (tm, tm), 0)
    tt = lax.broadcasted_iota(I32, (tm, tm), 1)
    earlier = jnp.where(ss < tt, 1.0, 0.0).astype(BF16)
    ohb = onehot.astype(BF16)
    tcnt = _dot(ohb, jnp.ones((tm, LANES), BF16))
    ee = lax.broadcasted_iota(I32, (N_EXPERTS, N_EXPERTS), 0)
    ff = lax.broadcasted_iota(I32, (N_EXPERTS, N_EXPERTS), 1)
    below = jnp.where(ff < ee, 1.0, 0.0).astype(BF16)
    t_hi, t_lo = _split2(tcnt)
    toff = _dot(below, t_hi) + _dot(below, t_lo)
    lpos = (_dot(ohb, earlier) + toff[:, 0:1]) * float(ROW_TILES)
    w_ref[...] = jnp.concatenate([w / wsum * ROUTE_SCALE for w in ws], axis=0)
    p_ref[...] = jnp.concatenate(
        [jnp.sum(jnp.where(row == ids[k], lpos, 0.0), axis=0, keepdims=True) for k in range(TOP_K)], axis=0).astype(I32)
    run = run_ref[...]
    rb4_ref[0] = run.astype(I32)
    tc_ref[0] = tcnt.astype(I32)
    run = run + tcnt
    run_ref[...] = run
    cnt_ref[...] = run.astype(I32)


def _route(logits, rb, tm):
    t = logits.shape[0]
    col = lambda i: (0, i)
    fixed = lambda i: (0, 0)
    tile = lambda i: (i, 0, 0)
    per_tile = jax.ShapeDtypeStruct((t // tm, N_EXPERTS, LANES), I32)
    return pl.pallas_call(
        functools.partial(_route_kernel, tm=tm),
        grid=(t // tm,),
        in_specs=[pl.BlockSpec((tm, N_EXPERTS), lambda i: (i, 0)), pl.BlockSpec((N_EXPERTS, 1), fixed)],
        out_specs=[pl.BlockSpec((TOP_K, tm), col)] * 2 + [pl.BlockSpec((1, N_EXPERTS, LANES), tile)] * 2
                  + [pl.BlockSpec((N_EXPERTS, LANES), fixed)],
        out_shape=[jax.ShapeDtypeStruct((TOP_K, t), F32), jax.ShapeDtypeStruct((TOP_K, t), I32), per_tile, per_tile,
                   jax.ShapeDtypeStruct((N_EXPERTS, LANES), I32)],
        scratch_shapes=[pltpu.VMEM((N_EXPERTS, LANES), F32)],
        compiler_params=_cparams(("arbitrary",)),
        name="route",
    )(logits, rb)


def _n_blocks_max(n_assign):
    return -(-(n_assign + N_EXPERTS * (EXPERT_BLOCK - 1)) // EXPERT_BLOCK)


def _plan_kernel(cnt_ref, ps_ref, blk_ref, nv_ref, *, nbp):
    cnt = cnt_ref[...]
    nb = lax.shift_right_logical(cnt + (EXPERT_BLOCK - 1), 8).astype(F32)
    nb8 = jnp.broadcast_to(nb, (SUBLANES, N_EXPERTS))
    nb_hi, nb_lo = _split2(nb8)
    ii = lax.broadcasted_iota(I32, (N_EXPERTS, N_EXPERTS), 0)
    jj = lax.broadcasted_iota(I32, (N_EXPERTS, N_EXPERTS), 1)
    upto = jnp.where(ii <= jj, 1.0, 0.0).astype(BF16)
    cum = (_dot(nb_hi, upto) + _dot(nb_lo, upto))[0:1, :]
    ps_ref[...] = ((cum - nb) * float(EXPERT_BLOCK)).astype(I32)
    bi = lax.broadcasted_iota(I32, (nbp, N_EXPERTS), 0).astype(F32)
    owner = jnp.sum(jnp.where(cum <= bi, 1.0, 0.0), axis=-1, keepdims=True)
    blk_ref[...] = jnp.minimum(owner, float(N_EXPERTS - 1)).astype(I32)
    nv_ref[...] = cum[:, N_EXPERTS - 1:N_EXPERTS].astype(I32)


def _plan(cnt, nbp):
    return pl.pallas_call(
        functools.partial(_plan_kernel, nbp=nbp),
        out_shape=[jax.ShapeDtypeStruct((1, N_EXPERTS), I32), jax.ShapeDtypeStruct((nbp, 1), I32),
                   jax.ShapeDtypeStruct((1, 1), I32)],
        name="plan",
    )(cnt)


def _segment_copies(tc_ref, rb_ref, ps_ref, make):
    def body(pair, local):
        for j in range(2):
            e = 2 * pair + j
            cnt = tc_ref[e]

            @pl.when(cnt > 0)
            def _(e=e, local=local, cnt=cnt, j=j):
                make(local, ps_ref[e] + rb_ref[e], cnt).start(priority=j)

            local = local + cnt
        return local

    lax.fori_loop(0, N_EXPERTS // 2, body, 0)


def _rows(ref, row, n):
    return ref.at[pl.ds(pl.multiple_of(row * ROW_TILES, ROW_TILES), n * ROW_TILES)]


def _dispatch_kernel(lp_ref, tc_ref, rb_ref, ps_ref, cnt_ref, nv_ref, h_ref, xs_ref, sorted_ref, zero_ref, sems, zsem,
                     *, tm, nbp):
    step = pl.program_id(0)

    @pl.when(step == 0)
    def _():
        zero_ref[...] = jnp.zeros_like(zero_ref)
        for wait in (False, True):
            def tail_body(bi, carry, wait=wait):
                cp = pltpu.make_async_copy(zero_ref, _rows(xs_ref, bi * EXPERT_BLOCK, EXPERT_BLOCK), zsem)
                cp.wait() if wait else cp.start()
                return carry
            lax.fori_loop(nv_ref[0], nbp, tail_body, 0)

            def pad_body(e, carry, wait=wait):
                cnt = cnt_ref[e]
                pad = (-cnt) & (EXPERT_BLOCK - 1)

                @pl.when(pad > 0)
                def _():
                    cp = pltpu.make_async_copy(_rows(zero_ref, 0, pad), _rows(xs_ref, ps_ref[e] + cnt, pad), zsem)
                    cp.wait() if wait else cp.start()

                return carry
            lax.fori_loop(0, N_EXPERTS, pad_body, 0)

    slot = step % 2
    mine = sorted_ref.at[slot]

    def token_body(t, carry):
        row = h_ref[pl.ds(pl.multiple_of(t * ROW_TILES, ROW_TILES), ROW_TILES), :]
        for k in range(TOP_K):
            mine[pl.ds(pl.multiple_of(lp_ref[t * TOP_K + k], ROW_TILES), ROW_TILES), :] = row
        return carry

    lax.fori_loop(0, tm, token_body, 0, unroll=2)
    _segment_copies(tc_ref, rb_ref, ps_ref,
                    lambda loc, glob, n: pltpu.make_async_copy(_rows(mine, loc, n), _rows(xs_ref, glob, n), sems.at[slot]))

    def drain(s):
        pltpu.make_async_copy(sorted_ref.at[s], _rows(xs_ref, 0, tm * TOP_K), sems.at[s]).wait()

    @pl.when(step > 0)
    def _():
        drain(1 - slot)

    @pl.when(step == pl.num_programs(0) - 1)
    def _():
        drain(slot)


def _dispatch(lp_flat, tcnt, runb, pstart, cnt, nv, hrows, nbp, tm):
    t = lp_flat.shape[0] // TOP_K
    per_tile = pl.BlockSpec((N_EXPERTS,), lambda i: (i,), memory_space=pltpu.SMEM)
    smem_all = pl.BlockSpec((N_EXPERTS,), lambda i: (0,), memory_space=pltpu.SMEM)
    return pl.pallas_call(
        functools.partial(_dispatch_kernel, tm=tm, nbp=nbp),
        grid=(t // tm,),
        in_specs=[pl.BlockSpec((tm * TOP_K,), lambda i: (i,), memory_space=pltpu.SMEM),
                  per_tile, per_tile, smem_all, smem_all,
                  pl.BlockSpec((1,), lambda i: (0,), memory_space=pltpu.SMEM),
                  pl.BlockSpec((tm * ROW_TILES, LANES), lambda i: (i, 0))],
        out_specs=pl.BlockSpec(memory_space=pl.ANY),
        out_shape=jax.ShapeDtypeStruct((nbp * EXPERT_BLOCK * ROW_TILES, LANES), F32),
        scratch_shapes=[pltpu.VMEM((2, tm * TOP_K * ROW_TILES, LANES), F32),
                        pltpu.VMEM((EXPERT_BLOCK * ROW_TILES, LANES), F32),
                        pltpu.SemaphoreType.DMA((2,)), pltpu.SemaphoreType.DMA],
        compiler_params=_cparams(("arbitrary",)),
        name="dispatch",
    )(lp_flat, tcnt, runb, pstart, cnt, nv, hrows)


def _experts_kernel(blk_ref, nv_ref, xs_hbm, wg_hbm, wu_hbm, wd_hbm, ys_hbm, step_ref, wgb_ref, wub_ref, wdb_ref, *, nbp):
    rows = EXPERT_BLOCK * ROW_TILES
    nv = nv_ref[0]
    cur = lambda i: jnp.minimum(i, nv - 1)
    xmap = lambda i: (cur(i), 0)
    wmap = lambda i: (blk_ref[cur(i)], 0, 0)
    step_ref[0] = 0

    def body(xs_ref, wg_ref, wu_ref, wd_ref, ys_ref):
        i = step_ref[0]
        step_ref[0] = i + 1

        @pl.when(i >= nv)
        def _():
            ys_ref[...] = jnp.zeros_like(ys_ref)

        @pl.when(i < nv)
        def _():
            prev = blk_ref[jnp.maximum(i - 1, 0)]

            @pl.when((i == 0) | (blk_ref[i] != prev))
            def _():
                wgb_ref[...] = wg_ref[0].astype(BF16)
                wub_ref[...] = wu_ref[0].astype(BF16)
                wdb_ref[...] = wd_ref[0].astype(BF16)

            xb = jnp.concatenate(
                [xs_ref[pl.ds(s, EXPERT_BLOCK, stride=ROW_TILES), :] for s in range(ROW_TILES)], axis=1).astype(BF16)
            hid = _silu(_dot(xb, wgb_ref[...])) * _dot(xb, wub_ref[...])
            y = _dot(hid.astype(BF16), wdb_ref[...])
            for s in range(ROW_TILES):
                ys_ref[pl.ds(s, EXPERT_BLOCK, stride=ROW_TILES), :] = y[:, LANES * s:LANES * (s + 1)]

    ahead = pl.Buffered(2, use_lookahead=True)
    pltpu.emit_pipeline(
        body,
        grid=(nbp,),
        in_specs=[pl.BlockSpec((rows, LANES), xmap, pipeline_mode=pl.Buffered(3)),
                  pl.BlockSpec((1, D_MODEL, D_EXPERT), wmap, pipeline_mode=ahead),
                  pl.BlockSpec((1, D_MODEL, D_EXPERT), wmap, pipeline_mode=ahead),
                  pl.BlockSpec((1, D_EXPERT, D_MODEL), wmap, pipeline_mode=ahead)],
        out_specs=[pl.BlockSpec((rows, LANES), lambda i: (i, 0))],
    )(xs_hbm, wg_hbm, wu_hbm, wd_hbm, ys_hbm)


def _experts(blk, nv, xs, wg, wu, wd, nbp):
    smem = pl.BlockSpec(memory_space=pltpu.SMEM)
    hbm = pl.BlockSpec(memory_space=pl.ANY)
    return pl.pallas_call(
        functools.partial(_experts_kernel, nbp=nbp),
        in_specs=[smem, smem, hbm, hbm, hbm, hbm],
        out_specs=hbm,
        out_shape=jax.ShapeDtypeStruct(xs.shape, F32),
        scratch_shapes=[pltpu.SMEM((1,), I32),
                        pltpu.VMEM((D_MODEL, D_EXPERT), BF16), pltpu.VMEM((D_MODEL, D_EXPERT), BF16),
                        pltpu.VMEM((D_EXPERT, D_MODEL), BF16)],
        compiler_params=pltpu.CompilerParams(vmem_limit_bytes=VMEM_LIMIT),
        name="experts",
    )(blk, nv, xs, wg, wu, wd)


def _combine_kernel(lp_ref, w_ref, tc_ref, rb_ref, tcn_ref, rbn_ref, ps_ref, ys_ref, base_ref, g2_ref, fg_ref, out_ref,
                    buf_ref, acc_ref, sems, *, tm):
    step = pl.program_id(0) * pl.num_programs(1) + pl.program_id(1)
    last = pl.num_programs(0) * pl.num_programs(1) - 1
    slot = step % 2

    def fetch(tcnt_ref, runb_ref, s):
        _segment_copies(tcnt_ref, runb_ref, ps_ref,
                        lambda loc, glob, n: pltpu.make_async_copy(_rows(ys_ref, glob, n), _rows(buf_ref.at[s], loc, n),
                                                                   sems.at[s]))

    @pl.when(step == 0)
    def _():
        fetch(tc_ref, rb_ref, slot)

    @pl.when(step < last)
    def _():
        fetch(tcn_ref, rbn_ref, 1 - slot)

    mine = buf_ref.at[slot]
    pltpu.make_async_copy(_rows(ys_ref, 0, tm * TOP_K), mine, sems.at[slot]).wait()

    def token_body(t, carry):
        j0 = t * TOP_K
        acc = jnp.zeros((ROW_TILES, LANES), F32)
        for k in range(TOP_K):
            acc = acc + w_ref[j0 + k] * mine[pl.ds(pl.multiple_of(lp_ref[j0 + k], ROW_TILES), ROW_TILES), :]
        acc_ref[pl.ds(pl.multiple_of(t * ROW_TILES, ROW_TILES), ROW_TILES), :] = acc
        return carry

    lax.fori_loop(0, tm, token_body, 0, unroll=2)
    routed = jnp.concatenate([acc_ref[pl.ds(s, tm, stride=ROW_TILES), :] for s in range(ROW_TILES)], axis=1)
    out_ref[0] = _rms(base_ref[0] + g2_ref[0] * routed, fg_ref[...])


def _combine(lp_flat, w_flat, tcnt, runb, pstart, ys, base, g2, fg, tm):
    b, l, _ = base.shape
    nt = l // tm
    flat = lambda bi, i: (bi * nt + i,)
    following = lambda bi, i: (jnp.minimum(bi * nt + i + 1, b * nt - 1),)
    smem_blk = pl.BlockSpec((tm * TOP_K,), flat, memory_space=pltpu.SMEM)
    per_tile = pl.BlockSpec((N_EXPERTS,), flat, memory_space=pltpu.SMEM)
    next_tile = pl.BlockSpec((N_EXPERTS,), following, memory_space=pltpu.SMEM)
    return pl.pallas_call(
        functools.partial(_combine_kernel, tm=tm),
        grid=(b, nt),
        in_specs=[smem_blk, smem_blk, per_tile, per_tile, next_tile, next_tile,
                  pl.BlockSpec((N_EXPERTS,), lambda bi, i: (0,), memory_space=pltpu.SMEM),
                  pl.BlockSpec(memory_space=pl.ANY),
                  pl.BlockSpec((1, tm, D_MODEL), lambda bi, i: (bi, i, 0)),
                  pl.BlockSpec((1, 1, D_MODEL), lambda bi, i: (bi, 0, 0)),
                  pl.BlockSpec((1, D_MODEL), lambda bi, i: (0, 0))],
        out_specs=pl.BlockSpec((1, tm, D_MODEL), lambda bi, i: (bi, i, 0)),
        out_shape=jax.ShapeDtypeStruct((b, l, D_MODEL), F32),
        scratch_shapes=[pltpu.VMEM((2, tm * TOP_K * ROW_TILES, LANES), F32),
                        pltpu.VMEM((tm * ROW_TILES, LANES), F32),
                        pltpu.SemaphoreType.DMA((2,))],
        compiler_params=_cparams(("arbitrary", "arbitrary")),
        name="combine",
    )(lp_flat, w_flat, tcnt, runb, tcnt, runb, pstart, ys, base, g2, fg)


def _mixer_inputs(h, shift, scale, gain, wm, wl, wa, ba):
    return _proj(h, shift, scale, gain, wm, wl, wa, ba)


def kernel(x, c, ctx, c_ctx, ada_w, ada_b, norm1_g, norm2_g, w_in, gla_wa_f, gla_ba_f, gla_wa_b, gla_ba_b, gla_norm_g, s5_lam_re_f, s5_lam_im_f, s5_log_step_f, s5_lam_re_b, s5_lam_im_b, s5_log_step_b, s5_b_re, s5_b_im, s5_c_re_f, s5_c_im_f, s5_c_re_b, s5_c_im_b, s5_d, s5_glu_w, s5_glu_b, w_out, router_w, router_b, exp_w_gate, exp_w_up, exp_w_down, sh_w_gate, sh_w_up, sh_w_down, final_norm_g):
    b, l, d = x.shape
    i = 0

    rows = -(-(b + 1) // SUBLANES) * SUBLANES
    cs = jnp.zeros((rows, d), F32).at[:b].set(c).at[b].set(c_ctx)
    mod = _adaln(cs, ada_w[i], ada_b[i][None, :])
    sh1, sc1, g1, sh2, sc2, g2 = [mod[:b, d * j:d * (j + 1)][:, None, :] for j in range(6)]
    csh1, csc1 = [jnp.broadcast_to(mod[b, d * j:d * (j + 1)][None, None, :], (b, 1, d)) for j in range(2)]

    w = w_in[i]
    o1, o2, o3, o4, o5, o6 = 256, 512, 1024, 1536, 1552, 1568
    wm = jnp.concatenate([w[:, :o4], w[:, o6:]], axis=1).astype(BF16)
    wl = jnp.zeros((d, LANES), F32).at[:, :2 * GLA_GATE_RANK].set(w[:, o4:o6]).astype(BF16)
    wa = jnp.zeros((LANES, 2 * GLA_DK), F32)
    wa = wa.at[:GLA_GATE_RANK, :GLA_DK].set(gla_wa_f[i]).at[GLA_GATE_RANK:2 * GLA_GATE_RANK, GLA_DK:].set(gla_wa_b[i])
    wa = wa.astype(BF16)
    ba = jnp.concatenate([gla_ba_f[i], gla_ba_b[i]])[None, :]
    n1 = norm1_g[i][None, :]

    pcols = jnp.stack([s5_lam_re_f[i], s5_lam_im_f[i],
                       jnp.broadcast_to(s5_log_step_f[i][:, None], (S5_GROUPS, S5_STATE)),
                       s5_lam_re_b[i], s5_lam_im_b[i],
                       jnp.broadcast_to(s5_log_step_b[i][:, None], (S5_GROUPS, S5_STATE)),
                       jnp.zeros((S5_GROUPS, S5_STATE), F32), jnp.zeros((S5_GROUPS, S5_STATE), F32)], axis=-1)
    prows = pcols.transpose(0, 2, 1)
    tile_b = lambda t: jnp.tile(t, (1, 1, S5_CHUNK))
    tile_c = lambda t: jnp.tile(t.transpose(0, 2, 1), (1, 1, S5_CHUNK))
    m_op, wt_op, v_op, ab4 = _s5gen(pcols, prows, tile_b(s5_b_re[i]), tile_b(s5_b_im[i]),
                                    tile_c(s5_c_re_f[i]), tile_c(s5_c_im_f[i]),
                                    tile_c(s5_c_re_b[i]), tile_c(s5_c_im_b[i]))
    ab = ab4.reshape(S5_GROUPS, 1, 4 * S5_STATE)

    cq, ck, cv, _, _, claf, clab, cuv = _proj(ctx, csh1, csc1, n1, wm, wl, wa, ba)
    zero_state = jnp.zeros((b, GLA_DV, GLA_DK), F32)
    gsf, gsb = _gla(cq, ck, cv, claf, clab, zero_state, zero_state, with_output=False)
    (x0,) = _s5(cuv, None, wt_op, None, ab, jnp.zeros((S5_GROUPS, b, S5_VEC), F32), b, with_output=False)

    q, k, v, go, u, laf, lab, uv = _proj(x, sh1, sc1, n1, wm, wl, wa, ba)
    o, _, _ = _gla(q, k, v, laf, lab, gsf, gsb, with_output=True)
    yvec, _ = _s5(uv, m_op, wt_op, v_op, ab, x0, b, with_output=True)

    rw_hi = router_w[i].astype(BF16)
    rw_lo = (router_w[i] - rw_hi.astype(F32)).astype(BF16)
    base, hrows, logits = _post(
        x, o, go, u, yvec, g1, sh2, sc2, g2,
        gla_norm_g[i][None, :], s5_d[i][None, :], s5_glu_w[i].astype(BF16),
        s5_glu_b[i][None, :], w_out[i].astype(BF16), norm2_g[i][None, :], rw_hi, rw_lo,
        jnp.concatenate([sh_w_gate[i], sh_w_up[i]], axis=1).astype(BF16), sh_w_down[i].astype(BF16))

    t = b * l
    tile = min(512, l)
    wts, lpos, runb, tcnt, cnt = _route(logits.reshape(t, N_EXPERTS), router_b[i][:, None], tile)
    cnt_flat = cnt[:, 0]
    runb_flat, tcnt_flat = runb[:, :, 0].reshape(-1), tcnt[:, :, 0].reshape(-1)
    nbp = -(-_n_blocks_max(t * TOP_K) // SUBLANES) * SUBLANES
    pstart, blk, nv = _plan(cnt_flat[None, :], nbp)
    lp_flat, w_flat = lpos.T.reshape(-1), wts.T.reshape(-1)
    ps_flat = pstart.reshape(-1)
    xs = _dispatch(lp_flat, tcnt_flat, runb_flat, ps_flat, cnt_flat, nv.reshape(-1), hrows, nbp, tile)
    assert exp_w_gate.shape[0] == 1, "single-layer block"
    ys = _experts(blk.reshape(-1), nv.reshape(-1), xs, exp_w_gate.reshape(exp_w_gate.shape[1:]),
                  exp_w_up.reshape(exp_w_up.shape[1:]), exp_w_down.reshape(exp_w_down.shape[1:]), nbp)
    return _combine(lp_flat, w_flat, tcnt_flat, runb_flat, ps_flat, ys, base, g2, final_norm_g[None, :], tile)
```

```python
import functools

import jax
import jax.numpy as jnp
from jax import lax
from jax.experimental import pallas as pl
from jax.experimental.pallas import tpu as pltpu

F32 = jnp.float32
BF16 = jnp.bfloat16
I32 = jnp.int32

D_MODEL = 1024
GLA_HEADS = 4
GLA_DK_HEAD = 64
GLA_DV_HEAD = 128
GLA_DK = 256
GLA_DV = 512
GLA_GATE_RANK = 16
GLA_GATE_TAU = 16.0
GLA_CHUNK = 64
D_S5 = 512
S5_GROUP_CH = 16
S5_GROUPS = 32
S5_STATE = 64
S5_CHUNK = 16
S5_VEC = S5_CHUNK * S5_GROUP_CH
N_EXPERTS = 256
TOP_K = 8
N_EXPERT_GROUPS = 8
TOPK_GROUPS = 4
D_EXPERT = 256
D_SHARED = 256
ROUTE_SCALE = 2.5
EPS = 1e-6

LANES = 128
SUBLANES = 8
ROW_TILES = D_MODEL // LANES
EXPERT_BLOCK = 128
EXPERT_BLOCK_LOG2 = 7
VMEM_LIMIT = 56 * 1024 * 1024


def _cparams(sem):
    return pltpu.CompilerParams(dimension_semantics=sem, vmem_limit_bytes=VMEM_LIMIT)


def _dot(a, b):
    return jnp.dot(a, b, preferred_element_type=F32)


def _dot_nt(a, b):
    return lax.dot_general(a, b, (((1,), (1,)), ((), ())), preferred_element_type=F32)


def _dot_tn(a, b):
    return lax.dot_general(a, b, (((0,), (0,)), ((), ())), preferred_element_type=F32)


def _split2(x):
    hi = x.astype(BF16)
    lo = (x - hi.astype(F32)).astype(BF16)
    return hi, lo


def _dot3(a, b_hi, b_lo):
    a_hi, a_lo = _split2(a)
    return _dot(a_hi, b_hi) + (_dot(a_hi, b_lo) + _dot(a_lo, b_hi))


def _silu(x):
    return x * jax.nn.sigmoid(x)


def _rms(x, g):
    return x * lax.rsqrt(jnp.mean(x * x, axis=-1, keepdims=True) + EPS) * g


def _adaln_kernel(c_ref, w_ref, b_ref, o_ref):
    s = _silu(c_ref[...])
    w_hi, w_lo = _split2(w_ref[...])
    o_ref[...] = _dot3(s, w_hi, w_lo) + b_ref[...]


def _adaln(cs, w, b):
    rows, n = cs.shape[0], w.shape[1]
    tn = 1024
    return pl.pallas_call(
        _adaln_kernel,
        grid=(n // tn,),
        in_specs=[pl.BlockSpec((rows, D_MODEL), lambda j: (0, 0)),
                  pl.BlockSpec((D_MODEL, tn), lambda j: (0, j)),
                  pl.BlockSpec((1, tn), lambda j: (0, j))],
        out_specs=pl.BlockSpec((rows, tn), lambda j: (0, j)),
        out_shape=jax.ShapeDtypeStruct((rows, n), F32),
        compiler_params=_cparams(("arbitrary",)),
        name="adaln",
    )(cs, w, b)


def _group_lane_masks(rows):
    grp = lax.shift_right_logical(lax.broadcasted_iota(I32, (rows, LANES), 1), 4)
    return [grp == j for j in range(LANES // S5_GROUP_CH)]


def _move_group(x, src, dst):
    shift = ((dst - src) * S5_GROUP_CH) % LANES
    return pltpu.roll(x, shift, 1) if shift else x


def _proj_kernel(x_ref, sh_ref, sc_ref, g_ref, wm_ref, wl_ref, wa_ref, ba_ref,
                 q_ref, k_ref, v_ref, go_ref, u_ref, laf_ref, lab_ref, uv_ref, ut_ref, *, tm):
    h = _rms(x_ref[0], g_ref[...]) * (1.0 + sc_ref[0]) + sh_ref[0]
    hb = h.astype(BF16)
    q_ref[0] = _dot(hb, wm_ref[:, 0:256]) * (GLA_DK_HEAD ** -0.5)
    k_ref[0] = _dot(hb, wm_ref[:, 256:512])
    v_ref[0] = _dot(hb, wm_ref[:, 512:1024])
    go_ref[0] = _dot(hb, wm_ref[:, 1024:1536])
    u = _dot(hb, wm_ref[:, 1536:2048])
    u_ref[0] = u
    for t in range(D_S5 // LANES):
        ut_ref[t] = u[:, LANES * t:LANES * (t + 1)]
    lr = _dot(hb, wl_ref[...])
    pre = _dot(lr.astype(BF16), wa_ref[...]) + ba_ref[...]
    la = (jnp.minimum(pre, 0.0) - jnp.log1p(jnp.exp(-jnp.abs(pre)))) * (1.0 / GLA_GATE_TAU)
    laf_ref[0] = la[:, 0:GLA_DK]
    lab_ref[0] = la[:, GLA_DK:2 * GLA_DK]
    nc = tm // S5_CHUNK
    gpt = LANES // S5_GROUP_CH
    masks = _group_lane_masks(nc)
    for t in range(D_S5 // LANES):
        steps = [ut_ref[t, pl.ds(s, nc, stride=S5_CHUNK), :] for s in range(S5_CHUNK)]
        for gl in range(gpt):
            for half in range(S5_VEC // LANES):
                acc = None
                for j in range(gpt):
                    piece = _move_group(steps[half * gpt + j], gl, j)
                    acc = piece if acc is None else jnp.where(masks[j], piece, acc)
                uv_ref[t * gpt + gl, :, LANES * half:LANES * (half + 1)] = acc.astype(BF16)


def _proj(x, shift, scale, gain, wm, wl, wa, ba):
    b, l, _ = x.shape
    tm = min(512, l)
    nt = l // tm
    row = lambda bi, i: (bi, i, 0)
    mod = lambda bi, i: (bi, 0, 0)
    full = lambda bi, i: (0, 0)
    widths = (GLA_DK, GLA_DK, GLA_DV, GLA_DV, D_S5, GLA_DK, GLA_DK)
    return pl.pallas_call(
        functools.partial(_proj_kernel, tm=tm),
        grid=(b, nt),
        in_specs=[pl.BlockSpec((1, tm, D_MODEL), row),
                  pl.BlockSpec((1, 1, D_MODEL), mod),
                  pl.BlockSpec((1, 1, D_MODEL), mod),
                  pl.BlockSpec((1, D_MODEL), full),
                  pl.BlockSpec(wm.shape, full),
                  pl.BlockSpec(wl.shape, full),
                  pl.BlockSpec(wa.shape, full),
                  pl.BlockSpec(ba.shape, full)],
        out_specs=[pl.BlockSpec((1, tm, w), row) for w in widths]
                  + [pl.BlockSpec((S5_GROUPS, tm // S5_CHUNK, S5_VEC), lambda bi, i: (0, bi * nt + i, 0))],
        out_shape=[jax.ShapeDtypeStruct((b, l, w), F32) for w in widths]
                  + [jax.ShapeDtypeStruct((S5_GROUPS, b * l // S5_CHUNK, S5_VEC), BF16)],
        scratch_shapes=[pltpu.VMEM((D_S5 // LANES, tm, LANES), F32)],
        compiler_params=_cparams(("arbitrary", "arbitrary")),
        name="proj",
    )(x, shift, scale, gain, wm, wl, wa, ba)


def _gla_kernel(*refs, n_chunks, with_output):
    if with_output:
        q_ref, k_ref, v_ref, laf_ref, lab_ref, s0f_ref, s0b_ref, o_ref, sf_ref, sb_ref, st_ref, ob_ref = refs
    else:
        q_ref, k_ref, v_ref, laf_ref, lab_ref, s0f_ref, s0b_ref, sf_ref, sb_ref, st_ref = refs
        o_ref = ob_ref = None
    c = GLA_CHUNK
    row2 = lax.broadcasted_iota(I32, (c, 2 * c), 0)
    col2 = lax.broadcasted_iota(I32, (c, 2 * c), 1) & (c - 1)
    rowk = lax.broadcasted_iota(I32, (c, GLA_DK), 0)
    colk = lax.broadcasted_iota(I32, (c, GLA_DK), 1)
    k_head = lax.shift_right_logical(colk, 6)
    v_head = lax.shift_right_logical(lax.broadcasted_iota(I32, (c, GLA_DV), 1), 7)
    st_own = (lax.shift_right_logical(lax.broadcasted_iota(I32, (GLA_DV, GLA_DK), 0), 7)
              == lax.shift_right_logical(lax.broadcasted_iota(I32, (GLA_DV, GLA_DK), 1), 6))
    st_ref[0] = s0f_ref[0]
    st_ref[1] = s0b_ref[0]

    def chunk(direction, idx):
        la_ref = laf_ref if direction == 0 else lab_ref
        keep2 = (row2 >= col2) if direction == 0 else (row2 <= col2)
        keep4 = (rowk >= (colk & (c - 1))) if direction == 0 else (rowk <= (colk & (c - 1)))
        r0 = pl.multiple_of(idx * c, c)
        q = q_ref[0, pl.ds(r0, c), :]
        k = k_ref[0, pl.ds(r0, c), :]
        v = v_ref[0, pl.ds(r0, c), :]
        la_hi, la_lo = _split2(la_ref[0, pl.ds(r0, c), :])
        cum = _dot(jnp.where(keep2, 1.0, 0.0).astype(BF16), jnp.concatenate([la_hi, la_lo], axis=0))
        tot = cum[c - 1:c, :] if direction == 0 else cum[0:1, :]
        qb = (q * jnp.exp(cum)).astype(BF16)
        ki = k * jnp.exp(-cum)
        ks = k * jnp.exp(tot - cum)
        dec = jnp.exp(tot)
        vb = v.astype(BF16)
        st = st_ref[direction]
        if with_output:
            kstack = jnp.concatenate([jnp.where(k_head == h, ki, 0.0) for h in range(GLA_HEADS)], axis=0).astype(BF16)
            vstack = jnp.concatenate([jnp.where(v_head == h, v, 0.0) for h in range(GLA_HEADS)], axis=0).astype(BF16)
            sc = jnp.where(keep4, _dot_nt(qb, kstack), 0.0)
            o = _dot(sc.astype(BF16), vstack) + _dot_nt(qb, st.astype(BF16))
            if direction == 0:
                o_ref[0, pl.ds(r0, c), :] = o
            else:
                ob_ref[pl.ds(r0, c), :] = o
        st_ref[direction] = st * dec + jnp.where(st_own, _dot_tn(vb, ks.astype(BF16)), 0.0)

    def body(ci, carry):
        chunk(0, ci)
        chunk(1, n_chunks - 1 - ci)
        return carry

    lax.fori_loop(0, n_chunks, body, 0, unroll=2)
    sf_ref[0] = st_ref[0]
    sb_ref[0] = st_ref[1]
    if with_output:
        o_ref[0] = o_ref[0] + ob_ref[...]


def _gla(q, k, v, laf, lab, s0f, s0b, with_output):
    b, l, _ = q.shape
    n_chunks = l // GLA_CHUNK
    seq = lambda bi: (bi, 0, 0)
    st_shape = (b, GLA_DV, GLA_DK)
    st_spec = pl.BlockSpec((1, GLA_DV, GLA_DK), seq)
    out_specs = [st_spec, st_spec]
    out_shape = [jax.ShapeDtypeStruct(st_shape, F32)] * 2
    if with_output:
        out_specs = [pl.BlockSpec((1, l, GLA_DV), seq)] + out_specs
        out_shape = [jax.ShapeDtypeStruct((b, l, GLA_DV), F32)] + out_shape
    return pl.pallas_call(
        functools.partial(_gla_kernel, n_chunks=n_chunks, with_output=with_output),
        grid=(b,),
        in_specs=[pl.BlockSpec((1, l, GLA_DK), seq),
                  pl.BlockSpec((1, l, GLA_DK), seq),
                  pl.BlockSpec((1, l, GLA_DV), seq),
                  pl.BlockSpec((1, l, GLA_DK), seq),
                  pl.BlockSpec((1, l, GLA_DK), seq),
                  st_spec, st_spec],
        out_specs=out_specs,
        out_shape=out_shape,
        scratch_shapes=[pltpu.VMEM((2, GLA_DV, GLA_DK), F32)]
                       + ([pltpu.VMEM((l, GLA_DV), F32)] if with_output else []),
        compiler_params=_cparams(("arbitrary",)),
        name="gla_out" if with_output else "gla_ctx",
    )(q, k, v, laf, lab, s0f, s0b)


def _s5gen_kernel(pc_ref, pr_ref, btr_ref, bti_ref, ctrf_ref, ctif_ref, ctrb_ref, ctib_ref,
                  m_ref, wt_ref, v_ref, ab_ref):
    pc = pc_ref[0]
    blk = lax.shift_right_logical(lax.broadcasted_iota(I32, (1, S5_VEC), 1), 4).astype(F32)
    lane = lax.broadcasted_iota(I32, (S5_GROUP_CH, S5_VEC), 1)
    n = float(S5_CHUNK)

    def cmul(ar, ai, br, bi):
        return ar * br - ai * bi, ar * bi + ai * br

    kcat = []
    for d in (0, 1):
        lre, lim, ls = pc[:, 3 * d:3 * d + 1], pc[:, 3 * d + 1:3 * d + 2], pc[:, 3 * d + 2:3 * d + 3]
        ctr = (ctrf_ref if d == 0 else ctrb_ref)[0]
        cti = (ctif_ref if d == 0 else ctib_ref)[0]
        step = jnp.exp(ls)
        mag = jnp.exp(lre * step)
        a_re = mag * jnp.cos(lim * step)
        a_im = mag * jnp.sin(lim * step)
        den = lre * lre + lim * lim
        f_re = ((a_re - 1.0) * lre + a_im * lim) / den
        f_im = (a_im * lre - (a_re - 1.0) * lim) / den
        bb_re, bb_im = cmul(f_re, f_im, btr_ref[0], bti_ref[0])

        def powers(e, lre=lre, lim=lim, step=step):
            m = jnp.exp(lre * step * e)
            ang = lim * step * e
            return m * jnp.cos(ang), m * jnp.sin(ang)

        w_re, w_im = cmul(*powers((n - 1.0 - blk) if d == 0 else blk), bb_re, bb_im)
        wt_ref[0, S5_STATE * d:S5_STATE * (d + 1), :] = w_re
        wt_ref[0, 2 * S5_STATE + S5_STATE * d:2 * S5_STATE + S5_STATE * (d + 1), :] = w_im
        c_re, c_im = cmul(*powers((blk + 1.0) if d == 0 else (n - blk)), ctr, cti)
        v_ref[0, S5_STATE * d:S5_STATE * (d + 1), :] = c_re
        v_ref[0, 2 * S5_STATE + S5_STATE * d:2 * S5_STATE + S5_STATE * (d + 1), :] = -c_im
        e_re, e_im = cmul(*powers(blk if d == 0 else (n - 1.0 - blk)), ctr, cti)
        b16r_hi, b16r_lo = _split2(bb_re[:, 0:S5_GROUP_CH])
        b16i_hi, b16i_lo = _split2(bb_im[:, 0:S5_GROUP_CH])
        er_hi, er_lo = _split2(e_re)
        ei_hi, ei_lo = _split2(e_im)
        kr = _dot_tn(b16r_hi, er_hi) + (_dot_tn(b16r_hi, er_lo) + _dot_tn(b16r_lo, er_hi))
        ki = _dot_tn(b16i_hi, ei_hi) + (_dot_tn(b16i_hi, ei_lo) + _dot_tn(b16i_lo, ei_hi))
        kcat.append(kr - ki)

    for s in range(S5_CHUNK):
        sh_f = S5_GROUP_CH * s
        fwd = kcat[0] if s == 0 else pltpu.roll(kcat[0], sh_f, 1)
        fwd = jnp.where(lane >= sh_f, fwd, 0.0)
        sh_b = S5_VEC - S5_GROUP_CH * (S5_CHUNK - 1 - s)
        bwd = kcat[1] if sh_b == S5_VEC else pltpu.roll(kcat[1], sh_b, 1)
        bwd = jnp.where(lane < S5_GROUP_CH * (s + 1), bwd, 0.0)
        m_ref[0, S5_GROUP_CH * s:S5_GROUP_CH * (s + 1), :] = fwd + bwd

    pr = pr_ref[0]
    for d in (0, 1):
        lre, lim, ls = pr[3 * d:3 * d + 1, :], pr[3 * d + 1:3 * d + 2, :], pr[3 * d + 2:3 * d + 3, :]
        stp = jnp.exp(ls) * n
        mg = jnp.exp(lre * stp)
        ab_ref[0, d:d + 1, :] = mg * jnp.cos(lim * stp)
        ab_ref[0, 2 + d:3 + d, :] = mg * jnp.sin(lim * stp)


def _s5gen(pc, pr, btr, bti, ctrf, ctif, ctrb, ctib):
    g = pc.shape[0]
    blk3 = lambda shape: pl.BlockSpec((1,) + shape, lambda i: (i, 0, 0))
    big = (S5_STATE, S5_VEC)
    sq = (S5_VEC, S5_VEC)
    return pl.pallas_call(
        _s5gen_kernel,
        grid=(g,),
        in_specs=[blk3((S5_STATE, 8)), blk3((8, S5_STATE))] + [blk3(big)] * 6,
        out_specs=[blk3(sq), blk3(sq), blk3(sq), blk3((4, S5_STATE))],
        out_shape=[jax.ShapeDtypeStruct((g,) + sq, F32)] * 3 + [jax.ShapeDtypeStruct((g, 4, S5_STATE), F32)],
        compiler_params=_cparams(("arbitrary",)),
        name="s5gen",
    )(pc, pr, btr, bti, ctrf, ctif, ctrb, ctib)


def _s5_kernel(*refs, n_chunks, nb, with_output):
    if with_output:
        u_ref, m_ref, wt_ref, v_ref, ab_ref, x0_ref, y_ref, xf_ref, z_ref, cin_ref = refs
    else:
        u_ref, wt_ref, ab_ref, x0_ref, xf_ref, z_ref = refs
    wtb = wt_ref[0].astype(BF16)
    for bi in range(nb):
        z = _dot_nt(u_ref[0, bi * n_chunks:(bi + 1) * n_chunks, :], wtb)
        z_ref[0, pl.ds(bi, n_chunks, stride=nb), :] = z[:, 0:LANES]
        z_ref[1, pl.ds(bi, n_chunks, stride=nb), :] = z[:, LANES:2 * LANES]
    ab = ab_ref[0]
    ar, ai = ab[:, 0:LANES], ab[:, LANES:2 * LANES]
    is_f = lax.broadcasted_iota(I32, (nb, LANES), 1) < S5_STATE
    x0 = x0_ref[0]

    def body(i, carry):
        xr, xi = carry
        rf = pl.multiple_of(i * nb, nb)
        rb = pl.multiple_of((n_chunks - 1 - i) * nb, nb)
        if with_output:
            cin_ref[0, pl.ds(rf, nb), 0:S5_STATE] = xr[:, 0:S5_STATE]
            cin_ref[1, pl.ds(rf, nb), 0:S5_STATE] = xi[:, 0:S5_STATE]
            cin_ref[0, pl.ds(rb, nb), S5_STATE:LANES] = xr[:, S5_STATE:LANES]
            cin_ref[1, pl.ds(rb, nb), S5_STATE:LANES] = xi[:, S5_STATE:LANES]
        zr = jnp.where(is_f, z_ref[0, pl.ds(rf, nb), :], z_ref[0, pl.ds(rb, nb), :])
        zi = jnp.where(is_f, z_ref[1, pl.ds(rf, nb), :], z_ref[1, pl.ds(rb, nb), :])
        return ar * xr - ai * xi + zr, ar * xi + ai * xr + zi

    xr, xi = lax.fori_loop(0, n_chunks, body, (x0[:, 0:LANES], x0[:, LANES:2 * LANES]), unroll=2)
    xf_ref[0, :, 0:LANES] = xr
    xf_ref[0, :, LANES:2 * LANES] = xi
    if with_output:
        mb = m_ref[0].astype(BF16)
        vb = v_ref[0].astype(BF16)
        for bi in range(nb):
            rows = slice(bi * n_chunks, (bi + 1) * n_chunks)
            carried = jnp.concatenate([cin_ref[0, pl.ds(bi, n_chunks, stride=nb), :],
                                       cin_ref[1, pl.ds(bi, n_chunks, stride=nb), :]], axis=1).astype(BF16)
            y_ref[0, rows, :] = _dot(u_ref[0, rows, :], mb) + _dot(carried, vb)


def _s5(uvec, m, wt, v, ab, x0, nb, with_output):
    g, rows, _ = uvec.shape
    n_chunks = rows // nb
    blk3 = lambda shape: pl.BlockSpec((1,) + shape, lambda i: (i, 0, 0))
    sq = (S5_VEC, S5_VEC)
    st = (nb, S5_VEC)
    if with_output:
        args = (uvec, m, wt, v, ab, x0)
        in_specs = [blk3((rows, S5_VEC)), blk3(sq), blk3(sq), blk3(sq), blk3((1, S5_VEC)), blk3(st)]
        out_specs = [blk3((rows, S5_VEC)), blk3(st)]
        out_shape = [jax.ShapeDtypeStruct((g, rows, S5_VEC), F32), jax.ShapeDtypeStruct((g,) + st, F32)]
        scratch = [pltpu.VMEM((S5_VEC // LANES, rows, LANES), F32), pltpu.VMEM((S5_VEC // LANES, rows, LANES), F32)]
    else:
        args = (uvec, wt, ab, x0)
        in_specs = [blk3((rows, S5_VEC)), blk3(sq), blk3((1, S5_VEC)), blk3(st)]
        out_specs = [blk3(st)]
        out_shape = [jax.ShapeDtypeStruct((g,) + st, F32)]
        scratch = [pltpu.VMEM((S5_VEC // LANES, rows, LANES), F32)]
    return pl.pallas_call(
        functools.partial(_s5_kernel, n_chunks=n_chunks, nb=nb, with_output=with_output),
        grid=(g,),
        in_specs=in_specs,
        out_specs=out_specs,
        out_shape=out_shape,
        scratch_shapes=scratch,
        compiler_params=_cparams(("arbitrary",)),
        name="s5_out" if with_output else "s5_ctx",
    )(*args)


def _post_kernel(x_ref, o_ref, go_ref, u_ref, yv_ref, g1_ref, sh2_ref, sc2_ref, g2_ref,
                 gn_ref, d_ref, gw_ref, gb_ref, wo_ref, n2_ref, rwh_ref, rwl_ref, sgu_ref, sd_ref,
                 base_ref, hrow_ref, lg_ref, y_ref, *, tm):
    nc = tm // S5_CHUNK
    gpt = LANES // S5_GROUP_CH
    masks = _group_lane_masks(nc)
    for s in range(S5_CHUNK):
        half, j = divmod(s, gpt)
        for t in range(D_S5 // LANES):
            acc = None
            for gl in range(gpt):
                piece = _move_group(yv_ref[t * gpt + gl, :, LANES * half:LANES * (half + 1)], j, gl)
                acc = piece if acc is None else jnp.where(masks[gl], piece, acc)
            y_ref[t, pl.ds(s, nc, stride=S5_CHUNK), :] = acc
    o = o_ref[0]
    gn = gn_ref[...]
    heads = [_rms(o[:, GLA_DV_HEAD * h:GLA_DV_HEAD * (h + 1)], gn) for h in range(GLA_HEADS)]
    gla_out = jnp.concatenate(heads, axis=1) * _silu(go_ref[0])
    yy = jnp.concatenate([y_ref[t] for t in range(D_S5 // LANES)], axis=1) + d_ref[...] * u_ref[0]
    z = 0.5 * yy * (1.0 + jnp.tanh(0.7978845608028654 * (yy + 0.044715 * (yy * yy * yy))))
    s5_out = z * jax.nn.sigmoid(_dot(z.astype(BF16), gw_ref[...]) + gb_ref[...])
    mix = jnp.concatenate([gla_out, s5_out], axis=1).astype(BF16)
    x1 = x_ref[0] + g1_ref[0] * _dot(mix, wo_ref[...])
    h2 = _rms(x1, n2_ref[...]) * (1.0 + sc2_ref[0]) + sh2_ref[0]
    lg_ref[0] = _dot3(h2, rwh_ref[...], rwl_ref[...])
    hb = h2.astype(BF16)
    gu = _dot(hb, sgu_ref[...])
    hid = _silu(gu[:, 0:D_SHARED]) * gu[:, D_SHARED:2 * D_SHARED]
    base_ref[0] = x1 + g2_ref[0] * _dot(hid.astype(BF16), sd_ref[...])
    for s in range(ROW_TILES):
        hrow_ref[pl.ds(s, tm, stride=ROW_TILES), :] = h2[:, LANES * s:LANES * (s + 1)]


def _post(x, o, go, u, yvec, g1, sh2, sc2, g2, gn, d, gw, gb, wo, n2, rwh, rwl, sgu, sd):
    b, l, _ = x.shape
    tm = min(512, l)
    nt = l // tm
    row = lambda bi, i: (bi, i, 0)
    mod = lambda bi, i: (bi, 0, 0)
    full = lambda bi, i: (0, 0)
    ws = (gn, d, gw, gb, wo, n2, rwh, rwl, sgu, sd)
    return pl.pallas_call(
        functools.partial(_post_kernel, tm=tm),
        grid=(b, nt),
        in_specs=[pl.BlockSpec((1, tm, D_MODEL), row)]
                 + [pl.BlockSpec((1, tm, 512), row)] * 3
                 + [pl.BlockSpec((S5_GROUPS, tm // S5_CHUNK, S5_VEC), lambda bi, i: (0, bi * nt + i, 0))]
                 + [pl.BlockSpec((1, 1, D_MODEL), mod)] * 4
                 + [pl.BlockSpec(w.shape, full) for w in ws],
        out_specs=[pl.BlockSpec((1, tm, D_MODEL), row),
                   pl.BlockSpec((tm * ROW_TILES, LANES), lambda bi, i: (bi * nt + i, 0)),
                   pl.BlockSpec((1, tm, N_EXPERTS), row)],
        out_shape=[jax.ShapeDtypeStruct((b, l, D_MODEL), F32),
                   jax.ShapeDtypeStruct((b * l * ROW_TILES, LANES), F32),
                   jax.ShapeDtypeStruct((b, l, N_EXPERTS), F32)],
        scratch_shapes=[pltpu.VMEM((D_S5 // LANES, tm, LANES), F32)],
        compiler_params=_cparams(("arbitrary", "arbitrary")),
        name="post",
    )(x, o, go, u, yvec, g1, sh2, sc2, g2, *ws)


def _route_kernel(lg_ref, rb_ref, w_ref, p_ref, rb4_ref, tc_ref, cnt_ref, run_ref, *, tm):
    @pl.when(pl.program_id(0) == 0)
    def _():
        run_ref[...] = jnp.zeros_like(run_ref)

    neg = -jnp.inf
    gsz = N_EXPERTS // N_EXPERT_GROUPS
    s = jax.nn.sigmoid(lg_ref[...].T)
    biased = s + rb_ref[...]
    row = lax.broadcasted_iota(I32, (N_EXPERTS, tm), 0).astype(F32)

    def first_max(m, idx):
        mx = jnp.max(m, axis=0, keepdims=True)
        ix = jnp.min(jnp.where(m == mx, idx, float(N_EXPERTS)), axis=0, keepdims=True)
        return mx, ix

    grow = lax.broadcasted_iota(I32, (gsz, tm), 0).astype(F32)
    gs = []
    for g in range(N_EXPERT_GROUPS):
        m, idx = biased[gsz * g:gsz * (g + 1), :], grow + float(gsz * g)
        m1, i1 = first_max(m, idx)
        gs.append(m1 + jnp.max(jnp.where(idx == i1, neg, m), axis=0, keepdims=True))
    kept = []
    for g in range(N_EXPERT_GROUPS):
        ahead = jnp.zeros((1, tm), F32)
        for j in range(N_EXPERT_GROUPS):
            if j < g:
                ahead = ahead + jnp.where(gs[j] >= gs[g], 1.0, 0.0)
            elif j > g:
                ahead = ahead + jnp.where(gs[j] > gs[g], 1.0, 0.0)
        kept.append(jnp.where(ahead < float(TOPK_GROUPS), biased[gsz * g:gsz * (g + 1), :], neg))
    masked = jnp.concatenate(kept, axis=0)

    onehot = jnp.zeros((N_EXPERTS, tm), F32)
    ids, ws = [], []
    for _ in range(TOP_K):
        _, ik = first_max(masked, row)
        hit = row == ik
        ids.append(ik)
        ws.append(jnp.sum(jnp.where(hit, s, 0.0), axis=0, keepdims=True))
        onehot = onehot + jnp.where(hit, 1.0, 0.0)
        masked = jnp.where(hit, neg, masked)
    wsum = ws[0]
    for k in range(1, TOP_K):
        wsum = wsum + ws[k]

    ss = lax.broadcasted_iota(I32, (tm, tm), 0)
    tt = lax.broadcasted_iota(I32, (tm, tm), 1)
    earlier = jnp.where(ss < tt, 1.0, 0.0).astype(BF16)
    ohb = onehot.astype(BF16)
    tcnt = _dot(ohb, jnp.ones((tm, LANES), BF16))
    ee = lax.broadcasted_iota(I32, (N_EXPERTS, N_EXPERTS), 0)
    ff = lax.broadcasted_iota(I32, (N_EXPERTS, N_EXPERTS), 1)
    below = jnp.where(ff < ee, 1.0, 0.0).astype(BF16)
    t_hi, t_lo = _split2(tcnt)
    toff = _dot(below, t_hi) + _dot(below, t_lo)
    lpos = (_dot(ohb, earlier) + toff[:, 0:1]) * float(ROW_TILES)
    w_ref[...] = jnp.concatenate([w / wsum * ROUTE_SCALE for w in ws], axis=0)
    p_ref[...] = jnp.concatenate(
        [jnp.sum(jnp.where(row == ids[k], lpos, 0.0), axis=0, keepdims=True) for k in range(TOP_K)], axis=0).astype(I32)
    run = run_ref[...]
    rb4_ref[0] = run.astype(I32)
    tc_ref[0] = tcnt.astype(I32)
    run = run + tcnt
    run_ref[...] = run
    cnt_ref[...] = run.astype(I32)


def _route(logits, rb, tm):
    t = logits.shape[0]
    col = lambda i: (0, i)
    fixed = lambda i: (0, 0)
    tile = lambda i: (i, 0, 0)
    per_tile = jax.ShapeDtypeStruct((t // tm, N_EXPERTS, LANES), I32)
    return pl.pallas_call(
        functools.partial(_route_kernel, tm=tm),
        grid=(t // tm,),
        in_specs=[pl.BlockSpec((tm, N_EXPERTS), lambda i: (i, 0)), pl.BlockSpec((N_EXPERTS, 1), fixed)],
        out_specs=[pl.BlockSpec((TOP_K, tm), col)] * 2 + [pl.BlockSpec((1, N_EXPERTS, LANES), tile)] * 2
                  + [pl.BlockSpec((N_EXPERTS, LANES), fixed)],
        out_shape=[jax.ShapeDtypeStruct((TOP_K, t), F32), jax.ShapeDtypeStruct((TOP_K, t), I32), per_tile, per_tile,
                   jax.ShapeDtypeStruct((N_EXPERTS, LANES), I32)],
        scratch_shapes=[pltpu.VMEM((N_EXPERTS, LANES), F32)],
        compiler_params=_cparams(("arbitrary",)),
        name="route",
    )(logits, rb)


def _n_blocks_max(n_assign):
    return -(-(n_assign + N_EXPERTS * (EXPERT_BLOCK - 1)) // EXPERT_BLOCK)


def _plan_kernel(cnt_ref, ps_ref, blk_ref, nv_ref, *, nbp):
    cnt = cnt_ref[...]
    nb = lax.shift_right_logical(cnt + (EXPERT_BLOCK - 1), EXPERT_BLOCK_LOG2).astype(F32)
    nb8 = jnp.broadcast_to(nb, (SUBLANES, N_EXPERTS))
    nb_hi, nb_lo = _split2(nb8)
    ii = lax.broadcasted_iota(I32, (N_EXPERTS, N_EXPERTS), 0)
    jj = lax.broadcasted_iota(I32, (N_EXPERTS, N_EXPERTS), 1)
    upto = jnp.where(ii <= jj, 1.0, 0.0).astype(BF16)
    cum = (_dot(nb_hi, upto) + _dot(nb_lo, upto))[0:1, :]
    ps_ref[...] = ((cum - nb) * float(EXPERT_BLOCK)).astype(I32)
    bi = lax.broadcasted_iota(I32, (nbp, N_EXPERTS), 0).astype(F32)
    owner = jnp.sum(jnp.where(cum <= bi, 1.0, 0.0), axis=-1, keepdims=True)
    blk_ref[...] = jnp.minimum(owner, float(N_EXPERTS - 1)).astype(I32)
    nv_ref[...] = cum[:, N_EXPERTS - 1:N_EXPERTS].astype(I32)


def _plan(cnt, nbp):
    return pl.pallas_call(
        functools.partial(_plan_kernel, nbp=nbp),
        out_shape=[jax.ShapeDtypeStruct((1, N_EXPERTS), I32), jax.ShapeDtypeStruct((nbp, 1), I32),
                   jax.ShapeDtypeStruct((1, 1), I32)],
        name="plan",
    )(cnt)


def _segment_copies(tc_ref, rb_ref, ps_ref, make):
    def body(pair, local):
        for j in range(2):
            e = 2 * pair + j
            cnt = tc_ref[e]

            @pl.when(cnt > 0)
            def _(e=e, local=local, cnt=cnt, j=j):
                make(local, ps_ref[e] + rb_ref[e], cnt).start(priority=j)

            local = local + cnt
        return local

    lax.fori_loop(0, N_EXPERTS // 2, body, 0)


def _rows(ref, row, n):
    return ref.at[pl.ds(pl.multiple_of(row * ROW_TILES, ROW_TILES), n * ROW_TILES)]


def _dispatch_kernel(lp_ref, tc_ref, rb_ref, ps_ref, cnt_ref, nv_ref, h_ref, xs_ref, sorted_ref, zero_ref, sems, zsem,
                     *, tm, nbp):
    step = pl.program_id(0)

    @pl.when(step == 0)
    def _():
        zero_ref[...] = jnp.zeros_like(zero_ref)
        for wait in (False, True):
            def tail_body(bi, carry, wait=wait):
                cp = pltpu.make_async_copy(zero_ref, _rows(xs_ref, bi * EXPERT_BLOCK, EXPERT_BLOCK), zsem)
                cp.wait() if wait else cp.start()
                return carry
            lax.fori_loop(nv_ref[0], nbp, tail_body, 0)

            def pad_body(e, carry, wait=wait):
                cnt = cnt_ref[e]
                pad = (-cnt) & (EXPERT_BLOCK - 1)

                @pl.when(pad > 0)
                def _():
                    cp = pltpu.make_async_copy(_rows(zero_ref, 0, pad), _rows(xs_ref, ps_ref[e] + cnt, pad), zsem)
                    cp.wait() if wait else cp.start()

                return carry
            lax.fori_loop(0, N_EXPERTS, pad_body, 0)

    slot = step % 2
    mine = sorted_ref.at[slot]

    def token_body(t, carry):
        row = h_ref[pl.ds(pl.multiple_of(t * ROW_TILES, ROW_TILES), ROW_TILES), :]
        for k in range(TOP_K):
            mine[pl.ds(pl.multiple_of(lp_ref[t * TOP_K + k], ROW_TILES), ROW_TILES), :] = row
        return carry

    lax.fori_loop(0, tm, token_body, 0, unroll=2)
    _segment_copies(tc_ref, rb_ref, ps_ref,
                    lambda loc, glob, n: pltpu.make_async_copy(_rows(mine, loc, n), _rows(xs_ref, glob, n), sems.at[slot]))

    def drain(s):
        pltpu.make_async_copy(sorted_ref.at[s], _rows(xs_ref, 0, tm * TOP_K), sems.at[s]).wait()

    @pl.when(step > 0)
    def _():
        drain(1 - slot)

    @pl.when(step == pl.num_programs(0) - 1)
    def _():
        drain(slot)


def _dispatch(lp_flat, tcnt, runb, pstart, cnt, nv, hrows, nbp, tm):
    t = lp_flat.shape[0] // TOP_K
    per_tile = pl.BlockSpec((N_EXPERTS,), lambda i: (i,), memory_space=pltpu.SMEM)
    smem_all = pl.BlockSpec((N_EXPERTS,), lambda i: (0,), memory_space=pltpu.SMEM)
    return pl.pallas_call(
        functools.partial(_dispatch_kernel, tm=tm, nbp=nbp),
        grid=(t // tm,),
        in_specs=[pl.BlockSpec((tm * TOP_K,), lambda i: (i,), memory_space=pltpu.SMEM),
                  per_tile, per_tile, smem_all, smem_all,
                  pl.BlockSpec((1,), lambda i: (0,), memory_space=pltpu.SMEM),
                  pl.BlockSpec((tm * ROW_TILES, LANES), lambda i: (i, 0))],
        out_specs=pl.BlockSpec(memory_space=pl.ANY),
        out_shape=jax.ShapeDtypeStruct((nbp * EXPERT_BLOCK * ROW_TILES, LANES), F32),
        scratch_shapes=[pltpu.VMEM((2, tm * TOP_K * ROW_TILES, LANES), F32),
                        pltpu.VMEM((EXPERT_BLOCK * ROW_TILES, LANES), F32),
                        pltpu.SemaphoreType.DMA((2,)), pltpu.SemaphoreType.DMA],
        compiler_params=_cparams(("arbitrary",)),
        name="dispatch",
    )(lp_flat, tcnt, runb, pstart, cnt, nv, hrows)


def _experts_kernel(blk_ref, nv_ref, xs_hbm, wg_hbm, wu_hbm, wd_hbm, ys_hbm, step_ref, wgb_ref, wub_ref, wdb_ref, *, nbp):
    rows = EXPERT_BLOCK * ROW_TILES
    nv = nv_ref[0]
    cur = lambda i: jnp.minimum(i, nv - 1)
    xmap = lambda i: (cur(i), 0)
    wmap = lambda i: (blk_ref[cur(i)], 0, 0)
    step_ref[0] = 0

    def body(xs_ref, wg_ref, wu_ref, wd_ref, ys_ref):
        i = step_ref[0]
        step_ref[0] = i + 1

        @pl.when(i >= nv)
        def _():
            ys_ref[...] = jnp.zeros_like(ys_ref)

        @pl.when(i < nv)
        def _():
            prev = blk_ref[jnp.maximum(i - 1, 0)]

            @pl.when((i == 0) | (blk_ref[i] != prev))
            def _():
                wgb_ref[...] = wg_ref[0].astype(BF16)
                wub_ref[...] = wu_ref[0].astype(BF16)
                wdb_ref[...] = wd_ref[0].astype(BF16)

            xb = jnp.concatenate(
                [xs_ref[pl.ds(s, EXPERT_BLOCK, stride=ROW_TILES), :] for s in range(ROW_TILES)], axis=1).astype(BF16)
            hid = _silu(_dot(xb, wgb_ref[...])) * _dot(xb, wub_ref[...])
            y = _dot(hid.astype(BF16), wdb_ref[...])
            for s in range(ROW_TILES):
                ys_ref[pl.ds(s, EXPERT_BLOCK, stride=ROW_TILES), :] = y[:, LANES * s:LANES * (s + 1)]

    ahead = pl.Buffered(2, use_lookahead=True)
    pltpu.emit_pipeline(
        body,
        grid=(nbp,),
        in_specs=[pl.BlockSpec((rows, LANES), xmap, pipeline_mode=pl.Buffered(3)),
                  pl.BlockSpec((1, D_MODEL, D_EXPERT), wmap, pipeline_mode=ahead),
                  pl.BlockSpec((1, D_MODEL, D_EXPERT), wmap, pipeline_mode=ahead),
                  pl.BlockSpec((1, D_EXPERT, D_MODEL), wmap, pipeline_mode=ahead)],
        out_specs=[pl.BlockSpec((rows, LANES), lambda i: (i, 0))],
    )(xs_hbm, wg_hbm, wu_hbm, wd_hbm, ys_hbm)


def _experts(blk, nv, xs, wg, wu, wd, nbp):
    smem = pl.BlockSpec(memory_space=pltpu.SMEM)
    hbm = pl.BlockSpec(memory_space=pl.ANY)
    return pl.pallas_call(
        functools.partial(_experts_kernel, nbp=nbp),
        in_specs=[smem, smem, hbm, hbm, hbm, hbm],
        out_specs=hbm,
        out_shape=jax.ShapeDtypeStruct(xs.shape, F32),
        scratch_shapes=[pltpu.SMEM((1,), I32),
                        pltpu.VMEM((D_MODEL, D_EXPERT), BF16), pltpu.VMEM((D_MODEL, D_EXPERT), BF16),
                        pltpu.VMEM((D_EXPERT, D_MODEL), BF16)],
        compiler_params=pltpu.CompilerParams(vmem_limit_bytes=VMEM_LIMIT),
        name="experts",
    )(blk, nv, xs, wg, wu, wd)


def _combine_kernel(lp_ref, w_ref, tc_ref, rb_ref, tcn_ref, rbn_ref, ps_ref, ys_ref, base_ref, g2_ref, fg_ref, out_ref,
                    buf_ref, acc_ref, sems, *, tm):
    step = pl.program_id(0) * pl.num_programs(1) + pl.program_id(1)
    last = pl.num_programs(0) * pl.num_programs(1) - 1
    slot = step % 2

    def fetch(tcnt_ref, runb_ref, s):
        _segment_copies(tcnt_ref, runb_ref, ps_ref,
                        lambda loc, glob, n: pltpu.make_async_copy(_rows(ys_ref, glob, n), _rows(buf_ref.at[s], loc, n),
                                                                   sems.at[s]))

    @pl.when(step == 0)
    def _():
        fetch(tc_ref, rb_ref, slot)

    @pl.when(step < last)
    def _():
        fetch(tcn_ref, rbn_ref, 1 - slot)

    mine = buf_ref.at[slot]
    pltpu.make_async_copy(_rows(ys_ref, 0, tm * TOP_K), mine, sems.at[slot]).wait()

    def token_body(t, carry):
        j0 = t * TOP_K
        acc = jnp.zeros((ROW_TILES, LANES), F32)
        for k in range(TOP_K):
            acc = acc + w_ref[j0 + k] * mine[pl.ds(pl.multiple_of(lp_ref[j0 + k], ROW_TILES), ROW_TILES), :]
        acc_ref[pl.ds(pl.multiple_of(t * ROW_TILES, ROW_TILES), ROW_TILES), :] = acc
        return carry

    lax.fori_loop(0, tm, token_body, 0, unroll=2)
    routed = jnp.concatenate([acc_ref[pl.ds(s, tm, stride=ROW_TILES), :] for s in range(ROW_TILES)], axis=1)
    out_ref[0] = _rms(base_ref[0] + g2_ref[0] * routed, fg_ref[...])


def _combine(lp_flat, w_flat, tcnt, runb, pstart, ys, base, g2, fg, tm):
    b, l, _ = base.shape
    nt = l // tm
    flat = lambda bi, i: (bi * nt + i,)
    following = lambda bi, i: (jnp.minimum(bi * nt + i + 1, b * nt - 1),)
    smem_blk = pl.BlockSpec((tm * TOP_K,), flat, memory_space=pltpu.SMEM)
    per_tile = pl.BlockSpec((N_EXPERTS,), flat, memory_space=pltpu.SMEM)
    next_tile = pl.BlockSpec((N_EXPERTS,), following, memory_space=pltpu.SMEM)
    return pl.pallas_call(
        functools.partial(_combine_kernel, tm=tm),
        grid=(b, nt),
        in_specs=[smem_blk, smem_blk, per_tile, per_tile, next_tile, next_tile,
                  pl.BlockSpec((N_EXPERTS,), lambda bi, i: (0,), memory_space=pltpu.SMEM),
                  pl.BlockSpec(memory_space=pl.ANY),
                  pl.BlockSpec((1, tm, D_MODEL), lambda bi, i: (bi, i, 0)),
                  pl.BlockSpec((1, 1, D_MODEL), lambda bi, i: (bi, 0, 0)),
                  pl.BlockSpec((1, D_MODEL), lambda bi, i: (0, 0))],
        out_specs=pl.BlockSpec((1, tm, D_MODEL), lambda bi, i: (bi, i, 0)),
        out_shape=jax.ShapeDtypeStruct((b, l, D_MODEL), F32),
        scratch_shapes=[pltpu.VMEM((2, tm * TOP_K * ROW_TILES, LANES), F32),
                        pltpu.VMEM((tm * ROW_TILES, LANES), F32),
                        pltpu.SemaphoreType.DMA((2,))],
        compiler_params=_cparams(("arbitrary", "arbitrary")),
        name="combine",
    )(lp_flat, w_flat, tcnt, runb, tcnt, runb, pstart, ys, base, g2, fg)


def _mixer_inputs(h, shift, scale, gain, wm, wl, wa, ba):
    return _proj(h, shift, scale, gain, wm, wl, wa, ba)


def kernel(x, c, ctx, c_ctx, ada_w, ada_b, norm1_g, norm2_g, w_in, gla_wa_f, gla_ba_f, gla_wa_b, gla_ba_b, gla_norm_g, s5_lam_re_f, s5_lam_im_f, s5_log_step_f, s5_lam_re_b, s5_lam_im_b, s5_log_step_b, s5_b_re, s5_b_im, s5_c_re_f, s5_c_im_f, s5_c_re_b, s5_c_im_b, s5_d, s5_glu_w, s5_glu_b, w_out, router_w, router_b, exp_w_gate, exp_w_up, exp_w_down, sh_w_gate, sh_w_up, sh_w_down, final_norm_g):
    b, l, d = x.shape
    i = 0

    rows = -(-(b + 1) // SUBLANES) * SUBLANES
    cs = jnp.zeros((rows, d), F32).at[:b].set(c).at[b].set(c_ctx)
    mod = _adaln(cs, ada_w[i], ada_b[i][None, :])
    sh1, sc1, g1, sh2, sc2, g2 = [mod[:b, d * j:d * (j + 1)][:, None, :] for j in range(6)]
    csh1, csc1 = [jnp.broadcast_to(mod[b, d * j:d * (j + 1)][None, None, :], (b, 1, d)) for j in range(2)]

    w = w_in[i]
    o1, o2, o3, o4, o5, o6 = 256, 512, 1024, 1536, 1552, 1568
    wm = jnp.concatenate([w[:, :o4], w[:, o6:]], axis=1).astype(BF16)
    wl = jnp.zeros((d, LANES), F32).at[:, :2 * GLA_GATE_RANK].set(w[:, o4:o6]).astype(BF16)
    wa = jnp.zeros((LANES, 2 * GLA_DK), F32)
    wa = wa.at[:GLA_GATE_RANK, :GLA_DK].set(gla_wa_f[i]).at[GLA_GATE_RANK:2 * GLA_GATE_RANK, GLA_DK:].set(gla_wa_b[i])
    wa = wa.astype(BF16)
    ba = jnp.concatenate([gla_ba_f[i], gla_ba_b[i]])[None, :]
    n1 = norm1_g[i][None, :]

    pcols = jnp.stack([s5_lam_re_f[i], s5_lam_im_f[i],
                       jnp.broadcast_to(s5_log_step_f[i][:, None], (S5_GROUPS, S5_STATE)),
                       s5_lam_re_b[i], s5_lam_im_b[i],
                       jnp.broadcast_to(s5_log_step_b[i][:, None], (S5_GROUPS, S5_STATE)),
                       jnp.zeros((S5_GROUPS, S5_STATE), F32), jnp.zeros((S5_GROUPS, S5_STATE), F32)], axis=-1)
    prows = pcols.transpose(0, 2, 1)
    tile_b = lambda t: jnp.tile(t, (1, 1, S5_CHUNK))
    tile_c = lambda t: jnp.tile(t.transpose(0, 2, 1), (1, 1, S5_CHUNK))
    m_op, wt_op, v_op, ab4 = _s5gen(pcols, prows, tile_b(s5_b_re[i]), tile_b(s5_b_im[i]),
                                    tile_c(s5_c_re_f[i]), tile_c(s5_c_im_f[i]),
                                    tile_c(s5_c_re_b[i]), tile_c(s5_c_im_b[i]))
    ab = ab4.reshape(S5_GROUPS, 1, 4 * S5_STATE)

    cq, ck, cv, _, _, claf, clab, cuv = _proj(ctx, csh1, csc1, n1, wm, wl, wa, ba)
    zero_state = jnp.zeros((b, GLA_DV, GLA_DK), F32)
    gsf, gsb = _gla(cq, ck, cv, claf, clab, zero_state, zero_state, with_output=False)
    (x0,) = _s5(cuv, None, wt_op, None, ab, jnp.zeros((S5_GROUPS, b, S5_VEC), F32), b, with_output=False)

    q, k, v, go, u, laf, lab, uv = _proj(x, sh1, sc1, n1, wm, wl, wa, ba)
    o, _, _ = _gla(q, k, v, laf, lab, gsf, gsb, with_output=True)
    yvec, _ = _s5(uv, m_op, wt_op, v_op, ab, x0, b, with_output=True)

    rw_hi = router_w[i].astype(BF16)
    rw_lo = (router_w[i] - rw_hi.astype(F32)).astype(BF16)
    base, hrows, logits = _post(
        x, o, go, u, yvec, g1, sh2, sc2, g2,
        gla_norm_g[i][None, :], s5_d[i][None, :], s5_glu_w[i].astype(BF16),
        s5_glu_b[i][None, :], w_out[i].astype(BF16), norm2_g[i][None, :], rw_hi, rw_lo,
        jnp.concatenate([sh_w_gate[i], sh_w_up[i]], axis=1).astype(BF16), sh_w_down[i].astype(BF16))

    t = b * l
    tile = min(512, l)
    wts, lpos, runb, tcnt, cnt = _route(logits.reshape(t, N_EXPERTS), router_b[i][:, None], tile)
    cnt_flat = cnt[:, 0]
    runb_flat, tcnt_flat = runb[:, :, 0].reshape(-1), tcnt[:, :, 0].reshape(-1)
    nbp = -(-_n_blocks_max(t * TOP_K) // SUBLANES) * SUBLANES
    pstart, blk, nv = _plan(cnt_flat[None, :], nbp)
    lp_flat, w_flat = lpos.T.reshape(-1), wts.T.reshape(-1)
    ps_flat = pstart.reshape(-1)
    xs = _dispatch(lp_flat, tcnt_flat, runb_flat, ps_flat, cnt_flat, nv.reshape(-1), hrows, nbp, tile)
    assert exp_w_gate.shape[0] == 1, "single-layer block"
    ys = _experts(blk.reshape(-1), nv.reshape(-1), xs, exp_w_gate.reshape(exp_w_gate.shape[1:]),
                  exp_w_up.reshape(exp_w_up.shape[1:]), exp_w_down.reshape(exp_w_down.shape[1:]), nbp)
    return _combine(lp_flat, w_flat, tcnt_flat, runb_flat, ps_flat, ys, base, g2, final_norm_g[None, :], tile)
```
